```python
import math
import jax, jax.numpy as jnp
from jax import lax
import numpy as np

D_MODEL = 1024
BATCH = 8
SEQ = 2048
DEPTH = 2
DEC_BATCH = 128
DEC_SEQ = 1
PAST_LEN = 16384
PAGE_SIZE = 128

D_MIX = D_MODEL
RET_WIDTH = D_MIX // 2
RET_HEADS = 4
RET_HEAD_DIM = RET_WIDTH // RET_HEADS
RET_CHUNK = 128
LRU_WIDTH = D_MIX // 4
LRU_BLOCKS = 4
LRU_BLOCK_DIM = LRU_WIDTH // LRU_BLOCKS
LRU_C = 8.0
CONV_WIDTH = 4
S5_WIDTH = D_MIX // 4
S5_GROUP = 16
S5_GROUPS = S5_WIDTH // S5_GROUP
S5_STATE = 64
D_IN = 4 * RET_WIDTH + 2 * LRU_WIDTH + 2 * S5_WIDTH
ROPE_BASE = 10000.0
EPS = 1e-6

kernel_name = 'hybrid_retention_rglru_s5_step'


def rmsnorm(x, g):
    xf = x.astype(jnp.float32)
    y = xf * lax.rsqrt(jnp.mean(xf * xf, axis=-1, keepdims=True) + EPS)
    return (y * g.astype(jnp.float32)).astype(x.dtype)


def rotary(x, pos0):
    L, d = x.shape[1], x.shape[-1]
    half = d // 2
    pos = pos0 + jnp.arange(L, dtype=jnp.float32)
    inv = ROPE_BASE ** (-jnp.arange(half, dtype=jnp.float32) / half)
    ang = pos[:, None] * inv[None, :]
    cos = jnp.cos(ang)[None, :, None, :].astype(x.dtype)
    sin = jnp.sin(ang)[None, :, None, :].astype(x.dtype)
    x1, x2 = x[..., :half], x[..., half:]
    return jnp.concatenate([x1 * cos - x2 * sin, x1 * sin + x2 * cos], axis=-1)


def retention_chunkwise(q, k, v, s0):
    bsz, L, H, dk = q.shape
    dv = v.shape[-1]
    C = RET_CHUNK if L % RET_CHUNK == 0 else L
    N = L // C
    dt = q.dtype
    log_g = jnp.log1p(-jnp.exp2(-5.0 - jnp.arange(H, dtype=jnp.float32)))
    idx = jnp.arange(C, dtype=jnp.float32)
    rel = idx[:, None] - idx[None, :]
    intra = jnp.where(rel[None] >= 0,
                      jnp.exp(jnp.maximum(rel, 0.0)[None] * log_g[:, None, None]), 0.0).astype(dt)
    q_dec = jnp.exp((idx + 1.0)[:, None] * log_g[None, :]).astype(dt)
    k_dec = jnp.exp((C - 1.0 - idx)[:, None] * log_g[None, :]).astype(dt)
    c_dec = jnp.exp(C * log_g).astype(dt)

    def blocks(t):
        return t.reshape(bsz, N, C, H, t.shape[-1]).swapaxes(0, 1)

    def step(S, qkv):
        qc, kc, vc = qkv
        sc = jnp.einsum('bihd,bjhd->bhij', qc, kc) * intra[None]
        o = (jnp.einsum('bhij,bjhe->bihe', sc, vc)
             + jnp.einsum('bihd,bhde->bihe', qc * q_dec[None, :, :, None], S))
        S = c_dec[None, :, None, None] * S + jnp.einsum('bjhd,bjhe->bhde', kc * k_dec[None, :, :, None], vc)
        return S, o

    s_last, o = lax.scan(step, s0.astype(dt), (blocks(q), blocks(k), blocks(v)))
    return o.swapaxes(0, 1).reshape(bsz, L, H, dv), s_last


def _lin_combine(e1, e2):
    a1, b1 = e1
    a2, b2 = e2
    return a1 * a2, a2 * b1 + b2


def linear_scan(a, b, h0):
    b = b.at[:, 0].add(a[:, 0] * h0)
    _, h = lax.associative_scan(_lin_combine, (a, b), axis=1)
    return h


def _cplx_combine(e1, e2):
    ar1, ai1, br1, bi1 = e1
    ar2, ai2, br2, bi2 = e2
    return (ar2 * ar1 - ai2 * ai1, ar2 * ai1 + ai2 * ar1,
            ar2 * br1 - ai2 * bi1 + br2, ar2 * bi1 + ai2 * br1 + bi2)


def complex_scan(ar, ai, br, bi, s0r, s0i):
    br = br.at[:, 0].add(ar[:, 0] * s0r - ai[:, 0] * s0i)
    bi = bi.at[:, 0].add(ar[:, 0] * s0i + ai[:, 0] * s0r)
    _, _, sr, si = lax.associative_scan(_cplx_combine, (ar, ai, br, bi), axis=1)
    return sr, si


def causal_conv(x, buf, w, b):
    L = x.shape[1]
    xp = jnp.concatenate([buf.astype(x.dtype), x], axis=1)
    out = b
    for j in range(CONV_WIDTH):
        out = out + xp[:, j:j + L] * w[j]
    return out, xp[:, -(CONV_WIDTH - 1):]


def hybrid_layer(x, c, pos0, s_ret, s_h, s_conv, s_re, s_im,
                 norm_g, w_ada, b_ada, w_in, ret_gn_g, conv_w, conv_b, w_rg, b_rg, w_ig, b_ig, lru_lambda,
                 s5_a_re, s5_a_im, s5_b_re, s5_b_im, s5_c_re, s5_c_im, s5_d, s5_log_dt, s5_w_glu, w_out):
    f32 = jnp.float32
    dt = x.dtype
    bsz, L, _ = x.shape
    mod = jax.nn.silu(c) @ w_ada + b_ada
    shift, scale, gate = jnp.split(mod, 3, axis=-1)
    h = rmsnorm(x, norm_g) * (1.0 + scale[:, None]) + shift[:, None]
    proj = h @ w_in
    sizes = [RET_WIDTH] * 4 + [LRU_WIDTH] * 2 + [S5_WIDTH] * 2
    q, k, v, g_ret, x_lru, g_lru, u_s5, g_s5 = jnp.split(proj, np.cumsum(sizes)[:-1].tolist(), axis=-1)

    q = rotary(q.reshape(bsz, L, RET_HEADS, RET_HEAD_DIM), pos0)
    k = rotary(k.reshape(bsz, L, RET_HEADS, RET_HEAD_DIM), pos0) * (RET_HEAD_DIM ** -0.5)
    v = v.reshape(bsz, L, RET_HEADS, RET_HEAD_DIM)
    o, s_ret_new = retention_chunkwise(q, k, v, s_ret)
    of = o.astype(f32)
    mu = jnp.mean(of, axis=-1, keepdims=True)
    var = jnp.mean(jnp.square(of - mu), axis=-1, keepdims=True)
    on = ((of - mu) * lax.rsqrt(var + EPS)).reshape(bsz, L, RET_WIDTH) * ret_gn_g.astype(f32)
    y_ret = on.astype(dt) * jax.nn.silu(g_ret)

    xc, conv_new = causal_conv(x_lru, s_conv, conv_w, conv_b)
    xb = xc.reshape(bsz, L, LRU_BLOCKS, LRU_BLOCK_DIM)
    r = jax.nn.sigmoid((jnp.einsum('blni,nij->blnj', xb, w_rg).reshape(bsz, L, LRU_WIDTH) + b_rg).astype(f32))
    ig = jax.nn.sigmoid((jnp.einsum('blni,nij->blnj', xb, w_ig).reshape(bsz, L, LRU_WIDTH) + b_ig).astype(f32))
    log_a = -LRU_C * r * jax.nn.softplus(-lru_lambda.astype(f32))
    a = jnp.exp(log_a)
    mult = jnp.sqrt(-jnp.expm1(2.0 * log_a))
    hh = linear_scan(a, mult * ig * xc.astype(f32), s_h.astype(f32))
    y_lru = hh.astype(dt) * jax.nn.silu(g_lru)

    ug = u_s5.reshape(bsz, L, S5_GROUPS, S5_GROUP).astype(f32)
    a_re, a_im = s5_a_re.astype(f32), s5_a_im.astype(f32)
    step_sz = jnp.exp(s5_log_dt.astype(f32))[:, None]
    mag = jnp.exp(step_sz * a_re)
    abar_r, abar_i = mag * jnp.cos(step_sz * a_im), mag * jnp.sin(step_sz * a_im)
    nr, ni = abar_r - 1.0, abar_i
    den = a_re * a_re + a_im * a_im
    fr, fi = (nr * a_re + ni * a_im) / den, (ni * a_re - nr * a_im) / den
    b_re, b_im = s5_b_re.astype(f32), s5_b_im.astype(f32)
    bbar_r = fr[..., None] * b_re - fi[..., None] * b_im
    bbar_i = fr[..., None] * b_im + fi[..., None] * b_re
    bu_r = jnp.einsum('blgc,gpc->blgp', ug, bbar_r)
    bu_i = jnp.einsum('blgc,gpc->blgp', ug, bbar_i)
    sr, si = complex_scan(jnp.broadcast_to(abar_r, bu_r.shape), jnp.broadcast_to(abar_i, bu_i.shape),
                          bu_r, bu_i, s_re.astype(f32), s_im.astype(f32))
    ys = (jnp.einsum('blgp,gcp->blgc', sr, s5_c_re.astype(f32))
          - jnp.einsum('blgp,gcp->blgc', si, s5_c_im.astype(f32))
          + s5_d.astype(f32).reshape(S5_GROUPS, S5_GROUP) * ug).reshape(bsz, L, S5_WIDTH)
    ys = jax.nn.gelu(ys)
    ys = ys * jax.nn.sigmoid(ys @ s5_w_glu.astype(f32))
    y_s5 = ys.astype(dt) * jax.nn.silu(g_s5)

    out = jnp.concatenate([y_ret, y_lru, y_s5], axis=-1) @ w_out
    x_new = x + gate[:, None] * out
    return x_new, (s_ret_new.astype(dt), hh[:, -1].astype(dt), conv_new.astype(dt),
                   sr[:, -1].astype(dt), si[:, -1].astype(dt))


def setup_inputs(seed: int = 0) -> dict:
    key = jax.random.key(seed)
    ks = iter(jax.random.split(key, 40))
    f32 = jnp.float32

    def nrm(shape, scale):
        return scale * jax.random.normal(next(ks), shape, f32)

    x_prompt = nrm((BATCH, SEQ, D_MODEL), 1.0)
    x_sample = nrm((DEC_BATCH, DEC_SEQ, D_MODEL), 1.0)
    state_ret = nrm((DEPTH, DEC_BATCH, RET_HEADS, RET_HEAD_DIM, RET_HEAD_DIM), 0.5)
    state_lru_h = nrm((DEPTH, DEC_BATCH, LRU_WIDTH), 0.5)
    state_lru_conv = nrm((DEPTH, DEC_BATCH, CONV_WIDTH - 1, LRU_WIDTH), 1.0)
    state_s5_re = nrm((DEPTH, DEC_BATCH, S5_GROUPS, S5_STATE), 0.1)
    state_s5_im = nrm((DEPTH, DEC_BATCH, S5_GROUPS, S5_STATE), 0.1)
    c_prompt = nrm((BATCH, D_MODEL), 1.0)
    c_sample = nrm((DEC_BATCH, D_MODEL), 1.0)
    norm_g = 1.0 + nrm((DEPTH, D_MODEL), 0.02)
    w_ada = nrm((DEPTH, D_MODEL, 3 * D_MODEL), 0.5 * D_MODEL ** -0.5)
    b_ada = nrm((DEPTH, 3 * D_MODEL), 0.02)
    w_in = nrm((DEPTH, D_MODEL, D_IN), D_MODEL ** -0.5)
    ret_gn_g = 1.0 + nrm((DEPTH, RET_WIDTH), 0.02)
    conv_w = nrm((DEPTH, CONV_WIDTH, LRU_WIDTH), CONV_WIDTH ** -0.5)
    conv_b = nrm((DEPTH, LRU_WIDTH), 0.02)
    w_rg = nrm((DEPTH, LRU_BLOCKS, LRU_BLOCK_DIM, LRU_BLOCK_DIM), LRU_BLOCK_DIM ** -0.5)
    b_rg = nrm((DEPTH, LRU_WIDTH), 0.02)
    w_ig = nrm((DEPTH, LRU_BLOCKS, LRU_BLOCK_DIM, LRU_BLOCK_DIM), LRU_BLOCK_DIM ** -0.5)
    b_ig = nrm((DEPTH, LRU_WIDTH), 0.02)
    u = jax.random.uniform(next(ks), (DEPTH, LRU_WIDTH), f32, 0.9, 0.999)
    a_base = u ** (1.0 / LRU_C)
    lru_lambda = jnp.log(a_base) - jnp.log1p(-a_base)
    n_idx = jnp.arange(S5_STATE, dtype=f32)
    s5_a_re = -0.5 + nrm((DEPTH, S5_GROUPS, S5_STATE), 0.01)
    s5_a_im = math.pi * n_idx + nrm((DEPTH, S5_GROUPS, S5_STATE), 0.01)
    s5_b_re = nrm((DEPTH, S5_GROUPS, S5_STATE, S5_GROUP), (2 * S5_GROUP) ** -0.5)
    s5_b_im = nrm((DEPTH, S5_GROUPS, S5_STATE, S5_GROUP), (2 * S5_GROUP) ** -0.5)
    s5_c_re = nrm((DEPTH, S5_GROUPS, S5_GROUP, S5_STATE), S5_STATE ** -0.5)
    s5_c_im = nrm((DEPTH, S5_GROUPS, S5_GROUP, S5_STATE), S5_STATE ** -0.5)
    s5_d = nrm((DEPTH, S5_WIDTH), 1.0)
    s5_log_dt = jax.random.uniform(next(ks), (DEPTH, S5_GROUPS), f32, math.log(1e-3), math.log(1e-1))
    s5_w_glu = nrm((DEPTH, S5_WIDTH, S5_WIDTH), S5_WIDTH ** -0.5)
    w_out = nrm((DEPTH, D_MIX, D_MODEL), D_MIX ** -0.5)
    final_g = 1.0 + nrm((D_MODEL,), 0.02)
    return {'x_prompt': x_prompt, 'x_sample': x_sample,
            'state_ret': state_ret, 'state_lru_h': state_lru_h, 'state_lru_conv': state_lru_conv,
            'state_s5_re': state_s5_re, 'state_s5_im': state_s5_im,
            'c_prompt': c_prompt, 'c_sample': c_sample,
            'norm_g': norm_g, 'w_ada': w_ada, 'b_ada': b_ada, 'w_in': w_in, 'ret_gn_g': ret_gn_g,
            'conv_w': conv_w, 'conv_b': conv_b, 'w_rg': w_rg, 'b_rg': b_rg, 'w_ig': w_ig, 'b_ig': b_ig,
            'lru_lambda': lru_lambda, 's5_a_re': s5_a_re, 's5_a_im': s5_a_im,
            's5_b_re': s5_b_re, 's5_b_im': s5_b_im, 's5_c_re': s5_c_re, 's5_c_im': s5_c_im,
            's5_d': s5_d, 's5_log_dt': s5_log_dt, 's5_w_glu': s5_w_glu, 'w_out': w_out, 'final_g': final_g}


def reference(x_prompt, x_sample, state_ret, state_lru_h, state_lru_conv, state_s5_re, state_s5_im,
              c_prompt, c_sample, norm_g, w_ada, b_ada, w_in, ret_gn_g, conv_w, conv_b, w_rg, b_rg,
              w_ig, b_ig, lru_lambda, s5_a_re, s5_a_im, s5_b_re, s5_b_im, s5_c_re, s5_c_im, s5_d,
              s5_log_dt, s5_w_glu, w_out, final_g):
    dt = x_prompt.dtype
    z_ret = jnp.zeros((BATCH, RET_HEADS, RET_HEAD_DIM, RET_HEAD_DIM), dt)
    z_h = jnp.zeros((BATCH, LRU_WIDTH), dt)
    z_conv = jnp.zeros((BATCH, CONV_WIDTH - 1, LRU_WIDTH), dt)
    z_s5 = jnp.zeros((BATCH, S5_GROUPS, S5_STATE), dt)
    xp, xs = x_prompt, x_sample
    new_p, new_s = [], []
    for l in range(DEPTH):
        w = (norm_g[l], w_ada[l], b_ada[l], w_in[l], ret_gn_g[l], conv_w[l], conv_b[l], w_rg[l], b_rg[l],
             w_ig[l], b_ig[l], lru_lambda[l], s5_a_re[l], s5_a_im[l], s5_b_re[l], s5_b_im[l],
             s5_c_re[l], s5_c_im[l], s5_d[l], s5_log_dt[l], s5_w_glu[l], w_out[l])
        xp, sp = hybrid_layer(xp, c_prompt, 0, z_ret, z_h, z_conv, z_s5, z_s5, *w)
        xs, ss = hybrid_layer(xs, c_sample, PAST_LEN, state_ret[l], state_lru_h[l], state_lru_conv[l],
                              state_s5_re[l], state_s5_im[l], *w)
        new_p.append(sp)
        new_s.append(ss)
    y_prompt = rmsnorm(xp, final_g)
    y_sample = rmsnorm(xs, final_g)
    ret_prompt = jnp.stack([s[0] for s in new_p])
    ret_sample = jnp.stack([s[0] for s in new_s])
    lru_h_prompt = jnp.stack([s[1] for s in new_p])
    lru_h_sample = jnp.stack([s[1] for s in new_s])
    lru_conv_prompt = jnp.stack([s[2] for s in new_p])
    lru_conv_sample = jnp.stack([s[2] for s in new_s])
    s5_re_prompt = jnp.stack([s[3] for s in new_p])
    s5_re_sample = jnp.stack([s[3] for s in new_s])
    s5_im_prompt = jnp.stack([s[4] for s in new_p])
    s5_im_sample = jnp.stack([s[4] for s in new_s])
    return (y_prompt, y_sample, ret_prompt, ret_sample, lru_h_prompt, lru_h_sample,
            lru_conv_prompt, lru_conv_sample, s5_re_prompt, s5_re_sample, s5_im_prompt, s5_im_sample)
```

```python
import functools
import math

import jax
import jax.numpy as jnp
import numpy as np
from jax import lax
from jax.experimental import pallas as pl
from jax.experimental.pallas import tpu as pltpu

RET_HEADS = 4
HEAD_DIM = 128
RET_WIDTH = RET_HEADS * HEAD_DIM
RET_CHUNK = 128
LRU_WIDTH = 256
LRU_BLOCKS = 4
LRU_C = 8.0
CONV_WIDTH = 4
S5_WIDTH = 256
S5_GROUP = 16
S5_GROUPS = 16
S5_STATE = 64
S5_FLAT = S5_GROUPS * S5_STATE
ROPE_BASE = 10000.0
EPS = 1e-6
PAST_LEN = 16384

OFF_Q = 0
OFF_K = OFF_Q + RET_WIDTH
OFF_V = OFF_K + RET_WIDTH
OFF_GR = OFF_V + RET_WIDTH
OFF_XL = OFF_GR + RET_WIDTH
OFF_GL = OFF_XL + LRU_WIDTH
OFF_U = OFF_GL + LRU_WIDTH
OFF_GS = OFF_U + S5_WIDTH
D_IN = OFF_GS + S5_WIDTH

SUBLANES = 8
SEQ_TILE = 512
DEC_BLOCK = 8
VMEM_LIMIT = 56 * 1024 * 1024

F32 = jnp.float32
BF16 = jnp.bfloat16


def _sigmoid(x):
    return 1.0 / (1.0 + jnp.exp(-x))


def _silu(x):
    return x * _sigmoid(x)


def _gelu_tanh(x):
    c = math.sqrt(2.0 / math.pi)
    return x * (0.5 * (1.0 + jnp.tanh(c * (x + 0.044715 * (x * x * x)))))


def _softplus(x):
    return jnp.maximum(x, 0.0) + jnp.log1p(jnp.exp(-jnp.abs(x)))


def _rmsnorm(x, g):
    ms = jnp.mean(x * x, axis=-1, keepdims=True)
    return x * lax.rsqrt(ms + EPS) * g


def _dot(a, b):
    return jnp.dot(a.astype(BF16), b.astype(BF16), preferred_element_type=F32)


def _dot_nt(a, b):
    return lax.dot_general(a.astype(BF16), b.astype(BF16), (((1,), (1,)), ((), ())),
                           preferred_element_type=F32)


def _rotary(x, cosf, sinf):
    return x * cosf + pltpu.roll(x, HEAD_DIM // 2, axis=1) * sinf


def _groupnorm(o, g):
    mu = jnp.mean(o, axis=-1, keepdims=True)
    d = o - mu
    var = jnp.mean(d * d, axis=-1, keepdims=True)
    return d * lax.rsqrt(var + EPS) * g


def _lru_coeffs(xc, gates, sp):
    r = _sigmoid(gates[:, :LRU_WIDTH])
    ig = _sigmoid(gates[:, LRU_WIDTH:])
    log_a = (-LRU_C) * r * sp
    a = jnp.exp(log_a)
    th = jnp.tanh(log_a)
    mult = jnp.sqrt(-2.0 * th / (1.0 - th))
    return a, mult * ig * xc


def _s5_tail(sr_si, u, gs, cd, s5d, wglu):
    ys = _dot(sr_si, cd) + s5d * u
    ys = _gelu_tanh(ys)
    ys = ys * _sigmoid(_dot(ys, wglu))
    return ys * _silu(gs)


def _s5_prep_kernel(are_ref, aim_ref, ldt_ref, bre_ref, bim_ref,
                    apr_ref, api_ref, bbr_ref, bbi_ref):
    depth = are_ref.shape[0]
    for l in range(depth):
        a_re = are_ref[l]
        a_im = aim_ref[l]
        step = jnp.exp(ldt_ref[l])
        mag = jnp.exp(step * a_re)
        abar_r = mag * jnp.cos(step * a_im)
        abar_i = mag * jnp.sin(step * a_im)
        nr, ni = abar_r - 1.0, abar_i
        den = a_re * a_re + a_im * a_im
        fr = (nr * a_re + ni * a_im) / den
        fi = (ni * a_re - nr * a_im) / den
        for g in range(S5_GROUPS):
            frg, fig = fr[g:g + 1, :], fi[g:g + 1, :]
            b_re, b_im = bre_ref[l, g], bim_ref[l, g]
            bbr_ref[l, g] = frg * b_re - fig * b_im
            bbi_ref[l, g] = frg * b_im + fig * b_re
        pr, pi = abar_r, abar_i
        apr_ref[l, 0] = pr
        api_ref[l, 0] = pi
        for m in range(1, SUBLANES):
            pr, pi = pr * abar_r - pi * abar_i, pr * abar_i + pi * abar_r
            apr_ref[l, m] = pr
            api_ref[l, m] = pi


def _s5_prep(s5_a_re, s5_a_im, s5_log_dt, s5_b_re, s5_b_im):
    depth = s5_a_re.shape[0]
    b_re_t = jnp.swapaxes(s5_b_re, 2, 3)
    b_im_t = jnp.swapaxes(s5_b_im, 2, 3)
    ldt = s5_log_dt.reshape(depth, S5_GROUPS, 1)
    out_shape = (
        jax.ShapeDtypeStruct((depth, SUBLANES, S5_GROUPS, S5_STATE), F32),
        jax.ShapeDtypeStruct((depth, SUBLANES, S5_GROUPS, S5_STATE), F32),
        jax.ShapeDtypeStruct((depth, S5_GROUPS, S5_GROUP, S5_STATE), F32),
        jax.ShapeDtypeStruct((depth, S5_GROUPS, S5_GROUP, S5_STATE), F32),
    )
    return pl.pallas_call(_s5_prep_kernel, out_shape=out_shape, name="s5_prep")(
        s5_a_re, s5_a_im, ldt, b_re_t, b_im_t)


def _ada_kernel(c_ref, w_ref, b_ref, o_ref):
    s = _silu(c_ref[...])
    o_ref[0] = _dot(s, w_ref[0]) + b_ref[0]


def _ada(c_all, w_ada, b_ada):
    depth, d, n3 = w_ada.shape
    rows = c_all.shape[0]
    tn = d
    return pl.pallas_call(
        _ada_kernel,
        grid=(depth, n3 // tn),
        in_specs=[
            pl.BlockSpec((rows, d), lambda l, j: (0, 0)),
            pl.BlockSpec((1, d, tn), lambda l, j: (l, 0, j)),
            pl.BlockSpec((1, 1, tn), lambda l, j: (l, 0, j)),
        ],
        out_specs=pl.BlockSpec((1, rows, tn), lambda l, j: (l, 0, j)),
        out_shape=jax.ShapeDtypeStruct((depth, rows, n3), F32),
        compiler_params=pltpu.CompilerParams(
            dimension_semantics=("arbitrary", "arbitrary"), vmem_limit_bytes=VMEM_LIMIT),
        name="adaln",
    )(c_all, w_ada, b_ada.reshape(depth, 1, n3))


def _prompt_kernel(x_ref, mod_ref, ng_ref, win_ref, wout_ref,
                   cq_ref, sq_ref, ck_ref, sk_ref, intra_ref, qdec_ref, kdec_ref, cdec_ref,
                   gng_ref, cw_ref, cb_ref, wg_ref, bg_ref, lam_ref,
                   bd_ref, cd_ref, apr_ref, api_ref, s5d_ref, wglu_ref, fg_ref,
                   y_ref, sret_ref, hl_ref, cv_ref, sre_ref, sim_ref,
                   proj_scr, ycat_scr, xpad_scr, ab_scr, st_scr, *, final):
    tl = x_ref.shape[1]
    t = pl.program_id(1)

    @pl.when(t == 0)
    def _():
        sret_ref[...] = jnp.zeros_like(sret_ref)
        hl_ref[...] = jnp.zeros_like(hl_ref)
        sre_ref[...] = jnp.zeros_like(sre_ref)
        sim_ref[...] = jnp.zeros_like(sim_ref)
        xpad_scr[pl.ds(0, SUBLANES), :] = jnp.zeros((SUBLANES, LRU_WIDTH), F32)

    x = x_ref[0]
    shift, scale = mod_ref[0, 0:1, :], mod_ref[0, 1:2, :]
    h = _rmsnorm(x, ng_ref[...]) * (1.0 + scale) + shift
    proj_scr[...] = _dot(h, win_ref[...])

    def ret_chunk(c, carry):
        r0 = pl.multiple_of(c * RET_CHUNK, RET_CHUNK)
        rows = pl.ds(r0, RET_CHUNK)
        cq, sq, ck, sk = cq_ref[rows, :], sq_ref[rows, :], ck_ref[rows, :], sk_ref[rows, :]
        for hd in range(RET_HEADS):
            lanes = pl.ds(hd * HEAD_DIM, HEAD_DIM)
            q = _rotary(proj_scr[rows, pl.ds(OFF_Q + hd * HEAD_DIM, HEAD_DIM)], cq, sq)
            k = _rotary(proj_scr[rows, pl.ds(OFF_K + hd * HEAD_DIM, HEAD_DIM)], ck, sk)
            v = proj_scr[rows, pl.ds(OFF_V + hd * HEAD_DIM, HEAD_DIM)].astype(BF16)
            s_old = sret_ref[0, hd]
            sc = _dot_nt(q, k) * intra_ref[hd]
            o = _dot(sc, v) + _dot(q * qdec_ref[hd], s_old)
            kd_t = (k * kdec_ref[hd]).T
            sret_ref[0, hd] = cdec_ref[hd] * s_old + _dot(kd_t, v)
            on = _groupnorm(o, gng_ref[:, lanes])
            g_ret = proj_scr[rows, pl.ds(OFF_GR + hd * HEAD_DIM, HEAD_DIM)]
            ycat_scr[rows, lanes] = (on * _silu(g_ret)).astype(BF16)
        return carry

    lax.fori_loop(0, tl // RET_CHUNK, ret_chunk, 0)

    sub = lax.broadcasted_iota(jnp.int32, (SUBLANES, 1), 0)

    xl = proj_scr[:, pl.ds(OFF_XL, LRU_WIDTH)]
    xpad_scr[pl.ds(SUBLANES, tl), :] = xl
    xc = cb_ref[...]
    for j in range(CONV_WIDTH):
        xc = xc + xpad_scr[pl.ds(SUBLANES - (CONV_WIDTH - 1) + j, tl), :] * cw_ref[pl.ds(j, 1), :]
    tail = xpad_scr[pl.ds(tl + SUBLANES - (CONV_WIDTH - 1), CONV_WIDTH - 1), :]
    cv_ref[0] = tail
    xpad_scr[pl.ds(SUBLANES - (CONV_WIDTH - 1), CONV_WIDTH - 1), :] = tail
    gates = _dot(xc, wg_ref[...]) + bg_ref[...]
    a, b = _lru_coeffs(xc, gates, _softplus(-lam_ref[...]))
    ab_scr[:, pl.ds(0, LRU_WIDTH)] = a
    ab_scr[:, pl.ds(LRU_WIDTH, LRU_WIDTH)] = b

    def lru_group(g, carry):
        rows = pl.ds(pl.multiple_of(g * SUBLANES, SUBLANES), SUBLANES)
        A = ab_scr[rows, pl.ds(0, LRU_WIDTH)]
        H = ab_scr[rows, pl.ds(LRU_WIDTH, LRU_WIDTH)]
        for kk in (1, 2, 4):
            m = sub >= kk
            a_s = jnp.where(m, pltpu.roll(A, kk, axis=0), 1.0)
            h_s = jnp.where(m, pltpu.roll(H, kk, axis=0), 0.0)
            H = H + A * h_s
            A = A * a_s
        hh = H + A * carry
        ab_scr[rows, pl.ds(LRU_WIDTH, LRU_WIDTH)] = hh
        return hh[SUBLANES - 1:SUBLANES, :]

    h_last = lax.fori_loop(0, tl // SUBLANES, lru_group, hl_ref[0])
    hl_ref[0] = h_last
    gl = proj_scr[:, pl.ds(OFF_GL, LRU_WIDTH)]
    ycat_scr[:, pl.ds(RET_WIDTH, LRU_WIDTH)] = (
        ab_scr[:, pl.ds(LRU_WIDTH, LRU_WIDTH)] * _silu(gl)).astype(BF16)

    u = proj_scr[:, pl.ds(OFF_U, S5_WIDTH)]
    st_scr[...] = _dot(u, bd_ref[...])
    re_l, im_l = pl.ds(0, S5_FLAT), pl.ds(S5_FLAT, S5_FLAT)
    qr, qi = apr_ref[...], api_ref[...]

    def s5_group(g, carry):
        cr, ci = carry
        rows = pl.ds(pl.multiple_of(g * SUBLANES, SUBLANES), SUBLANES)
        hr, hi = st_scr[rows, re_l], st_scr[rows, im_l]
        for kk in (1, 2, 4):
            m = sub >= kk
            pr = jnp.where(m, apr_ref[pl.ds(kk - 1, 1), :], 0.0)
            pi = jnp.where(m, api_ref[pl.ds(kk - 1, 1), :], 0.0)
            sr, si = pltpu.roll(hr, kk, axis=0), pltpu.roll(hi, kk, axis=0)
            hr, hi = hr + (pr * sr - pi * si), hi + (pr * si + pi * sr)
        hr = hr + (qr * cr - qi * ci)
        hi = hi + (qr * ci + qi * cr)
        st_scr[rows, re_l] = hr
        st_scr[rows, im_l] = hi
        return hr[SUBLANES - 1:SUBLANES, :], hi[SUBLANES - 1:SUBLANES, :]

    cr, ci = lax.fori_loop(0, tl // SUBLANES, s5_group, (sre_ref[0], sim_ref[0]))
    sre_ref[0] = cr
    sim_ref[0] = ci
    gs = proj_scr[:, pl.ds(OFF_GS, S5_WIDTH)]
    y_s5 = _s5_tail(st_scr[...], u, gs, cd_ref[...], s5d_ref[...], wglu_ref[...])
    ycat_scr[:, pl.ds(RET_WIDTH + LRU_WIDTH, S5_WIDTH)] = y_s5.astype(BF16)

    out = jnp.dot(ycat_scr[...], wout_ref[...], preferred_element_type=F32)
    x_new = x_ref[0] + mod_ref[0, 2:3, :] * out
    if final:
        x_new = _rmsnorm(x_new, fg_ref[...])
    y_ref[0] = x_new


def _prompt_layer(x, mod, w, tabs, *, final):
    bsz, seq, d = x.shape
    tl = min(SEQ_TILE, seq)
    nt = seq // tl
    const2 = lambda b, t: (0, 0)
    const3 = lambda b, t: (0, 0, 0)

    def full(a):
        return pl.BlockSpec(a.shape, const2 if a.ndim == 2 else const3)

    row_tab = pl.BlockSpec((tl, HEAD_DIM), lambda b, t: (t, 0))
    in_specs = [
        pl.BlockSpec((1, tl, d), lambda b, t: (b, t, 0)),
        pl.BlockSpec((1, 3, d), lambda b, t: (b, 0, 0)),
        full(w["norm_g"]), full(w["w_in"]), full(w["w_out"]),
        row_tab, row_tab, row_tab, row_tab,
        full(tabs["intra"]), full(tabs["q_dec"]), full(tabs["k_dec"]),
        pl.BlockSpec(memory_space=pltpu.SMEM),
        full(w["gn_g"]), full(w["conv_w"]), full(w["conv_b"]), full(w["w_gate"]), full(w["b_gate"]),
        full(w["lam"]), full(w["bd"]), full(w["cd"]), full(w["apr"]), full(w["api"]),
        full(w["s5_d"]), full(w["w_glu"]), full(w["final_g"]),
    ]
    out_shape = (
        jax.ShapeDtypeStruct((bsz, seq, d), F32),
        jax.ShapeDtypeStruct((bsz, RET_HEADS, HEAD_DIM, HEAD_DIM), F32),
        jax.ShapeDtypeStruct((bsz, 1, LRU_WIDTH), F32),
        jax.ShapeDtypeStruct((bsz, CONV_WIDTH - 1, LRU_WIDTH), F32),
        jax.ShapeDtypeStruct((bsz, 1, S5_FLAT), F32),
        jax.ShapeDtypeStruct((bsz, 1, S5_FLAT), F32),
    )
    out_specs = (
        pl.BlockSpec((1, tl, d), lambda b, t: (b, t, 0)),
        pl.BlockSpec((1, RET_HEADS, HEAD_DIM, HEAD_DIM), lambda b, t: (b, 0, 0, 0)),
        pl.BlockSpec((1, 1, LRU_WIDTH), lambda b, t: (b, 0, 0)),
        pl.BlockSpec((1, CONV_WIDTH - 1, LRU_WIDTH), lambda b, t: (b, 0, 0)),
        pl.BlockSpec((1, 1, S5_FLAT), lambda b, t: (b, 0, 0)),
        pl.BlockSpec((1, 1, S5_FLAT), lambda b, t: (b, 0, 0)),
    )
    scratch = [
        pltpu.VMEM((tl, D_IN), F32),
        pltpu.VMEM((tl, d), BF16),
        pltpu.VMEM((tl + SUBLANES, LRU_WIDTH), F32),
        pltpu.VMEM((tl, 2 * LRU_WIDTH), F32),
        pltpu.VMEM((tl, 2 * S5_FLAT), F32),
    ]
    return pl.pallas_call(
        functools.partial(_prompt_kernel, final=final),
        grid=(bsz, nt),
        in_specs=in_specs,
        out_specs=out_specs,
        out_shape=out_shape,
        scratch_shapes=scratch,
        compiler_params=pltpu.CompilerParams(
            dimension_semantics=("arbitrary", "arbitrary"), vmem_limit_bytes=VMEM_LIMIT),
        name="prompt_layer",
    )(x, mod, w["norm_g"], w["w_in"], w["w_out"],
      tabs["cos_q"], tabs["sin_q"], tabs["cos_k"], tabs["sin_k"],
      tabs["intra"], tabs["q_dec"], tabs["k_dec"], tabs["c_dec"],
      w["gn_g"], w["conv_w"], w["conv_b"], w["w_gate"], w["b_gate"], w["lam"],
      w["bd"], w["cd"], w["apr"], w["api"], w["s5_d"], w["w_glu"], w["final_g"])


def _decode_kernel(x_ref, mod_ref, ng_ref, win_ref, wout_ref,
                   cq_ref, sq_ref, ck_ref, sk_ref, gam_ref,
                   gng_ref, cw_ref, cb_ref, wg_ref, bg_ref, lam_ref,
                   bd_ref, cd_ref, apr_ref, api_ref, s5d_ref, wglu_ref, fg_ref,
                   sret_in, hl_in, cv_in, sre_in, sim_in,
                   y_ref, sret_ref, hl_ref, cv_ref, sre_ref, sim_ref,
                   qt_scr, kt_scr, v_scr, o_scr, gr_scr, ycat_scr, *, final):
    i = pl.program_id(0)
    nb = pl.num_programs(0)
    rows_n, d = x_ref.shape

    @pl.when(i == 0)
    def _():
        x = x_ref[...]
        shift, scale = mod_ref[:, pl.ds(0, d)], mod_ref[:, pl.ds(d, d)]
        h = _rmsnorm(x, ng_ref[...]) * (1.0 + scale) + shift
        proj = _dot(h, win_ref[...])
        cq, sq, ck, sk = cq_ref[...], sq_ref[...], ck_ref[...], sk_ref[...]
        for hd in range(RET_HEADS):
            lo = hd * HEAD_DIM
            qt_scr[hd] = _rotary(proj[:, OFF_Q + lo:OFF_Q + lo + HEAD_DIM], cq, sq).T
            kt_scr[hd] = _rotary(proj[:, OFF_K + lo:OFF_K + lo + HEAD_DIM], ck, sk).T
        v_scr[...] = proj[:, OFF_V:OFF_V + RET_WIDTH]
        gr_scr[...] = _silu(proj[:, OFF_GR:OFF_GR + RET_WIDTH])
        xl = proj[:, OFF_XL:OFF_XL + LRU_WIDTH]
        cs = cv_in[...]
        xc = cb_ref[...] + xl * cw_ref[pl.ds(CONV_WIDTH - 1, 1), :]
        for j in range(CONV_WIDTH - 1):
            xc = xc + cs[:, j * LRU_WIDTH:(j + 1) * LRU_WIDTH] * cw_ref[pl.ds(j, 1), :]
        cv_ref[:, pl.ds(0, 2 * LRU_WIDTH)] = cs[:, LRU_WIDTH:]
        cv_ref[:, pl.ds(2 * LRU_WIDTH, LRU_WIDTH)] = xl
        gates = _dot(xc, wg_ref[...]) + bg_ref[...]
        a, b = _lru_coeffs(xc, gates, _softplus(-lam_ref[...]))
        hh = b + a * hl_in[...]
        hl_ref[...] = hh
        ycat_scr[:, pl.ds(RET_WIDTH, LRU_WIDTH)] = (
            hh * _silu(proj[:, OFF_GL:OFF_GL + LRU_WIDTH])).astype(BF16)
        u = proj[:, OFF_U:OFF_U + S5_WIDTH]
        bu = _dot(u, bd_ref[...])
        ar, ai = apr_ref[pl.ds(0, 1), :], api_ref[pl.ds(0, 1), :]
        s0r, s0i = sre_in[...], sim_in[...]
        sr = bu[:, :S5_FLAT] + (ar * s0r - ai * s0i)
        si = bu[:, S5_FLAT:] + (ar * s0i + ai * s0r)
        sre_ref[...] = sr
        sim_ref[...] = si
        y_s5 = _s5_tail(jnp.concatenate([sr, si], axis=-1), u, proj[:, OFF_GS:OFF_GS + S5_WIDTH],
                        cd_ref[...], s5d_ref[...], wglu_ref[...])
        ycat_scr[:, pl.ds(RET_WIDTH + LRU_WIDTH, S5_WIDTH)] = y_s5.astype(BF16)

    lane = lax.broadcasted_iota(jnp.int32, (HEAD_DIM, rows_n), 1)
    blk_rows = pl.ds(pl.multiple_of(i * DEC_BLOCK, DEC_BLOCK), DEC_BLOCK)
    for hd in range(RET_HEADS):
        lanes = pl.ds(hd * HEAD_DIM, HEAD_DIM)
        vblk = v_scr[blk_rows, lanes]
        o_rows = []
        for j in range(DEC_BLOCK):
            sel = lane == i * DEC_BLOCK + j
            kcol = jnp.sum(jnp.where(sel, kt_scr[hd], 0.0), axis=1, keepdims=True)
            qcol = jnp.sum(jnp.where(sel, qt_scr[hd], 0.0), axis=1, keepdims=True)
            s_new = gam_ref[hd] * sret_in[j, hd] + kcol * vblk[j:j + 1, :]
            sret_ref[j, hd] = s_new
            o_rows.append(jnp.sum(qcol * s_new, axis=0, keepdims=True))
        o_scr[blk_rows, lanes] = jnp.concatenate(o_rows, axis=0)

    @pl.when(i == nb - 1)
    def _():
        for hd in range(RET_HEADS):
            lanes = pl.ds(hd * HEAD_DIM, HEAD_DIM)
            on = _groupnorm(o_scr[:, lanes], gng_ref[:, lanes])
            ycat_scr[:, lanes] = (on * gr_scr[:, lanes]).astype(BF16)
        out = jnp.dot(ycat_scr[...], wout_ref[...], preferred_element_type=F32)
        x_new = x_ref[...] + mod_ref[:, pl.ds(2 * d, d)] * out
        if final:
            x_new = _rmsnorm(x_new, fg_ref[...])
        y_ref[...] = x_new


def _decode_layer(x, mod, states, w, tabs, *, final):
    rows, d = x.shape
    s_ret, s_h, s_conv, s_re, s_im = states
    nb = rows // DEC_BLOCK
    const = lambda i: (0, 0)

    def full(a):
        return pl.BlockSpec(a.shape, const)

    smem = pl.BlockSpec(memory_space=pltpu.SMEM)
    sblk = pl.BlockSpec((DEC_BLOCK, RET_HEADS, HEAD_DIM, HEAD_DIM), lambda i: (i, 0, 0, 0))
    in_specs = [
        full(x), full(mod), full(w["norm_g"]), full(w["w_in"]), full(w["w_out"]),
        full(tabs["dcos_q"]), full(tabs["dsin_q"]), full(tabs["dcos_k"]), full(tabs["dsin_k"]), smem,
        full(w["gn_g"]), full(w["conv_w"]), full(w["conv_b"]), full(w["w_gate"]), full(w["b_gate"]),
        full(w["lam"]), full(w["bd"]), full(w["cd"]), full(w["apr"]), full(w["api"]),
        full(w["s5_d"]), full(w["w_glu"]), full(w["final_g"]),
        sblk, full(s_h), full(s_conv), full(s_re), full(s_im),
    ]
    out_shape = (
        jax.ShapeDtypeStruct((rows, d), F32),
        jax.ShapeDtypeStruct(s_ret.shape, F32),
        jax.ShapeDtypeStruct(s_h.shape, F32),
        jax.ShapeDtypeStruct(s_conv.shape, F32),
        jax.ShapeDtypeStruct(s_re.shape, F32),
        jax.ShapeDtypeStruct(s_im.shape, F32),
    )
    out_specs = (full(x), sblk, full(s_h), full(s_conv), full(s_re), full(s_im))
    scratch = [
        pltpu.VMEM((RET_HEADS, HEAD_DIM, rows), F32),
        pltpu.VMEM((RET_HEADS, HEAD_DIM, rows), F32),
        pltpu.VMEM((rows, RET_WIDTH), F32),
        pltpu.VMEM((rows, RET_WIDTH), F32),
        pltpu.VMEM((rows, RET_WIDTH), F32),
        pltpu.VMEM((rows, d), BF16),
    ]
    return pl.pallas_call(
        functools.partial(_decode_kernel, final=final),
        grid=(nb,),
        in_specs=in_specs,
        out_specs=out_specs,
        out_shape=out_shape,
        scratch_shapes=scratch,
        compiler_params=pltpu.CompilerParams(
            dimension_semantics=("arbitrary",), vmem_limit_bytes=VMEM_LIMIT),
        name="decode_layer",
    )(x, mod, w["norm_g"], w["w_in"], w["w_out"],
      tabs["dcos_q"], tabs["dsin_q"], tabs["dcos_k"], tabs["dsin_k"], tabs["gamma"],
      w["gn_g"], w["conv_w"], w["conv_b"], w["w_gate"], w["b_gate"], w["lam"],
      w["bd"], w["cd"], w["apr"], w["api"], w["s5_d"], w["w_glu"], w["final_g"],
      s_ret, s_h, s_conv, s_re, s_im)


def _rope_tables(pos, scale):
    half = HEAD_DIM // 2
    inv = ROPE_BASE ** (-jnp.arange(half, dtype=F32) / half)
    ang = pos[:, None] * inv[None, :]
    cos, sin = jnp.cos(ang), jnp.sin(ang)
    cosf = jnp.concatenate([cos, cos], axis=-1) * scale
    sinf = jnp.concatenate([-sin, sin], axis=-1) * scale
    return cosf, sinf


def _tables(seq):
    c = RET_CHUNK
    log_g = jnp.log1p(-jnp.exp2(-5.0 - jnp.arange(RET_HEADS, dtype=F32)))
    idx = jnp.arange(c, dtype=F32)
    rel = idx[:, None] - idx[None, :]
    intra = jnp.where(rel[None] >= 0,
                      jnp.exp(jnp.maximum(rel, 0.0)[None] * log_g[:, None, None]), 0.0)
    q_dec = jnp.exp((idx + 1.0)[:, None] * log_g[None, :])
    k_dec = jnp.exp((c - 1.0 - idx)[:, None] * log_g[None, :])
    q_dec = jnp.broadcast_to(q_dec.T[:, :, None], (RET_HEADS, c, HEAD_DIM))
    k_dec = jnp.broadcast_to(k_dec.T[:, :, None], (RET_HEADS, c, HEAD_DIM))
    c_dec = jnp.exp(c * log_g)
    gamma = jnp.exp(log_g)
    k_scale = HEAD_DIM ** -0.5
    cos_q, sin_q = _rope_tables(jnp.arange(seq, dtype=F32), 1.0)
    cos_k, sin_k = _rope_tables(jnp.arange(seq, dtype=F32), k_scale)
    dpos = PAST_LEN + jnp.arange(1, dtype=F32)
    dcos_q, dsin_q = _rope_tables(dpos, 1.0)
    dcos_k, dsin_k = _rope_tables(dpos, k_scale)
    return dict(intra=intra, q_dec=q_dec, k_dec=k_dec, c_dec=c_dec, gamma=gamma,
                cos_q=cos_q, sin_q=sin_q, cos_k=cos_k, sin_k=sin_k,
                dcos_q=dcos_q, dsin_q=dsin_q, dcos_k=dcos_k, dsin_k=dsin_k)


def _block_diag(blocks):
    n, r, c = blocks.shape
    eye = jnp.eye(n, dtype=blocks.dtype)
    return jnp.einsum("nrc,nm->nrmc", blocks, eye).reshape(n * r, n * c)


def kernel(x_prompt, x_sample, state_ret, state_lru_h, state_lru_conv, state_s5_re, state_s5_im,
           c_prompt, c_sample, norm_g, w_ada, b_ada, w_in, ret_gn_g, conv_w, conv_b, w_rg, b_rg,
           w_ig, b_ig, lru_lambda, s5_a_re, s5_a_im, s5_b_re, s5_b_im, s5_c_re, s5_c_im, s5_d,
           s5_log_dt, s5_w_glu, w_out, final_g):
    depth = w_in.shape[0]
    bsz, seq, d = x_prompt.shape
    dec = x_sample.shape[0]
    assert x_sample.shape[1] == 1 and seq % RET_CHUNK == 0 and dec % DEC_BLOCK == 0

    tabs = _tables(seq)
    mod_all = _ada(jnp.concatenate([c_prompt, c_sample], axis=0), w_ada, b_ada)
    apr, api, bbr, bbi = _s5_prep(s5_a_re, s5_a_im, s5_log_dt, s5_b_re, s5_b_im)

    xp = x_prompt
    xs = x_sample.reshape(dec, d)
    outs_p, outs_s = [], []
    for l in range(depth):
        bd = jnp.concatenate([_block_diag(bbr[l]), _block_diag(bbi[l])], axis=1)
        cd = jnp.concatenate([_block_diag(jnp.swapaxes(s5_c_re[l], 1, 2)),
                              _block_diag(jnp.swapaxes(-s5_c_im[l], 1, 2))], axis=0)
        w = dict(
            norm_g=norm_g[l].reshape(1, d),
            w_in=w_in[l].astype(BF16),
            w_out=w_out[l].astype(BF16),
            gn_g=ret_gn_g[l].reshape(1, RET_WIDTH),
            conv_w=conv_w[l],
            conv_b=conv_b[l].reshape(1, LRU_WIDTH),
            w_gate=jnp.concatenate([_block_diag(w_rg[l]), _block_diag(w_ig[l])], axis=1).astype(BF16),
            b_gate=jnp.concatenate([b_rg[l], b_ig[l]]).reshape(1, 2 * LRU_WIDTH),
            lam=lru_lambda[l].reshape(1, LRU_WIDTH),
            bd=bd.astype(BF16),
            cd=cd.astype(BF16),
            apr=apr[l].reshape(SUBLANES, S5_FLAT),
            api=api[l].reshape(SUBLANES, S5_FLAT),
            s5_d=s5_d[l].reshape(1, S5_WIDTH),
            w_glu=s5_w_glu[l].astype(BF16),
            final_g=final_g.reshape(1, d),
        )
        final = l == depth - 1
        mod_p = mod_all[l, :bsz].reshape(bsz, 3, d)
        mod_s = mod_all[l, bsz:]
        xp, sret, hl, cv, sre, sim = _prompt_layer(xp, mod_p, w, tabs, final=final)
        outs_p.append((sret, hl.reshape(bsz, LRU_WIDTH), cv,
                       sre.reshape(bsz, S5_GROUPS, S5_STATE), sim.reshape(bsz, S5_GROUPS, S5_STATE)))
        states = (state_ret[l], state_lru_h[l],
                  state_lru_conv[l].reshape(dec, (CONV_WIDTH - 1) * LRU_WIDTH),
                  state_s5_re[l].reshape(dec, S5_FLAT), state_s5_im[l].reshape(dec, S5_FLAT))
        xs, sret, hl, cv, sre, sim = _decode_layer(xs, mod_s, states, w, tabs, final=final)
        outs_s.append((sret, hl, cv.reshape(dec, CONV_WIDTH - 1, LRU_WIDTH),
                       sre.reshape(dec, S5_GROUPS, S5_STATE), sim.reshape(dec, S5_GROUPS, S5_STATE)))

    def stk(outs, k):
        return jnp.stack([o[k] for o in outs])

    return (xp, xs.reshape(dec, 1, d),
            stk(outs_p, 0), stk(outs_s, 0), stk(outs_p, 1), stk(outs_s, 1),
            stk(outs_p, 2), stk(outs_s, 2), stk(outs_p, 3), stk(outs_s, 3),
            stk(outs_p, 4), stk(outs_s, 4))
```

```python
import functools
import math

import jax
import jax.numpy as jnp
import numpy as np
from jax import lax
from jax.experimental import pallas as pl
from jax.experimental.pallas import tpu as pltpu

RET_HEADS = 4
HEAD_DIM = 128
RET_WIDTH = RET_HEADS * HEAD_DIM
RET_CHUNK = 128
LRU_WIDTH = 256
LRU_BLOCKS = 4
LRU_C = 8.0
CONV_WIDTH = 4
S5_WIDTH = 256
S5_GROUP = 16
S5_GROUPS = 16
S5_STATE = 64
S5_FLAT = S5_GROUPS * S5_STATE
ROPE_BASE = 10000.0
EPS = 1e-6
PAST_LEN = 16384

OFF_Q = 0
OFF_K = OFF_Q + RET_WIDTH
OFF_V = OFF_K + RET_WIDTH
OFF_GR = OFF_V + RET_WIDTH
OFF_XL = OFF_GR + RET_WIDTH
OFF_GL = OFF_XL + LRU_WIDTH
OFF_U = OFF_GL + LRU_WIDTH
OFF_GS = OFF_U + S5_WIDTH
D_IN = OFF_GS + S5_WIDTH

SUBLANES = 8
SEQ_TILE = 512
DEC_BLOCK = 8
VMEM_LIMIT = 56 * 1024 * 1024

F32 = jnp.float32
BF16 = jnp.bfloat16


def _sigmoid(x):
    return 1.0 / (1.0 + jnp.exp(-x))


def _silu(x):
    return x * _sigmoid(x)


def _gelu_tanh(x):
    c = math.sqrt(2.0 / math.pi)
    return x * (0.5 * (1.0 + jnp.tanh(c * (x + 0.044715 * (x * x * x)))))


def _softplus(x):
    return jnp.maximum(x, 0.0) + jnp.log1p(jnp.exp(-jnp.abs(x)))


def _rmsnorm(x, g):
    ms = jnp.mean(x * x, axis=-1, keepdims=True)
    return x * lax.rsqrt(ms + EPS) * g


def _dot(a, b):
    return jnp.dot(a.astype(BF16), b.astype(BF16), preferred_element_type=F32)


def _dot_nt(a, b):
    return lax.dot_general(a.astype(BF16), b.astype(BF16), (((1,), (1,)), ((), ())),
                           preferred_element_type=F32)


def _rotary(x, cosf, sinf):
    return x * cosf + pltpu.roll(x, HEAD_DIM // 2, axis=1) * sinf


def _groupnorm(o, g):
    mu = jnp.mean(o, axis=-1, keepdims=True)
    d = o - mu
    var = jnp.mean(d * d, axis=-1, keepdims=True)
    return d * lax.rsqrt(var + EPS) * g


def _lru_coeffs(xc, gates, sp):
    r = _sigmoid(gates[:, :LRU_WIDTH])
    ig = _sigmoid(gates[:, LRU_WIDTH:])
    log_a = (-LRU_C) * r * sp
    a = jnp.exp(log_a)
    th = jnp.tanh(log_a)
    mult = jnp.sqrt(-2.0 * th / (1.0 - th))
    return a, mult * ig * xc


def _s5_tail(sr_si, u, gs, cd, s5d, wglu):
    ys = _dot(sr_si, cd) + s5d * u
    ys = _gelu_tanh(ys)
    ys = ys * _sigmoid(_dot(ys, wglu))
    return ys * _silu(gs)


def _s5_prep_kernel(are_ref, aim_ref, ldt_ref, bre_ref, bim_ref,
                    apr_ref, api_ref, bbr_ref, bbi_ref):
    depth = are_ref.shape[0]
    for l in range(depth):
        a_re = are_ref[l]
        a_im = aim_ref[l]
        step = jnp.exp(ldt_ref[l])
        mag = jnp.exp(step * a_re)
        abar_r = mag * jnp.cos(step * a_im)
        abar_i = mag * jnp.sin(step * a_im)
        nr, ni = abar_r - 1.0, abar_i
        den = a_re * a_re + a_im * a_im
        fr = (nr * a_re + ni * a_im) / den
        fi = (ni * a_re - nr * a_im) / den
        for g in range(S5_GROUPS):
            frg, fig = fr[g:g + 1, :], fi[g:g + 1, :]
            b_re, b_im = bre_ref[l, g], bim_ref[l, g]
            bbr_ref[l, g] = frg * b_re - fig * b_im
            bbi_ref[l, g] = frg * b_im + fig * b_re
        pr, pi = abar_r, abar_i
        apr_ref[l, 0] = pr
        api_ref[l, 0] = pi
        for m in range(1, SUBLANES):
            pr, pi = pr * abar_r - pi * abar_i, pr * abar_i + pi * abar_r
            apr_ref[l, m] = pr
            api_ref[l, m] = pi


def _s5_prep(s5_a_re, s5_a_im, s5_log_dt, s5_b_re, s5_b_im):
    depth = s5_a_re.shape[0]
    b_re_t = jnp.swapaxes(s5_b_re, 2, 3)
    b_im_t = jnp.swapaxes(s5_b_im, 2, 3)
    ldt = s5_log_dt.reshape(depth, S5_GROUPS, 1)
    out_shape = (
        jax.ShapeDtypeStruct((depth, SUBLANES, S5_GROUPS, S5_STATE), F32),
        jax.ShapeDtypeStruct((depth, SUBLANES, S5_GROUPS, S5_STATE), F32),
        jax.ShapeDtypeStruct((depth, S5_GROUPS, S5_GROUP, S5_STATE), F32),
        jax.ShapeDtypeStruct((depth, S5_GROUPS, S5_GROUP, S5_STATE), F32),
    )
    return pl.pallas_call(_s5_prep_kernel, out_shape=out_shape, name="s5_prep")(
        s5_a_re, s5_a_im, ldt, b_re_t, b_im_t)


def _ada_kernel(c_ref, w_ref, b_ref, o_ref):
    s = _silu(c_ref[...])
    o_ref[0] = _dot(s, w_ref[0]) + b_ref[0]


def _ada(c_all, w_ada, b_ada):
    depth, d, n3 = w_ada.shape
    rows = c_all.shape[0]
    tn = d
    return pl.pallas_call(
        _ada_kernel,
        grid=(depth, n3 // tn),
        in_specs=[
            pl.BlockSpec((rows, d), lambda l, j: (0, 0)),
            pl.BlockSpec((1, d, tn), lambda l, j: (l, 0, j)),
            pl.BlockSpec((1, 1, tn), lambda l, j: (l, 0, j)),
        ],
        out_specs=pl.BlockSpec((1, rows, tn), lambda l, j: (l, 0, j)),
        out_shape=jax.ShapeDtypeStruct((depth, rows, n3), F32),
        compiler_params=pltpu.CompilerParams(
            dimension_semantics=("arbitrary", "arbitrary"), vmem_limit_bytes=VMEM_LIMIT),
        name="adaln",
    )(c_all, w_ada, b_ada.reshape(depth, 1, n3))


def _prompt_kernel(x_ref, mod_ref, ng_ref, win_ref, wout_ref,
                   cq_ref, sq_ref, ck_ref, sk_ref, intra_ref, qdec_ref, kdec_ref, cdec_ref,
                   gng_ref, cw_ref, cb_ref, wg_ref, bg_ref, lam_ref,
                   bd_ref, cd_ref, apr_ref, api_ref, s5d_ref, wglu_ref, fg_ref,
                   y_ref, sret_ref, hl_ref, cv_ref, sre_ref, sim_ref,
                   proj_scr, ycat_scr, xpad_scr, ab_scr, st_scr, *, final):
    tl = x_ref.shape[1]
    t = pl.program_id(1)
    (ng_ref, win_ref, wout_ref, gng_ref, cw_ref, cb_ref, wg_ref, bg_ref, lam_ref,
     bd_ref, cd_ref, apr_ref, api_ref, s5d_ref, wglu_ref) = (
        r.at[0] for r in (ng_ref, win_ref, wout_ref, gng_ref, cw_ref, cb_ref, wg_ref, bg_ref,
                          lam_ref, bd_ref, cd_ref, apr_ref, api_ref, s5d_ref, wglu_ref))

    @pl.when(t == 0)
    def _():
        sret_ref[...] = jnp.zeros_like(sret_ref)
        hl_ref[...] = jnp.zeros_like(hl_ref)
        sre_ref[...] = jnp.zeros_like(sre_ref)
        sim_ref[...] = jnp.zeros_like(sim_ref)
        xpad_scr[pl.ds(0, SUBLANES), :] = jnp.zeros((SUBLANES, LRU_WIDTH), F32)

    x = x_ref[0]
    shift, scale = mod_ref[0, 0:1, :], mod_ref[0, 1:2, :]
    h = _rmsnorm(x, ng_ref[...]) * (1.0 + scale) + shift
    proj_scr[...] = _dot(h, win_ref[...])

    def ret_chunk(c, carry):
        r0 = pl.multiple_of(c * RET_CHUNK, RET_CHUNK)
        rows = pl.ds(r0, RET_CHUNK)
        cq, sq, ck, sk = cq_ref[rows, :], sq_ref[rows, :], ck_ref[rows, :], sk_ref[rows, :]
        for hd in range(RET_HEADS):
            lanes = pl.ds(hd * HEAD_DIM, HEAD_DIM)
            q = _rotary(proj_scr[rows, pl.ds(OFF_Q + hd * HEAD_DIM, HEAD_DIM)], cq, sq)
            k = _rotary(proj_scr[rows, pl.ds(OFF_K + hd * HEAD_DIM, HEAD_DIM)], ck, sk)
            v = proj_scr[rows, pl.ds(OFF_V + hd * HEAD_DIM, HEAD_DIM)].astype(BF16)
            s_old = sret_ref[0, hd]
            sc = _dot_nt(q, k) * intra_ref[hd]
            o = _dot(sc, v) + _dot(q * qdec_ref[hd], s_old)
            kd_t = (k * kdec_ref[hd]).T
            sret_ref[0, hd] = cdec_ref[hd] * s_old + _dot(kd_t, v)
            on = _groupnorm(o, gng_ref[:, lanes])
            g_ret = proj_scr[rows, pl.ds(OFF_GR + hd * HEAD_DIM, HEAD_DIM)]
            ycat_scr[rows, lanes] = (on * _silu(g_ret)).astype(BF16)
        return carry

    lax.fori_loop(0, tl // RET_CHUNK, ret_chunk, 0)

    sub = lax.broadcasted_iota(jnp.int32, (SUBLANES, 1), 0)

    xl = proj_scr[:, pl.ds(OFF_XL, LRU_WIDTH)]
    xpad_scr[pl.ds(SUBLANES, tl), :] = xl
    xc = cb_ref[...]
    for j in range(CONV_WIDTH):
        xc = xc + xpad_scr[pl.ds(SUBLANES - (CONV_WIDTH - 1) + j, tl), :] * cw_ref[pl.ds(j, 1), :]
    tail = xpad_scr[pl.ds(tl + SUBLANES - (CONV_WIDTH - 1), CONV_WIDTH - 1), :]
    cv_ref[0] = tail
    xpad_scr[pl.ds(SUBLANES - (CONV_WIDTH - 1), CONV_WIDTH - 1), :] = tail
    gates = _dot(xc, wg_ref[...]) + bg_ref[...]
    a, b = _lru_coeffs(xc, gates, _softplus(-lam_ref[...]))
    ab_scr[:, pl.ds(0, LRU_WIDTH)] = a
    ab_scr[:, pl.ds(LRU_WIDTH, LRU_WIDTH)] = b

    def lru_group(g, carry):
        rows = pl.ds(pl.multiple_of(g * SUBLANES, SUBLANES), SUBLANES)
        A = ab_scr[rows, pl.ds(0, LRU_WIDTH)]
        H = ab_scr[rows, pl.ds(LRU_WIDTH, LRU_WIDTH)]
        for kk in (1, 2, 4):
            m = sub >= kk
            a_s = jnp.where(m, pltpu.roll(A, kk, axis=0), 1.0)
            h_s = jnp.where(m, pltpu.roll(H, kk, axis=0), 0.0)
            H = H + A * h_s
            A = A * a_s
        hh = H + A * carry
        ab_scr[rows, pl.ds(LRU_WIDTH, LRU_WIDTH)] = hh
        return hh[SUBLANES - 1:SUBLANES, :]

    h_last = lax.fori_loop(0, tl // SUBLANES, lru_group, hl_ref[0])
    hl_ref[0] = h_last
    gl = proj_scr[:, pl.ds(OFF_GL, LRU_WIDTH)]
    ycat_scr[:, pl.ds(RET_WIDTH, LRU_WIDTH)] = (
        ab_scr[:, pl.ds(LRU_WIDTH, LRU_WIDTH)] * _silu(gl)).astype(BF16)

    u = proj_scr[:, pl.ds(OFF_U, S5_WIDTH)]
    st_scr[...] = _dot(u, bd_ref[...])
    re_l, im_l = pl.ds(0, S5_FLAT), pl.ds(S5_FLAT, S5_FLAT)
    qr, qi = apr_ref[...], api_ref[...]

    def s5_group(g, carry):
        cr, ci = carry
        rows = pl.ds(pl.multiple_of(g * SUBLANES, SUBLANES), SUBLANES)
        hr, hi = st_scr[rows, re_l], st_scr[rows, im_l]
        for kk in (1, 2, 4):
            m = sub >= kk
            pr = jnp.where(m, apr_ref[pl.ds(kk - 1, 1), :], 0.0)
            pi = jnp.where(m, api_ref[pl.ds(kk - 1, 1), :], 0.0)
            sr, si = pltpu.roll(hr, kk, axis=0), pltpu.roll(hi, kk, axis=0)
            hr, hi = hr + (pr * sr - pi * si), hi + (pr * si + pi * sr)
        hr = hr + (qr * cr - qi * ci)
        hi = hi + (qr * ci + qi * cr)
        st_scr[rows, re_l] = hr
        st_scr[rows, im_l] = hi
        return hr[SUBLANES - 1:SUBLANES, :], hi[SUBLANES - 1:SUBLANES, :]

    cr, ci = lax.fori_loop(0, tl // SUBLANES, s5_group, (sre_ref[0], sim_ref[0]))
    sre_ref[0] = cr
    sim_ref[0] = ci
    gs = proj_scr[:, pl.ds(OFF_GS, S5_WIDTH)]
    y_s5 = _s5_tail(st_scr[...], u, gs, cd_ref[...], s5d_ref[...], wglu_ref[...])
    ycat_scr[:, pl.ds(RET_WIDTH + LRU_WIDTH, S5_WIDTH)] = y_s5.astype(BF16)

    out = jnp.dot(ycat_scr[...], wout_ref[...], preferred_element_type=F32)
    x_new = x_ref[0] + mod_ref[0, 2:3, :] * out
    if final:
        x_new = _rmsnorm(x_new, fg_ref[...])
    y_ref[0] = x_new


def _prompt_layer(x, mod, w, tabs, layer, *, final):
    bsz, seq, d = x.shape
    tl = min(SEQ_TILE, seq)
    nt = seq // tl

    def full(a):
        return pl.BlockSpec(a.shape, lambda b, t: (0,) * a.ndim)

    def lay(a):
        return pl.BlockSpec((1,) + a.shape[1:], lambda b, t: (layer,) + (0,) * (a.ndim - 1))

    row_tab = pl.BlockSpec((tl, HEAD_DIM), lambda b, t: (t, 0))
    in_specs = [
        pl.BlockSpec((1, tl, d), lambda b, t: (b, t, 0)),
        pl.BlockSpec((1, 3, d), lambda b, t: (b, 0, 0)),
        lay(w["norm_g"]), lay(w["w_in"]), lay(w["w_out"]),
        row_tab, row_tab, row_tab, row_tab,
        full(tabs["intra"]), full(tabs["q_dec"]), full(tabs["k_dec"]),
        pl.BlockSpec(memory_space=pltpu.SMEM),
        lay(w["gn_g"]), lay(w["conv_w"]), lay(w["conv_b"]), lay(w["w_gate"]), lay(w["b_gate"]),
        lay(w["lam"]), lay(w["bd"]), lay(w["cd"]), lay(w["apr"]), lay(w["api"]),
        lay(w["s5_d"]), lay(w["w_glu"]), full(w["final_g"]),
    ]
    out_shape = (
        jax.ShapeDtypeStruct((bsz, seq, d), F32),
        jax.ShapeDtypeStruct((bsz, RET_HEADS, HEAD_DIM, HEAD_DIM), F32),
        jax.ShapeDtypeStruct((bsz, 1, LRU_WIDTH), F32),
        jax.ShapeDtypeStruct((bsz, CONV_WIDTH - 1, LRU_WIDTH), F32),
        jax.ShapeDtypeStruct((bsz, 1, S5_FLAT), F32),
        jax.ShapeDtypeStruct((bsz, 1, S5_FLAT), F32),
    )
    out_specs = (
        pl.BlockSpec((1, tl, d), lambda b, t: (b, t, 0)),
        pl.BlockSpec((1, RET_HEADS, HEAD_DIM, HEAD_DIM), lambda b, t: (b, 0, 0, 0)),
        pl.BlockSpec((1, 1, LRU_WIDTH), lambda b, t: (b, 0, 0)),
        pl.BlockSpec((1, CONV_WIDTH - 1, LRU_WIDTH), lambda b, t: (b, 0, 0)),
        pl.BlockSpec((1, 1, S5_FLAT), lambda b, t: (b, 0, 0)),
        pl.BlockSpec((1, 1, S5_FLAT), lambda b, t: (b, 0, 0)),
    )
    scratch = [
        pltpu.VMEM((tl, D_IN), F32),
        pltpu.VMEM((tl, d), BF16),
        pltpu.VMEM((tl + SUBLANES, LRU_WIDTH), F32),
        pltpu.VMEM((tl, 2 * LRU_WIDTH), F32),
        pltpu.VMEM((tl, 2 * S5_FLAT), F32),
    ]
    return pl.pallas_call(
        functools.partial(_prompt_kernel, final=final),
        grid=(bsz, nt),
        in_specs=in_specs,
        out_specs=out_specs,
        out_shape=out_shape,
        scratch_shapes=scratch,
        compiler_params=pltpu.CompilerParams(
            dimension_semantics=("arbitrary", "arbitrary"), vmem_limit_bytes=VMEM_LIMIT),
        name="prompt_layer",
    )(x, mod, w["norm_g"], w["w_in"], w["w_out"],
      tabs["cos_q"], tabs["sin_q"], tabs["cos_k"], tabs["sin_k"],
      tabs["intra"], tabs["q_dec"], tabs["k_dec"], tabs["c_dec"],
      w["gn_g"], w["conv_w"], w["conv_b"], w["w_gate"], w["b_gate"], w["lam"],
      w["bd"], w["cd"], w["apr"], w["api"], w["s5_d"], w["w_glu"], w["final_g"])


def _decode_kernel(x_ref, mod_ref, ng_ref, win_ref, wout_ref,
                   cq_ref, sq_ref, ck_ref, sk_ref, gam_ref,
                   gng_ref, cw_ref, cb_ref, wg_ref, bg_ref, lam_ref,
                   bd_ref, cd_ref, apr_ref, api_ref, s5d_ref, wglu_ref, fg_ref,
                   sret_in, hl_in, cv_in, sre_in, sim_in,
                   y_ref, sret_ref, hl_ref, cv_ref, sre_ref, sim_ref,
                   qt_scr, kt_scr, v_scr, o_scr, gr_scr, ycat_scr, x_scr):
    layer = pl.program_id(0)
    i = pl.program_id(1)
    nb = pl.num_programs(1)
    rows_n, d = x_ref.shape
    (mod_ref, ng_ref, win_ref, wout_ref, gng_ref, cw_ref, cb_ref, wg_ref, bg_ref, lam_ref,
     bd_ref, cd_ref, apr_ref, api_ref, s5d_ref, wglu_ref,
     sret_in, hl_in, cv_in, sre_in, sim_in, sret_ref, hl_ref, cv_ref, sre_ref, sim_ref) = (
        r.at[0] for r in (mod_ref, ng_ref, win_ref, wout_ref, gng_ref, cw_ref, cb_ref, wg_ref,
                          bg_ref, lam_ref, bd_ref, cd_ref, apr_ref, api_ref, s5d_ref, wglu_ref,
                          sret_in, hl_in, cv_in, sre_in, sim_in,
                          sret_ref, hl_ref, cv_ref, sre_ref, sim_ref))

    @pl.when((i == 0) & (layer == 0))
    def _():
        x_scr[...] = x_ref[...]

    @pl.when(i == 0)
    def _():
        x = x_scr[...]
        shift, scale = mod_ref[:, pl.ds(0, d)], mod_ref[:, pl.ds(d, d)]
        h = _rmsnorm(x, ng_ref[...]) * (1.0 + scale) + shift
        proj = _dot(h, win_ref[...])
        cq, sq, ck, sk = cq_ref[...], sq_ref[...], ck_ref[...], sk_ref[...]
        for hd in range(RET_HEADS):
            lo = hd * HEAD_DIM
            qt_scr[hd] = _rotary(proj[:, OFF_Q + lo:OFF_Q + lo + HEAD_DIM], cq, sq).T
            kt_scr[hd] = _rotary(proj[:, OFF_K + lo:OFF_K + lo + HEAD_DIM], ck, sk).T
        v_scr[...] = proj[:, OFF_V:OFF_V + RET_WIDTH]
        gr_scr[...] = _silu(proj[:, OFF_GR:OFF_GR + RET_WIDTH])
        xl = proj[:, OFF_XL:OFF_XL + LRU_WIDTH]
        cs = cv_in[...]
        xc = cb_ref[...] + xl * cw_ref[pl.ds(CONV_WIDTH - 1, 1), :]
        for j in range(CONV_WIDTH - 1):
            xc = xc + cs[:, j * LRU_WIDTH:(j + 1) * LRU_WIDTH] * cw_ref[pl.ds(j, 1), :]
        cv_ref[:, pl.ds(0, 2 * LRU_WIDTH)] = cs[:, LRU_WIDTH:]
        cv_ref[:, pl.ds(2 * LRU_WIDTH, LRU_WIDTH)] = xl
        gates = _dot(xc, wg_ref[...]) + bg_ref[...]
        a, b = _lru_coeffs(xc, gates, _softplus(-lam_ref[...]))
        hh = b + a * hl_in[...]
        hl_ref[...] = hh
        ycat_scr[:, pl.ds(RET_WIDTH, LRU_WIDTH)] = (
            hh * _silu(proj[:, OFF_GL:OFF_GL + LRU_WIDTH])).astype(BF16)
        u = proj[:, OFF_U:OFF_U + S5_WIDTH]
        bu = _dot(u, bd_ref[...])
        ar, ai = apr_ref[pl.ds(0, 1), :], api_ref[pl.ds(0, 1), :]
        s0r, s0i = sre_in[...], sim_in[...]
        sr = bu[:, :S5_FLAT] + (ar * s0r - ai * s0i)
        si = bu[:, S5_FLAT:] + (ar * s0i + ai * s0r)
        sre_ref[...] = sr
        sim_ref[...] = si
        y_s5 = _s5_tail(jnp.concatenate([sr, si], axis=-1), u, proj[:, OFF_GS:OFF_GS + S5_WIDTH],
                        cd_ref[...], s5d_ref[...], wglu_ref[...])
        ycat_scr[:, pl.ds(RET_WIDTH + LRU_WIDTH, S5_WIDTH)] = y_s5.astype(BF16)

    lane = lax.broadcasted_iota(jnp.int32, (HEAD_DIM, rows_n), 1)
    blk_rows = pl.ds(pl.multiple_of(i * DEC_BLOCK, DEC_BLOCK), DEC_BLOCK)
    for hd in range(RET_HEADS):
        lanes = pl.ds(hd * HEAD_DIM, HEAD_DIM)
        vblk = v_scr[blk_rows, lanes]
        o_rows = []
        for j in range(DEC_BLOCK):
            sel = lane == i * DEC_BLOCK + j
            kcol = jnp.sum(jnp.where(sel, kt_scr[hd], 0.0), axis=1, keepdims=True)
            qcol = jnp.sum(jnp.where(sel, qt_scr[hd], 0.0), axis=1, keepdims=True)
            s_new = gam_ref[hd] * sret_in[j, hd] + kcol * vblk[j:j + 1, :]
            sret_ref[j, hd] = s_new
            o_rows.append(jnp.sum(qcol * s_new, axis=0, keepdims=True))
        o_scr[blk_rows, lanes] = jnp.concatenate(o_rows, axis=0)

    @pl.when(i == nb - 1)
    def _():
        for hd in range(RET_HEADS):
            lanes = pl.ds(hd * HEAD_DIM, HEAD_DIM)
            on = _groupnorm(o_scr[:, lanes], gng_ref[:, lanes])
            ycat_scr[:, lanes] = (on * gr_scr[:, lanes]).astype(BF16)
        out = jnp.dot(ycat_scr[...], wout_ref[...], preferred_element_type=F32)
        x_new = x_scr[...] + mod_ref[:, pl.ds(2 * d, d)] * out
        x_scr[...] = x_new
        y_ref[...] = _rmsnorm(x_new, fg_ref[...])


def _decode_layers(x, mod, states, w, tabs):
    rows, d = x.shape
    depth = mod.shape[0]
    s_ret, s_h, s_conv, s_re, s_im = states
    nb = rows // DEC_BLOCK

    def full(a):
        return pl.BlockSpec(a.shape, lambda l, i: (0,) * a.ndim)

    def lay(a):
        return pl.BlockSpec((1,) + a.shape[1:], lambda l, i: (l,) + (0,) * (a.ndim - 1))

    smem = pl.BlockSpec(memory_space=pltpu.SMEM)
    sblk = pl.BlockSpec((1, DEC_BLOCK, RET_HEADS, HEAD_DIM, HEAD_DIM), lambda l, i: (l, i, 0, 0, 0))
    in_specs = [
        full(x), lay(mod), lay(w["norm_g"]), lay(w["w_in"]), lay(w["w_out"]),
        full(tabs["dcos_q"]), full(tabs["dsin_q"]), full(tabs["dcos_k"]), full(tabs["dsin_k"]), smem,
        lay(w["gn_g"]), lay(w["conv_w"]), lay(w["conv_b"]), lay(w["w_gate"]), lay(w["b_gate"]),
        lay(w["lam"]), lay(w["bd"]), lay(w["cd"]), lay(w["apr"]), lay(w["api"]),
        lay(w["s5_d"]), lay(w["w_glu"]), full(w["final_g"]),
        sblk, lay(s_h), lay(s_conv), lay(s_re), lay(s_im),
    ]
    out_shape = (
        jax.ShapeDtypeStruct((rows, d), F32),
        jax.ShapeDtypeStruct(s_ret.shape, F32),
        jax.ShapeDtypeStruct(s_h.shape, F32),
        jax.ShapeDtypeStruct(s_conv.shape, F32),
        jax.ShapeDtypeStruct(s_re.shape, F32),
        jax.ShapeDtypeStruct(s_im.shape, F32),
    )
    out_specs = (full(x), sblk, lay(s_h), lay(s_conv), lay(s_re), lay(s_im))
    scratch = [
        pltpu.VMEM((RET_HEADS, HEAD_DIM, rows), F32),
        pltpu.VMEM((RET_HEADS, HEAD_DIM, rows), F32),
        pltpu.VMEM((rows, RET_WIDTH), F32),
        pltpu.VMEM((rows, RET_WIDTH), F32),
        pltpu.VMEM((rows, RET_WIDTH), F32),
        pltpu.VMEM((rows, d), BF16),
        pltpu.VMEM((rows, d), F32),
    ]
    return pl.pallas_call(
        _decode_kernel,
        grid=(depth, nb),
        in_specs=in_specs,
        out_specs=out_specs,
        out_shape=out_shape,
        scratch_shapes=scratch,
        compiler_params=pltpu.CompilerParams(
            dimension_semantics=("arbitrary", "arbitrary"), vmem_limit_bytes=VMEM_LIMIT),
        name="decode_layers",
    )(x, mod, w["norm_g"], w["w_in"], w["w_out"],
      tabs["dcos_q"], tabs["dsin_q"], tabs["dcos_k"], tabs["dsin_k"], tabs["gamma"],
      w["gn_g"], w["conv_w"], w["conv_b"], w["w_gate"], w["b_gate"], w["lam"],
      w["bd"], w["cd"], w["apr"], w["api"], w["s5_d"], w["w_glu"], w["final_g"],
      s_ret, s_h, s_conv, s_re, s_im)


def _rope_tables(pos, scale):
    half = HEAD_DIM // 2
    inv = ROPE_BASE ** (-jnp.arange(half, dtype=F32) / half)
    ang = pos[:, None] * inv[None, :]
    cos, sin = jnp.cos(ang), jnp.sin(ang)
    cosf = jnp.concatenate([cos, cos], axis=-1) * scale
    sinf = jnp.concatenate([-sin, sin], axis=-1) * scale
    return cosf, sinf


def _tables(seq):
    c = RET_CHUNK
    log_g = jnp.log1p(-jnp.exp2(-5.0 - jnp.arange(RET_HEADS, dtype=F32)))
    idx = jnp.arange(c, dtype=F32)
    rel = idx[:, None] - idx[None, :]
    intra = jnp.where(rel[None] >= 0,
                      jnp.exp(jnp.maximum(rel, 0.0)[None] * log_g[:, None, None]), 0.0)
    q_dec = jnp.exp((idx + 1.0)[:, None] * log_g[None, :])
    k_dec = jnp.exp((c - 1.0 - idx)[:, None] * log_g[None, :])
    q_dec = jnp.broadcast_to(q_dec.T[:, :, None], (RET_HEADS, c, HEAD_DIM))
    k_dec = jnp.broadcast_to(k_dec.T[:, :, None], (RET_HEADS, c, HEAD_DIM))
    c_dec = jnp.exp(c * log_g)
    gamma = jnp.exp(log_g)
    k_scale = HEAD_DIM ** -0.5
    cos_q, sin_q = _rope_tables(jnp.arange(seq, dtype=F32), 1.0)
    cos_k, sin_k = _rope_tables(jnp.arange(seq, dtype=F32), k_scale)
    dpos = PAST_LEN + jnp.arange(1, dtype=F32)
    dcos_q, dsin_q = _rope_tables(dpos, 1.0)
    dcos_k, dsin_k = _rope_tables(dpos, k_scale)
    return dict(intra=intra, q_dec=q_dec, k_dec=k_dec, c_dec=c_dec, gamma=gamma,
                cos_q=cos_q, sin_q=sin_q, cos_k=cos_k, sin_k=sin_k,
                dcos_q=dcos_q, dsin_q=dsin_q, dcos_k=dcos_k, dsin_k=dsin_k)


def _block_diag(blocks):
    n, r, c = blocks.shape
    eye = jnp.eye(n, dtype=blocks.dtype)
    return jnp.einsum("nrc,nm->nrmc", blocks, eye).reshape(n * r, n * c)


def kernel(x_prompt, x_sample, state_ret, state_lru_h, state_lru_conv, state_s5_re, state_s5_im,
           c_prompt, c_sample, norm_g, w_ada, b_ada, w_in, ret_gn_g, conv_w, conv_b, w_rg, b_rg,
           w_ig, b_ig, lru_lambda, s5_a_re, s5_a_im, s5_b_re, s5_b_im, s5_c_re, s5_c_im, s5_d,
           s5_log_dt, s5_w_glu, w_out, final_g):
    depth = w_in.shape[0]
    bsz, seq, d = x_prompt.shape
    dec = x_sample.shape[0]
    assert x_sample.shape[1] == 1 and seq % RET_CHUNK == 0 and dec % DEC_BLOCK == 0

    tabs = _tables(seq)
    mod_all = _ada(jnp.concatenate([c_prompt, c_sample], axis=0), w_ada, b_ada)
    apr, api, bbr, bbi = _s5_prep(s5_a_re, s5_a_im, s5_log_dt, s5_b_re, s5_b_im)

    bdiag = jax.vmap(_block_diag)
    bd = jnp.concatenate([bdiag(bbr), bdiag(bbi)], axis=2)
    cd = jnp.concatenate([bdiag(jnp.swapaxes(s5_c_re, 2, 3)),
                          bdiag(jnp.swapaxes(-s5_c_im, 2, 3))], axis=1)
    w = dict(
        norm_g=norm_g.reshape(depth, 1, d),
        w_in=w_in.astype(BF16),
        w_out=w_out.astype(BF16),
        gn_g=ret_gn_g.reshape(depth, 1, RET_WIDTH),
        conv_w=conv_w,
        conv_b=conv_b.reshape(depth, 1, LRU_WIDTH),
        w_gate=jnp.concatenate([bdiag(w_rg), bdiag(w_ig)], axis=2).astype(BF16),
        b_gate=jnp.concatenate([b_rg, b_ig], axis=1).reshape(depth, 1, 2 * LRU_WIDTH),
        lam=lru_lambda.reshape(depth, 1, LRU_WIDTH),
        bd=bd.astype(BF16),
        cd=cd.astype(BF16),
        apr=apr.reshape(depth, SUBLANES, S5_FLAT),
        api=api.reshape(depth, SUBLANES, S5_FLAT),
        s5_d=s5_d.reshape(depth, 1, S5_WIDTH),
        w_glu=s5_w_glu.astype(BF16),
        final_g=final_g.reshape(1, d),
    )

    xp = x_prompt
    outs_p = []
    for l in range(depth):
        mod_p = mod_all[l, :bsz].reshape(bsz, 3, d)
        xp, sret, hl, cv, sre, sim = _prompt_layer(xp, mod_p, w, tabs, l, final=l == depth - 1)
        outs_p.append((sret, hl.reshape(bsz, LRU_WIDTH), cv,
                       sre.reshape(bsz, S5_GROUPS, S5_STATE), sim.reshape(bsz, S5_GROUPS, S5_STATE)))

    states = (state_ret, state_lru_h,
              state_lru_conv.reshape(depth, dec, (CONV_WIDTH - 1) * LRU_WIDTH),
              state_s5_re.reshape(depth, dec, S5_FLAT), state_s5_im.reshape(depth, dec, S5_FLAT))
    xs, sret_s, hl_s, cv_s, sre_s, sim_s = _decode_layers(
        x_sample.reshape(dec, d), mod_all[:, bsz:], states, w, tabs)

    def stk(k):
        return jnp.stack([o[k] for o in outs_p])

    return (xp, xs.reshape(dec, 1, d),
            stk(0), sret_s, stk(1), hl_s,
            stk(2), cv_s.reshape(depth, dec, CONV_WIDTH - 1, LRU_WIDTH),
            stk(3), sre_s.reshape(depth, dec, S5_GROUPS, S5_STATE),
            stk(4), sim_s.reshape(depth, dec, S5_GROUPS, S5_STATE))
```

```python
import functools
import math

import jax
import jax.numpy as jnp
import numpy as np
from jax import lax
from jax.experimental import pallas as pl
from jax.experimental.pallas import tpu as pltpu

RET_HEADS = 4
HEAD_DIM = 128
RET_WIDTH = RET_HEADS * HEAD_DIM
RET_CHUNK = 128
LRU_WIDTH = 256
LRU_BLOCKS = 4
LRU_C = 8.0
CONV_WIDTH = 4
S5_WIDTH = 256
S5_GROUP = 16
S5_GROUPS = 16
S5_STATE = 64
S5_FLAT = S5_GROUPS * S5_STATE
ROPE_BASE = 10000.0
EPS = 1e-6
PAST_LEN = 16384

OFF_Q = 0
OFF_K = OFF_Q + RET_WIDTH
OFF_V = OFF_K + RET_WIDTH
OFF_GR = OFF_V + RET_WIDTH
OFF_XL = OFF_GR + RET_WIDTH
OFF_GL = OFF_XL + LRU_WIDTH
OFF_U = OFF_GL + LRU_WIDTH
OFF_GS = OFF_U + S5_WIDTH
D_IN = OFF_GS + S5_WIDTH

SUBLANES = 8
LANES = 128
SEQ_TILE = 512
DEC_BLOCK = 8
VMEM_LIMIT = 56 * 1024 * 1024

F32 = jnp.float32
BF16 = jnp.bfloat16


def _sigmoid(x):
    return 1.0 / (1.0 + jnp.exp(-x))


def _silu(x):
    return x * _sigmoid(x)


def _gelu_tanh(x):
    c = math.sqrt(2.0 / math.pi)
    return x * (0.5 * (1.0 + jnp.tanh(c * (x + 0.044715 * (x * x * x)))))


def _softplus(x):
    return jnp.maximum(x, 0.0) + jnp.log1p(jnp.exp(-jnp.abs(x)))


def _rmsnorm(x, g):
    ms = jnp.mean(x * x, axis=-1, keepdims=True)
    return x * lax.rsqrt(ms + EPS) * g


def _dot(a, b):
    return jnp.dot(a.astype(BF16), b.astype(BF16), preferred_element_type=F32)


def _dot_nt(a, b):
    return lax.dot_general(a.astype(BF16), b.astype(BF16), (((1,), (1,)), ((), ())),
                           preferred_element_type=F32)


def _rotary(x, cosf, sinf):
    return x * cosf + pltpu.roll(x, HEAD_DIM // 2, axis=1) * sinf


def _groupnorm(o, g):
    mu = jnp.mean(o, axis=-1, keepdims=True)
    d = o - mu
    var = jnp.mean(d * d, axis=-1, keepdims=True)
    return d * lax.rsqrt(var + EPS) * g


def _lru_coeffs(xc, gates, sp):
    r = _sigmoid(gates[:, :LRU_WIDTH])
    ig = _sigmoid(gates[:, LRU_WIDTH:])
    log_a = (-LRU_C) * r * sp
    a = jnp.exp(log_a)
    th = jnp.tanh(log_a)
    mult = jnp.sqrt(-2.0 * th / (1.0 - th))
    return a, mult * ig * xc


def _s5_tail(sr_si, u, gs, cd, s5d, wglu):
    ys = _dot(sr_si, cd) + s5d * u
    ys = _gelu_tanh(ys)
    ys = ys * _sigmoid(_dot(ys, wglu))
    return ys * _silu(gs)


def _s5_prep_kernel(are_ref, aim_ref, ldt_ref, bre_ref, bim_ref,
                    apr_ref, api_ref, bbr_ref, bbi_ref):
    depth = are_ref.shape[0]
    for l in range(depth):
        a_re = are_ref[l]
        a_im = aim_ref[l]
        step = jnp.exp(ldt_ref[l])
        mag = jnp.exp(step * a_re)
        abar_r = mag * jnp.cos(step * a_im)
        abar_i = mag * jnp.sin(step * a_im)
        nr, ni = abar_r - 1.0, abar_i
        den = a_re * a_re + a_im * a_im
        fr = (nr * a_re + ni * a_im) / den
        fi = (ni * a_re - nr * a_im) / den
        for g in range(S5_GROUPS):
            frg, fig = fr[g:g + 1, :], fi[g:g + 1, :]
            b_re, b_im = bre_ref[l, g], bim_ref[l, g]
            bbr_ref[l, g] = frg * b_re - fig * b_im
            bbi_ref[l, g] = frg * b_im + fig * b_re
        pr, pi = abar_r, abar_i
        apr_ref[l, 0] = pr
        api_ref[l, 0] = pi
        for m in range(1, apr_ref.shape[1]):
            pr, pi = pr * abar_r - pi * abar_i, pr * abar_i + pi * abar_r
            apr_ref[l, m] = pr
            api_ref[l, m] = pi


def _s5_prep(s5_a_re, s5_a_im, s5_log_dt, s5_b_re, s5_b_im, n_pow):
    depth = s5_a_re.shape[0]
    b_re_t = jnp.swapaxes(s5_b_re, 2, 3)
    b_im_t = jnp.swapaxes(s5_b_im, 2, 3)
    ldt = s5_log_dt.reshape(depth, S5_GROUPS, 1)
    out_shape = (
        jax.ShapeDtypeStruct((depth, n_pow, S5_GROUPS, S5_STATE), F32),
        jax.ShapeDtypeStruct((depth, n_pow, S5_GROUPS, S5_STATE), F32),
        jax.ShapeDtypeStruct((depth, S5_GROUPS, S5_GROUP, S5_STATE), F32),
        jax.ShapeDtypeStruct((depth, S5_GROUPS, S5_GROUP, S5_STATE), F32),
    )
    return pl.pallas_call(_s5_prep_kernel, out_shape=out_shape, name="s5_prep")(
        s5_a_re, s5_a_im, ldt, b_re_t, b_im_t)


def _ada_kernel(c_ref, w_ref, b_ref, o_ref):
    s = _silu(c_ref[...])
    o_ref[0] = _dot(s, w_ref[0]) + b_ref[0]


def _ada(c_all, w_ada, b_ada):
    depth, d, n3 = w_ada.shape
    rows = c_all.shape[0]
    tn = d
    return pl.pallas_call(
        _ada_kernel,
        grid=(depth, n3 // tn),
        in_specs=[
            pl.BlockSpec((rows, d), lambda l, j: (0, 0)),
            pl.BlockSpec((1, d, tn), lambda l, j: (l, 0, j)),
            pl.BlockSpec((1, 1, tn), lambda l, j: (l, 0, j)),
        ],
        out_specs=pl.BlockSpec((1, rows, tn), lambda l, j: (l, 0, j)),
        out_shape=jax.ShapeDtypeStruct((depth, rows, n3), F32),
        compiler_params=pltpu.CompilerParams(
            dimension_semantics=("arbitrary", "arbitrary"), vmem_limit_bytes=VMEM_LIMIT),
        name="adaln",
    )(c_all, w_ada, b_ada.reshape(depth, 1, n3))


def _prompt_kernel(x_ref, mod_ref, ng_ref, win_ref, wout_ref,
                   cq_ref, sq_ref, ck_ref, sk_ref, intra_ref, qdec_ref, kdec_ref, cdec_ref,
                   gng_ref, cw_ref, cb_ref, wg_ref, bg_ref, lam_ref,
                   bd_ref, cd_ref, apr_ref, api_ref, s5d_ref, wglu_ref, fg_ref,
                   y_ref, sret_ref, hl_ref, cv_ref, sre_ref, sim_ref,
                   proj_scr, ycat_scr, perm_scr, xpad_scr, ab_scr, st_scr, st16_scr, yp_scr, *, final):
    tl = x_ref.shape[1]
    seg = tl // SUBLANES
    t = pl.program_id(1)
    (ng_ref, win_ref, wout_ref, gng_ref, cw_ref, cb_ref, wg_ref, bg_ref, lam_ref,
     bd_ref, cd_ref, apr_ref, api_ref, s5d_ref, wglu_ref) = (
        r.at[0] for r in (ng_ref, win_ref, wout_ref, gng_ref, cw_ref, cb_ref, wg_ref, bg_ref,
                          lam_ref, bd_ref, cd_ref, apr_ref, api_ref, s5d_ref, wglu_ref))

    @pl.when(t == 0)
    def _():
        sret_ref[...] = jnp.zeros_like(sret_ref)
        hl_ref[...] = jnp.zeros_like(hl_ref)
        sre_ref[...] = jnp.zeros_like(sre_ref)
        sim_ref[...] = jnp.zeros_like(sim_ref)
        cv_ref[...] = jnp.zeros_like(cv_ref)

    x = x_ref[0]
    shift, scale = mod_ref[0, 0:1, :], mod_ref[0, 1:2, :]
    h = _rmsnorm(x, ng_ref[...]) * (1.0 + scale) + shift
    hb = h.astype(BF16)
    proj_scr[...] = jnp.dot(hb, win_ref[:, pl.ds(0, OFF_XL)], preferred_element_type=F32)
    proj_b = jnp.dot(hb, win_ref[:, pl.ds(OFF_XL, D_IN - OFF_XL)], preferred_element_type=F32)
    for m in range(tl // SUBLANES):
        s, r0 = divmod(m * SUBLANES, seg)
        for kb in range((D_IN - OFF_XL) // LANES):
            perm_scr[kb, pl.ds(r0 * SUBLANES + s, SUBLANES, stride=SUBLANES), :] = (
                proj_b[m * SUBLANES:(m + 1) * SUBLANES, kb * LANES:(kb + 1) * LANES])

    def ret_chunk(c, carry):
        r0 = pl.multiple_of(c * RET_CHUNK, RET_CHUNK)
        rows = pl.ds(r0, RET_CHUNK)
        cq, sq, ck, sk = cq_ref[rows, :], sq_ref[rows, :], ck_ref[rows, :], sk_ref[rows, :]
        for hd in range(RET_HEADS):
            lanes = pl.ds(hd * HEAD_DIM, HEAD_DIM)
            q = _rotary(proj_scr[rows, pl.ds(OFF_Q + hd * HEAD_DIM, HEAD_DIM)], cq, sq)
            k = _rotary(proj_scr[rows, pl.ds(OFF_K + hd * HEAD_DIM, HEAD_DIM)], ck, sk)
            v = proj_scr[rows, pl.ds(OFF_V + hd * HEAD_DIM, HEAD_DIM)].astype(BF16)
            s_old = sret_ref[0, hd]
            sc = _dot_nt(q, k) * intra_ref[hd]
            o = _dot(sc, v) + _dot(q * qdec_ref[hd], s_old)
            kd_t = (k * kdec_ref[hd]).T
            sret_ref[0, hd] = cdec_ref[hd] * s_old + _dot(kd_t, v)
            on = _groupnorm(o, gng_ref[:, lanes])
            g_ret = proj_scr[rows, pl.ds(OFF_GR + hd * HEAD_DIM, HEAD_DIM)]
            ycat_scr[rows, lanes] = (on * _silu(g_ret)).astype(BF16)
        return carry

    lax.fori_loop(0, tl // RET_CHUNK, ret_chunk, 0)

    sub = lax.broadcasted_iota(jnp.int32, (SUBLANES, 1), 0)
    rg = lambda r: pl.ds(r * SUBLANES, SUBLANES)

    def pcols(lo, width):
        return jnp.concatenate([perm_scr[kb] for kb in range(lo // LANES, (lo + width) // LANES)],
                               axis=1)

    xl = pcols(OFF_XL - OFF_XL, LRU_WIDTH)
    prev = cv_ref[0]
    npre = CONV_WIDTH - 1
    for m in range(1, CONV_WIDTH):
        grp = xl[(seg - m) * SUBLANES:(seg - m + 1) * SUBLANES, :]
        e_m = jnp.where(sub == 0, prev[npre - m:npre - m + 1, :], pltpu.roll(grp, 1, axis=0))
        xpad_scr[rg(npre - m), :] = e_m
    xpad_scr[pl.ds(npre * SUBLANES, tl), :] = xl
    cv_ref[0] = jnp.concatenate(
        [xl[(seg - m) * SUBLANES + SUBLANES - 1:(seg - m + 1) * SUBLANES, :]
         for m in range(npre, 0, -1)], axis=0)
    xc = cb_ref[...]
    for j in range(CONV_WIDTH):
        xc = xc + xpad_scr[pl.ds(j * SUBLANES, tl), :] * cw_ref[pl.ds(j, 1), :]
    gates = _dot(xc, wg_ref[...]) + bg_ref[...]
    a, b = _lru_coeffs(xc, gates, _softplus(-lam_ref[...]))
    a_l, b_l = pl.ds(0, LRU_WIDTH), pl.ds(LRU_WIDTH, LRU_WIDTH)
    ab_scr[:, a_l] = a
    ab_scr[:, b_l] = b
    hloc = jnp.zeros((SUBLANES, LRU_WIDTH), F32)
    prod = jnp.ones((SUBLANES, LRU_WIDTH), F32)
    for r in range(seg):
        ar = ab_scr[rg(r), a_l]
        hloc = ar * hloc + ab_scr[rg(r), b_l]
        prod = prod * ar
        ab_scr[rg(r), a_l] = prod
        ab_scr[rg(r), b_l] = hloc
    c = hl_ref[0]
    cs = [c]
    for s in range(SUBLANES - 1):
        c = hloc[s:s + 1, :] + prod[s:s + 1, :] * c
        cs.append(c)
    hl_ref[0] = hloc[SUBLANES - 1:, :] + prod[SUBLANES - 1:, :] * c
    c_all = jnp.concatenate(cs, axis=0)
    sgl = _silu(pcols(OFF_GL - OFF_XL, LRU_WIDTH))
    for r in range(seg):
        y = (ab_scr[rg(r), b_l] + ab_scr[rg(r), a_l] * c_all) * sgl[r * SUBLANES:(r + 1) * SUBLANES, :]
        for kb in range(LRU_WIDTH // LANES):
            yp_scr[kb, rg(r), :] = y[:, kb * LANES:(kb + 1) * LANES]

    u = pcols(OFF_U - OFF_XL, S5_WIDTH)
    st_scr[...] = _dot(u, bd_ref[...])
    re_l, im_l = pl.ds(0, S5_FLAT), pl.ds(S5_FLAT, S5_FLAT)
    bshape = (SUBLANES, S5_FLAT)
    a1r = jnp.broadcast_to(apr_ref[pl.ds(0, 1), :], bshape)
    a1i = jnp.broadcast_to(api_ref[pl.ds(0, 1), :], bshape)
    hr = jnp.zeros((SUBLANES, S5_FLAT), F32)
    hi = jnp.zeros((SUBLANES, S5_FLAT), F32)
    for r in range(seg):
        hr, hi = (a1r * hr - a1i * hi + st_scr[rg(r), re_l],
                  a1r * hi + a1i * hr + st_scr[rg(r), im_l])
        st_scr[rg(r), re_l] = hr
        st_scr[rg(r), im_l] = hi
    asr, asi = apr_ref[pl.ds(seg - 1, 1), :], api_ref[pl.ds(seg - 1, 1), :]
    cr, ci = sre_ref[0], sim_ref[0]
    crs, cis = [cr], [ci]
    for s in range(SUBLANES - 1):
        cr, ci = (hr[s:s + 1, :] + (asr * cr - asi * ci), hi[s:s + 1, :] + (asr * ci + asi * cr))
        crs.append(cr)
        cis.append(ci)
    sre_ref[0] = hr[SUBLANES - 1:, :] + (asr * cr - asi * ci)
    sim_ref[0] = hi[SUBLANES - 1:, :] + (asr * ci + asi * cr)
    cr_all, ci_all = jnp.concatenate(crs, axis=0), jnp.concatenate(cis, axis=0)
    for r2 in range(seg // 2):
        parts_r, parts_i = [], []
        for r in (2 * r2, 2 * r2 + 1):
            qr = jnp.broadcast_to(apr_ref[pl.ds(r, 1), :], bshape)
            qi = jnp.broadcast_to(api_ref[pl.ds(r, 1), :], bshape)
            parts_r.append(st_scr[rg(r), re_l] + (qr * cr_all - qi * ci_all))
            parts_i.append(st_scr[rg(r), im_l] + (qr * ci_all + qi * cr_all))
        rows2 = pl.ds(r2 * 2 * SUBLANES, 2 * SUBLANES)
        st16_scr[rows2, re_l] = jnp.concatenate(parts_r, axis=0).astype(BF16)
        st16_scr[rows2, im_l] = jnp.concatenate(parts_i, axis=0).astype(BF16)
    gs = pcols(OFF_GS - OFF_XL, S5_WIDTH)
    y_s5 = _s5_tail(st16_scr[...], u, gs, cd_ref[...], s5d_ref[...], wglu_ref[...])
    for kb in range(S5_WIDTH // LANES):
        yp_scr[LRU_WIDTH // LANES + kb] = y_s5[:, kb * LANES:(kb + 1) * LANES]

    for m2 in range(tl // (2 * SUBLANES)):
        for kb in range((LRU_WIDTH + S5_WIDTH) // LANES):
            halves = []
            for m in (2 * m2, 2 * m2 + 1):
                s, r0 = divmod(m * SUBLANES, seg)
                halves.append(yp_scr[kb, pl.ds(r0 * SUBLANES + s, SUBLANES, stride=SUBLANES), :])
            ycat_scr[pl.ds(m2 * 2 * SUBLANES, 2 * SUBLANES), pl.ds(RET_WIDTH + kb * LANES, LANES)] = (
                jnp.concatenate(halves, axis=0).astype(BF16))

    out = jnp.dot(ycat_scr[...], wout_ref[...], preferred_element_type=F32)
    x_new = x_ref[0] + mod_ref[0, 2:3, :] * out
    if final:
        x_new = _rmsnorm(x_new, fg_ref[...])
    y_ref[0] = x_new


def _prompt_layer(x, mod, w, tabs, layer, *, final):
    bsz, seq, d = x.shape
    tl = min(SEQ_TILE, seq)
    nt = seq // tl

    def full(a):
        return pl.BlockSpec(a.shape, lambda b, t: (0,) * a.ndim)

    def lay(a):
        return pl.BlockSpec((1,) + a.shape[1:], lambda b, t: (layer,) + (0,) * (a.ndim - 1))

    row_tab = pl.BlockSpec((tl, HEAD_DIM), lambda b, t: (t, 0))
    in_specs = [
        pl.BlockSpec((1, tl, d), lambda b, t: (b, t, 0)),
        pl.BlockSpec((1, 3, d), lambda b, t: (b, 0, 0)),
        lay(w["norm_g"]), lay(w["w_in"]), lay(w["w_out"]),
        row_tab, row_tab, row_tab, row_tab,
        full(tabs["intra"]), full(tabs["q_dec"]), full(tabs["k_dec"]),
        pl.BlockSpec(memory_space=pltpu.SMEM),
        lay(w["gn_g"]), lay(w["conv_w"]), lay(w["conv_b"]), lay(w["w_gate"]), lay(w["b_gate"]),
        lay(w["lam"]), lay(w["bd"]), lay(w["cd"]), lay(w["apr"]), lay(w["api"]),
        lay(w["s5_d"]), lay(w["w_glu"]), full(w["final_g"]),
    ]
    out_shape = (
        jax.ShapeDtypeStruct((bsz, seq, d), F32),
        jax.ShapeDtypeStruct((bsz, RET_HEADS, HEAD_DIM, HEAD_DIM), F32),
        jax.ShapeDtypeStruct((bsz, 1, LRU_WIDTH), F32),
        jax.ShapeDtypeStruct((bsz, CONV_WIDTH - 1, LRU_WIDTH), F32),
        jax.ShapeDtypeStruct((bsz, 1, S5_FLAT), F32),
        jax.ShapeDtypeStruct((bsz, 1, S5_FLAT), F32),
    )
    out_specs = (
        pl.BlockSpec((1, tl, d), lambda b, t: (b, t, 0)),
        pl.BlockSpec((1, RET_HEADS, HEAD_DIM, HEAD_DIM), lambda b, t: (b, 0, 0, 0)),
        pl.BlockSpec((1, 1, LRU_WIDTH), lambda b, t: (b, 0, 0)),
        pl.BlockSpec((1, CONV_WIDTH - 1, LRU_WIDTH), lambda b, t: (b, 0, 0)),
        pl.BlockSpec((1, 1, S5_FLAT), lambda b, t: (b, 0, 0)),
        pl.BlockSpec((1, 1, S5_FLAT), lambda b, t: (b, 0, 0)),
    )
    scratch = [
        pltpu.VMEM((tl, OFF_XL), F32),
        pltpu.VMEM((tl, d), BF16),
        pltpu.VMEM(((D_IN - OFF_XL) // LANES, tl, LANES), F32),
        pltpu.VMEM((tl + (CONV_WIDTH - 1) * SUBLANES, LRU_WIDTH), F32),
        pltpu.VMEM((tl, 2 * LRU_WIDTH), F32),
        pltpu.VMEM((tl, 2 * S5_FLAT), F32),
        pltpu.VMEM((tl, 2 * S5_FLAT), BF16),
        pltpu.VMEM(((LRU_WIDTH + S5_WIDTH) // LANES, tl, LANES), F32),
    ]
    return pl.pallas_call(
        functools.partial(_prompt_kernel, final=final),
        grid=(bsz, nt),
        in_specs=in_specs,
        out_specs=out_specs,
        out_shape=out_shape,
        scratch_shapes=scratch,
        compiler_params=pltpu.CompilerParams(
            dimension_semantics=("arbitrary", "arbitrary"), vmem_limit_bytes=VMEM_LIMIT),
        name="prompt_layer",
    )(x, mod, w["norm_g"], w["w_in"], w["w_out"],
      tabs["cos_q"], tabs["sin_q"], tabs["cos_k"], tabs["sin_k"],
      tabs["intra"], tabs["q_dec"], tabs["k_dec"], tabs["c_dec"],
      w["gn_g"], w["conv_w"], w["conv_b"], w["w_gate"], w["b_gate"], w["lam"],
      w["bd"], w["cd"], w["apr"], w["api"], w["s5_d"], w["w_glu"], w["final_g"])


def _decode_kernel(x_ref, mod_ref, ng_ref, win_ref, wout_ref,
                   cq_ref, sq_ref, ck_ref, sk_ref, gam_ref,
                   gng_ref, cw_ref, cb_ref, wg_ref, bg_ref, lam_ref,
                   bd_ref, cd_ref, apr_ref, api_ref, s5d_ref, wglu_ref, fg_ref,
                   sret_in, hl_in, cv_in, sre_in, sim_in,
                   y_ref, sret_ref, hl_ref, cv_ref, sre_ref, sim_ref,
                   qt_scr, kt_scr, v_scr, o_scr, gr_scr, ycat_scr, x_scr):
    layer = pl.program_id(0)
    i = pl.program_id(1)
    nb = pl.num_programs(1)
    rows_n, d = x_ref.shape
    (mod_ref, ng_ref, win_ref, wout_ref, gng_ref, cw_ref, cb_ref, wg_ref, bg_ref, lam_ref,
     bd_ref, cd_ref, apr_ref, api_ref, s5d_ref, wglu_ref,
     sret_in, hl_in, cv_in, sre_in, sim_in, sret_ref, hl_ref, cv_ref, sre_ref, sim_ref) = (
        r.at[0] for r in (mod_ref, ng_ref, win_ref, wout_ref, gng_ref, cw_ref, cb_ref, wg_ref,
                          bg_ref, lam_ref, bd_ref, cd_ref, apr_ref, api_ref, s5d_ref, wglu_ref,
                          sret_in, hl_in, cv_in, sre_in, sim_in,
                          sret_ref, hl_ref, cv_ref, sre_ref, sim_ref))

    @pl.when((i == 0) & (layer == 0))
    def _():
        x_scr[...] = x_ref[...]

    @pl.when(i == 0)
    def _():
        x = x_scr[...]
        shift, scale = mod_ref[:, pl.ds(0, d)], mod_ref[:, pl.ds(d, d)]
        h = _rmsnorm(x, ng_ref[...]) * (1.0 + scale) + shift
        proj = _dot(h, win_ref[...])
        cq, sq, ck, sk = cq_ref[...], sq_ref[...], ck_ref[...], sk_ref[...]
        for hd in range(RET_HEADS):
            lo = hd * HEAD_DIM
            qt_scr[hd] = _rotary(proj[:, OFF_Q + lo:OFF_Q + lo + HEAD_DIM], cq, sq).T
            kt_scr[hd] = _rotary(proj[:, OFF_K + lo:OFF_K + lo + HEAD_DIM], ck, sk).T
        v_scr[...] = proj[:, OFF_V:OFF_V + RET_WIDTH]
        gr_scr[...] = _silu(proj[:, OFF_GR:OFF_GR + RET_WIDTH])
        xl = proj[:, OFF_XL:OFF_XL + LRU_WIDTH]
        cs = cv_in[...]
        xc = cb_ref[...] + xl * cw_ref[pl.ds(CONV_WIDTH - 1, 1), :]
        for j in range(CONV_WIDTH - 1):
            xc = xc + cs[:, j * LRU_WIDTH:(j + 1) * LRU_WIDTH] * cw_ref[pl.ds(j, 1), :]
        cv_ref[:, pl.ds(0, 2 * LRU_WIDTH)] = cs[:, LRU_WIDTH:]
        cv_ref[:, pl.ds(2 * LRU_WIDTH, LRU_WIDTH)] = xl
        gates = _dot(xc, wg_ref[...]) + bg_ref[...]
        a, b = _lru_coeffs(xc, gates, _softplus(-lam_ref[...]))
        hh = b + a * hl_in[...]
        hl_ref[...] = hh
        ycat_scr[:, pl.ds(RET_WIDTH, LRU_WIDTH)] = (
            hh * _silu(proj[:, OFF_GL:OFF_GL + LRU_WIDTH])).astype(BF16)
        u = proj[:, OFF_U:OFF_U + S5_WIDTH]
        bu = _dot(u, bd_ref[...])
        ar, ai = apr_ref[pl.ds(0, 1), :], api_ref[pl.ds(0, 1), :]
        s0r, s0i = sre_in[...], sim_in[...]
        sr = bu[:, :S5_FLAT] + (ar * s0r - ai * s0i)
        si = bu[:, S5_FLAT:] + (ar * s0i + ai * s0r)
        sre_ref[...] = sr
        sim_ref[...] = si
        y_s5 = _s5_tail(jnp.concatenate([sr, si], axis=-1), u, proj[:, OFF_GS:OFF_GS + S5_WIDTH],
                        cd_ref[...], s5d_ref[...], wglu_ref[...])
        ycat_scr[:, pl.ds(RET_WIDTH + LRU_WIDTH, S5_WIDTH)] = y_s5.astype(BF16)

    lane = lax.broadcasted_iota(jnp.int32, (HEAD_DIM, rows_n), 1)
    blk_rows = pl.ds(pl.multiple_of(i * DEC_BLOCK, DEC_BLOCK), DEC_BLOCK)
    for hd in range(RET_HEADS):
        lanes = pl.ds(hd * HEAD_DIM, HEAD_DIM)
        vblk = v_scr[blk_rows, lanes]
        o_rows = []
        for j in range(DEC_BLOCK):
            sel = lane == i * DEC_BLOCK + j
            kcol = jnp.sum(jnp.where(sel, kt_scr[hd], 0.0), axis=1, keepdims=True)
            qcol = jnp.sum(jnp.where(sel, qt_scr[hd], 0.0), axis=1, keepdims=True)
            s_new = gam_ref[hd] * sret_in[j, hd] + kcol * vblk[j:j + 1, :]
            sret_ref[j, hd] = s_new
            o_rows.append(jnp.sum(qcol * s_new, axis=0, keepdims=True))
        o_scr[blk_rows, lanes] = jnp.concatenate(o_rows, axis=0)

    @pl.when(i == nb - 1)
    def _():
        for hd in range(RET_HEADS):
            lanes = pl.ds(hd * HEAD_DIM, HEAD_DIM)
            on = _groupnorm(o_scr[:, lanes], gng_ref[:, lanes])
            ycat_scr[:, lanes] = (on * gr_scr[:, lanes]).astype(BF16)
        out = jnp.dot(ycat_scr[...], wout_ref[...], preferred_element_type=F32)
        x_new = x_scr[...] + mod_ref[:, pl.ds(2 * d, d)] * out
        x_scr[...] = x_new
        y_ref[...] = _rmsnorm(x_new, fg_ref[...])


def _decode_layers(x, mod, states, w, tabs):
    rows, d = x.shape
    depth = mod.shape[0]
    s_ret, s_h, s_conv, s_re, s_im = states
    nb = rows // DEC_BLOCK

    def full(a):
        return pl.BlockSpec(a.shape, lambda l, i: (0,) * a.ndim)

    def lay(a):
        return pl.BlockSpec((1,) + a.shape[1:], lambda l, i: (l,) + (0,) * (a.ndim - 1))

    smem = pl.BlockSpec(memory_space=pltpu.SMEM)
    sblk = pl.BlockSpec((1, DEC_BLOCK, RET_HEADS, HEAD_DIM, HEAD_DIM), lambda l, i: (l, i, 0, 0, 0))
    in_specs = [
        full(x), lay(mod), lay(w["norm_g"]), lay(w["w_in"]), lay(w["w_out"]),
        full(tabs["dcos_q"]), full(tabs["dsin_q"]), full(tabs["dcos_k"]), full(tabs["dsin_k"]), smem,
        lay(w["gn_g"]), lay(w["conv_w"]), lay(w["conv_b"]), lay(w["w_gate"]), lay(w["b_gate"]),
        lay(w["lam"]), lay(w["bd"]), lay(w["cd"]), lay(w["apr"]), lay(w["api"]),
        lay(w["s5_d"]), lay(w["w_glu"]), full(w["final_g"]),
        sblk, lay(s_h), lay(s_conv), lay(s_re), lay(s_im),
    ]
    out_shape = (
        jax.ShapeDtypeStruct((rows, d), F32),
        jax.ShapeDtypeStruct(s_ret.shape, F32),
        jax.ShapeDtypeStruct(s_h.shape, F32),
        jax.ShapeDtypeStruct(s_conv.shape, F32),
        jax.ShapeDtypeStruct(s_re.shape, F32),
        jax.ShapeDtypeStruct(s_im.shape, F32),
    )
    out_specs = (full(x), sblk, lay(s_h), lay(s_conv), lay(s_re), lay(s_im))
    scratch = [
        pltpu.VMEM((RET_HEADS, HEAD_DIM, rows), F32),
        pltpu.VMEM((RET_HEADS, HEAD_DIM, rows), F32),
        pltpu.VMEM((rows, RET_WIDTH), F32),
        pltpu.VMEM((rows, RET_WIDTH), F32),
        pltpu.VMEM((rows, RET_WIDTH), F32),
        pltpu.VMEM((rows, d), BF16),
        pltpu.VMEM((rows, d), F32),
    ]
    return pl.pallas_call(
        _decode_kernel,
        grid=(depth, nb),
        in_specs=in_specs,
        out_specs=out_specs,
        out_shape=out_shape,
        scratch_shapes=scratch,
        compiler_params=pltpu.CompilerParams(
            dimension_semantics=("arbitrary", "arbitrary"), vmem_limit_bytes=VMEM_LIMIT),
        name="decode_layers",
    )(x, mod, w["norm_g"], w["w_in"], w["w_out"],
      tabs["dcos_q"], tabs["dsin_q"], tabs["dcos_k"], tabs["dsin_k"], tabs["gamma"],
      w["gn_g"], w["conv_w"], w["conv_b"], w["w_gate"], w["b_gate"], w["lam"],
      w["bd"], w["cd"], w["apr"], w["api"], w["s5_d"], w["w_glu"], w["final_g"],
      s_ret, s_h, s_conv, s_re, s_im)


def _rope_tables(pos, scale):
    half = HEAD_DIM // 2
    inv = ROPE_BASE ** (-jnp.arange(half, dtype=F32) / half)
    ang = pos[:, None] * inv[None, :]
    cos, sin = jnp.cos(ang), jnp.sin(ang)
    cosf = jnp.concatenate([cos, cos], axis=-1) * scale
    sinf = jnp.concatenate([-sin, sin], axis=-1) * scale
    return cosf, sinf


def _tables(seq):
    c = RET_CHUNK
    log_g = jnp.log1p(-jnp.exp2(-5.0 - jnp.arange(RET_HEADS, dtype=F32)))
    idx = jnp.arange(c, dtype=F32)
    rel = idx[:, None] - idx[None, :]
    intra = jnp.where(rel[None] >= 0,
                      jnp.exp(jnp.maximum(rel, 0.0)[None] * log_g[:, None, None]), 0.0)
    q_dec = jnp.exp((idx + 1.0)[:, None] * log_g[None, :])
    k_dec = jnp.exp((c - 1.0 - idx)[:, None] * log_g[None, :])
    q_dec = jnp.broadcast_to(q_dec.T[:, :, None], (RET_HEADS, c, HEAD_DIM))
    k_dec = jnp.broadcast_to(k_dec.T[:, :, None], (RET_HEADS, c, HEAD_DIM))
    c_dec = jnp.exp(c * log_g)
    gamma = jnp.exp(log_g)
    k_scale = HEAD_DIM ** -0.5
    cos_q, sin_q = _rope_tables(jnp.arange(seq, dtype=F32), 1.0)
    cos_k, sin_k = _rope_tables(jnp.arange(seq, dtype=F32), k_scale)
    dpos = PAST_LEN + jnp.arange(1, dtype=F32)
    dcos_q, dsin_q = _rope_tables(dpos, 1.0)
    dcos_k, dsin_k = _rope_tables(dpos, k_scale)
    return dict(intra=intra, q_dec=q_dec, k_dec=k_dec, c_dec=c_dec, gamma=gamma,
                cos_q=cos_q, sin_q=sin_q, cos_k=cos_k, sin_k=sin_k,
                dcos_q=dcos_q, dsin_q=dsin_q, dcos_k=dcos_k, dsin_k=dsin_k)


def _block_diag(blocks):
    n, r, c = blocks.shape
    eye = jnp.eye(n, dtype=blocks.dtype)
    return jnp.einsum("nrc,nm->nrmc", blocks, eye).reshape(n * r, n * c)


def kernel(x_prompt, x_sample, state_ret, state_lru_h, state_lru_conv, state_s5_re, state_s5_im,
           c_prompt, c_sample, norm_g, w_ada, b_ada, w_in, ret_gn_g, conv_w, conv_b, w_rg, b_rg,
           w_ig, b_ig, lru_lambda, s5_a_re, s5_a_im, s5_b_re, s5_b_im, s5_c_re, s5_c_im, s5_d,
           s5_log_dt, s5_w_glu, w_out, final_g):
    depth = w_in.shape[0]
    bsz, seq, d = x_prompt.shape
    dec = x_sample.shape[0]
    assert x_sample.shape[1] == 1 and seq % RET_CHUNK == 0 and dec % DEC_BLOCK == 0

    tabs = _tables(seq)
    mod_all = _ada(jnp.concatenate([c_prompt, c_sample], axis=0), w_ada, b_ada)
    n_pow = min(SEQ_TILE, seq) // SUBLANES
    apr, api, bbr, bbi = _s5_prep(s5_a_re, s5_a_im, s5_log_dt, s5_b_re, s5_b_im, n_pow)

    bdiag = jax.vmap(_block_diag)
    bd = jnp.concatenate([bdiag(bbr), bdiag(bbi)], axis=2)
    cd = jnp.concatenate([bdiag(jnp.swapaxes(s5_c_re, 2, 3)),
                          bdiag(jnp.swapaxes(-s5_c_im, 2, 3))], axis=1)
    w = dict(
        norm_g=norm_g.reshape(depth, 1, d),
        w_in=w_in.astype(BF16),
        w_out=w_out.astype(BF16),
        gn_g=ret_gn_g.reshape(depth, 1, RET_WIDTH),
        conv_w=conv_w,
        conv_b=conv_b.reshape(depth, 1, LRU_WIDTH),
        w_gate=jnp.concatenate([bdiag(w_rg), bdiag(w_ig)], axis=2).astype(BF16),
        b_gate=jnp.concatenate([b_rg, b_ig], axis=1).reshape(depth, 1, 2 * LRU_WIDTH),
        lam=lru_lambda.reshape(depth, 1, LRU_WIDTH),
        bd=bd.astype(BF16),
        cd=cd.astype(BF16),
        apr=apr.reshape(depth, n_pow, S5_FLAT),
        api=api.reshape(depth, n_pow, S5_FLAT),
        s5_d=s5_d.reshape(depth, 1, S5_WIDTH),
        w_glu=s5_w_glu.astype(BF16),
        final_g=final_g.reshape(1, d),
    )

    xp = x_prompt
    outs_p = []
    for l in range(depth):
        mod_p = mod_all[l, :bsz].reshape(bsz, 3, d)
        xp, sret, hl, cv, sre, sim = _prompt_layer(xp, mod_p, w, tabs, l, final=l == depth - 1)
        outs_p.append((sret, hl.reshape(bsz, LRU_WIDTH), cv,
                       sre.reshape(bsz, S5_GROUPS, S5_STATE), sim.reshape(bsz, S5_GROUPS, S5_STATE)))

    states = (state_ret, state_lru_h,
              state_lru_conv.reshape(depth, dec, (CONV_WIDTH - 1) * LRU_WIDTH),
              state_s5_re.reshape(depth, dec, S5_FLAT), state_s5_im.reshape(depth, dec, S5_FLAT))
    xs, sret_s, hl_s, cv_s, sre_s, sim_s = _decode_layers(
        x_sample.reshape(dec, d), mod_all[:, bsz:], states, w, tabs)

    def stk(k):
        return jnp.stack([o[k] for o in outs_p])

    return (xp, xs.reshape(dec, 1, d),
            stk(0), sret_s, stk(1), hl_s,
            stk(2), cv_s.reshape(depth, dec, CONV_WIDTH - 1, LRU_WIDTH),
            stk(3), sre_s.reshape(depth, dec, S5_GROUPS, S5_STATE),
            stk(4), sim_s.reshape(depth, dec, S5_GROUPS, S5_STATE))
```

```python
import functools
import math

import jax
import jax.numpy as jnp
import numpy as np
from jax import lax
from jax.experimental import pallas as pl
from jax.experimental.pallas import tpu as pltpu

RET_HEADS = 4
HEAD_DIM = 128
RET_WIDTH = RET_HEADS * HEAD_DIM
RET_CHUNK = 128
LRU_WIDTH = 256
LRU_BLOCKS = 4
LRU_C = 8.0
CONV_WIDTH = 4
S5_WIDTH = 256
S5_GROUP = 16
S5_GROUPS = 16
S5_STATE = 64
S5_FLAT = S5_GROUPS * S5_STATE
ROPE_BASE = 10000.0
EPS = 1e-6
PAST_LEN = 16384

OFF_Q = 0
OFF_K = OFF_Q + RET_WIDTH
OFF_V = OFF_K + RET_WIDTH
OFF_GR = OFF_V + RET_WIDTH
OFF_XL = OFF_GR + RET_WIDTH
OFF_GL = OFF_XL + LRU_WIDTH
OFF_U = OFF_GL + LRU_WIDTH
OFF_GS = OFF_U + S5_WIDTH
D_IN = OFF_GS + S5_WIDTH

SUBLANES = 8
LANES = 128
SEQ_TILE = 512
DEC_BLOCK = 8
VMEM_LIMIT = 56 * 1024 * 1024

F32 = jnp.float32
BF16 = jnp.bfloat16


def _sigmoid(x):
    return 1.0 / (1.0 + jnp.exp(-x))


def _silu(x):
    return x * _sigmoid(x)


def _gelu_tanh(x):
    c = math.sqrt(2.0 / math.pi)
    return x * (0.5 * (1.0 + jnp.tanh(c * (x + 0.044715 * (x * x * x)))))


def _softplus(x):
    return jnp.maximum(x, 0.0) + jnp.log1p(jnp.exp(-jnp.abs(x)))


def _rmsnorm(x, g):
    ms = jnp.mean(x * x, axis=-1, keepdims=True)
    return x * lax.rsqrt(ms + EPS) * g


def _dot(a, b):
    return jnp.dot(a.astype(BF16), b.astype(BF16), preferred_element_type=F32)


def _dot_nt(a, b):
    return lax.dot_general(a.astype(BF16), b.astype(BF16), (((1,), (1,)), ((), ())),
                           preferred_element_type=F32)


def _rotary(x, cosf, sinf):
    return x * cosf + pltpu.roll(x, HEAD_DIM // 2, axis=1) * sinf


def _groupnorm(o, g):
    mu = jnp.mean(o, axis=-1, keepdims=True)
    d = o - mu
    var = jnp.mean(d * d, axis=-1, keepdims=True)
    return d * lax.rsqrt(var + EPS) * g


def _lru_coeffs(xc, gates, sp):
    r = _sigmoid(gates[:, :LRU_WIDTH])
    ig = _sigmoid(gates[:, LRU_WIDTH:])
    log_a = (-LRU_C) * r * sp
    a = jnp.exp(log_a)
    th = jnp.tanh(log_a)
    mult = jnp.sqrt(-2.0 * th / (1.0 - th))
    return a, mult * ig * xc


def _s5_tail(sr_si, u, gs, cd, s5d, wglu):
    ys = _dot(sr_si, cd) + s5d * u
    ys = _gelu_tanh(ys)
    ys = ys * _sigmoid(_dot(ys, wglu))
    return ys * _silu(gs)


def _s5_prep_kernel(are_ref, aim_ref, ldt_ref, bre_ref, bim_ref,
                    apr_ref, api_ref, bbr_ref, bbi_ref):
    depth = are_ref.shape[0]
    for l in range(depth):
        a_re = are_ref[l]
        a_im = aim_ref[l]
        step = jnp.exp(ldt_ref[l])
        mag = jnp.exp(step * a_re)
        abar_r = mag * jnp.cos(step * a_im)
        abar_i = mag * jnp.sin(step * a_im)
        nr, ni = abar_r - 1.0, abar_i
        den = a_re * a_re + a_im * a_im
        fr = (nr * a_re + ni * a_im) / den
        fi = (ni * a_re - nr * a_im) / den
        for g in range(S5_GROUPS):
            frg, fig = fr[g:g + 1, :], fi[g:g + 1, :]
            b_re, b_im = bre_ref[l, g], bim_ref[l, g]
            bbr_ref[l, g] = frg * b_re - fig * b_im
            bbi_ref[l, g] = frg * b_im + fig * b_re
        pr, pi = abar_r, abar_i
        apr_ref[l, 0] = pr
        api_ref[l, 0] = pi
        for m in range(1, apr_ref.shape[1]):
            pr, pi = pr * abar_r - pi * abar_i, pr * abar_i + pi * abar_r
            apr_ref[l, m] = pr
            api_ref[l, m] = pi


def _s5_prep(s5_a_re, s5_a_im, s5_log_dt, s5_b_re, s5_b_im, n_pow):
    depth = s5_a_re.shape[0]
    b_re_t = jnp.swapaxes(s5_b_re, 2, 3)
    b_im_t = jnp.swapaxes(s5_b_im, 2, 3)
    ldt = s5_log_dt.reshape(depth, S5_GROUPS, 1)
    out_shape = (
        jax.ShapeDtypeStruct((depth, n_pow, S5_GROUPS, S5_STATE), F32),
        jax.ShapeDtypeStruct((depth, n_pow, S5_GROUPS, S5_STATE), F32),
        jax.ShapeDtypeStruct((depth, S5_GROUPS, S5_GROUP, S5_STATE), F32),
        jax.ShapeDtypeStruct((depth, S5_GROUPS, S5_GROUP, S5_STATE), F32),
    )
    return pl.pallas_call(_s5_prep_kernel, out_shape=out_shape, name="s5_prep")(
        s5_a_re, s5_a_im, ldt, b_re_t, b_im_t)


def _ada_kernel(c_ref, w_ref, b_ref, o_ref):
    s = _silu(c_ref[...])
    o_ref[0] = _dot(s, w_ref[0]) + b_ref[0]


def _ada(c_all, w_ada, b_ada):
    depth, d, n3 = w_ada.shape
    rows = c_all.shape[0]
    tn = d
    return pl.pallas_call(
        _ada_kernel,
        grid=(depth, n3 // tn),
        in_specs=[
            pl.BlockSpec((rows, d), lambda l, j: (0, 0)),
            pl.BlockSpec((1, d, tn), lambda l, j: (l, 0, j)),
            pl.BlockSpec((1, 1, tn), lambda l, j: (l, 0, j)),
        ],
        out_specs=pl.BlockSpec((1, rows, tn), lambda l, j: (l, 0, j)),
        out_shape=jax.ShapeDtypeStruct((depth, rows, n3), F32),
        compiler_params=pltpu.CompilerParams(
            dimension_semantics=("arbitrary", "arbitrary"), vmem_limit_bytes=VMEM_LIMIT),
        name="adaln",
    )(c_all, w_ada, b_ada.reshape(depth, 1, n3))


def _prompt_kernel(x_ref, mod_ref, ng_ref, win_ref, wout_ref,
                   cq_ref, sq_ref, ck_ref, sk_ref, intra_ref, qdec_ref, kdec_ref, cdec_ref,
                   gng_ref, cw_ref, cb_ref, wg_ref, bg_ref, lam_ref,
                   bd_ref, cd_ref, apr_ref, api_ref, s5d_ref, wglu_ref, fg_ref,
                   y_ref, sret_ref, hl_ref, cv_ref, sre_ref, sim_ref,
                   proj_scr, ycat_scr, perm_scr, xpad_scr, ab_scr, st_scr, st16_scr, yp_scr, *, final):
    tl = x_ref.shape[1]
    seg = tl // SUBLANES
    t = pl.program_id(1)
    (ng_ref, win_ref, wout_ref, gng_ref, cw_ref, cb_ref, wg_ref, bg_ref, lam_ref,
     bd_ref, cd_ref, apr_ref, api_ref, s5d_ref, wglu_ref) = (
        r.at[0] for r in (ng_ref, win_ref, wout_ref, gng_ref, cw_ref, cb_ref, wg_ref, bg_ref,
                          lam_ref, bd_ref, cd_ref, apr_ref, api_ref, s5d_ref, wglu_ref))

    @pl.when(t == 0)
    def _():
        sret_ref[...] = jnp.zeros_like(sret_ref)
        hl_ref[...] = jnp.zeros_like(hl_ref)
        sre_ref[...] = jnp.zeros_like(sre_ref)
        sim_ref[...] = jnp.zeros_like(sim_ref)
        cv_ref[...] = jnp.zeros_like(cv_ref)

    x = x_ref[0]
    shift, scale = mod_ref[0, 0:1, :], mod_ref[0, 1:2, :]
    h = _rmsnorm(x, ng_ref[...]) * (1.0 + scale) + shift
    hb = h.astype(BF16)
    proj_scr[...] = jnp.dot(hb, win_ref[:, pl.ds(0, OFF_XL)], preferred_element_type=F32)
    proj_b = jnp.dot(hb, win_ref[:, pl.ds(OFF_XL, D_IN - OFF_XL)], preferred_element_type=F32)
    for m in range(tl // SUBLANES):
        s, r0 = divmod(m * SUBLANES, seg)
        for kb in range((D_IN - OFF_XL) // LANES):
            perm_scr[kb, pl.ds(r0 * SUBLANES + s, SUBLANES, stride=SUBLANES), :] = (
                proj_b[m * SUBLANES:(m + 1) * SUBLANES, kb * LANES:(kb + 1) * LANES])

    def ret_chunk(c, carry):
        r0 = pl.multiple_of(c * RET_CHUNK, RET_CHUNK)
        rows = pl.ds(r0, RET_CHUNK)
        cq, sq, ck, sk = cq_ref[rows, :], sq_ref[rows, :], ck_ref[rows, :], sk_ref[rows, :]
        for hd in range(RET_HEADS):
            lanes = pl.ds(hd * HEAD_DIM, HEAD_DIM)
            q = _rotary(proj_scr[rows, pl.ds(OFF_Q + hd * HEAD_DIM, HEAD_DIM)], cq, sq)
            k = _rotary(proj_scr[rows, pl.ds(OFF_K + hd * HEAD_DIM, HEAD_DIM)], ck, sk)
            v = proj_scr[rows, pl.ds(OFF_V + hd * HEAD_DIM, HEAD_DIM)].astype(BF16)
            s_old = sret_ref[0, hd]
            sc = _dot_nt(q, k) * intra_ref[hd]
            o = _dot(sc, v) + _dot(q * qdec_ref[hd], s_old)
            kd_t = (k * kdec_ref[hd]).T
            sret_ref[0, hd] = cdec_ref[hd] * s_old + _dot(kd_t, v)
            on = _groupnorm(o, gng_ref[:, lanes])
            g_ret = proj_scr[rows, pl.ds(OFF_GR + hd * HEAD_DIM, HEAD_DIM)]
            ycat_scr[rows, lanes] = (on * _silu(g_ret)).astype(BF16)
        return carry

    for c in range(tl // RET_CHUNK):
        ret_chunk(c, 0)

    sub = lax.broadcasted_iota(jnp.int32, (SUBLANES, 1), 0)
    rg = lambda r: pl.ds(r * SUBLANES, SUBLANES)

    def pcols(lo, width):
        return jnp.concatenate([perm_scr[kb] for kb in range(lo // LANES, (lo + width) // LANES)],
                               axis=1)

    xl = pcols(OFF_XL - OFF_XL, LRU_WIDTH)
    prev = cv_ref[0]
    npre = CONV_WIDTH - 1
    for m in range(1, CONV_WIDTH):
        grp = xl[(seg - m) * SUBLANES:(seg - m + 1) * SUBLANES, :]
        e_m = jnp.where(sub == 0, prev[npre - m:npre - m + 1, :], pltpu.roll(grp, 1, axis=0))
        xpad_scr[rg(npre - m), :] = e_m
    xpad_scr[pl.ds(npre * SUBLANES, tl), :] = xl
    cv_ref[0] = jnp.concatenate(
        [xl[(seg - m) * SUBLANES + SUBLANES - 1:(seg - m + 1) * SUBLANES, :]
         for m in range(npre, 0, -1)], axis=0)
    xc = cb_ref[...]
    for j in range(CONV_WIDTH):
        xc = xc + xpad_scr[pl.ds(j * SUBLANES, tl), :] * cw_ref[pl.ds(j, 1), :]
    gates = _dot(xc, wg_ref[...]) + bg_ref[...]
    a, b = _lru_coeffs(xc, gates, _softplus(-lam_ref[...]))
    a_l, b_l = pl.ds(0, LRU_WIDTH), pl.ds(LRU_WIDTH, LRU_WIDTH)
    ab_scr[:, a_l] = a
    ab_scr[:, b_l] = b
    hloc = jnp.zeros((SUBLANES, LRU_WIDTH), F32)
    prod = jnp.ones((SUBLANES, LRU_WIDTH), F32)
    for r in range(seg):
        ar = ab_scr[rg(r), a_l]
        hloc = ar * hloc + ab_scr[rg(r), b_l]
        prod = prod * ar
        ab_scr[rg(r), a_l] = prod
        ab_scr[rg(r), b_l] = hloc
    c = hl_ref[0]
    cs = [c]
    for s in range(SUBLANES - 1):
        c = hloc[s:s + 1, :] + prod[s:s + 1, :] * c
        cs.append(c)
    hl_ref[0] = hloc[SUBLANES - 1:, :] + prod[SUBLANES - 1:, :] * c
    c_all = jnp.concatenate(cs, axis=0)
    sgl = _silu(pcols(OFF_GL - OFF_XL, LRU_WIDTH))
    for r in range(seg):
        y = (ab_scr[rg(r), b_l] + ab_scr[rg(r), a_l] * c_all) * sgl[r * SUBLANES:(r + 1) * SUBLANES, :]
        for kb in range(LRU_WIDTH // LANES):
            yp_scr[kb, rg(r), :] = y[:, kb * LANES:(kb + 1) * LANES]

    u = pcols(OFF_U - OFF_XL, S5_WIDTH)
    st_scr[...] = _dot(u, bd_ref[...])
    re_l, im_l = pl.ds(0, S5_FLAT), pl.ds(S5_FLAT, S5_FLAT)
    bshape = (SUBLANES, S5_FLAT)
    a1r = jnp.broadcast_to(apr_ref[pl.ds(0, 1), :], bshape)
    a1i = jnp.broadcast_to(api_ref[pl.ds(0, 1), :], bshape)
    hr = jnp.zeros((SUBLANES, S5_FLAT), F32)
    hi = jnp.zeros((SUBLANES, S5_FLAT), F32)
    for r in range(seg):
        hr, hi = (a1r * hr - a1i * hi + st_scr[rg(r), re_l],
                  a1r * hi + a1i * hr + st_scr[rg(r), im_l])
        st_scr[rg(r), re_l] = hr
        st_scr[rg(r), im_l] = hi
    asr, asi = apr_ref[pl.ds(seg - 1, 1), :], api_ref[pl.ds(seg - 1, 1), :]
    cr, ci = sre_ref[0], sim_ref[0]
    crs, cis = [cr], [ci]
    for s in range(SUBLANES - 1):
        cr, ci = (hr[s:s + 1, :] + (asr * cr - asi * ci), hi[s:s + 1, :] + (asr * ci + asi * cr))
        crs.append(cr)
        cis.append(ci)
    sre_ref[0] = hr[SUBLANES - 1:, :] + (asr * cr - asi * ci)
    sim_ref[0] = hi[SUBLANES - 1:, :] + (asr * ci + asi * cr)
    cr_all, ci_all = jnp.concatenate(crs, axis=0), jnp.concatenate(cis, axis=0)
    for r2 in range(seg // 2):
        parts_r, parts_i = [], []
        for r in (2 * r2, 2 * r2 + 1):
            qr = jnp.broadcast_to(apr_ref[pl.ds(r, 1), :], bshape)
            qi = jnp.broadcast_to(api_ref[pl.ds(r, 1), :], bshape)
            parts_r.append(st_scr[rg(r), re_l] + (qr * cr_all - qi * ci_all))
            parts_i.append(st_scr[rg(r), im_l] + (qr * ci_all + qi * cr_all))
        rows2 = pl.ds(r2 * 2 * SUBLANES, 2 * SUBLANES)
        st16_scr[rows2, re_l] = jnp.concatenate(parts_r, axis=0).astype(BF16)
        st16_scr[rows2, im_l] = jnp.concatenate(parts_i, axis=0).astype(BF16)
    gs = pcols(OFF_GS - OFF_XL, S5_WIDTH)
    y_s5 = _s5_tail(st16_scr[...], u, gs, cd_ref[...], s5d_ref[...], wglu_ref[...])
    for kb in range(S5_WIDTH // LANES):
        yp_scr[LRU_WIDTH // LANES + kb] = y_s5[:, kb * LANES:(kb + 1) * LANES]

    for m2 in range(tl // (2 * SUBLANES)):
        for kb in range((LRU_WIDTH + S5_WIDTH) // LANES):
            halves = []
            for m in (2 * m2, 2 * m2 + 1):
                s, r0 = divmod(m * SUBLANES, seg)
                halves.append(yp_scr[kb, pl.ds(r0 * SUBLANES + s, SUBLANES, stride=SUBLANES), :])
            ycat_scr[pl.ds(m2 * 2 * SUBLANES, 2 * SUBLANES), pl.ds(RET_WIDTH + kb * LANES, LANES)] = (
                jnp.concatenate(halves, axis=0).astype(BF16))

    out = jnp.dot(ycat_scr[...], wout_ref[...], preferred_element_type=F32)
    x_new = x_ref[0] + mod_ref[0, 2:3, :] * out
    if final:
        x_new = _rmsnorm(x_new, fg_ref[...])
    y_ref[0] = x_new


def _prompt_layer(x, mod, w, tabs, layer, *, final):
    bsz, seq, d = x.shape
    tl = min(SEQ_TILE, seq)
    nt = seq // tl

    def full(a):
        return pl.BlockSpec(a.shape, lambda b, t: (0,) * a.ndim)

    def lay(a):
        return pl.BlockSpec((1,) + a.shape[1:], lambda b, t: (layer,) + (0,) * (a.ndim - 1))

    row_tab = pl.BlockSpec((tl, HEAD_DIM), lambda b, t: (t, 0))
    in_specs = [
        pl.BlockSpec((1, tl, d), lambda b, t: (b, t, 0)),
        pl.BlockSpec((1, 3, d), lambda b, t: (b, 0, 0)),
        lay(w["norm_g"]), lay(w["w_in"]), lay(w["w_out"]),
        row_tab, row_tab, row_tab, row_tab,
        full(tabs["intra"]), full(tabs["q_dec"]), full(tabs["k_dec"]),
        pl.BlockSpec(memory_space=pltpu.SMEM),
        lay(w["gn_g"]), lay(w["conv_w"]), lay(w["conv_b"]), lay(w["w_gate"]), lay(w["b_gate"]),
        lay(w["lam"]), lay(w["bd"]), lay(w["cd"]), lay(w["apr"]), lay(w["api"]),
        lay(w["s5_d"]), lay(w["w_glu"]), full(w["final_g"]),
    ]
    out_shape = (
        jax.ShapeDtypeStruct((bsz, seq, d), F32),
        jax.ShapeDtypeStruct((bsz, RET_HEADS, HEAD_DIM, HEAD_DIM), F32),
        jax.ShapeDtypeStruct((bsz, 1, LRU_WIDTH), F32),
        jax.ShapeDtypeStruct((bsz, CONV_WIDTH - 1, LRU_WIDTH), F32),
        jax.ShapeDtypeStruct((bsz, 1, S5_FLAT), F32),
        jax.ShapeDtypeStruct((bsz, 1, S5_FLAT), F32),
    )
    out_specs = (
        pl.BlockSpec((1, tl, d), lambda b, t: (b, t, 0)),
        pl.BlockSpec((1, RET_HEADS, HEAD_DIM, HEAD_DIM), lambda b, t: (b, 0, 0, 0)),
        pl.BlockSpec((1, 1, LRU_WIDTH), lambda b, t: (b, 0, 0)),
        pl.BlockSpec((1, CONV_WIDTH - 1, LRU_WIDTH), lambda b, t: (b, 0, 0)),
        pl.BlockSpec((1, 1, S5_FLAT), lambda b, t: (b, 0, 0)),
        pl.BlockSpec((1, 1, S5_FLAT), lambda b, t: (b, 0, 0)),
    )
    scratch = [
        pltpu.VMEM((tl, OFF_XL), F32),
        pltpu.VMEM((tl, d), BF16),
        pltpu.VMEM(((D_IN - OFF_XL) // LANES, tl, LANES), F32),
        pltpu.VMEM((tl + (CONV_WIDTH - 1) * SUBLANES, LRU_WIDTH), F32),
        pltpu.VMEM((tl, 2 * LRU_WIDTH), F32),
        pltpu.VMEM((tl, 2 * S5_FLAT), F32),
        pltpu.VMEM((tl, 2 * S5_FLAT), BF16),
        pltpu.VMEM(((LRU_WIDTH + S5_WIDTH) // LANES, tl, LANES), F32),
    ]
    return pl.pallas_call(
        functools.partial(_prompt_kernel, final=final),
        grid=(bsz, nt),
        in_specs=in_specs,
        out_specs=out_specs,
        out_shape=out_shape,
        scratch_shapes=scratch,
        compiler_params=pltpu.CompilerParams(
            dimension_semantics=("arbitrary", "arbitrary"), vmem_limit_bytes=VMEM_LIMIT),
        name="prompt_layer",
    )(x, mod, w["norm_g"], w["w_in"], w["w_out"],
      tabs["cos_q"], tabs["sin_q"], tabs["cos_k"], tabs["sin_k"],
      tabs["intra"], tabs["q_dec"], tabs["k_dec"], tabs["c_dec"],
      w["gn_g"], w["conv_w"], w["conv_b"], w["w_gate"], w["b_gate"], w["lam"],
      w["bd"], w["cd"], w["apr"], w["api"], w["s5_d"], w["w_glu"], w["final_g"])


def _decode_kernel(x_ref, mod_ref, ng_ref, win_ref, wout_ref,
                   cq_ref, sq_ref, ck_ref, sk_ref, gam_ref,
                   gng_ref, cw_ref, cb_ref, wg_ref, bg_ref, lam_ref,
                   bd_ref, cd_ref, apr_ref, api_ref, s5d_ref, wglu_ref, fg_ref,
                   sret_in, hl_in, cv_in, sre_in, sim_in,
                   y_ref, sret_ref, hl_ref, cv_ref, sre_ref, sim_ref,
                   qt_scr, kt_scr, v_scr, o_scr, gr_scr, ycat_scr, x_scr):
    layer = pl.program_id(0)
    i = pl.program_id(1)
    nb = pl.num_programs(1)
    rows_n, d = x_ref.shape
    (mod_ref, ng_ref, win_ref, wout_ref, gng_ref, cw_ref, cb_ref, wg_ref, bg_ref, lam_ref,
     bd_ref, cd_ref, apr_ref, api_ref, s5d_ref, wglu_ref,
     sret_in, hl_in, cv_in, sre_in, sim_in, sret_ref, hl_ref, cv_ref, sre_ref, sim_ref) = (
        r.at[0] for r in (mod_ref, ng_ref, win_ref, wout_ref, gng_ref, cw_ref, cb_ref, wg_ref,
                          bg_ref, lam_ref, bd_ref, cd_ref, apr_ref, api_ref, s5d_ref, wglu_ref,
                          sret_in, hl_in, cv_in, sre_in, sim_in,
                          sret_ref, hl_ref, cv_ref, sre_ref, sim_ref))

    @pl.when((i == 0) & (layer == 0))
    def _():
        x_scr[...] = x_ref[...]

    @pl.when(i == 0)
    def _():
        x = x_scr[...]
        shift, scale = mod_ref[:, pl.ds(0, d)], mod_ref[:, pl.ds(d, d)]
        h = _rmsnorm(x, ng_ref[...]) * (1.0 + scale) + shift
        proj = _dot(h, win_ref[...])
        cq, sq, ck, sk = cq_ref[...], sq_ref[...], ck_ref[...], sk_ref[...]
        for hd in range(RET_HEADS):
            lo = hd * HEAD_DIM
            qt_scr[hd] = _rotary(proj[:, OFF_Q + lo:OFF_Q + lo + HEAD_DIM], cq, sq).T
            kt_scr[hd] = _rotary(proj[:, OFF_K + lo:OFF_K + lo + HEAD_DIM], ck, sk).T
        v_scr[...] = proj[:, OFF_V:OFF_V + RET_WIDTH]
        gr_scr[...] = _silu(proj[:, OFF_GR:OFF_GR + RET_WIDTH])
        xl = proj[:, OFF_XL:OFF_XL + LRU_WIDTH]
        cs = cv_in[...]
        xc = cb_ref[...] + xl * cw_ref[pl.ds(CONV_WIDTH - 1, 1), :]
        for j in range(CONV_WIDTH - 1):
            xc = xc + cs[:, j * LRU_WIDTH:(j + 1) * LRU_WIDTH] * cw_ref[pl.ds(j, 1), :]
        cv_ref[:, pl.ds(0, 2 * LRU_WIDTH)] = cs[:, LRU_WIDTH:]
        cv_ref[:, pl.ds(2 * LRU_WIDTH, LRU_WIDTH)] = xl
        gates = _dot(xc, wg_ref[...]) + bg_ref[...]
        a, b = _lru_coeffs(xc, gates, _softplus(-lam_ref[...]))
        hh = b + a * hl_in[...]
        hl_ref[...] = hh
        ycat_scr[:, pl.ds(RET_WIDTH, LRU_WIDTH)] = (
            hh * _silu(proj[:, OFF_GL:OFF_GL + LRU_WIDTH])).astype(BF16)
        u = proj[:, OFF_U:OFF_U + S5_WIDTH]
        bu = _dot(u, bd_ref[...])
        ar, ai = apr_ref[pl.ds(0, 1), :], api_ref[pl.ds(0, 1), :]
        s0r, s0i = sre_in[...], sim_in[...]
        sr = bu[:, :S5_FLAT] + (ar * s0r - ai * s0i)
        si = bu[:, S5_FLAT:] + (ar * s0i + ai * s0r)
        sre_ref[...] = sr
        sim_ref[...] = si
        y_s5 = _s5_tail(jnp.concatenate([sr, si], axis=-1), u, proj[:, OFF_GS:OFF_GS + S5_WIDTH],
                        cd_ref[...], s5d_ref[...], wglu_ref[...])
        ycat_scr[:, pl.ds(RET_WIDTH + LRU_WIDTH, S5_WIDTH)] = y_s5.astype(BF16)

    lane = lax.broadcasted_iota(jnp.int32, (HEAD_DIM, rows_n), 1)
    blk_rows = pl.ds(pl.multiple_of(i * DEC_BLOCK, DEC_BLOCK), DEC_BLOCK)
    for hd in range(RET_HEADS):
        lanes = pl.ds(hd * HEAD_DIM, HEAD_DIM)
        vblk = v_scr[blk_rows, lanes]
        o_rows = []
        for j in range(DEC_BLOCK):
            sel = lane == i * DEC_BLOCK + j
            kcol = jnp.sum(jnp.where(sel, kt_scr[hd], 0.0), axis=1, keepdims=True)
            qcol = jnp.sum(jnp.where(sel, qt_scr[hd], 0.0), axis=1, keepdims=True)
            s_new = gam_ref[hd] * sret_in[j, hd] + kcol * vblk[j:j + 1, :]
            sret_ref[j, hd] = s_new
            o_rows.append(jnp.sum(qcol * s_new, axis=0, keepdims=True))
        o_scr[blk_rows, lanes] = jnp.concatenate(o_rows, axis=0)

    @pl.when(i == nb - 1)
    def _():
        for hd in range(RET_HEADS):
            lanes = pl.ds(hd * HEAD_DIM, HEAD_DIM)
            on = _groupnorm(o_scr[:, lanes], gng_ref[:, lanes])
            ycat_scr[:, lanes] = (on * gr_scr[:, lanes]).astype(BF16)
        out = jnp.dot(ycat_scr[...], wout_ref[...], preferred_element_type=F32)
        x_new = x_scr[...] + mod_ref[:, pl.ds(2 * d, d)] * out
        x_scr[...] = x_new
        y_ref[...] = _rmsnorm(x_new, fg_ref[...])


def _decode_layers(x, mod, states, w, tabs):
    rows, d = x.shape
    depth = mod.shape[0]
    s_ret, s_h, s_conv, s_re, s_im = states
    nb = rows // DEC_BLOCK

    def full(a):
        return pl.BlockSpec(a.shape, lambda l, i: (0,) * a.ndim)

    def lay(a):
        return pl.BlockSpec((1,) + a.shape[1:], lambda l, i: (l,) + (0,) * (a.ndim - 1))

    smem = pl.BlockSpec(memory_space=pltpu.SMEM)
    sblk = pl.BlockSpec((1, DEC_BLOCK, RET_HEADS, HEAD_DIM, HEAD_DIM), lambda l, i: (l, i, 0, 0, 0))
    in_specs = [
        full(x), lay(mod), lay(w["norm_g"]), lay(w["w_in"]), lay(w["w_out"]),
        full(tabs["dcos_q"]), full(tabs["dsin_q"]), full(tabs["dcos_k"]), full(tabs["dsin_k"]), smem,
        lay(w["gn_g"]), lay(w["conv_w"]), lay(w["conv_b"]), lay(w["w_gate"]), lay(w["b_gate"]),
        lay(w["lam"]), lay(w["bd"]), lay(w["cd"]), lay(w["apr"]), lay(w["api"]),
        lay(w["s5_d"]), lay(w["w_glu"]), full(w["final_g"]),
        sblk, lay(s_h), lay(s_conv), lay(s_re), lay(s_im),
    ]
    out_shape = (
        jax.ShapeDtypeStruct((rows, d), F32),
        jax.ShapeDtypeStruct(s_ret.shape, F32),
        jax.ShapeDtypeStruct(s_h.shape, F32),
        jax.ShapeDtypeStruct(s_conv.shape, F32),
        jax.ShapeDtypeStruct(s_re.shape, F32),
        jax.ShapeDtypeStruct(s_im.shape, F32),
    )
    out_specs = (full(x), sblk, lay(s_h), lay(s_conv), lay(s_re), lay(s_im))
    scratch = [
        pltpu.VMEM((RET_HEADS, HEAD_DIM, rows), F32),
        pltpu.VMEM((RET_HEADS, HEAD_DIM, rows), F32),
        pltpu.VMEM((rows, RET_WIDTH), F32),
        pltpu.VMEM((rows, RET_WIDTH), F32),
        pltpu.VMEM((rows, RET_WIDTH), F32),
        pltpu.VMEM((rows, d), BF16),
        pltpu.VMEM((rows, d), F32),
    ]
    return pl.pallas_call(
        _decode_kernel,
        grid=(depth, nb),
        in_specs=in_specs,
        out_specs=out_specs,
        out_shape=out_shape,
        scratch_shapes=scratch,
        compiler_params=pltpu.CompilerParams(
            dimension_semantics=("arbitrary", "arbitrary"), vmem_limit_bytes=VMEM_LIMIT),
        name="decode_layers",
    )(x, mod, w["norm_g"], w["w_in"], w["w_out"],
      tabs["dcos_q"], tabs["dsin_q"], tabs["dcos_k"], tabs["dsin_k"], tabs["gamma"],
      w["gn_g"], w["conv_w"], w["conv_b"], w["w_gate"], w["b_gate"], w["lam"],
      w["bd"], w["cd"], w["apr"], w["api"], w["s5_d"], w["w_glu"], w["final_g"],
      s_ret, s_h, s_conv, s_re, s_im)


def _rope_tables(pos, scale):
    half = HEAD_DIM // 2
    inv = ROPE_BASE ** (-jnp.arange(half, dtype=F32) / half)
    ang = pos[:, None] * inv[None, :]
    cos, sin = jnp.cos(ang), jnp.sin(ang)
    cosf = jnp.concatenate([cos, cos], axis=-1) * scale
    sinf = jnp.concatenate([-sin, sin], axis=-1) * scale
    return cosf, sinf


def _tables(seq):
    c = RET_CHUNK
    log_g = jnp.log1p(-jnp.exp2(-5.0 - jnp.arange(RET_HEADS, dtype=F32)))
    idx = jnp.arange(c, dtype=F32)
    rel = idx[:, None] - idx[None, :]
    intra = jnp.where(rel[None] >= 0,
                      jnp.exp(jnp.maximum(rel, 0.0)[None] * log_g[:, None, None]), 0.0)
    q_dec = jnp.exp((idx + 1.0)[:, None] * log_g[None, :])
    k_dec = jnp.exp((c - 1.0 - idx)[:, None] * log_g[None, :])
    q_dec = jnp.broadcast_to(q_dec.T[:, :, None], (RET_HEADS, c, HEAD_DIM))
    k_dec = jnp.broadcast_to(k_dec.T[:, :, None], (RET_HEADS, c, HEAD_DIM))
    c_dec = jnp.exp(c * log_g)
    gamma = jnp.exp(log_g)
    k_scale = HEAD_DIM ** -0.5
    cos_q, sin_q = _rope_tables(jnp.arange(seq, dtype=F32), 1.0)
    cos_k, sin_k = _rope_tables(jnp.arange(seq, dtype=F32), k_scale)
    dpos = PAST_LEN + jnp.arange(1, dtype=F32)
    dcos_q, dsin_q = _rope_tables(dpos, 1.0)
    dcos_k, dsin_k = _rope_tables(dpos, k_scale)
    return dict(intra=intra, q_dec=q_dec, k_dec=k_dec, c_dec=c_dec, gamma=gamma,
                cos_q=cos_q, sin_q=sin_q, cos_k=cos_k, sin_k=sin_k,
                dcos_q=dcos_q, dsin_q=dsin_q, dcos_k=dcos_k, dsin_k=dsin_k)


def _block_diag(blocks):
    n, r, c = blocks.shape
    eye = jnp.eye(n, dtype=blocks.dtype)
    return jnp.einsum("nrc,nm->nrmc", blocks, eye).reshape(n * r, n * c)


def kernel(x_prompt, x_sample, state_ret, state_lru_h, state_lru_conv, state_s5_re, state_s5_im,
           c_prompt, c_sample, norm_g, w_ada, b_ada, w_in, ret_gn_g, conv_w, conv_b, w_rg, b_rg,
           w_ig, b_ig, lru_lambda, s5_a_re, s5_a_im, s5_b_re, s5_b_im, s5_c_re, s5_c_im, s5_d,
           s5_log_dt, s5_w_glu, w_out, final_g):
    depth = w_in.shape[0]
    bsz, seq, d = x_prompt.shape
    dec = x_sample.shape[0]
    assert x_sample.shape[1] == 1 and seq % RET_CHUNK == 0 and dec % DEC_BLOCK == 0

    tabs = _tables(seq)
    mod_all = _ada(jnp.concatenate([c_prompt, c_sample], axis=0), w_ada, b_ada)
    n_pow = min(SEQ_TILE, seq) // SUBLANES
    apr, api, bbr, bbi = _s5_prep(s5_a_re, s5_a_im, s5_log_dt, s5_b_re, s5_b_im, n_pow)

    bdiag = jax.vmap(_block_diag)
    bd = jnp.concatenate([bdiag(bbr), bdiag(bbi)], axis=2)
    cd = jnp.concatenate([bdiag(jnp.swapaxes(s5_c_re, 2, 3)),
                          bdiag(jnp.swapaxes(-s5_c_im, 2, 3))], axis=1)
    w = dict(
        norm_g=norm_g.reshape(depth, 1, d),
        w_in=w_in.astype(BF16),
        w_out=w_out.astype(BF16),
        gn_g=ret_gn_g.reshape(depth, 1, RET_WIDTH),
        conv_w=conv_w,
        conv_b=conv_b.reshape(depth, 1, LRU_WIDTH),
        w_gate=jnp.concatenate([bdiag(w_rg), bdiag(w_ig)], axis=2).astype(BF16),
        b_gate=jnp.concatenate([b_rg, b_ig], axis=1).reshape(depth, 1, 2 * LRU_WIDTH),
        lam=lru_lambda.reshape(depth, 1, LRU_WIDTH),
        bd=bd.astype(BF16),
        cd=cd.astype(BF16),
        apr=apr.reshape(depth, n_pow, S5_FLAT),
        api=api.reshape(depth, n_pow, S5_FLAT),
        s5_d=s5_d.reshape(depth, 1, S5_WIDTH),
        w_glu=s5_w_glu.astype(BF16),
        final_g=final_g.reshape(1, d),
    )

    xp = x_prompt
    outs_p = []
    for l in range(depth):
        mod_p = mod_all[l, :bsz].reshape(bsz, 3, d)
        xp, sret, hl, cv, sre, sim = _prompt_layer(xp, mod_p, w, tabs, l, final=l == depth - 1)
        outs_p.append((sret, hl.reshape(bsz, LRU_WIDTH), cv,
                       sre.reshape(bsz, S5_GROUPS, S5_STATE), sim.reshape(bsz, S5_GROUPS, S5_STATE)))

    states = (state_ret, state_lru_h,
              state_lru_conv.reshape(depth, dec, (CONV_WIDTH - 1) * LRU_WIDTH),
              state_s5_re.reshape(depth, dec, S5_FLAT), state_s5_im.reshape(depth, dec, S5_FLAT))
    xs, sret_s, hl_s, cv_s, sre_s, sim_s = _decode_layers(
        x_sample.reshape(dec, d), mod_all[:, bsz:], states, w, tabs)

    def stk(k):
        return jnp.stack([o[k] for o in outs_p])

    return (xp, xs.reshape(dec, 1, d),
            stk(0), sret_s, stk(1), hl_s,
            stk(2), cv_s.reshape(depth, dec, CONV_WIDTH - 1, LRU_WIDTH),
            stk(3), sre_s.reshape(depth, dec, S5_GROUPS, S5_STATE),
            stk(4), sim_s.reshape(depth, dec, S5_GROUPS, S5_STATE))
```

```python
import functools
import math

import jax
import jax.numpy as jnp
import numpy as np
from jax import lax
from jax.experimental import pallas as pl
from jax.experimental.pallas import tpu as pltpu

RET_HEADS = 4
HEAD_DIM = 128
RET_WIDTH = RET_HEADS * HEAD_DIM
RET_CHUNK = 128
LRU_WIDTH = 256
LRU_BLOCKS = 4
LRU_C = 8.0
CONV_WIDTH = 4
S5_WIDTH = 256
S5_GROUP = 16
S5_GROUPS = 16
S5_STATE = 64
S5_FLAT = S5_GROUPS * S5_STATE
ROPE_BASE = 10000.0
EPS = 1e-6
PAST_LEN = 16384

OFF_Q = 0
OFF_K = OFF_Q + RET_WIDTH
OFF_V = OFF_K + RET_WIDTH
OFF_GR = OFF_V + RET_WIDTH
OFF_XL = OFF_GR + RET_WIDTH
OFF_GL = OFF_XL + LRU_WIDTH
OFF_U = OFF_GL + LRU_WIDTH
OFF_GS = OFF_U + S5_WIDTH
D_IN = OFF_GS + S5_WIDTH

SUBLANES = 8
LANES = 128
SEQ_TILE = 512
PROJ_CHUNK = 256
DEC_BLOCK = 8
VMEM_LIMIT = 56 * 1024 * 1024

F32 = jnp.float32
BF16 = jnp.bfloat16


def _sigmoid(x):
    return 1.0 / (1.0 + jnp.exp(-x))


def _silu(x):
    return x * _sigmoid(x)


def _gelu_tanh(x):
    c = math.sqrt(2.0 / math.pi)
    return x * (0.5 * (1.0 + jnp.tanh(c * (x + 0.044715 * (x * x * x)))))


def _softplus(x):
    return jnp.maximum(x, 0.0) + jnp.log1p(jnp.exp(-jnp.abs(x)))


def _rmsnorm(x, g):
    ms = jnp.mean(x * x, axis=-1, keepdims=True)
    return x * lax.rsqrt(ms + EPS) * g


def _dot(a, b):
    return jnp.dot(a.astype(BF16), b.astype(BF16), preferred_element_type=F32)


def _dot_nt(a, b):
    return lax.dot_general(a.astype(BF16), b.astype(BF16), (((1,), (1,)), ((), ())),
                           preferred_element_type=F32)


def _rotary(x, cosf, sinf):
    return x * cosf + pltpu.roll(x, HEAD_DIM // 2, axis=1) * sinf


def _groupnorm(o, g):
    mu = jnp.mean(o, axis=-1, keepdims=True)
    d = o - mu
    var = jnp.mean(d * d, axis=-1, keepdims=True)
    return d * lax.rsqrt(var + EPS) * g


def _lru_coeffs(xc, gates, sp):
    r = _sigmoid(gates[:, :LRU_WIDTH])
    ig = _sigmoid(gates[:, LRU_WIDTH:])
    log_a = (-LRU_C) * r * sp
    a = jnp.exp(log_a)
    th = jnp.tanh(log_a)
    mult = jnp.sqrt(-2.0 * th / (1.0 - th))
    return a, mult * ig * xc


def _s5_tail(sr_si, u, gs, cd, s5d, wglu):
    ys = _dot(sr_si, cd) + s5d * u
    ys = _gelu_tanh(ys)
    ys = ys * _sigmoid(_dot(ys, wglu))
    return ys * _silu(gs)


def _s5_prep_kernel(are_ref, aim_ref, ldt_ref, bre_ref, bim_ref,
                    apr_ref, api_ref, bbr_ref, bbi_ref):
    depth = are_ref.shape[0]
    for l in range(depth):
        a_re = are_ref[l]
        a_im = aim_ref[l]
        step = jnp.exp(ldt_ref[l])
        mag = jnp.exp(step * a_re)
        abar_r = mag * jnp.cos(step * a_im)
        abar_i = mag * jnp.sin(step * a_im)
        nr, ni = abar_r - 1.0, abar_i
        den = a_re * a_re + a_im * a_im
        fr = (nr * a_re + ni * a_im) / den
        fi = (ni * a_re - nr * a_im) / den
        for g in range(S5_GROUPS):
            frg, fig = fr[g:g + 1, :], fi[g:g + 1, :]
            b_re, b_im = bre_ref[l, g], bim_ref[l, g]
            bbr_ref[l, g] = frg * b_re - fig * b_im
            bbi_ref[l, g] = frg * b_im + fig * b_re
        pr, pi = abar_r, abar_i
        apr_ref[l, 0] = pr
        api_ref[l, 0] = pi
        for m in range(1, apr_ref.shape[1]):
            pr, pi = pr * abar_r - pi * abar_i, pr * abar_i + pi * abar_r
            apr_ref[l, m] = pr
            api_ref[l, m] = pi


def _s5_prep(s5_a_re, s5_a_im, s5_log_dt, s5_b_re, s5_b_im, n_pow):
    depth = s5_a_re.shape[0]
    b_re_t = jnp.swapaxes(s5_b_re, 2, 3)
    b_im_t = jnp.swapaxes(s5_b_im, 2, 3)
    ldt = s5_log_dt.reshape(depth, S5_GROUPS, 1)
    out_shape = (
        jax.ShapeDtypeStruct((depth, n_pow, S5_GROUPS, S5_STATE), F32),
        jax.ShapeDtypeStruct((depth, n_pow, S5_GROUPS, S5_STATE), F32),
        jax.ShapeDtypeStruct((depth, S5_GROUPS, S5_GROUP, S5_STATE), F32),
        jax.ShapeDtypeStruct((depth, S5_GROUPS, S5_GROUP, S5_STATE), F32),
    )
    return pl.pallas_call(_s5_prep_kernel, out_shape=out_shape, name="s5_prep")(
        s5_a_re, s5_a_im, ldt, b_re_t, b_im_t)


def _ada_kernel(c_ref, w_ref, b_ref, o_ref):
    s = _silu(c_ref[...])
    o_ref[0] = _dot(s, w_ref[0]) + b_ref[0]


def _ada(c_all, w_ada, b_ada):
    depth, d, n3 = w_ada.shape
    rows = c_all.shape[0]
    tn = d
    return pl.pallas_call(
        _ada_kernel,
        grid=(depth, n3 // tn),
        in_specs=[
            pl.BlockSpec((rows, d), lambda l, j: (0, 0)),
            pl.BlockSpec((1, d, tn), lambda l, j: (l, 0, j)),
            pl.BlockSpec((1, 1, tn), lambda l, j: (l, 0, j)),
        ],
        out_specs=pl.BlockSpec((1, rows, tn), lambda l, j: (l, 0, j)),
        out_shape=jax.ShapeDtypeStruct((depth, rows, n3), F32),
        compiler_params=pltpu.CompilerParams(
            dimension_semantics=("arbitrary", "arbitrary"), vmem_limit_bytes=VMEM_LIMIT),
        name="adaln",
    )(c_all, w_ada, b_ada.reshape(depth, 1, n3))


def _prompt_kernel(x_ref, mod_ref, ng_ref, win_ref, wout_ref,
                   cq_ref, sq_ref, ck_ref, sk_ref, intra_ref, qdec_ref, kdec_ref, cdec_ref,
                   gng_ref, cw_ref, cb_ref, wg_ref, bg_ref, lam_ref,
                   bd_ref, cd_ref, apr_ref, api_ref, s5d_ref, wglu_ref, fg_ref,
                   y_ref, sret_ref, hl_ref, cv_ref, sre_ref, sim_ref,
                   proj_scr, ycat_scr, perm_scr, xpad_scr, ab_scr, st_scr, st16_scr, yp_scr, h_scr,
                   ph_scr, hloc_scr, *, final):
    tl = x_ref.shape[1]
    seg = tl // SUBLANES
    t = pl.program_id(1)
    (ng_ref, win_ref, wout_ref, gng_ref, cw_ref, cb_ref, wg_ref, bg_ref, lam_ref,
     bd_ref, cd_ref, apr_ref, api_ref, s5d_ref, wglu_ref) = (
        r.at[0] for r in (ng_ref, win_ref, wout_ref, gng_ref, cw_ref, cb_ref, wg_ref, bg_ref,
                          lam_ref, bd_ref, cd_ref, apr_ref, api_ref, s5d_ref, wglu_ref))

    @pl.when(t == 0)
    def _():
        sret_ref[...] = jnp.zeros_like(sret_ref)
        hl_ref[...] = jnp.zeros_like(hl_ref)
        sre_ref[...] = jnp.zeros_like(sre_ref)
        sim_ref[...] = jnp.zeros_like(sim_ref)
        cv_ref[...] = jnp.zeros_like(cv_ref)

    x = x_ref[0]
    shift, scale = mod_ref[0, 0:1, :], mod_ref[0, 1:2, :]
    h = _rmsnorm(x, ng_ref[...] * (1.0 + scale)) + shift
    h_scr[...] = h.astype(BF16)
    proj_b = jnp.dot(h_scr[...], win_ref[:, pl.ds(OFF_XL, D_IN - OFF_XL)],
                     preferred_element_type=F32)

    def proj_a_chunk(ci):
        cols = pl.ds(ci * PROJ_CHUNK, PROJ_CHUNK)
        proj_scr[:, cols] = jnp.dot(h_scr[...], win_ref[:, cols], preferred_element_type=F32)

    for m in range(tl // SUBLANES):
        s, r0 = divmod(m * SUBLANES, seg)
        for kb in range((D_IN - OFF_XL) // LANES):
            perm_scr[kb, pl.ds(r0 * SUBLANES + s, SUBLANES, stride=SUBLANES), :] = (
                proj_b[m * SUBLANES:(m + 1) * SUBLANES, kb * LANES:(kb + 1) * LANES])

    def ret_piece(c, hd):
        rows = pl.ds(c * RET_CHUNK, RET_CHUNK)
        lanes = pl.ds(hd * HEAD_DIM, HEAD_DIM)
        q = _rotary(proj_scr[rows, pl.ds(OFF_Q + hd * HEAD_DIM, HEAD_DIM)],
                    cq_ref[rows, :], sq_ref[rows, :])
        k = _rotary(proj_scr[rows, pl.ds(OFF_K + hd * HEAD_DIM, HEAD_DIM)],
                    ck_ref[rows, :], sk_ref[rows, :])
        v = proj_scr[rows, pl.ds(OFF_V + hd * HEAD_DIM, HEAD_DIM)].astype(BF16)
        s_old = sret_ref[0, hd]
        sc = _dot_nt(q, k) * intra_ref[hd]
        o = _dot(sc, v) + _dot(q * qdec_ref[hd], s_old)
        kd_t = (k * kdec_ref[hd]).T
        sret_ref[0, hd] = cdec_ref[hd] * s_old + _dot(kd_t, v)
        on = _groupnorm(o, gng_ref[:, lanes])
        g_ret = proj_scr[rows, pl.ds(OFF_GR + hd * HEAD_DIM, HEAD_DIM)]
        ycat_scr[rows, lanes] = (on * _silu(g_ret)).astype(BF16)

    sub = lax.broadcasted_iota(jnp.int32, (SUBLANES, 1), 0)
    rg = lambda r: pl.ds(r * SUBLANES, SUBLANES)

    def pcols(lo, width):
        return jnp.concatenate([perm_scr[kb] for kb in range(lo // LANES, (lo + width) // LANES)],
                               axis=1)

    n_chunks = OFF_XL // PROJ_CHUNK
    s5_parts = n_chunks - 3

    proj_a_chunk(0)
    xl = pcols(OFF_XL - OFF_XL, LRU_WIDTH)
    prev = cv_ref[0]
    npre = CONV_WIDTH - 1
    for m in range(1, CONV_WIDTH):
        grp = xl[(seg - m) * SUBLANES:(seg - m + 1) * SUBLANES, :]
        e_m = jnp.where(sub == 0, prev[npre - m:npre - m + 1, :], pltpu.roll(grp, 1, axis=0))
        xpad_scr[rg(npre - m), :] = e_m
    xpad_scr[pl.ds(npre * SUBLANES, tl), :] = xl
    cv_ref[0] = jnp.concatenate(
        [xl[(seg - m) * SUBLANES + SUBLANES - 1:(seg - m + 1) * SUBLANES, :]
         for m in range(npre, 0, -1)], axis=0)
    xc = cb_ref[...]
    for j in range(CONV_WIDTH):
        xc = xc + xpad_scr[pl.ds(j * SUBLANES, tl), :] * cw_ref[pl.ds(j, 1), :]
    gates = _dot(xc, wg_ref[...]) + bg_ref[...]
    a, b = _lru_coeffs(xc, gates, _softplus(-lam_ref[...]))
    a_l, b_l = pl.ds(0, LRU_WIDTH), pl.ds(LRU_WIDTH, LRU_WIDTH)
    ab_scr[:, a_l] = a
    ab_scr[:, b_l] = b
    proj_a_chunk(1)
    hloc = jnp.zeros((SUBLANES, LRU_WIDTH), F32)
    prod = jnp.ones((SUBLANES, LRU_WIDTH), F32)
    for r in range(seg):
        ar = ab_scr[rg(r), a_l]
        hloc = ar * hloc + ab_scr[rg(r), b_l]
        prod = prod * ar
        ph_scr[rg(r), a_l] = prod
        ph_scr[rg(r), b_l] = hloc
    c = hl_ref[0]
    cs = [c]
    for s in range(SUBLANES - 1):
        c = hloc[s:s + 1, :] + prod[s:s + 1, :] * c
        cs.append(c)
    hl_ref[0] = hloc[SUBLANES - 1:, :] + prod[SUBLANES - 1:, :] * c
    c_all = jnp.concatenate(cs, axis=0)
    sgl = _silu(pcols(OFF_GL - OFF_XL, LRU_WIDTH))
    for r in range(seg):
        y = (ph_scr[rg(r), b_l] + ph_scr[rg(r), a_l] * c_all) * sgl[r * SUBLANES:(r + 1) * SUBLANES, :]
        for kb in range(LRU_WIDTH // LANES):
            yp_scr[kb, rg(r), :] = y[:, kb * LANES:(kb + 1) * LANES]

    proj_a_chunk(2)
    u = pcols(OFF_U - OFF_XL, S5_WIDTH)
    st_scr[...] = _dot(u, bd_ref[...])
    re_l, im_l = pl.ds(0, S5_FLAT), pl.ds(S5_FLAT, S5_FLAT)
    bshape = (SUBLANES, S5_FLAT)
    a1r = jnp.broadcast_to(apr_ref[pl.ds(0, 1), :], bshape)
    a1i = jnp.broadcast_to(api_ref[pl.ds(0, 1), :], bshape)
    hr = jnp.zeros((SUBLANES, S5_FLAT), F32)
    hi = jnp.zeros((SUBLANES, S5_FLAT), F32)
    for r in range(seg):
        if r % (-(-seg // s5_parts)) == 0:
            proj_a_chunk(3 + r // (-(-seg // s5_parts)))
        hr, hi = (a1r * hr - a1i * hi + st_scr[rg(r), re_l],
                  a1r * hi + a1i * hr + st_scr[rg(r), im_l])
        hloc_scr[rg(r), re_l] = hr
        hloc_scr[rg(r), im_l] = hi
    asr, asi = apr_ref[pl.ds(seg - 1, 1), :], api_ref[pl.ds(seg - 1, 1), :]
    cr, ci = sre_ref[0], sim_ref[0]
    crs, cis = [cr], [ci]
    for s in range(SUBLANES - 1):
        cr, ci = (hr[s:s + 1, :] + (asr * cr - asi * ci), hi[s:s + 1, :] + (asr * ci + asi * cr))
        crs.append(cr)
        cis.append(ci)
    sre_ref[0] = hr[SUBLANES - 1:, :] + (asr * cr - asi * ci)
    sim_ref[0] = hi[SUBLANES - 1:, :] + (asr * ci + asi * cr)
    cr_all, ci_all = jnp.concatenate(crs, axis=0), jnp.concatenate(cis, axis=0)
    n_pieces = (tl // RET_CHUNK) * RET_HEADS
    per_piece = (seg // 2) // n_pieces
    for r2 in range(seg // 2):
        if r2 % per_piece == 0:
            ret_piece(*divmod(r2 // per_piece, RET_HEADS))
        parts_r, parts_i = [], []
        for r in (2 * r2, 2 * r2 + 1):
            qr = jnp.broadcast_to(apr_ref[pl.ds(r, 1), :], bshape)
            qi = jnp.broadcast_to(api_ref[pl.ds(r, 1), :], bshape)
            parts_r.append(hloc_scr[rg(r), re_l] + (qr * cr_all - qi * ci_all))
            parts_i.append(hloc_scr[rg(r), im_l] + (qr * ci_all + qi * cr_all))
        rows2 = pl.ds(r2 * 2 * SUBLANES, 2 * SUBLANES)
        st16_scr[rows2, re_l] = jnp.concatenate(parts_r, axis=0).astype(BF16)
        st16_scr[rows2, im_l] = jnp.concatenate(parts_i, axis=0).astype(BF16)
    gs = pcols(OFF_GS - OFF_XL, S5_WIDTH)
    y_s5 = _s5_tail(st16_scr[...], u, gs, cd_ref[...], s5d_ref[...], wglu_ref[...])
    for kb in range(S5_WIDTH // LANES):
        yp_scr[LRU_WIDTH // LANES + kb] = y_s5[:, kb * LANES:(kb + 1) * LANES]

    for m2 in range(tl // (2 * SUBLANES)):
        for kb in range((LRU_WIDTH + S5_WIDTH) // LANES):
            halves = []
            for m in (2 * m2, 2 * m2 + 1):
                s, r0 = divmod(m * SUBLANES, seg)
                halves.append(yp_scr[kb, pl.ds(r0 * SUBLANES + s, SUBLANES, stride=SUBLANES), :])
            ycat_scr[pl.ds(m2 * 2 * SUBLANES, 2 * SUBLANES), pl.ds(RET_WIDTH + kb * LANES, LANES)] = (
                jnp.concatenate(halves, axis=0).astype(BF16))

    out = jnp.dot(ycat_scr[...], wout_ref[...], preferred_element_type=F32)
    x_new = x_ref[0] + mod_ref[0, 2:3, :] * out
    if final:
        x_new = _rmsnorm(x_new, fg_ref[...])
    y_ref[0] = x_new


def _prompt_layer(x, mod, w, tabs, layer, *, final):
    bsz, seq, d = x.shape
    tl = min(SEQ_TILE, seq)
    nt = seq // tl

    def full(a):
        return pl.BlockSpec(a.shape, lambda b, t: (0,) * a.ndim)

    def lay(a):
        return pl.BlockSpec((1,) + a.shape[1:], lambda b, t: (layer,) + (0,) * (a.ndim - 1))

    row_tab = pl.BlockSpec((tl, HEAD_DIM), lambda b, t: (t, 0))
    in_specs = [
        pl.BlockSpec((1, tl, d), lambda b, t: (b, t, 0)),
        pl.BlockSpec((1, 3, d), lambda b, t: (b, 0, 0)),
        lay(w["norm_g"]), lay(w["w_in"]), lay(w["w_out"]),
        row_tab, row_tab, row_tab, row_tab,
        full(tabs["intra"]), full(tabs["q_dec"]), full(tabs["k_dec"]),
        pl.BlockSpec(memory_space=pltpu.SMEM),
        lay(w["gn_g"]), lay(w["conv_w"]), lay(w["conv_b"]), lay(w["w_gate"]), lay(w["b_gate"]),
        lay(w["lam"]), lay(w["bd"]), lay(w["cd"]), lay(w["apr"]), lay(w["api"]),
        lay(w["s5_d"]), lay(w["w_glu"]), full(w["final_g"]),
    ]
    out_shape = (
        jax.ShapeDtypeStruct((bsz, seq, d), F32),
        jax.ShapeDtypeStruct((bsz, RET_HEADS, HEAD_DIM, HEAD_DIM), F32),
        jax.ShapeDtypeStruct((bsz, 1, LRU_WIDTH), F32),
        jax.ShapeDtypeStruct((bsz, CONV_WIDTH - 1, LRU_WIDTH), F32),
        jax.ShapeDtypeStruct((bsz, 1, S5_FLAT), F32),
        jax.ShapeDtypeStruct((bsz, 1, S5_FLAT), F32),
    )
    out_specs = (
        pl.BlockSpec((1, tl, d), lambda b, t: (b, t, 0)),
        pl.BlockSpec((1, RET_HEADS, HEAD_DIM, HEAD_DIM), lambda b, t: (b, 0, 0, 0)),
        pl.BlockSpec((1, 1, LRU_WIDTH), lambda b, t: (b, 0, 0)),
        pl.BlockSpec((1, CONV_WIDTH - 1, LRU_WIDTH), lambda b, t: (b, 0, 0)),
        pl.BlockSpec((1, 1, S5_FLAT), lambda b, t: (b, 0, 0)),
        pl.BlockSpec((1, 1, S5_FLAT), lambda b, t: (b, 0, 0)),
    )
    scratch = [
        pltpu.VMEM((tl, OFF_XL), F32),
        pltpu.VMEM((tl, d), BF16),
        pltpu.VMEM(((D_IN - OFF_XL) // LANES, tl, LANES), F32),
        pltpu.VMEM((tl + (CONV_WIDTH - 1) * SUBLANES, LRU_WIDTH), F32),
        pltpu.VMEM((tl, 2 * LRU_WIDTH), F32),
        pltpu.VMEM((tl, 2 * S5_FLAT), F32),
        pltpu.VMEM((tl, 2 * S5_FLAT), BF16),
        pltpu.VMEM(((LRU_WIDTH + S5_WIDTH) // LANES, tl, LANES), F32),
        pltpu.VMEM((tl, d), BF16),
        pltpu.VMEM((tl, 2 * LRU_WIDTH), F32),
        pltpu.VMEM((tl, 2 * S5_FLAT), F32),
    ]
    return pl.pallas_call(
        functools.partial(_prompt_kernel, final=final),
        grid=(bsz, nt),
        in_specs=in_specs,
        out_specs=out_specs,
        out_shape=out_shape,
        scratch_shapes=scratch,
        compiler_params=pltpu.CompilerParams(
            dimension_semantics=("arbitrary", "arbitrary"), vmem_limit_bytes=VMEM_LIMIT),
        name="prompt_layer",
    )(x, mod, w["norm_g"], w["w_in"], w["w_out"],
      tabs["cos_q"], tabs["sin_q"], tabs["cos_k"], tabs["sin_k"],
      tabs["intra"], tabs["q_dec"], tabs["k_dec"], tabs["c_dec"],
      w["gn_g"], w["conv_w"], w["conv_b"], w["w_gate"], w["b_gate"], w["lam"],
      w["bd"], w["cd"], w["apr"], w["api"], w["s5_d"], w["w_glu"], w["final_g"])


def _decode_kernel(x_ref, mod_ref, ng_ref, win_ref, wout_ref,
                   cq_ref, sq_ref, ck_ref, sk_ref, gam_ref,
                   gng_ref, cw_ref, cb_ref, wg_ref, bg_ref, lam_ref,
                   bd_ref, cd_ref, apr_ref, api_ref, s5d_ref, wglu_ref, fg_ref,
                   sret_in, hl_in, cv_in, sre_in, sim_in,
                   y_ref, sret_ref, hl_ref, cv_ref, sre_ref, sim_ref,
                   qt_scr, kt_scr, v_scr, o_scr, gr_scr, ycat_scr, x_scr):
    layer = pl.program_id(0)
    i = pl.program_id(1)
    nb = pl.num_programs(1)
    rows_n, d = x_ref.shape
    (mod_ref, ng_ref, win_ref, wout_ref, gng_ref, cw_ref, cb_ref, wg_ref, bg_ref, lam_ref,
     bd_ref, cd_ref, apr_ref, api_ref, s5d_ref, wglu_ref,
     sret_in, hl_in, cv_in, sre_in, sim_in, sret_ref, hl_ref, cv_ref, sre_ref, sim_ref) = (
        r.at[0] for r in (mod_ref, ng_ref, win_ref, wout_ref, gng_ref, cw_ref, cb_ref, wg_ref,
                          bg_ref, lam_ref, bd_ref, cd_ref, apr_ref, api_ref, s5d_ref, wglu_ref,
                          sret_in, hl_in, cv_in, sre_in, sim_in,
                          sret_ref, hl_ref, cv_ref, sre_ref, sim_ref))

    @pl.when((i == 0) & (layer == 0))
    def _():
        x_scr[...] = x_ref[...]

    @pl.when(i == 0)
    def _():
        x = x_scr[...]
        shift, scale = mod_ref[:, pl.ds(0, d)], mod_ref[:, pl.ds(d, d)]
        h = _rmsnorm(x, ng_ref[...]) * (1.0 + scale) + shift
        proj = _dot(h, win_ref[...])
        cq, sq, ck, sk = cq_ref[...], sq_ref[...], ck_ref[...], sk_ref[...]
        for hd in range(RET_HEADS):
            lo = hd * HEAD_DIM
            qt_scr[hd] = _rotary(proj[:, OFF_Q + lo:OFF_Q + lo + HEAD_DIM], cq, sq).T
            kt_scr[hd] = _rotary(proj[:, OFF_K + lo:OFF_K + lo + HEAD_DIM], ck, sk).T
        v_scr[...] = proj[:, OFF_V:OFF_V + RET_WIDTH]
        gr_scr[...] = _silu(proj[:, OFF_GR:OFF_GR + RET_WIDTH])
        xl = proj[:, OFF_XL:OFF_XL + LRU_WIDTH]
        cs = cv_in[...]
        xc = cb_ref[...] + xl * cw_ref[pl.ds(CONV_WIDTH - 1, 1), :]
        for j in range(CONV_WIDTH - 1):
            xc = xc + cs[:, j * LRU_WIDTH:(j + 1) * LRU_WIDTH] * cw_ref[pl.ds(j, 1), :]
        cv_ref[:, pl.ds(0, 2 * LRU_WIDTH)] = cs[:, LRU_WIDTH:]
        cv_ref[:, pl.ds(2 * LRU_WIDTH, LRU_WIDTH)] = xl
        gates = _dot(xc, wg_ref[...]) + bg_ref[...]
        a, b = _lru_coeffs(xc, gates, _softplus(-lam_ref[...]))
        hh = b + a * hl_in[...]
        hl_ref[...] = hh
        ycat_scr[:, pl.ds(RET_WIDTH, LRU_WIDTH)] = (
            hh * _silu(proj[:, OFF_GL:OFF_GL + LRU_WIDTH])).astype(BF16)
        u = proj[:, OFF_U:OFF_U + S5_WIDTH]
        bu = _dot(u, bd_ref[...])
        ar, ai = apr_ref[pl.ds(0, 1), :], api_ref[pl.ds(0, 1), :]
        s0r, s0i = sre_in[...], sim_in[...]
        sr = bu[:, :S5_FLAT] + (ar * s0r - ai * s0i)
        si = bu[:, S5_FLAT:] + (ar * s0i + ai * s0r)
        sre_ref[...] = sr
        sim_ref[...] = si
        y_s5 = _s5_tail(jnp.concatenate([sr, si], axis=-1), u, proj[:, OFF_GS:OFF_GS + S5_WIDTH],
                        cd_ref[...], s5d_ref[...], wglu_ref[...])
        ycat_scr[:, pl.ds(RET_WIDTH + LRU_WIDTH, S5_WIDTH)] = y_s5.astype(BF16)

    lane = lax.broadcasted_iota(jnp.int32, (HEAD_DIM, rows_n), 1)
    blk_rows = pl.ds(pl.multiple_of(i * DEC_BLOCK, DEC_BLOCK), DEC_BLOCK)
    for hd in range(RET_HEADS):
        lanes = pl.ds(hd * HEAD_DIM, HEAD_DIM)
        vblk = v_scr[blk_rows, lanes]
        o_rows = []
        for j in range(DEC_BLOCK):
            sel = lane == i * DEC_BLOCK + j
            kcol = jnp.sum(jnp.where(sel, kt_scr[hd], 0.0), axis=1, keepdims=True)
            qcol = jnp.sum(jnp.where(sel, qt_scr[hd], 0.0), axis=1, keepdims=True)
            s_new = gam_ref[hd] * sret_in[j, hd] + kcol * vblk[j:j + 1, :]
            sret_ref[j, hd] = s_new
            o_rows.append(jnp.sum(qcol * s_new, axis=0, keepdims=True))
        o_scr[blk_rows, lanes] = jnp.concatenate(o_rows, axis=0)

    @pl.when(i == nb - 1)
    def _():
        for hd in range(RET_HEADS):
            lanes = pl.ds(hd * HEAD_DIM, HEAD_DIM)
            on = _groupnorm(o_scr[:, lanes], gng_ref[:, lanes])
            ycat_scr[:, lanes] = (on * gr_scr[:, lanes]).astype(BF16)
        out = jnp.dot(ycat_scr[...], wout_ref[...], preferred_element_type=F32)
        x_new = x_scr[...] + mod_ref[:, pl.ds(2 * d, d)] * out
        x_scr[...] = x_new
        y_ref[...] = _rmsnorm(x_new, fg_ref[...])


def _decode_layers(x, mod, states, w, tabs):
    rows, d = x.shape
    depth = mod.shape[0]
    s_ret, s_h, s_conv, s_re, s_im = states
    nb = rows // DEC_BLOCK

    def full(a):
        return pl.BlockSpec(a.shape, lambda l, i: (0,) * a.ndim)

    def lay(a):
        return pl.BlockSpec((1,) + a.shape[1:], lambda l, i: (l,) + (0,) * (a.ndim - 1))

    smem = pl.BlockSpec(memory_space=pltpu.SMEM)
    sblk = pl.BlockSpec((1, DEC_BLOCK, RET_HEADS, HEAD_DIM, HEAD_DIM), lambda l, i: (l, i, 0, 0, 0))
    in_specs = [
        full(x), lay(mod), lay(w["norm_g"]), lay(w["w_in"]), lay(w["w_out"]),
        full(tabs["dcos_q"]), full(tabs["dsin_q"]), full(tabs["dcos_k"]), full(tabs["dsin_k"]), smem,
        lay(w["gn_g"]), lay(w["conv_w"]), lay(w["conv_b"]), lay(w["w_gate"]), lay(w["b_gate"]),
        lay(w["lam"]), lay(w["bd"]), lay(w["cd"]), lay(w["apr"]), lay(w["api"]),
        lay(w["s5_d"]), lay(w["w_glu"]), full(w["final_g"]),
        sblk, lay(s_h), lay(s_conv), lay(s_re), lay(s_im),
    ]
    out_shape = (
        jax.ShapeDtypeStruct((rows, d), F32),
        jax.ShapeDtypeStruct(s_ret.shape, F32),
        jax.ShapeDtypeStruct(s_h.shape, F32),
        jax.ShapeDtypeStruct(s_conv.shape, F32),
        jax.ShapeDtypeStruct(s_re.shape, F32),
        jax.ShapeDtypeStruct(s_im.shape, F32),
    )
    out_specs = (full(x), sblk, lay(s_h), lay(s_conv), lay(s_re), lay(s_im))
    scratch = [
        pltpu.VMEM((RET_HEADS, HEAD_DIM, rows), F32),
        pltpu.VMEM((RET_HEADS, HEAD_DIM, rows), F32),
        pltpu.VMEM((rows, RET_WIDTH), F32),
        pltpu.VMEM((rows, RET_WIDTH), F32),
        pltpu.VMEM((rows, RET_WIDTH), F32),
        pltpu.VMEM((rows, d), BF16),
        pltpu.VMEM((rows, d), F32),
    ]
    return pl.pallas_call(
        _decode_kernel,
        grid=(depth, nb),
        in_specs=in_specs,
        out_specs=out_specs,
        out_shape=out_shape,
        scratch_shapes=scratch,
        compiler_params=pltpu.CompilerParams(
            dimension_semantics=("arbitrary", "arbitrary"), vmem_limit_bytes=VMEM_LIMIT),
        name="decode_layers",
    )(x, mod, w["norm_g"], w["w_in"], w["w_out"],
      tabs["dcos_q"], tabs["dsin_q"], tabs["dcos_k"], tabs["dsin_k"], tabs["gamma"],
      w["gn_g"], w["conv_w"], w["conv_b"], w["w_gate"], w["b_gate"], w["lam"],
      w["bd"], w["cd"], w["apr"], w["api"], w["s5_d"], w["w_glu"], w["final_g"],
      s_ret, s_h, s_conv, s_re, s_im)


def _rope_tables(pos, scale):
    half = HEAD_DIM // 2
    inv = ROPE_BASE ** (-jnp.arange(half, dtype=F32) / half)
    ang = pos[:, None] * inv[None, :]
    cos, sin = jnp.cos(ang), jnp.sin(ang)
    cosf = jnp.concatenate([cos, cos], axis=-1) * scale
    sinf = jnp.concatenate([-sin, sin], axis=-1) * scale
    return cosf, sinf


def _tables(seq):
    c = RET_CHUNK
    log_g = jnp.log1p(-jnp.exp2(-5.0 - jnp.arange(RET_HEADS, dtype=F32)))
    idx = jnp.arange(c, dtype=F32)
    rel = idx[:, None] - idx[None, :]
    intra = jnp.where(rel[None] >= 0,
                      jnp.exp(jnp.maximum(rel, 0.0)[None] * log_g[:, None, None]), 0.0)
    q_dec = jnp.exp((idx + 1.0)[:, None] * log_g[None, :])
    k_dec = jnp.exp((c - 1.0 - idx)[:, None] * log_g[None, :])
    q_dec = jnp.broadcast_to(q_dec.T[:, :, None], (RET_HEADS, c, HEAD_DIM))
    k_dec = jnp.broadcast_to(k_dec.T[:, :, None], (RET_HEADS, c, HEAD_DIM))
    c_dec = jnp.exp(c * log_g)
    gamma = jnp.exp(log_g)
    k_scale = HEAD_DIM ** -0.5
    cos_q, sin_q = _rope_tables(jnp.arange(seq, dtype=F32), 1.0)
    cos_k, sin_k = _rope_tables(jnp.arange(seq, dtype=F32), k_scale)
    dpos = PAST_LEN + jnp.arange(1, dtype=F32)
    dcos_q, dsin_q = _rope_tables(dpos, 1.0)
    dcos_k, dsin_k = _rope_tables(dpos, k_scale)
    return dict(intra=intra, q_dec=q_dec, k_dec=k_dec, c_dec=c_dec, gamma=gamma,
                cos_q=cos_q, sin_q=sin_q, cos_k=cos_k, sin_k=sin_k,
                dcos_q=dcos_q, dsin_q=dsin_q, dcos_k=dcos_k, dsin_k=dsin_k)


def _block_diag(blocks):
    n, r, c = blocks.shape
    eye = jnp.eye(n, dtype=blocks.dtype)
    return jnp.einsum("nrc,nm->nrmc", blocks, eye).reshape(n * r, n * c)


def kernel(x_prompt, x_sample, state_ret, state_lru_h, state_lru_conv, state_s5_re, state_s5_im,
           c_prompt, c_sample, norm_g, w_ada, b_ada, w_in, ret_gn_g, conv_w, conv_b, w_rg, b_rg,
           w_ig, b_ig, lru_lambda, s5_a_re, s5_a_im, s5_b_re, s5_b_im, s5_c_re, s5_c_im, s5_d,
           s5_log_dt, s5_w_glu, w_out, final_g):
    depth = w_in.shape[0]
    bsz, seq, d = x_prompt.shape
    dec = x_sample.shape[0]
    assert x_sample.shape[1] == 1 and seq % RET_CHUNK == 0 and dec % DEC_BLOCK == 0

    tabs = _tables(seq)
    mod_all = _ada(jnp.concatenate([c_prompt, c_sample], axis=0), w_ada, b_ada)
    n_pow = min(SEQ_TILE, seq) // SUBLANES
    apr, api, bbr, bbi = _s5_prep(s5_a_re, s5_a_im, s5_log_dt, s5_b_re, s5_b_im, n_pow)

    bdiag = jax.vmap(_block_diag)
    bd = jnp.concatenate([bdiag(bbr), bdiag(bbi)], axis=2)
    cd = jnp.concatenate([bdiag(jnp.swapaxes(s5_c_re, 2, 3)),
                          bdiag(jnp.swapaxes(-s5_c_im, 2, 3))], axis=1)
    w = dict(
        norm_g=norm_g.reshape(depth, 1, d),
        w_in=w_in.astype(BF16),
        w_out=w_out.astype(BF16),
        gn_g=ret_gn_g.reshape(depth, 1, RET_WIDTH),
        conv_w=conv_w,
        conv_b=conv_b.reshape(depth, 1, LRU_WIDTH),
        w_gate=jnp.concatenate([bdiag(w_rg), bdiag(w_ig)], axis=2).astype(BF16),
        b_gate=jnp.concatenate([b_rg, b_ig], axis=1).reshape(depth, 1, 2 * LRU_WIDTH),
        lam=lru_lambda.reshape(depth, 1, LRU_WIDTH),
        bd=bd.astype(BF16),
        cd=cd.astype(BF16),
        apr=apr.reshape(depth, n_pow, S5_FLAT),
        api=api.reshape(depth, n_pow, S5_FLAT),
        s5_d=s5_d.reshape(depth, 1, S5_WIDTH),
        w_glu=s5_w_glu.astype(BF16),
        final_g=final_g.reshape(1, d),
    )

    xp = x_prompt
    outs_p = []
    for l in range(depth):
        mod_p = mod_all[l, :bsz].reshape(bsz, 3, d)
        xp, sret, hl, cv, sre, sim = _prompt_layer(xp, mod_p, w, tabs, l, final=l == depth - 1)
        outs_p.append((sret, hl.reshape(bsz, LRU_WIDTH), cv,
                       sre.reshape(bsz, S5_GROUPS, S5_STATE), sim.reshape(bsz, S5_GROUPS, S5_STATE)))

    states = (state_ret, state_lru_h,
              state_lru_conv.reshape(depth, dec, (CONV_WIDTH - 1) * LRU_WIDTH),
              state_s5_re.reshape(depth, dec, S5_FLAT), state_s5_im.reshape(depth, dec, S5_FLAT))
    xs, sret_s, hl_s, cv_s, sre_s, sim_s = _decode_layers(
        x_sample.reshape(dec, d), mod_all[:, bsz:], states, w, tabs)

    def stk(k):
        return jnp.stack([o[k] for o in outs_p])

    return (xp, xs.reshape(dec, 1, d),
            stk(0), sret_s, stk(1), hl_s,
            stk(2), cv_s.reshape(depth, dec, CONV_WIDTH - 1, LRU_WIDTH),
            stk(3), sre_s.reshape(depth, dec, S5_GROUPS, S5_STATE),
            stk(4), sim_s.reshape(depth, dec, S5_GROUPS, S5_STATE))
```

```python
import functools
import math

import jax
import jax.numpy as jnp
import numpy as np
from jax import lax
from jax.experimental import pallas as pl
from jax.experimental.pallas import tpu as pltpu

RET_HEADS = 4
HEAD_DIM = 128
RET_WIDTH = RET_HEADS * HEAD_DIM
RET_CHUNK = 128
LRU_WIDTH = 256
LRU_BLOCKS = 4
LRU_C = 8.0
CONV_WIDTH = 4
S5_WIDTH = 256
S5_GROUP = 16
S5_GROUPS = 16
S5_STATE = 64
S5_FLAT = S5_GROUPS * S5_STATE
ROPE_BASE = 10000.0
EPS = 1e-6
PAST_LEN = 16384

OFF_Q = 0
OFF_K = OFF_Q + RET_WIDTH
OFF_V = OFF_K + RET_WIDTH
OFF_GR = OFF_V + RET_WIDTH
OFF_XL = OFF_GR + RET_WIDTH
OFF_GL = OFF_XL + LRU_WIDTH
OFF_U = OFF_GL + LRU_WIDTH
OFF_GS = OFF_U + S5_WIDTH
D_IN = OFF_GS + S5_WIDTH

SUBLANES = 8
LANES = 128
SEQ_TILE = 512
PROJ_CHUNK = 256
DEC_BLOCK = 8
VMEM_LIMIT = 56 * 1024 * 1024

F32 = jnp.float32
BF16 = jnp.bfloat16


def _sigmoid(x):
    return 1.0 / (1.0 + jnp.exp(-x))


def _silu(x):
    return x * _sigmoid(x)


def _gelu_tanh(x):
    c = math.sqrt(2.0 / math.pi)
    return x * (0.5 * (1.0 + jnp.tanh(c * (x + 0.044715 * (x * x * x)))))


def _softplus(x):
    return jnp.maximum(x, 0.0) + jnp.log1p(jnp.exp(-jnp.abs(x)))


def _rmsnorm(x, g):
    ms = jnp.mean(x * x, axis=-1, keepdims=True)
    return x * lax.rsqrt(ms + EPS) * g


def _dot(a, b):
    return jnp.dot(a.astype(BF16), b.astype(BF16), preferred_element_type=F32)


def _dot_nt(a, b):
    return lax.dot_general(a.astype(BF16), b.astype(BF16), (((1,), (1,)), ((), ())),
                           preferred_element_type=F32)


def _rotary(x, cosf, sinf):
    return x * cosf + pltpu.roll(x, HEAD_DIM // 2, axis=1) * sinf


def _groupnorm(o, g):
    mu = jnp.mean(o, axis=-1, keepdims=True)
    d = o - mu
    var = jnp.mean(d * d, axis=-1, keepdims=True)
    return d * lax.rsqrt(var + EPS) * g


def _lru_coeffs(xc, gates, sp):
    r = _sigmoid(gates[:, :LRU_WIDTH])
    ig = _sigmoid(gates[:, LRU_WIDTH:])
    log_a = (-LRU_C) * r * sp
    a = jnp.exp(log_a)
    th = jnp.tanh(log_a)
    mult = jnp.sqrt(-2.0 * th / (1.0 - th))
    return a, mult * ig * xc


def _s5_tail(sr_si, u, gs, cd, s5d, wglu):
    ys = _dot(sr_si, cd) + s5d * u
    ys = _gelu_tanh(ys)
    ys = ys * _sigmoid(_dot(ys, wglu))
    return ys * _silu(gs)


def _s5_prep_kernel(are_ref, aim_ref, ldt_ref, bre_ref, bim_ref,
                    apr_ref, api_ref, bbr_ref, bbi_ref):
    depth = are_ref.shape[0]
    for l in range(depth):
        a_re = are_ref[l]
        a_im = aim_ref[l]
        step = jnp.exp(ldt_ref[l])
        mag = jnp.exp(step * a_re)
        abar_r = mag * jnp.cos(step * a_im)
        abar_i = mag * jnp.sin(step * a_im)
        nr, ni = abar_r - 1.0, abar_i
        den = a_re * a_re + a_im * a_im
        fr = (nr * a_re + ni * a_im) / den
        fi = (ni * a_re - nr * a_im) / den
        for g in range(S5_GROUPS):
            frg, fig = fr[g:g + 1, :], fi[g:g + 1, :]
            b_re, b_im = bre_ref[l, g], bim_ref[l, g]
            bbr_ref[l, g] = frg * b_re - fig * b_im
            bbi_ref[l, g] = frg * b_im + fig * b_re
        pr, pi = abar_r, abar_i
        apr_ref[l, 0] = pr
        api_ref[l, 0] = pi
        for m in range(1, apr_ref.shape[1]):
            pr, pi = pr * abar_r - pi * abar_i, pr * abar_i + pi * abar_r
            apr_ref[l, m] = pr
            api_ref[l, m] = pi


def _s5_prep(s5_a_re, s5_a_im, s5_log_dt, s5_b_re, s5_b_im, n_pow):
    depth = s5_a_re.shape[0]
    b_re_t = jnp.swapaxes(s5_b_re, 2, 3)
    b_im_t = jnp.swapaxes(s5_b_im, 2, 3)
    ldt = s5_log_dt.reshape(depth, S5_GROUPS, 1)
    out_shape = (
        jax.ShapeDtypeStruct((depth, n_pow, S5_GROUPS, S5_STATE), F32),
        jax.ShapeDtypeStruct((depth, n_pow, S5_GROUPS, S5_STATE), F32),
        jax.ShapeDtypeStruct((depth, S5_GROUPS, S5_GROUP, S5_STATE), F32),
        jax.ShapeDtypeStruct((depth, S5_GROUPS, S5_GROUP, S5_STATE), F32),
    )
    return pl.pallas_call(_s5_prep_kernel, out_shape=out_shape, name="s5_prep")(
        s5_a_re, s5_a_im, ldt, b_re_t, b_im_t)


def _ada_kernel(c_ref, w_ref, b_ref, o_ref):
    s = _silu(c_ref[...])
    o_ref[0] = _dot(s, w_ref[0]) + b_ref[0]


def _ada(c_all, w_ada, b_ada):
    depth, d, n3 = w_ada.shape
    rows = c_all.shape[0]
    tn = d
    return pl.pallas_call(
        _ada_kernel,
        grid=(depth, n3 // tn),
        in_specs=[
            pl.BlockSpec((rows, d), lambda l, j: (0, 0)),
            pl.BlockSpec((1, d, tn), lambda l, j: (l, 0, j)),
            pl.BlockSpec((1, 1, tn), lambda l, j: (l, 0, j)),
        ],
        out_specs=pl.BlockSpec((1, rows, tn), lambda l, j: (l, 0, j)),
        out_shape=jax.ShapeDtypeStruct((depth, rows, n3), F32),
        compiler_params=pltpu.CompilerParams(
            dimension_semantics=("arbitrary", "arbitrary"), vmem_limit_bytes=VMEM_LIMIT),
        name="adaln",
    )(c_all, w_ada, b_ada.reshape(depth, 1, n3))


def _prompt_kernel(x_ref, mod_ref, ng_ref, win_ref, wout_ref,
                   cq_ref, sq_ref, ck_ref, sk_ref, intra_ref, qdec_ref, kdec_ref, cdec_ref,
                   gng_ref, cw_ref, cb_ref, wg_ref, bg_ref, lam_ref,
                   bd_ref, cd_ref, apr_ref, api_ref, s5d_ref, wglu_ref, fg_ref,
                   y_ref, sret_ref, hl_ref, cv_ref, sre_ref, sim_ref,
                   proj_scr, ycat_scr, perm_scr, xpad_scr, ab_scr, st_scr, st16_scr, yp_scr, h_scr,
                   ph_scr, hloc_scr, *, final):
    tl = x_ref.shape[1]
    seg = tl // SUBLANES
    t = pl.program_id(1)
    (ng_ref, win_ref, wout_ref, gng_ref, cw_ref, cb_ref, wg_ref, bg_ref, lam_ref,
     bd_ref, cd_ref, apr_ref, api_ref, s5d_ref, wglu_ref) = (
        r.at[0] for r in (ng_ref, win_ref, wout_ref, gng_ref, cw_ref, cb_ref, wg_ref, bg_ref,
                          lam_ref, bd_ref, cd_ref, apr_ref, api_ref, s5d_ref, wglu_ref))

    @pl.when(t == 0)
    def _():
        sret_ref[...] = jnp.zeros_like(sret_ref)
        hl_ref[...] = jnp.zeros_like(hl_ref)
        sre_ref[...] = jnp.zeros_like(sre_ref)
        sim_ref[...] = jnp.zeros_like(sim_ref)
        cv_ref[...] = jnp.zeros_like(cv_ref)

    x = x_ref[0]
    shift, scale = mod_ref[0, 0:1, :], mod_ref[0, 1:2, :]
    h = _rmsnorm(x, ng_ref[...] * (1.0 + scale)) + shift
    h_scr[...] = h.astype(BF16)
    proj_b = jnp.dot(h_scr[...], win_ref[:, pl.ds(OFF_XL, D_IN - OFF_XL)],
                     preferred_element_type=F32)

    def proj_a_chunk(ci):
        cols = pl.ds(ci * PROJ_CHUNK, PROJ_CHUNK)
        proj_scr[:, cols] = jnp.dot(h_scr[...], win_ref[:, cols], preferred_element_type=F32)

    for m in range(tl // SUBLANES):
        s, r0 = divmod(m * SUBLANES, seg)
        for kb in range((D_IN - OFF_XL) // LANES):
            perm_scr[kb, pl.ds(r0 * SUBLANES + s, SUBLANES, stride=SUBLANES), :] = (
                proj_b[m * SUBLANES:(m + 1) * SUBLANES, kb * LANES:(kb + 1) * LANES])

    def ret_piece(c, hd):
        rows = pl.ds(c * RET_CHUNK, RET_CHUNK)
        lanes = pl.ds(hd * HEAD_DIM, HEAD_DIM)
        q = _rotary(proj_scr[rows, pl.ds(OFF_Q + hd * HEAD_DIM, HEAD_DIM)],
                    cq_ref[rows, :], sq_ref[rows, :])
        k = _rotary(proj_scr[rows, pl.ds(OFF_K + hd * HEAD_DIM, HEAD_DIM)],
                    ck_ref[rows, :], sk_ref[rows, :])
        v = proj_scr[rows, pl.ds(OFF_V + hd * HEAD_DIM, HEAD_DIM)].astype(BF16)
        s_old = sret_ref[0, hd]
        sc = _dot_nt(q, k) * intra_ref[hd]
        o = _dot(sc, v) + _dot(q * qdec_ref[hd], s_old)
        kd_t = (k * kdec_ref[hd]).T
        sret_ref[0, hd] = cdec_ref[hd] * s_old + _dot(kd_t, v)
        on = _groupnorm(o, gng_ref[:, lanes])
        g_ret = proj_scr[rows, pl.ds(OFF_GR + hd * HEAD_DIM, HEAD_DIM)]
        ycat_scr[rows, lanes] = (on * _silu(g_ret)).astype(BF16)

    sub = lax.broadcasted_iota(jnp.int32, (SUBLANES, 1), 0)
    rg = lambda r: pl.ds(r * SUBLANES, SUBLANES)

    def pcols(lo, width):
        return jnp.concatenate([perm_scr[kb] for kb in range(lo // LANES, (lo + width) // LANES)],
                               axis=1)

    n_chunks = OFF_XL // PROJ_CHUNK
    s5_parts = n_chunks - 3

    proj_a_chunk(0)
    xl = pcols(OFF_XL - OFF_XL, LRU_WIDTH)
    prev = cv_ref[0]
    npre = CONV_WIDTH - 1
    for m in range(1, CONV_WIDTH):
        grp = xl[(seg - m) * SUBLANES:(seg - m + 1) * SUBLANES, :]
        e_m = jnp.where(sub == 0, prev[npre - m:npre - m + 1, :], pltpu.roll(grp, 1, axis=0))
        xpad_scr[rg(npre - m), :] = e_m
    xpad_scr[pl.ds(npre * SUBLANES, tl), :] = xl
    cv_ref[0] = jnp.concatenate(
        [xl[(seg - m) * SUBLANES + SUBLANES - 1:(seg - m + 1) * SUBLANES, :]
         for m in range(npre, 0, -1)], axis=0)
    xc = cb_ref[...]
    for j in range(CONV_WIDTH):
        xc = xc + xpad_scr[pl.ds(j * SUBLANES, tl), :] * cw_ref[pl.ds(j, 1), :]
    gates = _dot(xc, wg_ref[...]) + bg_ref[...]
    a, b = _lru_coeffs(xc, gates, _softplus(-lam_ref[...]))
    a_l, b_l = pl.ds(0, LRU_WIDTH), pl.ds(LRU_WIDTH, LRU_WIDTH)
    ab_scr[:, a_l] = a
    ab_scr[:, b_l] = b
    proj_a_chunk(1)
    hloc = jnp.zeros((SUBLANES, LRU_WIDTH), F32)
    prod = jnp.ones((SUBLANES, LRU_WIDTH), F32)
    for r in range(seg):
        ar = ab_scr[rg(r), a_l]
        hloc = ar * hloc + ab_scr[rg(r), b_l]
        prod = prod * ar
        ph_scr[rg(r), a_l] = prod
        ph_scr[rg(r), b_l] = hloc
    c = hl_ref[0]
    cs = [c]
    for s in range(SUBLANES - 1):
        c = hloc[s:s + 1, :] + prod[s:s + 1, :] * c
        cs.append(c)
    hl_ref[0] = hloc[SUBLANES - 1:, :] + prod[SUBLANES - 1:, :] * c
    c_all = jnp.concatenate(cs, axis=0)
    sgl = _silu(pcols(OFF_GL - OFF_XL, LRU_WIDTH))
    for r in range(seg):
        y = (ph_scr[rg(r), b_l] + ph_scr[rg(r), a_l] * c_all) * sgl[r * SUBLANES:(r + 1) * SUBLANES, :]
        for kb in range(LRU_WIDTH // LANES):
            yp_scr[kb, rg(r), :] = y[:, kb * LANES:(kb + 1) * LANES]

    proj_a_chunk(2)
    u = pcols(OFF_U - OFF_XL, S5_WIDTH)
    st_scr[...] = _dot(u, bd_ref[...])
    re_l, im_l = pl.ds(0, S5_FLAT), pl.ds(S5_FLAT, S5_FLAT)
    bshape = (SUBLANES, S5_FLAT)
    a1r = jnp.broadcast_to(apr_ref[pl.ds(0, 1), :], bshape)
    a1i = jnp.broadcast_to(api_ref[pl.ds(0, 1), :], bshape)
    hr = jnp.zeros((SUBLANES, S5_FLAT), F32)
    hi = jnp.zeros((SUBLANES, S5_FLAT), F32)
    for r in range(seg):
        if r % (-(-seg // s5_parts)) == 0:
            proj_a_chunk(3 + r // (-(-seg // s5_parts)))
        hr, hi = (a1r * hr - a1i * hi + st_scr[rg(r), re_l],
                  a1r * hi + a1i * hr + st_scr[rg(r), im_l])
        hloc_scr[rg(r), re_l] = hr
        hloc_scr[rg(r), im_l] = hi
    asr, asi = apr_ref[pl.ds(seg - 1, 1), :], api_ref[pl.ds(seg - 1, 1), :]
    cr, ci = sre_ref[0], sim_ref[0]
    crs, cis = [cr], [ci]
    for s in range(SUBLANES - 1):
        cr, ci = (hr[s:s + 1, :] + (asr * cr - asi * ci), hi[s:s + 1, :] + (asr * ci + asi * cr))
        crs.append(cr)
        cis.append(ci)
    sre_ref[0] = hr[SUBLANES - 1:, :] + (asr * cr - asi * ci)
    sim_ref[0] = hi[SUBLANES - 1:, :] + (asr * ci + asi * cr)
    cr_all, ci_all = jnp.concatenate(crs, axis=0), jnp.concatenate(cis, axis=0)
    n_pieces = (tl // RET_CHUNK) * RET_HEADS
    per_piece = (seg // 2) // n_pieces
    for r2 in range(seg // 2):
        if r2 % per_piece == 0:
            ret_piece(*divmod(r2 // per_piece, RET_HEADS))
        parts_r, parts_i = [], []
        for r in (2 * r2, 2 * r2 + 1):
            qr = jnp.broadcast_to(apr_ref[pl.ds(r, 1), :], bshape)
            qi = jnp.broadcast_to(api_ref[pl.ds(r, 1), :], bshape)
            parts_r.append(hloc_scr[rg(r), re_l] + (qr * cr_all - qi * ci_all))
            parts_i.append(hloc_scr[rg(r), im_l] + (qr * ci_all + qi * cr_all))
        rows2 = pl.ds(r2 * 2 * SUBLANES, 2 * SUBLANES)
        st16_scr[rows2, re_l] = jnp.concatenate(parts_r, axis=0).astype(BF16)
        st16_scr[rows2, im_l] = jnp.concatenate(parts_i, axis=0).astype(BF16)
    gs = pcols(OFF_GS - OFF_XL, S5_WIDTH)
    y_s5 = _s5_tail(st16_scr[...], u, gs, cd_ref[...], s5d_ref[...], wglu_ref[...])
    for kb in range(S5_WIDTH // LANES):
        yp_scr[LRU_WIDTH // LANES + kb] = y_s5[:, kb * LANES:(kb + 1) * LANES]

    for m2 in range(tl // (2 * SUBLANES)):
        for kb in range((LRU_WIDTH + S5_WIDTH) // LANES):
            halves = []
            for m in (2 * m2, 2 * m2 + 1):
                s, r0 = divmod(m * SUBLANES, seg)
                halves.append(yp_scr[kb, pl.ds(r0 * SUBLANES + s, SUBLANES, stride=SUBLANES), :])
            ycat_scr[pl.ds(m2 * 2 * SUBLANES, 2 * SUBLANES), pl.ds(RET_WIDTH + kb * LANES, LANES)] = (
                jnp.concatenate(halves, axis=0).astype(BF16))

    out = jnp.dot(ycat_scr[...], wout_ref[...], preferred_element_type=F32)
    x_new = x_ref[0] + mod_ref[0, 2:3, :] * out
    if final:
        x_new = _rmsnorm(x_new, fg_ref[...])
    y_ref[0] = x_new


def _prompt_layer(x, mod, w, tabs, layer, *, final):
    bsz, seq, d = x.shape
    tl = min(SEQ_TILE, seq)
    nt = seq // tl

    def full(a):
        return pl.BlockSpec(a.shape, lambda b, t: (0,) * a.ndim)

    def lay(a):
        return pl.BlockSpec((1,) + a.shape[1:], lambda b, t: (layer,) + (0,) * (a.ndim - 1))

    row_tab = pl.BlockSpec((tl, HEAD_DIM), lambda b, t: (t, 0))
    in_specs = [
        pl.BlockSpec((1, tl, d), lambda b, t: (b, t, 0)),
        pl.BlockSpec((1, 3, d), lambda b, t: (b, 0, 0)),
        lay(w["norm_g"]), lay(w["w_in"]), lay(w["w_out"]),
        row_tab, row_tab, row_tab, row_tab,
        full(tabs["intra"]), full(tabs["q_dec"]), full(tabs["k_dec"]),
        pl.BlockSpec(memory_space=pltpu.SMEM),
        lay(w["gn_g"]), lay(w["conv_w"]), lay(w["conv_b"]), lay(w["w_gate"]), lay(w["b_gate"]),
        lay(w["lam"]), lay(w["bd"]), lay(w["cd"]), lay(w["apr"]), lay(w["api"]),
        lay(w["s5_d"]), lay(w["w_glu"]), full(w["final_g"]),
    ]
    out_shape = (
        jax.ShapeDtypeStruct((bsz, seq, d), F32),
        jax.ShapeDtypeStruct((bsz, RET_HEADS, HEAD_DIM, HEAD_DIM), F32),
        jax.ShapeDtypeStruct((bsz, 1, LRU_WIDTH), F32),
        jax.ShapeDtypeStruct((bsz, CONV_WIDTH - 1, LRU_WIDTH), F32),
        jax.ShapeDtypeStruct((bsz, 1, S5_FLAT), F32),
        jax.ShapeDtypeStruct((bsz, 1, S5_FLAT), F32),
    )
    out_specs = (
        pl.BlockSpec((1, tl, d), lambda b, t: (b, t, 0)),
        pl.BlockSpec((1, RET_HEADS, HEAD_DIM, HEAD_DIM), lambda b, t: (b, 0, 0, 0)),
        pl.BlockSpec((1, 1, LRU_WIDTH), lambda b, t: (b, 0, 0)),
        pl.BlockSpec((1, CONV_WIDTH - 1, LRU_WIDTH), lambda b, t: (b, 0, 0)),
        pl.BlockSpec((1, 1, S5_FLAT), lambda b, t: (b, 0, 0)),
        pl.BlockSpec((1, 1, S5_FLAT), lambda b, t: (b, 0, 0)),
    )
    scratch = [
        pltpu.VMEM((tl, OFF_XL), F32),
        pltpu.VMEM((tl, d), BF16),
        pltpu.VMEM(((D_IN - OFF_XL) // LANES, tl, LANES), F32),
        pltpu.VMEM((tl + (CONV_WIDTH - 1) * SUBLANES, LRU_WIDTH), F32),
        pltpu.VMEM((tl, 2 * LRU_WIDTH), F32),
        pltpu.VMEM((tl, 2 * S5_FLAT), F32),
        pltpu.VMEM((tl, 2 * S5_FLAT), BF16),
        pltpu.VMEM(((LRU_WIDTH + S5_WIDTH) // LANES, tl, LANES), F32),
        pltpu.VMEM((tl, d), BF16),
        pltpu.VMEM((tl, 2 * LRU_WIDTH), F32),
        pltpu.VMEM((tl, 2 * S5_FLAT), F32),
    ]
    return pl.pallas_call(
        functools.partial(_prompt_kernel, final=final),
        grid=(bsz, nt),
        in_specs=in_specs,
        out_specs=out_specs,
        out_shape=out_shape,
        scratch_shapes=scratch,
        compiler_params=pltpu.CompilerParams(
            dimension_semantics=("arbitrary", "arbitrary"), vmem_limit_bytes=VMEM_LIMIT),
        name="prompt_layer",
    )(x, mod, w["norm_g"], w["w_in"], w["w_out"],
      tabs["cos_q"], tabs["sin_q"], tabs["cos_k"], tabs["sin_k"],
      tabs["intra"], tabs["q_dec"], tabs["k_dec"], tabs["c_dec"],
      w["gn_g"], w["conv_w"], w["conv_b"], w["w_gate"], w["b_gate"], w["lam"],
      w["bd"], w["cd"], w["apr"], w["api"], w["s5_d"], w["w_glu"], w["final_g"])


def _decode_kernel(x_ref, mod_ref, ng_ref, win_ref, wout_ref,
                   cq_ref, sq_ref, ck_ref, sk_ref, gam_ref,
                   gng_ref, cw_ref, cb_ref, wg_ref, bg_ref, lam_ref,
                   bd_ref, cd_ref, apr_ref, api_ref, s5d_ref, wglu_ref, fg_ref,
                   sret_in, hl_in, cv_in, sre_in, sim_in, sel_ref,
                   y_ref, sret_ref, hl_ref, cv_ref, sre_ref, sim_ref,
                   qkt_scr, v_scr, o_scr, gr_scr, ycat_scr, x_scr):
    layer = pl.program_id(0)
    i = pl.program_id(1)
    nb = pl.num_programs(1)
    rows_n, d = x_ref.shape
    (mod_ref, ng_ref, win_ref, wout_ref, gng_ref, cw_ref, cb_ref, wg_ref, bg_ref, lam_ref,
     bd_ref, cd_ref, apr_ref, api_ref, s5d_ref, wglu_ref,
     sret_in, hl_in, cv_in, sre_in, sim_in, sret_ref, hl_ref, cv_ref, sre_ref, sim_ref) = (
        r.at[0] for r in (mod_ref, ng_ref, win_ref, wout_ref, gng_ref, cw_ref, cb_ref, wg_ref,
                          bg_ref, lam_ref, bd_ref, cd_ref, apr_ref, api_ref, s5d_ref, wglu_ref,
                          sret_in, hl_in, cv_in, sre_in, sim_in,
                          sret_ref, hl_ref, cv_ref, sre_ref, sim_ref))

    @pl.when((i == 0) & (layer == 0))
    def _():
        x_scr[...] = x_ref[...]

    @pl.when(i == 0)
    def _():
        x = x_scr[...]
        shift, scale = mod_ref[:, pl.ds(0, d)], mod_ref[:, pl.ds(d, d)]
        h = _rmsnorm(x, ng_ref[...]) * (1.0 + scale) + shift
        proj = _dot(h, win_ref[...])
        cq, sq, ck, sk = cq_ref[...], sq_ref[...], ck_ref[...], sk_ref[...]
        for hd in range(RET_HEADS):
            lo = hd * HEAD_DIM
            for part, src in enumerate((_rotary(proj[:, OFF_Q + lo:OFF_Q + lo + HEAD_DIM], cq, sq),
                                        _rotary(proj[:, OFF_K + lo:OFF_K + lo + HEAD_DIM], ck, sk))):
                xt = src.T
                hi = xt.astype(BF16)
                trows = pl.ds((2 * hd + part) * HEAD_DIM, HEAD_DIM)
                qkt_scr[trows, pl.ds(0, rows_n)] = hi
                qkt_scr[trows, pl.ds(rows_n, rows_n)] = (xt - hi.astype(F32)).astype(BF16)
        v_scr[...] = proj[:, OFF_V:OFF_V + RET_WIDTH]
        gr_scr[...] = _silu(proj[:, OFF_GR:OFF_GR + RET_WIDTH])
        xl = proj[:, OFF_XL:OFF_XL + LRU_WIDTH]
        cs = cv_in[...]
        xc = cb_ref[...] + xl * cw_ref[pl.ds(CONV_WIDTH - 1, 1), :]
        for j in range(CONV_WIDTH - 1):
            xc = xc + cs[:, j * LRU_WIDTH:(j + 1) * LRU_WIDTH] * cw_ref[pl.ds(j, 1), :]
        cv_ref[:, pl.ds(0, 2 * LRU_WIDTH)] = cs[:, LRU_WIDTH:]
        cv_ref[:, pl.ds(2 * LRU_WIDTH, LRU_WIDTH)] = xl
        gates = _dot(xc, wg_ref[...]) + bg_ref[...]
        a, b = _lru_coeffs(xc, gates, _softplus(-lam_ref[...]))
        hh = b + a * hl_in[...]
        hl_ref[...] = hh
        ycat_scr[:, pl.ds(RET_WIDTH, LRU_WIDTH)] = (
            hh * _silu(proj[:, OFF_GL:OFF_GL + LRU_WIDTH])).astype(BF16)
        u = proj[:, OFF_U:OFF_U + S5_WIDTH]
        bu = _dot(u, bd_ref[...])
        ar, ai = apr_ref[pl.ds(0, 1), :], api_ref[pl.ds(0, 1), :]
        s0r, s0i = sre_in[...], sim_in[...]
        sr = bu[:, :S5_FLAT] + (ar * s0r - ai * s0i)
        si = bu[:, S5_FLAT:] + (ar * s0i + ai * s0r)
        sre_ref[...] = sr
        sim_ref[...] = si
        y_s5 = _s5_tail(jnp.concatenate([sr, si], axis=-1), u, proj[:, OFF_GS:OFF_GS + S5_WIDTH],
                        cd_ref[...], s5d_ref[...], wglu_ref[...])
        ycat_scr[:, pl.ds(RET_WIDTH + LRU_WIDTH, S5_WIDTH)] = y_s5.astype(BF16)

    blk_rows = pl.ds(pl.multiple_of(i * DEC_BLOCK, DEC_BLOCK), DEC_BLOCK)
    o_rows = [[] for _ in range(RET_HEADS)]
    vblks = [v_scr[blk_rows, pl.ds(hd * HEAD_DIM, HEAD_DIM)] for hd in range(RET_HEADS)]
    for jp in range(DEC_BLOCK // 2):
        sel = sel_ref[0, :, pl.ds(jp * 2 * HEAD_DIM, 2 * HEAD_DIM)]
        tiles = jnp.dot(qkt_scr[...], jnp.concatenate([sel, sel], axis=0),
                        preferred_element_type=F32)
        for hd in range(RET_HEADS):
            qb = tiles[2 * hd * HEAD_DIM:(2 * hd + 1) * HEAD_DIM, :]
            kb = tiles[(2 * hd + 1) * HEAD_DIM:(2 * hd + 2) * HEAD_DIM, :]
            for j in (2 * jp, 2 * jp + 1):
                cols = slice((j % 2) * HEAD_DIM, (j % 2 + 1) * HEAD_DIM)
                s_new = gam_ref[hd] * sret_in[j, hd] + kb[:, cols] * vblks[hd][j:j + 1, :]
                sret_ref[j, hd] = s_new
                o_rows[hd].append(jnp.sum(qb[:, cols] * s_new, axis=0, keepdims=True))
    for hd in range(RET_HEADS):
        o_scr[blk_rows, pl.ds(hd * HEAD_DIM, HEAD_DIM)] = jnp.concatenate(o_rows[hd], axis=0)

    @pl.when(i == nb - 1)
    def _():
        for hd in range(RET_HEADS):
            lanes = pl.ds(hd * HEAD_DIM, HEAD_DIM)
            on = _groupnorm(o_scr[:, lanes], gng_ref[:, lanes])
            ycat_scr[:, lanes] = (on * gr_scr[:, lanes]).astype(BF16)
        out = jnp.dot(ycat_scr[...], wout_ref[...], preferred_element_type=F32)
        x_new = x_scr[...] + mod_ref[:, pl.ds(2 * d, d)] * out
        x_scr[...] = x_new
        y_ref[...] = _rmsnorm(x_new, fg_ref[...])


def _decode_layers(x, mod, states, w, tabs):
    rows, d = x.shape
    depth = mod.shape[0]
    s_ret, s_h, s_conv, s_re, s_im = states
    nb = rows // DEC_BLOCK

    def full(a):
        return pl.BlockSpec(a.shape, lambda l, i: (0,) * a.ndim)

    def lay(a):
        return pl.BlockSpec((1,) + a.shape[1:], lambda l, i: (l,) + (0,) * (a.ndim - 1))

    smem = pl.BlockSpec(memory_space=pltpu.SMEM)
    sblk = pl.BlockSpec((1, DEC_BLOCK, RET_HEADS, HEAD_DIM, HEAD_DIM), lambda l, i: (l, i, 0, 0, 0))
    in_specs = [
        full(x), lay(mod), lay(w["norm_g"]), lay(w["w_in"]), lay(w["w_out"]),
        full(tabs["dcos_q"]), full(tabs["dsin_q"]), full(tabs["dcos_k"]), full(tabs["dsin_k"]), smem,
        lay(w["gn_g"]), lay(w["conv_w"]), lay(w["conv_b"]), lay(w["w_gate"]), lay(w["b_gate"]),
        lay(w["lam"]), lay(w["bd"]), lay(w["cd"]), lay(w["apr"]), lay(w["api"]),
        lay(w["s5_d"]), lay(w["w_glu"]), full(w["final_g"]),
        sblk, lay(s_h), lay(s_conv), lay(s_re), lay(s_im),
        pl.BlockSpec((1,) + tabs["row_select"].shape[1:], lambda l, i: (i, 0, 0)),
    ]
    out_shape = (
        jax.ShapeDtypeStruct((rows, d), F32),
        jax.ShapeDtypeStruct(s_ret.shape, F32),
        jax.ShapeDtypeStruct(s_h.shape, F32),
        jax.ShapeDtypeStruct(s_conv.shape, F32),
        jax.ShapeDtypeStruct(s_re.shape, F32),
        jax.ShapeDtypeStruct(s_im.shape, F32),
    )
    out_specs = (full(x), sblk, lay(s_h), lay(s_conv), lay(s_re), lay(s_im))
    scratch = [
        pltpu.VMEM((RET_HEADS * 2 * HEAD_DIM, 2 * rows), BF16),
        pltpu.VMEM((rows, RET_WIDTH), F32),
        pltpu.VMEM((rows, RET_WIDTH), F32),
        pltpu.VMEM((rows, RET_WIDTH), F32),
        pltpu.VMEM((rows, d), BF16),
        pltpu.VMEM((rows, d), F32),
    ]
    return pl.pallas_call(
        _decode_kernel,
        grid=(depth, nb),
        in_specs=in_specs,
        out_specs=out_specs,
        out_shape=out_shape,
        scratch_shapes=scratch,
        compiler_params=pltpu.CompilerParams(
            dimension_semantics=("arbitrary", "arbitrary"), vmem_limit_bytes=VMEM_LIMIT),
        name="decode_layers",
    )(x, mod, w["norm_g"], w["w_in"], w["w_out"],
      tabs["dcos_q"], tabs["dsin_q"], tabs["dcos_k"], tabs["dsin_k"], tabs["gamma"],
      w["gn_g"], w["conv_w"], w["conv_b"], w["w_gate"], w["b_gate"], w["lam"],
      w["bd"], w["cd"], w["apr"], w["api"], w["s5_d"], w["w_glu"], w["final_g"],
      s_ret, s_h, s_conv, s_re, s_im, tabs["row_select"])


def _rope_tables(pos, scale):
    half = HEAD_DIM // 2
    inv = ROPE_BASE ** (-jnp.arange(half, dtype=F32) / half)
    ang = pos[:, None] * inv[None, :]
    cos, sin = jnp.cos(ang), jnp.sin(ang)
    cosf = jnp.concatenate([cos, cos], axis=-1) * scale
    sinf = jnp.concatenate([-sin, sin], axis=-1) * scale
    return cosf, sinf


def _tables(seq, dec):
    c = RET_CHUNK
    log_g = jnp.log1p(-jnp.exp2(-5.0 - jnp.arange(RET_HEADS, dtype=F32)))
    idx = jnp.arange(c, dtype=F32)
    rel = idx[:, None] - idx[None, :]
    intra = jnp.where(rel[None] >= 0,
                      jnp.exp(jnp.maximum(rel, 0.0)[None] * log_g[:, None, None]), 0.0)
    q_dec = jnp.exp((idx + 1.0)[:, None] * log_g[None, :])
    k_dec = jnp.exp((c - 1.0 - idx)[:, None] * log_g[None, :])
    q_dec = jnp.broadcast_to(q_dec.T[:, :, None], (RET_HEADS, c, HEAD_DIM))
    k_dec = jnp.broadcast_to(k_dec.T[:, :, None], (RET_HEADS, c, HEAD_DIM))
    c_dec = jnp.exp(c * log_g)
    gamma = jnp.exp(log_g)
    k_scale = HEAD_DIM ** -0.5
    cos_q, sin_q = _rope_tables(jnp.arange(seq, dtype=F32), 1.0)
    cos_k, sin_k = _rope_tables(jnp.arange(seq, dtype=F32), k_scale)
    dpos = PAST_LEN + jnp.arange(1, dtype=F32)
    dcos_q, dsin_q = _rope_tables(dpos, 1.0)
    dcos_k, dsin_k = _rope_tables(dpos, k_scale)
    blk = jnp.arange(dec // DEC_BLOCK)[:, None, None]
    row = jnp.arange(dec)[None, :, None]
    col = jnp.arange(DEC_BLOCK * HEAD_DIM)[None, None, :]
    row_select = (row == blk * DEC_BLOCK + col // HEAD_DIM).astype(BF16)
    return dict(intra=intra, q_dec=q_dec, k_dec=k_dec, c_dec=c_dec, gamma=gamma,
                row_select=row_select,
                cos_q=cos_q, sin_q=sin_q, cos_k=cos_k, sin_k=sin_k,
                dcos_q=dcos_q, dsin_q=dsin_q, dcos_k=dcos_k, dsin_k=dsin_k)


def _block_diag(blocks):
    n, r, c = blocks.shape
    eye = jnp.eye(n, dtype=blocks.dtype)
    return jnp.einsum("nrc,nm->nrmc", blocks, eye).reshape(n * r, n * c)


def kernel(x_prompt, x_sample, state_ret, state_lru_h, state_lru_conv, state_s5_re, state_s5_im,
           c_prompt, c_sample, norm_g, w_ada, b_ada, w_in, ret_gn_g, conv_w, conv_b, w_rg, b_rg,
           w_ig, b_ig, lru_lambda, s5_a_re, s5_a_im, s5_b_re, s5_b_im, s5_c_re, s5_c_im, s5_d,
           s5_log_dt, s5_w_glu, w_out, final_g):
    depth = w_in.shape[0]
    bsz, seq, d = x_prompt.shape
    dec = x_sample.shape[0]
    assert x_sample.shape[1] == 1 and seq % RET_CHUNK == 0 and dec % DEC_BLOCK == 0

    tabs = _tables(seq, dec)
    mod_all = _ada(jnp.concatenate([c_prompt, c_sample], axis=0), w_ada, b_ada)
    n_pow = min(SEQ_TILE, seq) // SUBLANES
    apr, api, bbr, bbi = _s5_prep(s5_a_re, s5_a_im, s5_log_dt, s5_b_re, s5_b_im, n_pow)

    bdiag = jax.vmap(_block_diag)
    bd = jnp.concatenate([bdiag(bbr), bdiag(bbi)], axis=2)
    cd = jnp.concatenate([bdiag(jnp.swapaxes(s5_c_re, 2, 3)),
                          bdiag(jnp.swapaxes(-s5_c_im, 2, 3))], axis=1)
    w = dict(
        norm_g=norm_g.reshape(depth, 1, d),
        w_in=w_in.astype(BF16),
        w_out=w_out.astype(BF16),
        gn_g=ret_gn_g.reshape(depth, 1, RET_WIDTH),
        conv_w=conv_w,
        conv_b=conv_b.reshape(depth, 1, LRU_WIDTH),
        w_gate=jnp.concatenate([bdiag(w_rg), bdiag(w_ig)], axis=2).astype(BF16),
        b_gate=jnp.concatenate([b_rg, b_ig], axis=1).reshape(depth, 1, 2 * LRU_WIDTH),
        lam=lru_lambda.reshape(depth, 1, LRU_WIDTH),
        bd=bd.astype(BF16),
        cd=cd.astype(BF16),
        apr=apr.reshape(depth, n_pow, S5_FLAT),
        api=api.reshape(depth, n_pow, S5_FLAT),
        s5_d=s5_d.reshape(depth, 1, S5_WIDTH),
        w_glu=s5_w_glu.astype(BF16),
        final_g=final_g.reshape(1, d),
    )

    xp = x_prompt
    outs_p = []
    for l in range(depth):
        mod_p = mod_all[l, :bsz].reshape(bsz, 3, d)
        xp, sret, hl, cv, sre, sim = _prompt_layer(xp, mod_p, w, tabs, l, final=l == depth - 1)
        outs_p.append((sret, hl.reshape(bsz, LRU_WIDTH), cv,
                       sre.reshape(bsz, S5_GROUPS, S5_STATE), sim.reshape(bsz, S5_GROUPS, S5_STATE)))

    states = (state_ret, state_lru_h,
              state_lru_conv.reshape(depth, dec, (CONV_WIDTH - 1) * LRU_WIDTH),
              state_s5_re.reshape(depth, dec, S5_FLAT), state_s5_im.reshape(depth, dec, S5_FLAT))
    xs, sret_s, hl_s, cv_s, sre_s, sim_s = _decode_layers(
        x_sample.reshape(dec, d), mod_all[:, bsz:], states, w, tabs)

    def stk(k):
        return jnp.stack([o[k] for o in outs_p])

    return (xp, xs.reshape(dec, 1, d),
            stk(0), sret_s, stk(1), hl_s,
            stk(2), cv_s.reshape(depth, dec, CONV_WIDTH - 1, LRU_WIDTH),
            stk(3), sre_s.reshape(depth, dec, S5_GROUPS, S5_STATE),
            stk(4), sim_s.reshape(depth, dec, S5_GROUPS, S5_STATE))
```

```python
import functools
import math

import jax
import jax.numpy as jnp
import numpy as np
from jax import lax
from jax.experimental import pallas as pl
from jax.experimental.pallas import tpu as pltpu

RET_HEADS = 4
HEAD_DIM = 128
RET_WIDTH = RET_HEADS * HEAD_DIM
RET_CHUNK = 128
LRU_WIDTH = 256
LRU_BLOCKS = 4
LRU_C = 8.0
CONV_WIDTH = 4
S5_WIDTH = 256
S5_GROUP = 16
S5_GROUPS = 16
S5_STATE = 64
S5_FLAT = S5_GROUPS * S5_STATE
ROPE_BASE = 10000.0
EPS = 1e-6
PAST_LEN = 16384

OFF_Q = 0
OFF_K = OFF_Q + RET_WIDTH
OFF_V = OFF_K + RET_WIDTH
OFF_GR = OFF_V + RET_WIDTH
OFF_XL = OFF_GR + RET_WIDTH
OFF_GL = OFF_XL + LRU_WIDTH
OFF_U = OFF_GL + LRU_WIDTH
OFF_GS = OFF_U + S5_WIDTH
D_IN = OFF_GS + S5_WIDTH

SUBLANES = 8
LANES = 128
SEQ_TILE = 512
PROJ_CHUNK = 256
DEC_BLOCK = 16
VMEM_LIMIT = 56 * 1024 * 1024

F32 = jnp.float32
BF16 = jnp.bfloat16


def _sigmoid(x):
    return 1.0 / (1.0 + jnp.exp(-x))


def _silu(x):
    return x * _sigmoid(x)


def _gelu_tanh(x):
    c = math.sqrt(2.0 / math.pi)
    return x * (0.5 * (1.0 + jnp.tanh(c * (x + 0.044715 * (x * x * x)))))


def _softplus(x):
    return jnp.maximum(x, 0.0) + jnp.log1p(jnp.exp(-jnp.abs(x)))


def _rmsnorm(x, g):
    ms = jnp.mean(x * x, axis=-1, keepdims=True)
    return x * lax.rsqrt(ms + EPS) * g


def _dot(a, b):
    return jnp.dot(a.astype(BF16), b.astype(BF16), preferred_element_type=F32)


def _dot_nt(a, b):
    return lax.dot_general(a.astype(BF16), b.astype(BF16), (((1,), (1,)), ((), ())),
                           preferred_element_type=F32)


def _rotary(x, cosf, sinf):
    return x * cosf + pltpu.roll(x, HEAD_DIM // 2, axis=1) * sinf


def _groupnorm(o, g):
    mu = jnp.mean(o, axis=-1, keepdims=True)
    d = o - mu
    var = jnp.mean(d * d, axis=-1, keepdims=True)
    return d * lax.rsqrt(var + EPS) * g


def _lru_coeffs(xc, gates, sp):
    r = _sigmoid(gates[:, :LRU_WIDTH])
    ig = _sigmoid(gates[:, LRU_WIDTH:])
    log_a = (-LRU_C) * r * sp
    a = jnp.exp(log_a)
    th = jnp.tanh(log_a)
    mult = jnp.sqrt(-2.0 * th / (1.0 - th))
    return a, mult * ig * xc


def _s5_tail(sr_si, u, gs, cd, s5d, wglu):
    ys = _dot(sr_si, cd) + s5d * u
    ys = _gelu_tanh(ys)
    ys = ys * _sigmoid(_dot(ys, wglu))
    return ys * _silu(gs)


def _s5_prep_kernel(are_ref, aim_ref, ldt_ref, bre_ref, bim_ref,
                    apr_ref, api_ref, bbr_ref, bbi_ref):
    depth = are_ref.shape[0]
    for l in range(depth):
        a_re = are_ref[l]
        a_im = aim_ref[l]
        step = jnp.exp(ldt_ref[l])
        mag = jnp.exp(step * a_re)
        abar_r = mag * jnp.cos(step * a_im)
        abar_i = mag * jnp.sin(step * a_im)
        nr, ni = abar_r - 1.0, abar_i
        den = a_re * a_re + a_im * a_im
        fr = (nr * a_re + ni * a_im) / den
        fi = (ni * a_re - nr * a_im) / den
        for g in range(S5_GROUPS):
            frg, fig = fr[g:g + 1, :], fi[g:g + 1, :]
            b_re, b_im = bre_ref[l, g], bim_ref[l, g]
            bbr_ref[l, g] = frg * b_re - fig * b_im
            bbi_ref[l, g] = frg * b_im + fig * b_re
        pr, pi = abar_r, abar_i
        apr_ref[l, 0] = pr
        api_ref[l, 0] = pi
        for m in range(1, apr_ref.shape[1]):
            pr, pi = pr * abar_r - pi * abar_i, pr * abar_i + pi * abar_r
            apr_ref[l, m] = pr
            api_ref[l, m] = pi


def _s5_prep(s5_a_re, s5_a_im, s5_log_dt, s5_b_re, s5_b_im, n_pow):
    depth = s5_a_re.shape[0]
    b_re_t = jnp.swapaxes(s5_b_re, 2, 3)
    b_im_t = jnp.swapaxes(s5_b_im, 2, 3)
    ldt = s5_log_dt.reshape(depth, S5_GROUPS, 1)
    out_shape = (
        jax.ShapeDtypeStruct((depth, n_pow, S5_GROUPS, S5_STATE), F32),
        jax.ShapeDtypeStruct((depth, n_pow, S5_GROUPS, S5_STATE), F32),
        jax.ShapeDtypeStruct((depth, S5_GROUPS, S5_GROUP, S5_STATE), F32),
        jax.ShapeDtypeStruct((depth, S5_GROUPS, S5_GROUP, S5_STATE), F32),
    )
    return pl.pallas_call(_s5_prep_kernel, out_shape=out_shape, name="s5_prep")(
        s5_a_re, s5_a_im, ldt, b_re_t, b_im_t)


def _ada_kernel(c_ref, w_ref, b_ref, o_ref):
    s = _silu(c_ref[...])
    o_ref[0] = _dot(s, w_ref[0]) + b_ref[0]


def _ada(c_all, w_ada, b_ada):
    depth, d, n3 = w_ada.shape
    rows = c_all.shape[0]
    tn = d
    return pl.pallas_call(
        _ada_kernel,
        grid=(depth, n3 // tn),
        in_specs=[
            pl.BlockSpec((rows, d), lambda l, j: (0, 0)),
            pl.BlockSpec((1, d, tn), lambda l, j: (l, 0, j)),
            pl.BlockSpec((1, 1, tn), lambda l, j: (l, 0, j)),
        ],
        out_specs=pl.BlockSpec((1, rows, tn), lambda l, j: (l, 0, j)),
        out_shape=jax.ShapeDtypeStruct((depth, rows, n3), F32),
        compiler_params=pltpu.CompilerParams(
            dimension_semantics=("arbitrary", "arbitrary"), vmem_limit_bytes=VMEM_LIMIT),
        name="adaln",
    )(c_all, w_ada, b_ada.reshape(depth, 1, n3))


def _prompt_kernel(x_ref, mod_ref, ng_ref, win_ref, wout_ref,
                   cq_ref, sq_ref, ck_ref, sk_ref, intra_ref, qdec_ref, kdec_ref, cdec_ref,
                   gng_ref, cw_ref, cb_ref, wg_ref, bg_ref, lam_ref,
                   bd_ref, cd_ref, apr_ref, api_ref, s5d_ref, wglu_ref, fg_ref,
                   y_ref, sret_ref, hl_ref, cv_ref, sre_ref, sim_ref,
                   proj_scr, ycat_scr, perm_scr, xpad_scr, ab_scr, st_scr, st16_scr, yp_scr, h_scr,
                   ph_scr, hloc_scr, *, final):
    tl = x_ref.shape[1]
    seg = tl // SUBLANES
    t = pl.program_id(1)
    (ng_ref, win_ref, wout_ref, gng_ref, cw_ref, cb_ref, wg_ref, bg_ref, lam_ref,
     bd_ref, cd_ref, apr_ref, api_ref, s5d_ref, wglu_ref) = (
        r.at[0] for r in (ng_ref, win_ref, wout_ref, gng_ref, cw_ref, cb_ref, wg_ref, bg_ref,
                          lam_ref, bd_ref, cd_ref, apr_ref, api_ref, s5d_ref, wglu_ref))

    @pl.when(t == 0)
    def _():
        sret_ref[...] = jnp.zeros_like(sret_ref)
        hl_ref[...] = jnp.zeros_like(hl_ref)
        sre_ref[...] = jnp.zeros_like(sre_ref)
        sim_ref[...] = jnp.zeros_like(sim_ref)
        cv_ref[...] = jnp.zeros_like(cv_ref)

    x = x_ref[0]
    shift, scale = mod_ref[0, 0:1, :], mod_ref[0, 1:2, :]
    h = _rmsnorm(x, ng_ref[...] * (1.0 + scale)) + shift
    h_scr[...] = h.astype(BF16)
    proj_b = jnp.dot(h_scr[...], win_ref[:, pl.ds(OFF_XL, D_IN - OFF_XL)],
                     preferred_element_type=F32)

    def proj_a_chunk(ci):
        cols = pl.ds(ci * PROJ_CHUNK, PROJ_CHUNK)
        proj_scr[:, cols] = jnp.dot(h_scr[...], win_ref[:, cols], preferred_element_type=F32)

    for m in range(tl // SUBLANES):
        s, r0 = divmod(m * SUBLANES, seg)
        for kb in range((D_IN - OFF_XL) // LANES):
            perm_scr[kb, pl.ds(r0 * SUBLANES + s, SUBLANES, stride=SUBLANES), :] = (
                proj_b[m * SUBLANES:(m + 1) * SUBLANES, kb * LANES:(kb + 1) * LANES])

    def ret_piece(c, hd):
        rows = pl.ds(c * RET_CHUNK, RET_CHUNK)
        lanes = pl.ds(hd * HEAD_DIM, HEAD_DIM)
        q = _rotary(proj_scr[rows, pl.ds(OFF_Q + hd * HEAD_DIM, HEAD_DIM)],
                    cq_ref[rows, :], sq_ref[rows, :])
        k = _rotary(proj_scr[rows, pl.ds(OFF_K + hd * HEAD_DIM, HEAD_DIM)],
                    ck_ref[rows, :], sk_ref[rows, :])
        v = proj_scr[rows, pl.ds(OFF_V + hd * HEAD_DIM, HEAD_DIM)].astype(BF16)
        s_old = sret_ref[0, hd]
        sc = _dot_nt(q, k) * intra_ref[hd]
        o = _dot(sc, v) + _dot(q * qdec_ref[hd], s_old)
        kd_t = (k * kdec_ref[hd]).T
        sret_ref[0, hd] = cdec_ref[hd] * s_old + _dot(kd_t, v)
        on = _groupnorm(o, gng_ref[:, lanes])
        g_ret = proj_scr[rows, pl.ds(OFF_GR + hd * HEAD_DIM, HEAD_DIM)]
        ycat_scr[rows, lanes] = (on * _silu(g_ret)).astype(BF16)

    sub = lax.broadcasted_iota(jnp.int32, (SUBLANES, 1), 0)
    rg = lambda r: pl.ds(r * SUBLANES, SUBLANES)

    def pcols(lo, width):
        return jnp.concatenate([perm_scr[kb] for kb in range(lo // LANES, (lo + width) // LANES)],
                               axis=1)

    n_chunks = OFF_XL // PROJ_CHUNK
    s5_parts = n_chunks - 3

    proj_a_chunk(0)
    xl = pcols(OFF_XL - OFF_XL, LRU_WIDTH)
    prev = cv_ref[0]
    npre = CONV_WIDTH - 1
    for m in range(1, CONV_WIDTH):
        grp = xl[(seg - m) * SUBLANES:(seg - m + 1) * SUBLANES, :]
        e_m = jnp.where(sub == 0, prev[npre - m:npre - m + 1, :], pltpu.roll(grp, 1, axis=0))
        xpad_scr[rg(npre - m), :] = e_m
    xpad_scr[pl.ds(npre * SUBLANES, tl), :] = xl
    cv_ref[0] = jnp.concatenate(
        [xl[(seg - m) * SUBLANES + SUBLANES - 1:(seg - m + 1) * SUBLANES, :]
         for m in range(npre, 0, -1)], axis=0)
    xc = cb_ref[...]
    for j in range(CONV_WIDTH):
        xc = xc + xpad_scr[pl.ds(j * SUBLANES, tl), :] * cw_ref[pl.ds(j, 1), :]
    gates = _dot(xc, wg_ref[...]) + bg_ref[...]
    a, b = _lru_coeffs(xc, gates, _softplus(-lam_ref[...]))
    a_l, b_l = pl.ds(0, LRU_WIDTH), pl.ds(LRU_WIDTH, LRU_WIDTH)
    ab_scr[:, a_l] = a
    ab_scr[:, b_l] = b
    proj_a_chunk(1)
    hloc = jnp.zeros((SUBLANES, LRU_WIDTH), F32)
    prod = jnp.ones((SUBLANES, LRU_WIDTH), F32)
    for r in range(seg):
        ar = ab_scr[rg(r), a_l]
        hloc = ar * hloc + ab_scr[rg(r), b_l]
        prod = prod * ar
        ph_scr[rg(r), a_l] = prod
        ph_scr[rg(r), b_l] = hloc
    c = hl_ref[0]
    cs = [c]
    for s in range(SUBLANES - 1):
        c = hloc[s:s + 1, :] + prod[s:s + 1, :] * c
        cs.append(c)
    hl_ref[0] = hloc[SUBLANES - 1:, :] + prod[SUBLANES - 1:, :] * c
    c_all = jnp.concatenate(cs, axis=0)
    sgl = _silu(pcols(OFF_GL - OFF_XL, LRU_WIDTH))
    for r in range(seg):
        y = (ph_scr[rg(r), b_l] + ph_scr[rg(r), a_l] * c_all) * sgl[r * SUBLANES:(r + 1) * SUBLANES, :]
        for kb in range(LRU_WIDTH // LANES):
            yp_scr[kb, rg(r), :] = y[:, kb * LANES:(kb + 1) * LANES]

    proj_a_chunk(2)
    u = pcols(OFF_U - OFF_XL, S5_WIDTH)
    st_scr[...] = _dot(u, bd_ref[...])
    re_l, im_l = pl.ds(0, S5_FLAT), pl.ds(S5_FLAT, S5_FLAT)
    bshape = (SUBLANES, S5_FLAT)
    a1r = jnp.broadcast_to(apr_ref[pl.ds(0, 1), :], bshape)
    a1i = jnp.broadcast_to(api_ref[pl.ds(0, 1), :], bshape)
    hr = jnp.zeros((SUBLANES, S5_FLAT), F32)
    hi = jnp.zeros((SUBLANES, S5_FLAT), F32)
    for r in range(seg):
        if r % (-(-seg // s5_parts)) == 0:
            proj_a_chunk(3 + r // (-(-seg // s5_parts)))
        hr, hi = (a1r * hr - a1i * hi + st_scr[rg(r), re_l],
                  a1r * hi + a1i * hr + st_scr[rg(r), im_l])
        hloc_scr[rg(r), re_l] = hr
        hloc_scr[rg(r), im_l] = hi
    asr, asi = apr_ref[pl.ds(seg - 1, 1), :], api_ref[pl.ds(seg - 1, 1), :]
    cr, ci = sre_ref[0], sim_ref[0]
    crs, cis = [cr], [ci]
    for s in range(SUBLANES - 1):
        cr, ci = (hr[s:s + 1, :] + (asr * cr - asi * ci), hi[s:s + 1, :] + (asr * ci + asi * cr))
        crs.append(cr)
        cis.append(ci)
    sre_ref[0] = hr[SUBLANES - 1:, :] + (asr * cr - asi * ci)
    sim_ref[0] = hi[SUBLANES - 1:, :] + (asr * ci + asi * cr)
    cr_all, ci_all = jnp.concatenate(crs, axis=0), jnp.concatenate(cis, axis=0)
    n_pieces = (tl // RET_CHUNK) * RET_HEADS
    per_piece = (seg // 2) // n_pieces
    for r2 in range(seg // 2):
        if r2 % per_piece == 0:
            ret_piece(*divmod(r2 // per_piece, RET_HEADS))
        parts_r, parts_i = [], []
        for r in (2 * r2, 2 * r2 + 1):
            qr = jnp.broadcast_to(apr_ref[pl.ds(r, 1), :], bshape)
            qi = jnp.broadcast_to(api_ref[pl.ds(r, 1), :], bshape)
            parts_r.append(hloc_scr[rg(r), re_l] + (qr * cr_all - qi * ci_all))
            parts_i.append(hloc_scr[rg(r), im_l] + (qr * ci_all + qi * cr_all))
        rows2 = pl.ds(r2 * 2 * SUBLANES, 2 * SUBLANES)
        st16_scr[rows2, re_l] = jnp.concatenate(parts_r, axis=0).astype(BF16)
        st16_scr[rows2, im_l] = jnp.concatenate(parts_i, axis=0).astype(BF16)
    gs = pcols(OFF_GS - OFF_XL, S5_WIDTH)
    y_s5 = _s5_tail(st16_scr[...], u, gs, cd_ref[...], s5d_ref[...], wglu_ref[...])
    for kb in range(S5_WIDTH // LANES):
        yp_scr[LRU_WIDTH // LANES + kb] = y_s5[:, kb * LANES:(kb + 1) * LANES]

    for m2 in range(tl // (2 * SUBLANES)):
        for kb in range((LRU_WIDTH + S5_WIDTH) // LANES):
            halves = []
            for m in (2 * m2, 2 * m2 + 1):
                s, r0 = divmod(m * SUBLANES, seg)
                halves.append(yp_scr[kb, pl.ds(r0 * SUBLANES + s, SUBLANES, stride=SUBLANES), :])
            ycat_scr[pl.ds(m2 * 2 * SUBLANES, 2 * SUBLANES), pl.ds(RET_WIDTH + kb * LANES, LANES)] = (
                jnp.concatenate(halves, axis=0).astype(BF16))

    out = jnp.dot(ycat_scr[...], wout_ref[...], preferred_element_type=F32)
    x_new = x_ref[0] + mod_ref[0, 2:3, :] * out
    if final:
        x_new = _rmsnorm(x_new, fg_ref[...])
    y_ref[0] = x_new


def _prompt_layer(x, mod, w, tabs, layer, *, final):
    bsz, seq, d = x.shape
    tl = min(SEQ_TILE, seq)
    nt = seq // tl

    def full(a):
        return pl.BlockSpec(a.shape, lambda b, t: (0,) * a.ndim)

    def lay(a):
        return pl.BlockSpec((1,) + a.shape[1:], lambda b, t: (layer,) + (0,) * (a.ndim - 1))

    row_tab = pl.BlockSpec((tl, HEAD_DIM), lambda b, t: (t, 0))
    in_specs = [
        pl.BlockSpec((1, tl, d), lambda b, t: (b, t, 0)),
        pl.BlockSpec((1, 3, d), lambda b, t: (b, 0, 0)),
        lay(w["norm_g"]), lay(w["w_in"]), lay(w["w_out"]),
        row_tab, row_tab, row_tab, row_tab,
        full(tabs["intra"]), full(tabs["q_dec"]), full(tabs["k_dec"]),
        pl.BlockSpec(memory_space=pltpu.SMEM),
        lay(w["gn_g"]), lay(w["conv_w"]), lay(w["conv_b"]), lay(w["w_gate"]), lay(w["b_gate"]),
        lay(w["lam"]), lay(w["bd"]), lay(w["cd"]), lay(w["apr"]), lay(w["api"]),
        lay(w["s5_d"]), lay(w["w_glu"]), full(w["final_g"]),
    ]
    out_shape = (
        jax.ShapeDtypeStruct((bsz, seq, d), F32),
        jax.ShapeDtypeStruct((bsz, RET_HEADS, HEAD_DIM, HEAD_DIM), F32),
        jax.ShapeDtypeStruct((bsz, 1, LRU_WIDTH), F32),
        jax.ShapeDtypeStruct((bsz, CONV_WIDTH - 1, LRU_WIDTH), F32),
        jax.ShapeDtypeStruct((bsz, 1, S5_FLAT), F32),
        jax.ShapeDtypeStruct((bsz, 1, S5_FLAT), F32),
    )
    out_specs = (
        pl.BlockSpec((1, tl, d), lambda b, t: (b, t, 0)),
        pl.BlockSpec((1, RET_HEADS, HEAD_DIM, HEAD_DIM), lambda b, t: (b, 0, 0, 0)),
        pl.BlockSpec((1, 1, LRU_WIDTH), lambda b, t: (b, 0, 0)),
        pl.BlockSpec((1, CONV_WIDTH - 1, LRU_WIDTH), lambda b, t: (b, 0, 0)),
        pl.BlockSpec((1, 1, S5_FLAT), lambda b, t: (b, 0, 0)),
        pl.BlockSpec((1, 1, S5_FLAT), lambda b, t: (b, 0, 0)),
    )
    scratch = [
        pltpu.VMEM((tl, OFF_XL), F32),
        pltpu.VMEM((tl, d), BF16),
        pltpu.VMEM(((D_IN - OFF_XL) // LANES, tl, LANES), F32),
        pltpu.VMEM((tl + (CONV_WIDTH - 1) * SUBLANES, LRU_WIDTH), F32),
        pltpu.VMEM((tl, 2 * LRU_WIDTH), F32),
        pltpu.VMEM((tl, 2 * S5_FLAT), F32),
        pltpu.VMEM((tl, 2 * S5_FLAT), BF16),
        pltpu.VMEM(((LRU_WIDTH + S5_WIDTH) // LANES, tl, LANES), F32),
        pltpu.VMEM((tl, d), BF16),
        pltpu.VMEM((tl, 2 * LRU_WIDTH), F32),
        pltpu.VMEM((tl, 2 * S5_FLAT), F32),
    ]
    return pl.pallas_call(
        functools.partial(_prompt_kernel, final=final),
        grid=(bsz, nt),
        in_specs=in_specs,
        out_specs=out_specs,
        out_shape=out_shape,
        scratch_shapes=scratch,
        compiler_params=pltpu.CompilerParams(
            dimension_semantics=("arbitrary", "arbitrary"), vmem_limit_bytes=VMEM_LIMIT),
        name="prompt_layer",
    )(x, mod, w["norm_g"], w["w_in"], w["w_out"],
      tabs["cos_q"], tabs["sin_q"], tabs["cos_k"], tabs["sin_k"],
      tabs["intra"], tabs["q_dec"], tabs["k_dec"], tabs["c_dec"],
      w["gn_g"], w["conv_w"], w["conv_b"], w["w_gate"], w["b_gate"], w["lam"],
      w["bd"], w["cd"], w["apr"], w["api"], w["s5_d"], w["w_glu"], w["final_g"])


def _decode_kernel(x_ref, mod_ref, ng_ref, win_ref, wout_ref,
                   cq_ref, sq_ref, ck_ref, sk_ref, gam_ref,
                   gng_ref, cw_ref, cb_ref, wg_ref, bg_ref, lam_ref,
                   bd_ref, cd_ref, apr_ref, api_ref, s5d_ref, wglu_ref, fg_ref,
                   sret_in, hl_in, cv_in, sre_in, sim_in, sel_ref,
                   y_ref, sret_ref, hl_ref, cv_ref, sre_ref, sim_ref,
                   qkt_scr, v_scr, o_scr, gr_scr, ycat_scr, x_scr):
    layer = pl.program_id(0)
    i = pl.program_id(1)
    nb = pl.num_programs(1)
    rows_n, d = x_ref.shape
    (mod_ref, ng_ref, win_ref, wout_ref, gng_ref, cw_ref, cb_ref, wg_ref, bg_ref, lam_ref,
     bd_ref, cd_ref, apr_ref, api_ref, s5d_ref, wglu_ref,
     sret_in, hl_in, cv_in, sre_in, sim_in, sret_ref, hl_ref, cv_ref, sre_ref, sim_ref) = (
        r.at[0] for r in (mod_ref, ng_ref, win_ref, wout_ref, gng_ref, cw_ref, cb_ref, wg_ref,
                          bg_ref, lam_ref, bd_ref, cd_ref, apr_ref, api_ref, s5d_ref, wglu_ref,
                          sret_in, hl_in, cv_in, sre_in, sim_in,
                          sret_ref, hl_ref, cv_ref, sre_ref, sim_ref))

    @pl.when((i == 0) & (layer == 0))
    def _():
        x_scr[...] = x_ref[...]

    @pl.when(i == 0)
    def _():
        x = x_scr[...]
        shift, scale = mod_ref[:, pl.ds(0, d)], mod_ref[:, pl.ds(d, d)]
        h = _rmsnorm(x, ng_ref[...]) * (1.0 + scale) + shift
        proj = _dot(h, win_ref[...])
        cq, sq, ck, sk = cq_ref[...], sq_ref[...], ck_ref[...], sk_ref[...]
        for hd in range(RET_HEADS):
            lo = hd * HEAD_DIM
            for part, src in enumerate((_rotary(proj[:, OFF_Q + lo:OFF_Q + lo + HEAD_DIM], cq, sq),
                                        _rotary(proj[:, OFF_K + lo:OFF_K + lo + HEAD_DIM], ck, sk))):
                xt = src.T
                hi = xt.astype(BF16)
                trows = pl.ds((2 * hd + part) * HEAD_DIM, HEAD_DIM)
                qkt_scr[trows, pl.ds(0, rows_n)] = hi
                qkt_scr[trows, pl.ds(rows_n, rows_n)] = (xt - hi.astype(F32)).astype(BF16)
        v_scr[...] = proj[:, OFF_V:OFF_V + RET_WIDTH]
        gr_scr[...] = _silu(proj[:, OFF_GR:OFF_GR + RET_WIDTH])
        xl = proj[:, OFF_XL:OFF_XL + LRU_WIDTH]
        cs = cv_in[...]
        xc = cb_ref[...] + xl * cw_ref[pl.ds(CONV_WIDTH - 1, 1), :]
        for j in range(CONV_WIDTH - 1):
            xc = xc + cs[:, j * LRU_WIDTH:(j + 1) * LRU_WIDTH] * cw_ref[pl.ds(j, 1), :]
        cv_ref[:, pl.ds(0, 2 * LRU_WIDTH)] = cs[:, LRU_WIDTH:]
        cv_ref[:, pl.ds(2 * LRU_WIDTH, LRU_WIDTH)] = xl
        gates = _dot(xc, wg_ref[...]) + bg_ref[...]
        a, b = _lru_coeffs(xc, gates, _softplus(-lam_ref[...]))
        hh = b + a * hl_in[...]
        hl_ref[...] = hh
        ycat_scr[:, pl.ds(RET_WIDTH, LRU_WIDTH)] = (
            hh * _silu(proj[:, OFF_GL:OFF_GL + LRU_WIDTH])).astype(BF16)
        u = proj[:, OFF_U:OFF_U + S5_WIDTH]
        bu = _dot(u, bd_ref[...])
        ar, ai = apr_ref[pl.ds(0, 1), :], api_ref[pl.ds(0, 1), :]
        s0r, s0i = sre_in[...], sim_in[...]
        sr = bu[:, :S5_FLAT] + (ar * s0r - ai * s0i)
        si = bu[:, S5_FLAT:] + (ar * s0i + ai * s0r)
        sre_ref[...] = sr
        sim_ref[...] = si
        y_s5 = _s5_tail(jnp.concatenate([sr, si], axis=-1), u, proj[:, OFF_GS:OFF_GS + S5_WIDTH],
                        cd_ref[...], s5d_ref[...], wglu_ref[...])
        ycat_scr[:, pl.ds(RET_WIDTH + LRU_WIDTH, S5_WIDTH)] = y_s5.astype(BF16)

    blk_rows = pl.ds(pl.multiple_of(i * DEC_BLOCK, DEC_BLOCK), DEC_BLOCK)
    o_rows = [[] for _ in range(RET_HEADS)]
    vblks = [v_scr[blk_rows, pl.ds(hd * HEAD_DIM, HEAD_DIM)] for hd in range(RET_HEADS)]
    for jp in range(DEC_BLOCK // 2):
        sel = sel_ref[0, :, pl.ds(jp * 2 * HEAD_DIM, 2 * HEAD_DIM)]
        tiles = jnp.dot(qkt_scr[...], jnp.concatenate([sel, sel], axis=0),
                        preferred_element_type=F32)
        for hd in range(RET_HEADS):
            qb = tiles[2 * hd * HEAD_DIM:(2 * hd + 1) * HEAD_DIM, :]
            kb = tiles[(2 * hd + 1) * HEAD_DIM:(2 * hd + 2) * HEAD_DIM, :]
            for j in (2 * jp, 2 * jp + 1):
                cols = slice((j % 2) * HEAD_DIM, (j % 2 + 1) * HEAD_DIM)
                s_new = gam_ref[hd] * sret_in[j, hd] + kb[:, cols] * vblks[hd][j:j + 1, :]
                sret_ref[j, hd] = s_new
                o_rows[hd].append(jnp.sum(qb[:, cols] * s_new, axis=0, keepdims=True))
    for hd in range(RET_HEADS):
        o_scr[blk_rows, pl.ds(hd * HEAD_DIM, HEAD_DIM)] = jnp.concatenate(o_rows[hd], axis=0)

    @pl.when(i == nb - 1)
    def _():
        for hd in range(RET_HEADS):
            lanes = pl.ds(hd * HEAD_DIM, HEAD_DIM)
            on = _groupnorm(o_scr[:, lanes], gng_ref[:, lanes])
            ycat_scr[:, lanes] = (on * gr_scr[:, lanes]).astype(BF16)
        out = jnp.dot(ycat_scr[...], wout_ref[...], preferred_element_type=F32)
        x_new = x_scr[...] + mod_ref[:, pl.ds(2 * d, d)] * out
        x_scr[...] = x_new
        y_ref[...] = _rmsnorm(x_new, fg_ref[...])


def _decode_layers(x, mod, states, w, tabs):
    rows, d = x.shape
    depth = mod.shape[0]
    s_ret, s_h, s_conv, s_re, s_im = states
    nb = rows // DEC_BLOCK

    def full(a):
        return pl.BlockSpec(a.shape, lambda l, i: (0,) * a.ndim)

    def lay(a):
        return pl.BlockSpec((1,) + a.shape[1:], lambda l, i: (l,) + (0,) * (a.ndim - 1))

    smem = pl.BlockSpec(memory_space=pltpu.SMEM)
    sblk = pl.BlockSpec((1, DEC_BLOCK, RET_HEADS, HEAD_DIM, HEAD_DIM), lambda l, i: (l, i, 0, 0, 0))
    in_specs = [
        full(x), lay(mod), lay(w["norm_g"]), lay(w["w_in"]), lay(w["w_out"]),
        full(tabs["dcos_q"]), full(tabs["dsin_q"]), full(tabs["dcos_k"]), full(tabs["dsin_k"]), smem,
        lay(w["gn_g"]), lay(w["conv_w"]), lay(w["conv_b"]), lay(w["w_gate"]), lay(w["b_gate"]),
        lay(w["lam"]), lay(w["bd"]), lay(w["cd"]), lay(w["apr"]), lay(w["api"]),
        lay(w["s5_d"]), lay(w["w_glu"]), full(w["final_g"]),
        sblk, lay(s_h), lay(s_conv), lay(s_re), lay(s_im),
        pl.BlockSpec((1,) + tabs["row_select"].shape[1:], lambda l, i: (i, 0, 0)),
    ]
    out_shape = (
        jax.ShapeDtypeStruct((rows, d), F32),
        jax.ShapeDtypeStruct(s_ret.shape, F32),
        jax.ShapeDtypeStruct(s_h.shape, F32),
        jax.ShapeDtypeStruct(s_conv.shape, F32),
        jax.ShapeDtypeStruct(s_re.shape, F32),
        jax.ShapeDtypeStruct(s_im.shape, F32),
    )
    out_specs = (full(x), sblk, lay(s_h), lay(s_conv), lay(s_re), lay(s_im))
    scratch = [
        pltpu.VMEM((RET_HEADS * 2 * HEAD_DIM, 2 * rows), BF16),
        pltpu.VMEM((rows, RET_WIDTH), F32),
        pltpu.VMEM((rows, RET_WIDTH), F32),
        pltpu.VMEM((rows, RET_WIDTH), F32),
        pltpu.VMEM((rows, d), BF16),
        pltpu.VMEM((rows, d), F32),
    ]
    return pl.pallas_call(
        _decode_kernel,
        grid=(depth, nb),
        in_specs=in_specs,
        out_specs=out_specs,
        out_shape=out_shape,
        scratch_shapes=scratch,
        compiler_params=pltpu.CompilerParams(
            dimension_semantics=("arbitrary", "arbitrary"), vmem_limit_bytes=VMEM_LIMIT),
        name="decode_layers",
    )(x, mod, w["norm_g"], w["w_in"], w["w_out"],
      tabs["dcos_q"], tabs["dsin_q"], tabs["dcos_k"], tabs["dsin_k"], tabs["gamma"],
      w["gn_g"], w["conv_w"], w["conv_b"], w["w_gate"], w["b_gate"], w["lam"],
      w["bd"], w["cd"], w["apr"], w["api"], w["s5_d"], w["w_glu"], w["final_g"],
      s_ret, s_h, s_conv, s_re, s_im, tabs["row_select"])


def _rope_tables(pos, scale):
    half = HEAD_DIM // 2
    inv = ROPE_BASE ** (-jnp.arange(half, dtype=F32) / half)
    ang = pos[:, None] * inv[None, :]
    cos, sin = jnp.cos(ang), jnp.sin(ang)
    cosf = jnp.concatenate([cos, cos], axis=-1) * scale
    sinf = jnp.concatenate([-sin, sin], axis=-1) * scale
    return cosf, sinf


def _tables(seq, dec):
    c = RET_CHUNK
    log_g = jnp.log1p(-jnp.exp2(-5.0 - jnp.arange(RET_HEADS, dtype=F32)))
    idx = jnp.arange(c, dtype=F32)
    rel = idx[:, None] - idx[None, :]
    intra = jnp.where(rel[None] >= 0,
                      jnp.exp(jnp.maximum(rel, 0.0)[None] * log_g[:, None, None]), 0.0)
    q_dec = jnp.exp((idx + 1.0)[:, None] * log_g[None, :])
    k_dec = jnp.exp((c - 1.0 - idx)[:, None] * log_g[None, :])
    q_dec = jnp.broadcast_to(q_dec.T[:, :, None], (RET_HEADS, c, HEAD_DIM))
    k_dec = jnp.broadcast_to(k_dec.T[:, :, None], (RET_HEADS, c, HEAD_DIM))
    c_dec = jnp.exp(c * log_g)
    gamma = jnp.exp(log_g)
    k_scale = HEAD_DIM ** -0.5
    cos_q, sin_q = _rope_tables(jnp.arange(seq, dtype=F32), 1.0)
    cos_k, sin_k = _rope_tables(jnp.arange(seq, dtype=F32), k_scale)
    dpos = PAST_LEN + jnp.arange(1, dtype=F32)
    dcos_q, dsin_q = _rope_tables(dpos, 1.0)
    dcos_k, dsin_k = _rope_tables(dpos, k_scale)
    blk = jnp.arange(dec // DEC_BLOCK)[:, None, None]
    row = jnp.arange(dec)[None, :, None]
    col = jnp.arange(DEC_BLOCK * HEAD_DIM)[None, None, :]
    row_select = (row == blk * DEC_BLOCK + col // HEAD_DIM).astype(BF16)
    return dict(intra=intra, q_dec=q_dec, k_dec=k_dec, c_dec=c_dec, gamma=gamma,
                row_select=row_select,
                cos_q=cos_q, sin_q=sin_q, cos_k=cos_k, sin_k=sin_k,
                dcos_q=dcos_q, dsin_q=dsin_q, dcos_k=dcos_k, dsin_k=dsin_k)


def _block_diag(blocks):
    n, r, c = blocks.shape
    eye = jnp.eye(n, dtype=blocks.dtype)
    return jnp.einsum("nrc,nm->nrmc", blocks, eye).reshape(n * r, n * c)


def kernel(x_prompt, x_sample, state_ret, state_lru_h, state_lru_conv, state_s5_re, state_s5_im,
           c_prompt, c_sample, norm_g, w_ada, b_ada, w_in, ret_gn_g, conv_w, conv_b, w_rg, b_rg,
           w_ig, b_ig, lru_lambda, s5_a_re, s5_a_im, s5_b_re, s5_b_im, s5_c_re, s5_c_im, s5_d,
           s5_log_dt, s5_w_glu, w_out, final_g):
    depth = w_in.shape[0]
    bsz, seq, d = x_prompt.shape
    dec = x_sample.shape[0]
    assert x_sample.shape[1] == 1 and seq % RET_CHUNK == 0 and dec % DEC_BLOCK == 0

    with jax.ensure_compile_time_eval():
        tabs = _tables(seq, dec)
    mod_all = _ada(jnp.concatenate([c_prompt, c_sample], axis=0), w_ada, b_ada)
    n_pow = min(SEQ_TILE, seq) // SUBLANES
    apr, api, bbr, bbi = _s5_prep(s5_a_re, s5_a_im, s5_log_dt, s5_b_re, s5_b_im, n_pow)

    bdiag = jax.vmap(_block_diag)
    bd = jnp.concatenate([bdiag(bbr), bdiag(bbi)], axis=2)
    cd = jnp.concatenate([bdiag(jnp.swapaxes(s5_c_re, 2, 3)),
                          bdiag(jnp.swapaxes(-s5_c_im, 2, 3))], axis=1)
    w = dict(
        norm_g=norm_g.reshape(depth, 1, d),
        w_in=w_in.astype(BF16),
        w_out=w_out.astype(BF16),
        gn_g=ret_gn_g.reshape(depth, 1, RET_WIDTH),
        conv_w=conv_w,
        conv_b=conv_b.reshape(depth, 1, LRU_WIDTH),
        w_gate=jnp.concatenate([bdiag(w_rg), bdiag(w_ig)], axis=2).astype(BF16),
        b_gate=jnp.concatenate([b_rg, b_ig], axis=1).reshape(depth, 1, 2 * LRU_WIDTH),
        lam=lru_lambda.reshape(depth, 1, LRU_WIDTH),
        bd=bd.astype(BF16),
        cd=cd.astype(BF16),
        apr=apr.reshape(depth, n_pow, S5_FLAT),
        api=api.reshape(depth, n_pow, S5_FLAT),
        s5_d=s5_d.reshape(depth, 1, S5_WIDTH),
        w_glu=s5_w_glu.astype(BF16),
        final_g=final_g.reshape(1, d),
    )

    xp = x_prompt
    outs_p = []
    for l in range(depth):
        mod_p = mod_all[l, :bsz].reshape(bsz, 3, d)
        xp, sret, hl, cv, sre, sim = _prompt_layer(xp, mod_p, w, tabs, l, final=l == depth - 1)
        outs_p.append((sret, hl.reshape(bsz, LRU_WIDTH), cv,
                       sre.reshape(bsz, S5_GROUPS, S5_STATE), sim.reshape(bsz, S5_GROUPS, S5_STATE)))

    states = (state_ret, state_lru_h,
              state_lru_conv.reshape(depth, dec, (CONV_WIDTH - 1) * LRU_WIDTH),
              state_s5_re.reshape(depth, dec, S5_FLAT), state_s5_im.reshape(depth, dec, S5_FLAT))
    xs, sret_s, hl_s, cv_s, sre_s, sim_s = _decode_layers(
        x_sample.reshape(dec, d), mod_all[:, bsz:], states, w, tabs)

    def stk(k):
        return jnp.stack([o[k] for o in outs_p])

    return (xp, xs.reshape(dec, 1, d),
            stk(0), sret_s, stk(1), hl_s,
            stk(2), cv_s.reshape(depth, dec, CONV_WIDTH - 1, LRU_WIDTH),
            stk(3), sre_s.reshape(depth, dec, S5_GROUPS, S5_STATE),
            stk(4), sim_s.reshape(depth, dec, S5_GROUPS, S5_STATE))
```

```python
import functools
import math

import jax
import jax.numpy as jnp
import numpy as np
from jax import lax
from jax.experimental import pallas as pl
from jax.experimental.pallas import tpu as pltpu

RET_HEADS = 4
HEAD_DIM = 128
RET_WIDTH = RET_HEADS * HEAD_DIM
RET_CHUNK = 128
LRU_WIDTH = 256
LRU_BLOCKS = 4
LRU_C = 8.0
CONV_WIDTH = 4
S5_WIDTH = 256
S5_GROUP = 16
S5_GROUPS = 16
S5_STATE = 64
S5_FLAT = S5_GROUPS * S5_STATE
ROPE_BASE = 10000.0
EPS = 1e-6
PAST_LEN = 16384

OFF_Q = 0
OFF_K = OFF_Q + RET_WIDTH
OFF_V = OFF_K + RET_WIDTH
OFF_GR = OFF_V + RET_WIDTH
OFF_XL = OFF_GR + RET_WIDTH
OFF_GL = OFF_XL + LRU_WIDTH
OFF_U = OFF_GL + LRU_WIDTH
OFF_GS = OFF_U + S5_WIDTH
D_IN = OFF_GS + S5_WIDTH

SUBLANES = 8
LANES = 128
SEQ_TILE = 512
PROJ_CHUNK = 256
DEC_BLOCK = 16
VMEM_LIMIT = 56 * 1024 * 1024

F32 = jnp.float32
BF16 = jnp.bfloat16


def _sigmoid(x):
    return 1.0 / (1.0 + jnp.exp(-x))


def _silu(x):
    return x * _sigmoid(x)


def _gelu_tanh(x):
    c = math.sqrt(2.0 / math.pi)
    return x * (0.5 * (1.0 + jnp.tanh(c * (x + 0.044715 * (x * x * x)))))


def _softplus(x):
    return jnp.maximum(x, 0.0) + jnp.log1p(jnp.exp(-jnp.abs(x)))


def _rmsnorm(x, g):
    ms = jnp.mean(x * x, axis=-1, keepdims=True)
    return x * lax.rsqrt(ms + EPS) * g


def _dot(a, b):
    return jnp.dot(a.astype(BF16), b.astype(BF16), preferred_element_type=F32)


def _dot_nt(a, b):
    return lax.dot_general(a.astype(BF16), b.astype(BF16), (((1,), (1,)), ((), ())),
                           preferred_element_type=F32)


def _rotary(x, cosf, sinf):
    return x * cosf + pltpu.roll(x, HEAD_DIM // 2, axis=1) * sinf


def _groupnorm(o, g):
    mu = jnp.mean(o, axis=-1, keepdims=True)
    d = o - mu
    var = jnp.mean(d * d, axis=-1, keepdims=True)
    return d * lax.rsqrt(var + EPS) * g


def _lru_coeffs(xc, gates, sp):
    r = _sigmoid(gates[:, :LRU_WIDTH])
    ig = _sigmoid(gates[:, LRU_WIDTH:])
    log_a = (-LRU_C) * r * sp
    a = jnp.exp(log_a)
    th = jnp.tanh(log_a)
    mult = jnp.sqrt(-2.0 * th / (1.0 - th))
    return a, mult * ig * xc


def _s5_tail(sr_si, u, gs, cd, s5d, wglu):
    ys = _dot(sr_si, cd) + s5d * u
    ys = _gelu_tanh(ys)
    ys = ys * _sigmoid(_dot(ys, wglu))
    return ys * _silu(gs)


def _s5_prep_kernel(are_ref, aim_ref, ldt_ref, bre_ref, bim_ref,
                    apr_ref, api_ref, bbr_ref, bbi_ref):
    depth = are_ref.shape[0]
    for l in range(depth):
        a_re = are_ref[l]
        a_im = aim_ref[l]
        step = jnp.exp(ldt_ref[l])
        mag = jnp.exp(step * a_re)
        abar_r = mag * jnp.cos(step * a_im)
        abar_i = mag * jnp.sin(step * a_im)
        nr, ni = abar_r - 1.0, abar_i
        den = a_re * a_re + a_im * a_im
        fr = (nr * a_re + ni * a_im) / den
        fi = (ni * a_re - nr * a_im) / den
        for g in range(S5_GROUPS):
            frg, fig = fr[g:g + 1, :], fi[g:g + 1, :]
            b_re, b_im = bre_ref[l, g], bim_ref[l, g]
            bbr_ref[l, g] = frg * b_re - fig * b_im
            bbi_ref[l, g] = frg * b_im + fig * b_re
        pr, pi = abar_r, abar_i
        apr_ref[l, 0] = pr
        api_ref[l, 0] = pi
        for m in range(1, apr_ref.shape[1]):
            pr, pi = pr * abar_r - pi * abar_i, pr * abar_i + pi * abar_r
            apr_ref[l, m] = pr
            api_ref[l, m] = pi


def _s5_prep(s5_a_re, s5_a_im, s5_log_dt, s5_b_re, s5_b_im, n_pow):
    depth = s5_a_re.shape[0]
    b_re_t = jnp.swapaxes(s5_b_re, 2, 3)
    b_im_t = jnp.swapaxes(s5_b_im, 2, 3)
    ldt = s5_log_dt.reshape(depth, S5_GROUPS, 1)
    out_shape = (
        jax.ShapeDtypeStruct((depth, n_pow, S5_GROUPS, S5_STATE), F32),
        jax.ShapeDtypeStruct((depth, n_pow, S5_GROUPS, S5_STATE), F32),
        jax.ShapeDtypeStruct((depth, S5_GROUPS, S5_GROUP, S5_STATE), F32),
        jax.ShapeDtypeStruct((depth, S5_GROUPS, S5_GROUP, S5_STATE), F32),
    )
    return pl.pallas_call(_s5_prep_kernel, out_shape=out_shape, name="s5_prep")(
        s5_a_re, s5_a_im, ldt, b_re_t, b_im_t)


def _ada_kernel(c_ref, w_ref, b_ref, o_ref):
    s = _silu(c_ref[...])
    o_ref[0] = _dot(s, w_ref[0]) + b_ref[0]


def _ada(c_all, w_ada, b_ada):
    depth, d, n3 = w_ada.shape
    rows = c_all.shape[0]
    tn = d
    return pl.pallas_call(
        _ada_kernel,
        grid=(depth, n3 // tn),
        in_specs=[
            pl.BlockSpec((rows, d), lambda l, j: (0, 0)),
            pl.BlockSpec((1, d, tn), lambda l, j: (l, 0, j)),
            pl.BlockSpec((1, 1, tn), lambda l, j: (l, 0, j)),
        ],
        out_specs=pl.BlockSpec((1, rows, tn), lambda l, j: (l, 0, j)),
        out_shape=jax.ShapeDtypeStruct((depth, rows, n3), F32),
        compiler_params=pltpu.CompilerParams(
            dimension_semantics=("arbitrary", "arbitrary"), vmem_limit_bytes=VMEM_LIMIT),
        name="adaln",
    )(c_all, w_ada, b_ada.reshape(depth, 1, n3))


def _prompt_kernel(x_ref, mod_ref, ng_ref, win_ref, wout_ref,
                   cq_ref, sq_ref, ck_ref, sk_ref, intra_ref, qdec_ref, kdec_ref, cdec_ref,
                   gng_ref, cw_ref, cb_ref, wg_ref, bg_ref, lam_ref,
                   bd_ref, cd_ref, apr_ref, api_ref, s5d_ref, wglu_ref, fg_ref,
                   y_ref, sret_ref, hl_ref, cv_ref, sre_ref, sim_ref,
                   proj_scr, ycat_scr, perm_scr, xpad_scr, ab_scr, st_scr, st16_scr, yp_scr, h_scr,
                   ph_scr, hloc_scr, *, final):
    tl = x_ref.shape[1]
    seg = tl // SUBLANES
    t = pl.program_id(1)
    (ng_ref, win_ref, wout_ref, gng_ref, cw_ref, cb_ref, wg_ref, bg_ref, lam_ref,
     bd_ref, cd_ref, apr_ref, api_ref, s5d_ref, wglu_ref) = (
        r.at[0] for r in (ng_ref, win_ref, wout_ref, gng_ref, cw_ref, cb_ref, wg_ref, bg_ref,
                          lam_ref, bd_ref, cd_ref, apr_ref, api_ref, s5d_ref, wglu_ref))

    @pl.when(t == 0)
    def _():
        sret_ref[...] = jnp.zeros_like(sret_ref)
        hl_ref[...] = jnp.zeros_like(hl_ref)
        sre_ref[...] = jnp.zeros_like(sre_ref)
        sim_ref[...] = jnp.zeros_like(sim_ref)
        cv_ref[...] = jnp.zeros_like(cv_ref)

    x = x_ref[0]
    shift, scale = mod_ref[0, 0:1, :], mod_ref[0, 1:2, :]
    h = _rmsnorm(x, ng_ref[...] * (1.0 + scale)) + shift
    h_scr[...] = h.astype(BF16)
    proj_b = jnp.dot(h_scr[...], win_ref[:, pl.ds(OFF_XL, D_IN - OFF_XL)],
                     preferred_element_type=F32)

    def proj_a_chunk(ci):
        cols = pl.ds(ci * PROJ_CHUNK, PROJ_CHUNK)
        proj_scr[:, cols] = jnp.dot(h_scr[...], win_ref[:, cols], preferred_element_type=F32)

    for m in range(tl // SUBLANES):
        s, r0 = divmod(m * SUBLANES, seg)
        for kb in range((D_IN - OFF_XL) // LANES):
            perm_scr[kb, pl.ds(r0 * SUBLANES + s, SUBLANES, stride=SUBLANES), :] = (
                proj_b[m * SUBLANES:(m + 1) * SUBLANES, kb * LANES:(kb + 1) * LANES])

    def ret_piece(c, hd):
        rows = pl.ds(c * RET_CHUNK, RET_CHUNK)
        lanes = pl.ds(hd * HEAD_DIM, HEAD_DIM)
        q = _rotary(proj_scr[rows, pl.ds(OFF_Q + hd * HEAD_DIM, HEAD_DIM)],
                    cq_ref[rows, :], sq_ref[rows, :])
        k = _rotary(proj_scr[rows, pl.ds(OFF_K + hd * HEAD_DIM, HEAD_DIM)],
                    ck_ref[rows, :], sk_ref[rows, :])
        v = proj_scr[rows, pl.ds(OFF_V + hd * HEAD_DIM, HEAD_DIM)].astype(BF16)
        s_old = sret_ref[0, hd]
        sc = _dot_nt(q, k) * intra_ref[hd]
        o = _dot(sc, v) + _dot(q * qdec_ref[hd], s_old)
        kd_t = (k * kdec_ref[hd]).T
        sret_ref[0, hd] = cdec_ref[hd] * s_old + _dot(kd_t, v)
        on = _groupnorm(o, gng_ref[:, lanes])
        g_ret = proj_scr[rows, pl.ds(OFF_GR + hd * HEAD_DIM, HEAD_DIM)]
        ycat_scr[rows, lanes] = (on * _silu(g_ret)).astype(BF16)

    sub = lax.broadcasted_iota(jnp.int32, (SUBLANES, 1), 0)
    rg = lambda r: pl.ds(r * SUBLANES, SUBLANES)

    def pcols(lo, width):
        return jnp.concatenate([perm_scr[kb] for kb in range(lo // LANES, (lo + width) // LANES)],
                               axis=1)

    n_chunks = OFF_XL // PROJ_CHUNK
    s5_parts = n_chunks - 3

    proj_a_chunk(0)
    xl = pcols(OFF_XL - OFF_XL, LRU_WIDTH)
    prev = cv_ref[0]
    npre = CONV_WIDTH - 1
    for m in range(1, CONV_WIDTH):
        grp = xl[(seg - m) * SUBLANES:(seg - m + 1) * SUBLANES, :]
        e_m = jnp.where(sub == 0, prev[npre - m:npre - m + 1, :], pltpu.roll(grp, 1, axis=0))
        xpad_scr[rg(npre - m), :] = e_m
    xpad_scr[pl.ds(npre * SUBLANES, tl), :] = xl
    cv_ref[0] = jnp.concatenate(
        [xl[(seg - m) * SUBLANES + SUBLANES - 1:(seg - m + 1) * SUBLANES, :]
         for m in range(npre, 0, -1)], axis=0)
    xc = cb_ref[...]
    for j in range(CONV_WIDTH):
        xc = xc + xpad_scr[pl.ds(j * SUBLANES, tl), :] * cw_ref[pl.ds(j, 1), :]
    gates = _dot(xc, wg_ref[...]) + bg_ref[...]
    a, b = _lru_coeffs(xc, gates, _softplus(-lam_ref[...]))
    a_l, b_l = pl.ds(0, LRU_WIDTH), pl.ds(LRU_WIDTH, LRU_WIDTH)
    ab_scr[:, a_l] = a
    ab_scr[:, b_l] = b
    proj_a_chunk(1)
    hloc = jnp.zeros((SUBLANES, LRU_WIDTH), F32)
    prod = jnp.ones((SUBLANES, LRU_WIDTH), F32)
    for r in range(seg):
        ar = ab_scr[rg(r), a_l]
        hloc = ar * hloc + ab_scr[rg(r), b_l]
        prod = prod * ar
        ph_scr[rg(r), a_l] = prod
        ph_scr[rg(r), b_l] = hloc
    c = hl_ref[0]
    cs = [c]
    for s in range(SUBLANES - 1):
        c = hloc[s:s + 1, :] + prod[s:s + 1, :] * c
        cs.append(c)
    hl_ref[0] = hloc[SUBLANES - 1:, :] + prod[SUBLANES - 1:, :] * c
    c_all = jnp.concatenate(cs, axis=0)
    sgl = _silu(pcols(OFF_GL - OFF_XL, LRU_WIDTH))
    for r in range(seg):
        y = (ph_scr[rg(r), b_l] + ph_scr[rg(r), a_l] * c_all) * sgl[r * SUBLANES:(r + 1) * SUBLANES, :]
        for kb in range(LRU_WIDTH // LANES):
            yp_scr[kb, rg(r), :] = y[:, kb * LANES:(kb + 1) * LANES]

    proj_a_chunk(2)
    u = pcols(OFF_U - OFF_XL, S5_WIDTH)
    st_scr[...] = _dot(u, bd_ref[...])
    re_l, im_l = pl.ds(0, S5_FLAT), pl.ds(S5_FLAT, S5_FLAT)
    bshape = (SUBLANES, S5_FLAT)
    a1r = jnp.broadcast_to(apr_ref[pl.ds(0, 1), :], bshape)
    a1i = jnp.broadcast_to(api_ref[pl.ds(0, 1), :], bshape)
    hr = jnp.zeros((SUBLANES, S5_FLAT), F32)
    hi = jnp.zeros((SUBLANES, S5_FLAT), F32)
    for r in range(seg):
        if r % (-(-seg // s5_parts)) == 0:
            proj_a_chunk(3 + r // (-(-seg // s5_parts)))
        hr, hi = (a1r * hr - a1i * hi + st_scr[rg(r), re_l],
                  a1r * hi + a1i * hr + st_scr[rg(r), im_l])
        hloc_scr[rg(r), re_l] = hr
        hloc_scr[rg(r), im_l] = hi
    asr, asi = apr_ref[pl.ds(seg - 1, 1), :], api_ref[pl.ds(seg - 1, 1), :]
    cr, ci = sre_ref[0], sim_ref[0]
    crs, cis = [cr], [ci]
    for s in range(SUBLANES - 1):
        cr, ci = (hr[s:s + 1, :] + (asr * cr - asi * ci), hi[s:s + 1, :] + (asr * ci + asi * cr))
        crs.append(cr)
        cis.append(ci)
    sre_ref[0] = hr[SUBLANES - 1:, :] + (asr * cr - asi * ci)
    sim_ref[0] = hi[SUBLANES - 1:, :] + (asr * ci + asi * cr)
    cr_all, ci_all = jnp.concatenate(crs, axis=0), jnp.concatenate(cis, axis=0)
    n_pieces = (tl // RET_CHUNK) * RET_HEADS
    per_piece = (seg // 2) // n_pieces
    n_col = S5_FLAT // LANES
    it = 0
    for cb in range(n_col):
        lr, li = pl.ds(cb * LANES, LANES), pl.ds(S5_FLAT + cb * LANES, LANES)
        c_r, c_i = cr_all[:, cb * LANES:(cb + 1) * LANES], ci_all[:, cb * LANES:(cb + 1) * LANES]
        for r2 in range(seg // 2):
            if it % (per_piece * n_col) == 0:
                ret_piece(*divmod(it // (per_piece * n_col), RET_HEADS))
            it += 1
            parts_r, parts_i = [], []
            for r in (2 * r2, 2 * r2 + 1):
                qr = jnp.broadcast_to(apr_ref[pl.ds(r, 1), lr], (SUBLANES, LANES))
                qi = jnp.broadcast_to(api_ref[pl.ds(r, 1), lr], (SUBLANES, LANES))
                parts_r.append(hloc_scr[rg(r), lr] + (qr * c_r - qi * c_i))
                parts_i.append(hloc_scr[rg(r), li] + (qr * c_i + qi * c_r))
            rows2 = pl.ds(r2 * 2 * SUBLANES, 2 * SUBLANES)
            st16_scr[rows2, lr] = jnp.concatenate(parts_r, axis=0).astype(BF16)
            st16_scr[rows2, li] = jnp.concatenate(parts_i, axis=0).astype(BF16)
    gs = pcols(OFF_GS - OFF_XL, S5_WIDTH)
    y_s5 = _s5_tail(st16_scr[...], u, gs, cd_ref[...], s5d_ref[...], wglu_ref[...])
    for kb in range(S5_WIDTH // LANES):
        yp_scr[LRU_WIDTH // LANES + kb] = y_s5[:, kb * LANES:(kb + 1) * LANES]

    for m2 in range(tl // (2 * SUBLANES)):
        for kb in range((LRU_WIDTH + S5_WIDTH) // LANES):
            halves = []
            for m in (2 * m2, 2 * m2 + 1):
                s, r0 = divmod(m * SUBLANES, seg)
                halves.append(yp_scr[kb, pl.ds(r0 * SUBLANES + s, SUBLANES, stride=SUBLANES), :])
            ycat_scr[pl.ds(m2 * 2 * SUBLANES, 2 * SUBLANES), pl.ds(RET_WIDTH + kb * LANES, LANES)] = (
                jnp.concatenate(halves, axis=0).astype(BF16))

    out = jnp.dot(ycat_scr[...], wout_ref[...], preferred_element_type=F32)
    x_new = x_ref[0] + mod_ref[0, 2:3, :] * out
    if final:
        x_new = _rmsnorm(x_new, fg_ref[...])
    y_ref[0] = x_new


def _prompt_layer(x, mod, w, tabs, layer, *, final):
    bsz, seq, d = x.shape
    tl = min(SEQ_TILE, seq)
    nt = seq // tl

    def full(a):
        return pl.BlockSpec(a.shape, lambda b, t: (0,) * a.ndim)

    def lay(a):
        return pl.BlockSpec((1,) + a.shape[1:], lambda b, t: (layer,) + (0,) * (a.ndim - 1))

    row_tab = pl.BlockSpec((tl, HEAD_DIM), lambda b, t: (t, 0))
    in_specs = [
        pl.BlockSpec((1, tl, d), lambda b, t: (b, t, 0)),
        pl.BlockSpec((1, 3, d), lambda b, t: (b, 0, 0)),
        lay(w["norm_g"]), lay(w["w_in"]), lay(w["w_out"]),
        row_tab, row_tab, row_tab, row_tab,
        full(tabs["intra"]), full(tabs["q_dec"]), full(tabs["k_dec"]),
        pl.BlockSpec(memory_space=pltpu.SMEM),
        lay(w["gn_g"]), lay(w["conv_w"]), lay(w["conv_b"]), lay(w["w_gate"]), lay(w["b_gate"]),
        lay(w["lam"]), lay(w["bd"]), lay(w["cd"]), lay(w["apr"]), lay(w["api"]),
        lay(w["s5_d"]), lay(w["w_glu"]), full(w["final_g"]),
    ]
    out_shape = (
        jax.ShapeDtypeStruct((bsz, seq, d), F32),
        jax.ShapeDtypeStruct((bsz, RET_HEADS, HEAD_DIM, HEAD_DIM), F32),
        jax.ShapeDtypeStruct((bsz, 1, LRU_WIDTH), F32),
        jax.ShapeDtypeStruct((bsz, CONV_WIDTH - 1, LRU_WIDTH), F32),
        jax.ShapeDtypeStruct((bsz, 1, S5_FLAT), F32),
        jax.ShapeDtypeStruct((bsz, 1, S5_FLAT), F32),
    )
    out_specs = (
        pl.BlockSpec((1, tl, d), lambda b, t: (b, t, 0)),
        pl.BlockSpec((1, RET_HEADS, HEAD_DIM, HEAD_DIM), lambda b, t: (b, 0, 0, 0)),
        pl.BlockSpec((1, 1, LRU_WIDTH), lambda b, t: (b, 0, 0)),
        pl.BlockSpec((1, CONV_WIDTH - 1, LRU_WIDTH), lambda b, t: (b, 0, 0)),
        pl.BlockSpec((1, 1, S5_FLAT), lambda b, t: (b, 0, 0)),
        pl.BlockSpec((1, 1, S5_FLAT), lambda b, t: (b, 0, 0)),
    )
    scratch = [
        pltpu.VMEM((tl, OFF_XL), F32),
        pltpu.VMEM((tl, d), BF16),
        pltpu.VMEM(((D_IN - OFF_XL) // LANES, tl, LANES), F32),
        pltpu.VMEM((tl + (CONV_WIDTH - 1) * SUBLANES, LRU_WIDTH), F32),
        pltpu.VMEM((tl, 2 * LRU_WIDTH), F32),
        pltpu.VMEM((tl, 2 * S5_FLAT), F32),
        pltpu.VMEM((tl, 2 * S5_FLAT), BF16),
        pltpu.VMEM(((LRU_WIDTH + S5_WIDTH) // LANES, tl, LANES), F32),
        pltpu.VMEM((tl, d), BF16),
        pltpu.VMEM((tl, 2 * LRU_WIDTH), F32),
        pltpu.VMEM((tl, 2 * S5_FLAT), F32),
    ]
    return pl.pallas_call(
        functools.partial(_prompt_kernel, final=final),
        grid=(bsz, nt),
        in_specs=in_specs,
        out_specs=out_specs,
        out_shape=out_shape,
        scratch_shapes=scratch,
        compiler_params=pltpu.CompilerParams(
            dimension_semantics=("arbitrary", "arbitrary"), vmem_limit_bytes=VMEM_LIMIT),
        name="prompt_layer",
    )(x, mod, w["norm_g"], w["w_in"], w["w_out"],
      tabs["cos_q"], tabs["sin_q"], tabs["cos_k"], tabs["sin_k"],
      tabs["intra"], tabs["q_dec"], tabs["k_dec"], tabs["c_dec"],
      w["gn_g"], w["conv_w"], w["conv_b"], w["w_gate"], w["b_gate"], w["lam"],
      w["bd"], w["cd"], w["apr"], w["api"], w["s5_d"], w["w_glu"], w["final_g"])


def _decode_kernel(x_ref, mod_ref, ng_ref, win_ref, wout_ref,
                   cq_ref, sq_ref, ck_ref, sk_ref, gam_ref,
                   gng_ref, cw_ref, cb_ref, wg_ref, bg_ref, lam_ref,
                   bd_ref, cd_ref, apr_ref, api_ref, s5d_ref, wglu_ref, fg_ref,
                   sret_in, hl_in, cv_in, sre_in, sim_in, sel_ref,
                   y_ref, sret_ref, hl_ref, cv_ref, sre_ref, sim_ref,
                   qkt_scr, v_scr, o_scr, gr_scr, ycat_scr, x_scr):
    layer = pl.program_id(0)
    i = pl.program_id(1)
    nb = pl.num_programs(1)
    rows_n, d = x_ref.shape
    (mod_ref, ng_ref, win_ref, wout_ref, gng_ref, cw_ref, cb_ref, wg_ref, bg_ref, lam_ref,
     bd_ref, cd_ref, apr_ref, api_ref, s5d_ref, wglu_ref,
     sret_in, hl_in, cv_in, sre_in, sim_in, sret_ref, hl_ref, cv_ref, sre_ref, sim_ref) = (
        r.at[0] for r in (mod_ref, ng_ref, win_ref, wout_ref, gng_ref, cw_ref, cb_ref, wg_ref,
                          bg_ref, lam_ref, bd_ref, cd_ref, apr_ref, api_ref, s5d_ref, wglu_ref,
                          sret_in, hl_in, cv_in, sre_in, sim_in,
                          sret_ref, hl_ref, cv_ref, sre_ref, sim_ref))

    @pl.when((i == 0) & (layer == 0))
    def _():
        x_scr[...] = x_ref[...]

    @pl.when(i == 0)
    def _():
        x = x_scr[...]
        shift, scale = mod_ref[:, pl.ds(0, d)], mod_ref[:, pl.ds(d, d)]
        h = _rmsnorm(x, ng_ref[...]) * (1.0 + scale) + shift
        proj = _dot(h, win_ref[...])
        cq, sq, ck, sk = cq_ref[...], sq_ref[...], ck_ref[...], sk_ref[...]
        for hd in range(RET_HEADS):
            lo = hd * HEAD_DIM
            for part, src in enumerate((_rotary(proj[:, OFF_Q + lo:OFF_Q + lo + HEAD_DIM], cq, sq),
                                        _rotary(proj[:, OFF_K + lo:OFF_K + lo + HEAD_DIM], ck, sk))):
                xt = src.T
                hi = xt.astype(BF16)
                trows = pl.ds((2 * hd + part) * HEAD_DIM, HEAD_DIM)
                qkt_scr[trows, pl.ds(0, rows_n)] = hi
                qkt_scr[trows, pl.ds(rows_n, rows_n)] = (xt - hi.astype(F32)).astype(BF16)
        v_scr[...] = proj[:, OFF_V:OFF_V + RET_WIDTH]
        gr_scr[...] = _silu(proj[:, OFF_GR:OFF_GR + RET_WIDTH])
        xl = proj[:, OFF_XL:OFF_XL + LRU_WIDTH]
        cs = cv_in[...]
        xc = cb_ref[...] + xl * cw_ref[pl.ds(CONV_WIDTH - 1, 1), :]
        for j in range(CONV_WIDTH - 1):
            xc = xc + cs[:, j * LRU_WIDTH:(j + 1) * LRU_WIDTH] * cw_ref[pl.ds(j, 1), :]
        cv_ref[:, pl.ds(0, 2 * LRU_WIDTH)] = cs[:, LRU_WIDTH:]
        cv_ref[:, pl.ds(2 * LRU_WIDTH, LRU_WIDTH)] = xl
        gates = _dot(xc, wg_ref[...]) + bg_ref[...]
        a, b = _lru_coeffs(xc, gates, _softplus(-lam_ref[...]))
        hh = b + a * hl_in[...]
        hl_ref[...] = hh
        ycat_scr[:, pl.ds(RET_WIDTH, LRU_WIDTH)] = (
            hh * _silu(proj[:, OFF_GL:OFF_GL + LRU_WIDTH])).astype(BF16)
        u = proj[:, OFF_U:OFF_U + S5_WIDTH]
        bu = _dot(u, bd_ref[...])
        ar, ai = apr_ref[pl.ds(0, 1), :], api_ref[pl.ds(0, 1), :]
        s0r, s0i = sre_in[...], sim_in[...]
        sr = bu[:, :S5_FLAT] + (ar * s0r - ai * s0i)
        si = bu[:, S5_FLAT:] + (ar * s0i + ai * s0r)
        sre_ref[...] = sr
        sim_ref[...] = si
        y_s5 = _s5_tail(jnp.concatenate([sr, si], axis=-1), u, proj[:, OFF_GS:OFF_GS + S5_WIDTH],
                        cd_ref[...], s5d_ref[...], wglu_ref[...])
        ycat_scr[:, pl.ds(RET_WIDTH + LRU_WIDTH, S5_WIDTH)] = y_s5.astype(BF16)

    blk_rows = pl.ds(pl.multiple_of(i * DEC_BLOCK, DEC_BLOCK), DEC_BLOCK)
    o_rows = [[] for _ in range(RET_HEADS)]
    vblks = [v_scr[blk_rows, pl.ds(hd * HEAD_DIM, HEAD_DIM)] for hd in range(RET_HEADS)]
    for jp in range(DEC_BLOCK // 2):
        sel = sel_ref[0, :, pl.ds(jp * 2 * HEAD_DIM, 2 * HEAD_DIM)]
        tiles = jnp.dot(qkt_scr[...], jnp.concatenate([sel, sel], axis=0),
                        preferred_element_type=F32)
        for hd in range(RET_HEADS):
            qb = tiles[2 * hd * HEAD_DIM:(2 * hd + 1) * HEAD_DIM, :]
            kb = tiles[(2 * hd + 1) * HEAD_DIM:(2 * hd + 2) * HEAD_DIM, :]
            for j in (2 * jp, 2 * jp + 1):
                cols = slice((j % 2) * HEAD_DIM, (j % 2 + 1) * HEAD_DIM)
                s_new = gam_ref[hd] * sret_in[j, hd] + kb[:, cols] * vblks[hd][j:j + 1, :]
                sret_ref[j, hd] = s_new
                o_rows[hd].append(jnp.sum(qb[:, cols] * s_new, axis=0, keepdims=True))
    for hd in range(RET_HEADS):
        o_scr[blk_rows, pl.ds(hd * HEAD_DIM, HEAD_DIM)] = jnp.concatenate(o_rows[hd], axis=0)

    @pl.when(i == nb - 1)
    def _():
        for hd in range(RET_HEADS):
            lanes = pl.ds(hd * HEAD_DIM, HEAD_DIM)
            on = _groupnorm(o_scr[:, lanes], gng_ref[:, lanes])
            ycat_scr[:, lanes] = (on * gr_scr[:, lanes]).astype(BF16)
        out = jnp.dot(ycat_scr[...], wout_ref[...], preferred_element_type=F32)
        x_new = x_scr[...] + mod_ref[:, pl.ds(2 * d, d)] * out
        x_scr[...] = x_new
        y_ref[...] = _rmsnorm(x_new, fg_ref[...])


def _decode_layers(x, mod, states, w, tabs):
    rows, d = x.shape
    depth = mod.shape[0]
    s_ret, s_h, s_conv, s_re, s_im = states
    nb = rows // DEC_BLOCK

    def full(a):
        return pl.BlockSpec(a.shape, lambda l, i: (0,) * a.ndim)

    def lay(a):
        return pl.BlockSpec((1,) + a.shape[1:], lambda l, i: (l,) + (0,) * (a.ndim - 1))

    smem = pl.BlockSpec(memory_space=pltpu.SMEM)
    sblk = pl.BlockSpec((1, DEC_BLOCK, RET_HEADS, HEAD_DIM, HEAD_DIM), lambda l, i: (l, i, 0, 0, 0))
    in_specs = [
        full(x), lay(mod), lay(w["norm_g"]), lay(w["w_in"]), lay(w["w_out"]),
        full(tabs["dcos_q"]), full(tabs["dsin_q"]), full(tabs["dcos_k"]), full(tabs["dsin_k"]), smem,
        lay(w["gn_g"]), lay(w["conv_w"]), lay(w["conv_b"]), lay(w["w_gate"]), lay(w["b_gate"]),
        lay(w["lam"]), lay(w["bd"]), lay(w["cd"]), lay(w["apr"]), lay(w["api"]),
        lay(w["s5_d"]), lay(w["w_glu"]), full(w["final_g"]),
        sblk, lay(s_h), lay(s_conv), lay(s_re), lay(s_im),
        pl.BlockSpec((1,) + tabs["row_select"].shape[1:], lambda l, i: (i, 0, 0)),
    ]
    out_shape = (
        jax.ShapeDtypeStruct((rows, d), F32),
        jax.ShapeDtypeStruct(s_ret.shape, F32),
        jax.ShapeDtypeStruct(s_h.shape, F32),
        jax.ShapeDtypeStruct(s_conv.shape, F32),
        jax.ShapeDtypeStruct(s_re.shape, F32),
        jax.ShapeDtypeStruct(s_im.shape, F32),
    )
    out_specs = (full(x), sblk, lay(s_h), lay(s_conv), lay(s_re), lay(s_im))
    scratch = [
        pltpu.VMEM((RET_HEADS * 2 * HEAD_DIM, 2 * rows), BF16),
        pltpu.VMEM((rows, RET_WIDTH), F32),
        pltpu.VMEM((rows, RET_WIDTH), F32),
        pltpu.VMEM((rows, RET_WIDTH), F32),
        pltpu.VMEM((rows, d), BF16),
        pltpu.VMEM((rows, d), F32),
    ]
    return pl.pallas_call(
        _decode_kernel,
        grid=(depth, nb),
        in_specs=in_specs,
        out_specs=out_specs,
        out_shape=out_shape,
        scratch_shapes=scratch,
        compiler_params=pltpu.CompilerParams(
            dimension_semantics=("arbitrary", "arbitrary"), vmem_limit_bytes=VMEM_LIMIT),
        name="decode_layers",
    )(x, mod, w["norm_g"], w["w_in"], w["w_out"],
      tabs["dcos_q"], tabs["dsin_q"], tabs["dcos_k"], tabs["dsin_k"], tabs["gamma"],
      w["gn_g"], w["conv_w"], w["conv_b"], w["w_gate"], w["b_gate"], w["lam"],
      w["bd"], w["cd"], w["apr"], w["api"], w["s5_d"], w["w_glu"], w["final_g"],
      s_ret, s_h, s_conv, s_re, s_im, tabs["row_select"])


def _rope_tables(pos, scale):
    half = HEAD_DIM // 2
    inv = ROPE_BASE ** (-np.arange(half, dtype=np.float64) / half)
    ang = pos[:, None] * inv[None, :]
    cos, sin = np.cos(ang), np.sin(ang)
    cosf = np.concatenate([cos, cos], axis=-1) * scale
    sinf = np.concatenate([-sin, sin], axis=-1) * scale
    return cosf, sinf


def _tables(seq, dec):
    c = RET_CHUNK
    log_g = np.log1p(-np.exp2(-5.0 - np.arange(RET_HEADS, dtype=np.float64)))
    idx = np.arange(c, dtype=np.float64)
    rel = idx[:, None] - idx[None, :]
    intra = np.where(rel[None] >= 0, np.exp(np.maximum(rel, 0.0)[None] * log_g[:, None, None]), 0.0)
    q_dec = np.exp((idx + 1.0)[:, None] * log_g[None, :])
    k_dec = np.exp((c - 1.0 - idx)[:, None] * log_g[None, :])
    q_dec = np.broadcast_to(q_dec.T[:, :, None], (RET_HEADS, c, HEAD_DIM))
    k_dec = np.broadcast_to(k_dec.T[:, :, None], (RET_HEADS, c, HEAD_DIM))
    c_dec = np.exp(c * log_g)
    gamma = np.exp(log_g)
    k_scale = HEAD_DIM ** -0.5
    cos_q, sin_q = _rope_tables(np.arange(seq, dtype=np.float64), 1.0)
    cos_k, sin_k = _rope_tables(np.arange(seq, dtype=np.float64), k_scale)
    dpos = PAST_LEN + np.arange(1, dtype=np.float64)
    dcos_q, dsin_q = _rope_tables(dpos, 1.0)
    dcos_k, dsin_k = _rope_tables(dpos, k_scale)
    blk = np.arange(dec // DEC_BLOCK)[:, None, None]
    row = np.arange(dec)[None, :, None]
    col = np.arange(DEC_BLOCK * HEAD_DIM)[None, None, :]
    row_select = row == blk * DEC_BLOCK + col // HEAD_DIM
    tabs = dict(intra=intra, q_dec=q_dec, k_dec=k_dec, c_dec=c_dec, gamma=gamma,
                cos_q=cos_q, sin_q=sin_q, cos_k=cos_k, sin_k=sin_k,
                dcos_q=dcos_q, dsin_q=dsin_q, dcos_k=dcos_k, dsin_k=dsin_k)
    tabs = {k: jnp.asarray(np.ascontiguousarray(v), dtype=F32) for k, v in tabs.items()}
    tabs["row_select"] = jnp.asarray(row_select, dtype=BF16)
    return tabs


def _block_diag(blocks):
    n, r, c = blocks.shape
    eye = jnp.eye(n, dtype=blocks.dtype)
    return jnp.einsum("nrc,nm->nrmc", blocks, eye).reshape(n * r, n * c)


def kernel(x_prompt, x_sample, state_ret, state_lru_h, state_lru_conv, state_s5_re, state_s5_im,
           c_prompt, c_sample, norm_g, w_ada, b_ada, w_in, ret_gn_g, conv_w, conv_b, w_rg, b_rg,
           w_ig, b_ig, lru_lambda, s5_a_re, s5_a_im, s5_b_re, s5_b_im, s5_c_re, s5_c_im, s5_d,
           s5_log_dt, s5_w_glu, w_out, final_g):
    depth = w_in.shape[0]
    bsz, seq, d = x_prompt.shape
    dec = x_sample.shape[0]
    assert x_sample.shape[1] == 1 and seq % RET_CHUNK == 0 and dec % DEC_BLOCK == 0

    tabs = _tables(seq, dec)
    mod_all = _ada(jnp.concatenate([c_prompt, c_sample], axis=0), w_ada, b_ada)
    n_pow = min(SEQ_TILE, seq) // SUBLANES
    apr, api, bbr, bbi = _s5_prep(s5_a_re, s5_a_im, s5_log_dt, s5_b_re, s5_b_im, n_pow)

    bdiag = jax.vmap(_block_diag)
    bd = jnp.concatenate([bdiag(bbr), bdiag(bbi)], axis=2)
    cd = jnp.concatenate([bdiag(jnp.swapaxes(s5_c_re, 2, 3)),
                          bdiag(jnp.swapaxes(-s5_c_im, 2, 3))], axis=1)
    w = dict(
        norm_g=norm_g.reshape(depth, 1, d),
        w_in=w_in.astype(BF16),
        w_out=w_out.astype(BF16),
        gn_g=ret_gn_g.reshape(depth, 1, RET_WIDTH),
        conv_w=conv_w,
        conv_b=conv_b.reshape(depth, 1, LRU_WIDTH),
        w_gate=jnp.concatenate([bdiag(w_rg), bdiag(w_ig)], axis=2).astype(BF16),
        b_gate=jnp.concatenate([b_rg, b_ig], axis=1).reshape(depth, 1, 2 * LRU_WIDTH),
        lam=lru_lambda.reshape(depth, 1, LRU_WIDTH),
        bd=bd.astype(BF16),
        cd=cd.astype(BF16),
        apr=apr.reshape(depth, n_pow, S5_FLAT),
        api=api.reshape(depth, n_pow, S5_FLAT),
        s5_d=s5_d.reshape(depth, 1, S5_WIDTH),
        w_glu=s5_w_glu.astype(BF16),
        final_g=final_g.reshape(1, d),
    )

    xp = x_prompt
    outs_p = []
    for l in range(depth):
        mod_p = mod_all[l, :bsz].reshape(bsz, 3, d)
        xp, sret, hl, cv, sre, sim = _prompt_layer(xp, mod_p, w, tabs, l, final=l == depth - 1)
        outs_p.append((sret, hl.reshape(bsz, LRU_WIDTH), cv,
                       sre.reshape(bsz, S5_GROUPS, S5_STATE), sim.reshape(bsz, S5_GROUPS, S5_STATE)))

    states = (state_ret, state_lru_h,
              state_lru_conv.reshape(depth, dec, (CONV_WIDTH - 1) * LRU_WIDTH),
              state_s5_re.reshape(depth, dec, S5_FLAT), state_s5_im.reshape(depth, dec, S5_FLAT))
    xs, sret_s, hl_s, cv_s, sre_s, sim_s = _decode_layers(
        x_sample.reshape(dec, d), mod_all[:, bsz:], states, w, tabs)

    def stk(k):
        return jnp.stack([o[k] for o in outs_p])

    return (xp, xs.reshape(dec, 1, d),
            stk(0), sret_s, stk(1), hl_s,
            stk(2), cv_s.reshape(depth, dec, CONV_WIDTH - 1, LRU_WIDTH),
            stk(3), sre_s.reshape(depth, dec, S5_GROUPS, S5_STATE),
            stk(4), sim_s.reshape(depth, dec, S5_GROUPS, S5_STATE))
```

```python
import functools
import math

import jax
import jax.numpy as jnp
import numpy as np
from jax import lax
from jax.experimental import pallas as pl
from jax.experimental.pallas import tpu as pltpu

RET_HEADS = 4
HEAD_DIM = 128
RET_WIDTH = RET_HEADS * HEAD_DIM
RET_CHUNK = 128
LRU_WIDTH = 256
LRU_BLOCKS = 4
LRU_C = 8.0
CONV_WIDTH = 4
S5_WIDTH = 256
S5_GROUP = 16
S5_GROUPS = 16
S5_STATE = 64
S5_FLAT = S5_GROUPS * S5_STATE
ROPE_BASE = 10000.0
EPS = 1e-6
PAST_LEN = 16384

OFF_Q = 0
OFF_K = OFF_Q + RET_WIDTH
OFF_V = OFF_K + RET_WIDTH
OFF_GR = OFF_V + RET_WIDTH
OFF_XL = OFF_GR + RET_WIDTH
OFF_GL = OFF_XL + LRU_WIDTH
OFF_U = OFF_GL + LRU_WIDTH
OFF_GS = OFF_U + S5_WIDTH
D_IN = OFF_GS + S5_WIDTH

SUBLANES = 8
LANES = 128
SEQ_TILE = 512
PROJ_CHUNK = 256
DEC_BLOCK = 16
VMEM_LIMIT = 56 * 1024 * 1024

F32 = jnp.float32
BF16 = jnp.bfloat16


def _sigmoid(x):
    return 1.0 / (1.0 + jnp.exp(-x))


def _silu(x):
    return x * _sigmoid(x)


def _gelu_tanh(x):
    c = math.sqrt(2.0 / math.pi)
    return x * (0.5 * (1.0 + jnp.tanh(c * (x + 0.044715 * (x * x * x)))))


def _softplus(x):
    return jnp.maximum(x, 0.0) + jnp.log1p(jnp.exp(-jnp.abs(x)))


def _rmsnorm(x, g):
    ms = jnp.mean(x * x, axis=-1, keepdims=True)
    return x * lax.rsqrt(ms + EPS) * g


def _dot(a, b):
    return jnp.dot(a.astype(BF16), b.astype(BF16), preferred_element_type=F32)


def _dot_nt(a, b):
    return lax.dot_general(a.astype(BF16), b.astype(BF16), (((1,), (1,)), ((), ())),
                           preferred_element_type=F32)


def _rotary(x, cosf, sinf):
    return x * cosf + pltpu.roll(x, HEAD_DIM // 2, axis=1) * sinf


def _groupnorm(o, g):
    mu = jnp.mean(o, axis=-1, keepdims=True)
    d = o - mu
    var = jnp.mean(d * d, axis=-1, keepdims=True)
    return d * lax.rsqrt(var + EPS) * g


def _lru_coeffs(xc, gates, sp):
    r = _sigmoid(gates[:, :LRU_WIDTH])
    ig = _sigmoid(gates[:, LRU_WIDTH:])
    log_a = (-LRU_C) * r * sp
    a = jnp.exp(log_a)
    th = jnp.tanh(log_a)
    mult = jnp.sqrt(-2.0 * th / (1.0 - th))
    return a, mult * ig * xc


def _s5_tail(sr_si, u, gs, cd, s5d, wglu):
    ys = _dot(sr_si, cd) + s5d * u
    ys = _gelu_tanh(ys)
    ys = ys * _sigmoid(_dot(ys, wglu))
    return ys * _silu(gs)


def _s5_prep_kernel(are_ref, aim_ref, ldt_ref, bre_ref, bim_ref,
                    apr_ref, api_ref, bbr_ref, bbi_ref):
    depth = are_ref.shape[0]
    for l in range(depth):
        a_re = are_ref[l]
        a_im = aim_ref[l]
        step = jnp.exp(ldt_ref[l])
        mag = jnp.exp(step * a_re)
        abar_r = mag * jnp.cos(step * a_im)
        abar_i = mag * jnp.sin(step * a_im)
        nr, ni = abar_r - 1.0, abar_i
        den = a_re * a_re + a_im * a_im
        fr = (nr * a_re + ni * a_im) / den
        fi = (ni * a_re - nr * a_im) / den
        for g in range(S5_GROUPS):
            frg, fig = fr[g:g + 1, :], fi[g:g + 1, :]
            b_re, b_im = bre_ref[l, g], bim_ref[l, g]
            bbr_ref[l, g] = frg * b_re - fig * b_im
            bbi_ref[l, g] = frg * b_im + fig * b_re
        pr, pi = abar_r, abar_i
        apr_ref[l, 0] = pr
        api_ref[l, 0] = pi
        for m in range(1, apr_ref.shape[1]):
            pr, pi = pr * abar_r - pi * abar_i, pr * abar_i + pi * abar_r
            apr_ref[l, m] = pr
            api_ref[l, m] = pi


def _s5_prep(s5_a_re, s5_a_im, s5_log_dt, s5_b_re, s5_b_im, n_pow):
    depth = s5_a_re.shape[0]
    b_re_t = jnp.swapaxes(s5_b_re, 2, 3)
    b_im_t = jnp.swapaxes(s5_b_im, 2, 3)
    ldt = s5_log_dt.reshape(depth, S5_GROUPS, 1)
    out_shape = (
        jax.ShapeDtypeStruct((depth, n_pow, S5_GROUPS, S5_STATE), F32),
        jax.ShapeDtypeStruct((depth, n_pow, S5_GROUPS, S5_STATE), F32),
        jax.ShapeDtypeStruct((depth, S5_GROUPS, S5_GROUP, S5_STATE), F32),
        jax.ShapeDtypeStruct((depth, S5_GROUPS, S5_GROUP, S5_STATE), F32),
    )
    return pl.pallas_call(_s5_prep_kernel, out_shape=out_shape, name="s5_prep")(
        s5_a_re, s5_a_im, ldt, b_re_t, b_im_t)


def _ada_kernel(c_ref, w_ref, b_ref, o_ref):
    s = _silu(c_ref[...])
    o_ref[0] = _dot(s, w_ref[0]) + b_ref[0]


def _ada(c_all, w_ada, b_ada):
    depth, d, n3 = w_ada.shape
    rows = c_all.shape[0]
    tn = d
    return pl.pallas_call(
        _ada_kernel,
        grid=(depth, n3 // tn),
        in_specs=[
            pl.BlockSpec((rows, d), lambda l, j: (0, 0)),
            pl.BlockSpec((1, d, tn), lambda l, j: (l, 0, j)),
            pl.BlockSpec((1, 1, tn), lambda l, j: (l, 0, j)),
        ],
        out_specs=pl.BlockSpec((1, rows, tn), lambda l, j: (l, 0, j)),
        out_shape=jax.ShapeDtypeStruct((depth, rows, n3), F32),
        compiler_params=pltpu.CompilerParams(
            dimension_semantics=("arbitrary", "arbitrary"), vmem_limit_bytes=VMEM_LIMIT),
        name="adaln",
    )(c_all, w_ada, b_ada.reshape(depth, 1, n3))


def _prompt_kernel(x_ref, mod_ref, ng_ref, win_ref, wout_ref,
                   cq_ref, sq_ref, ck_ref, sk_ref, intra_ref, qdec_ref, kdec_ref, cdec_ref,
                   gng_ref, cw_ref, cb_ref, wg_ref, bg_ref, lam_ref,
                   bd_ref, cd_ref, apr_ref, api_ref, s5d_ref, wglu_ref, fg_ref,
                   y_ref, sret_ref, hl_ref, cv_ref, sre_ref, sim_ref,
                   proj_scr, ycat_scr, perm_scr, xpad_scr, ab_scr, st_scr, st16_scr, yp_scr, h_scr,
                   ph_scr, hloc_scr, *, final):
    tl = x_ref.shape[1]
    seg = tl // SUBLANES
    t = pl.program_id(1)
    (ng_ref, win_ref, wout_ref, gng_ref, cw_ref, cb_ref, wg_ref, bg_ref, lam_ref,
     bd_ref, cd_ref, apr_ref, api_ref, s5d_ref, wglu_ref) = (
        r.at[0] for r in (ng_ref, win_ref, wout_ref, gng_ref, cw_ref, cb_ref, wg_ref, bg_ref,
                          lam_ref, bd_ref, cd_ref, apr_ref, api_ref, s5d_ref, wglu_ref))

    @pl.when(t == 0)
    def _():
        sret_ref[...] = jnp.zeros_like(sret_ref)
        hl_ref[...] = jnp.zeros_like(hl_ref)
        sre_ref[...] = jnp.zeros_like(sre_ref)
        sim_ref[...] = jnp.zeros_like(sim_ref)
        cv_ref[...] = jnp.zeros_like(cv_ref)

    x = x_ref[0]
    shift, scale = mod_ref[0, 0:1, :], mod_ref[0, 1:2, :]
    h = _rmsnorm(x, ng_ref[...] * (1.0 + scale)) + shift
    h_scr[...] = h.astype(BF16)
    proj_b = jnp.dot(h_scr[...], win_ref[:, pl.ds(OFF_XL, D_IN - OFF_XL)],
                     preferred_element_type=F32)

    def proj_a_chunk(ci):
        cols = pl.ds(ci * PROJ_CHUNK, PROJ_CHUNK)
        proj_scr[:, cols] = jnp.dot(h_scr[...], win_ref[:, cols], preferred_element_type=F32)

    for m in range(tl // SUBLANES):
        s, r0 = divmod(m * SUBLANES, seg)
        for kb in range((D_IN - OFF_XL) // LANES):
            perm_scr[kb, pl.ds(r0 * SUBLANES + s, SUBLANES, stride=SUBLANES), :] = (
                proj_b[m * SUBLANES:(m + 1) * SUBLANES, kb * LANES:(kb + 1) * LANES])

    def ret_piece(c, hd):
        rows = pl.ds(c * RET_CHUNK, RET_CHUNK)
        lanes = pl.ds(hd * HEAD_DIM, HEAD_DIM)
        q = _rotary(proj_scr[rows, pl.ds(OFF_Q + hd * HEAD_DIM, HEAD_DIM)],
                    cq_ref[rows, :], sq_ref[rows, :])
        k = _rotary(proj_scr[rows, pl.ds(OFF_K + hd * HEAD_DIM, HEAD_DIM)],
                    ck_ref[rows, :], sk_ref[rows, :])
        v = proj_scr[rows, pl.ds(OFF_V + hd * HEAD_DIM, HEAD_DIM)].astype(BF16)
        s_old = sret_ref[0, hd]
        sc = _dot_nt(q, k) * intra_ref[hd]
        o = _dot(sc, v) + _dot(q * qdec_ref[hd], s_old)
        kd_t = (k * kdec_ref[hd]).T
        sret_ref[0, hd] = cdec_ref[hd] * s_old + _dot(kd_t, v)
        on = _groupnorm(o, gng_ref[:, lanes])
        g_ret = proj_scr[rows, pl.ds(OFF_GR + hd * HEAD_DIM, HEAD_DIM)]
        ycat_scr[rows, lanes] = (on * _silu(g_ret)).astype(BF16)

    sub = lax.broadcasted_iota(jnp.int32, (SUBLANES, 1), 0)
    rg = lambda r: pl.ds(r * SUBLANES, SUBLANES)

    def pcols(lo, width):
        return jnp.concatenate([perm_scr[kb] for kb in range(lo // LANES, (lo + width) // LANES)],
                               axis=1)

    n_chunks = OFF_XL // PROJ_CHUNK
    s5_parts = n_chunks - 3

    proj_a_chunk(0)
    xl = pcols(OFF_XL - OFF_XL, LRU_WIDTH)
    prev = cv_ref[0]
    npre = CONV_WIDTH - 1
    for m in range(1, CONV_WIDTH):
        grp = xl[(seg - m) * SUBLANES:(seg - m + 1) * SUBLANES, :]
        e_m = jnp.where(sub == 0, prev[npre - m:npre - m + 1, :], pltpu.roll(grp, 1, axis=0))
        xpad_scr[rg(npre - m), :] = e_m
    xpad_scr[pl.ds(npre * SUBLANES, tl), :] = xl
    cv_ref[0] = jnp.concatenate(
        [xl[(seg - m) * SUBLANES + SUBLANES - 1:(seg - m + 1) * SUBLANES, :]
         for m in range(npre, 0, -1)], axis=0)
    xc = cb_ref[...]
    for j in range(CONV_WIDTH):
        xc = xc + xpad_scr[pl.ds(j * SUBLANES, tl), :] * cw_ref[pl.ds(j, 1), :]
    gates = _dot(xc, wg_ref[...]) + bg_ref[...]
    a, b = _lru_coeffs(xc, gates, _softplus(-lam_ref[...]))
    a_l, b_l = pl.ds(0, LRU_WIDTH), pl.ds(LRU_WIDTH, LRU_WIDTH)
    ab_scr[:, a_l] = a
    ab_scr[:, b_l] = b
    proj_a_chunk(1)
    hloc = jnp.zeros((SUBLANES, LRU_WIDTH), F32)
    prod = jnp.ones((SUBLANES, LRU_WIDTH), F32)
    for r in range(seg):
        ar = ab_scr[rg(r), a_l]
        hloc = ar * hloc + ab_scr[rg(r), b_l]
        prod = prod * ar
        ph_scr[rg(r), a_l] = prod
        ph_scr[rg(r), b_l] = hloc
    c = hl_ref[0]
    cs = [c]
    for s in range(SUBLANES - 1):
        c = hloc[s:s + 1, :] + prod[s:s + 1, :] * c
        cs.append(c)
    hl_ref[0] = hloc[SUBLANES - 1:, :] + prod[SUBLANES - 1:, :] * c
    c_all = jnp.concatenate(cs, axis=0)
    sgl = _silu(pcols(OFF_GL - OFF_XL, LRU_WIDTH))
    for r in range(seg):
        y = (ph_scr[rg(r), b_l] + ph_scr[rg(r), a_l] * c_all) * sgl[r * SUBLANES:(r + 1) * SUBLANES, :]
        for kb in range(LRU_WIDTH // LANES):
            yp_scr[kb, rg(r), :] = y[:, kb * LANES:(kb + 1) * LANES]

    proj_a_chunk(2)
    u = pcols(OFF_U - OFF_XL, S5_WIDTH)
    st_scr[...] = _dot(u, bd_ref[...])
    re_l, im_l = pl.ds(0, S5_FLAT), pl.ds(S5_FLAT, S5_FLAT)
    bshape = (SUBLANES, S5_FLAT)
    a1r = jnp.broadcast_to(apr_ref[pl.ds(0, 1), :], bshape)
    a1i = jnp.broadcast_to(api_ref[pl.ds(0, 1), :], bshape)
    hr = jnp.zeros((SUBLANES, S5_FLAT), F32)
    hi = jnp.zeros((SUBLANES, S5_FLAT), F32)
    for r in range(seg):
        if r % (-(-seg // s5_parts)) == 0:
            proj_a_chunk(3 + r // (-(-seg // s5_parts)))
        hr, hi = (a1r * hr - a1i * hi + st_scr[rg(r), re_l],
                  a1r * hi + a1i * hr + st_scr[rg(r), im_l])
        hloc_scr[rg(r), re_l] = hr
        hloc_scr[rg(r), im_l] = hi
    asr, asi = apr_ref[pl.ds(seg - 1, 1), :], api_ref[pl.ds(seg - 1, 1), :]
    cr, ci = sre_ref[0], sim_ref[0]
    crs, cis = [cr], [ci]
    for s in range(SUBLANES - 1):
        cr, ci = (hr[s:s + 1, :] + (asr * cr - asi * ci), hi[s:s + 1, :] + (asr * ci + asi * cr))
        crs.append(cr)
        cis.append(ci)
    sre_ref[0] = hr[SUBLANES - 1:, :] + (asr * cr - asi * ci)
    sim_ref[0] = hi[SUBLANES - 1:, :] + (asr * ci + asi * cr)
    cr_all, ci_all = jnp.concatenate(crs, axis=0), jnp.concatenate(cis, axis=0)
    n_pieces = (tl // RET_CHUNK) * RET_HEADS
    per_piece = (seg // 2) // n_pieces
    for r2 in range(seg // 2):
        if r2 % per_piece == 0:
            ret_piece(*divmod(r2 // per_piece, RET_HEADS))
        parts_r, parts_i = [], []
        for r in (2 * r2, 2 * r2 + 1):
            qr = jnp.broadcast_to(apr_ref[pl.ds(r, 1), :], bshape)
            qi = jnp.broadcast_to(api_ref[pl.ds(r, 1), :], bshape)
            parts_r.append(hloc_scr[rg(r), re_l] + (qr * cr_all - qi * ci_all))
            parts_i.append(hloc_scr[rg(r), im_l] + (qr * ci_all + qi * cr_all))
        rows2 = pl.ds(r2 * 2 * SUBLANES, 2 * SUBLANES)
        st16_scr[rows2, re_l] = jnp.concatenate(parts_r, axis=0).astype(BF16)
        st16_scr[rows2, im_l] = jnp.concatenate(parts_i, axis=0).astype(BF16)
    gs = pcols(OFF_GS - OFF_XL, S5_WIDTH)
    y_s5 = _s5_tail(st16_scr[...], u, gs, cd_ref[...], s5d_ref[...], wglu_ref[...])
    for kb in range(S5_WIDTH // LANES):
        yp_scr[LRU_WIDTH // LANES + kb] = y_s5[:, kb * LANES:(kb + 1) * LANES]

    for m2 in range(tl // (2 * SUBLANES)):
        for kb in range((LRU_WIDTH + S5_WIDTH) // LANES):
            halves = []
            for m in (2 * m2, 2 * m2 + 1):
                s, r0 = divmod(m * SUBLANES, seg)
                halves.append(yp_scr[kb, pl.ds(r0 * SUBLANES + s, SUBLANES, stride=SUBLANES), :])
            ycat_scr[pl.ds(m2 * 2 * SUBLANES, 2 * SUBLANES), pl.ds(RET_WIDTH + kb * LANES, LANES)] = (
                jnp.concatenate(halves, axis=0).astype(BF16))

    out = jnp.dot(ycat_scr[...], wout_ref[...], preferred_element_type=F32)
    x_new = x_ref[0] + mod_ref[0, 2:3, :] * out
    if final:
        x_new = _rmsnorm(x_new, fg_ref[...])
    y_ref[0] = x_new


def _prompt_layer(x, mod, w, tabs, layer, *, final):
    bsz, seq, d = x.shape
    tl = min(SEQ_TILE, seq)
    nt = seq // tl

    def full(a):
        return pl.BlockSpec(a.shape, lambda b, t: (0,) * a.ndim)

    def lay(a):
        return pl.BlockSpec((1,) + a.shape[1:], lambda b, t: (layer,) + (0,) * (a.ndim - 1))

    row_tab = pl.BlockSpec((tl, HEAD_DIM), lambda b, t: (t, 0))
    in_specs = [
        pl.BlockSpec((1, tl, d), lambda b, t: (b, t, 0)),
        pl.BlockSpec((1, 3, d), lambda b, t: (b, 0, 0)),
        lay(w["norm_g"]), lay(w["w_in"]), lay(w["w_out"]),
        row_tab, row_tab, row_tab, row_tab,
        full(tabs["intra"]), full(tabs["q_dec"]), full(tabs["k_dec"]),
        pl.BlockSpec(memory_space=pltpu.SMEM),
        lay(w["gn_g"]), lay(w["conv_w"]), lay(w["conv_b"]), lay(w["w_gate"]), lay(w["b_gate"]),
        lay(w["lam"]), lay(w["bd"]), lay(w["cd"]), lay(w["apr"]), lay(w["api"]),
        lay(w["s5_d"]), lay(w["w_glu"]), full(w["final_g"]),
    ]
    out_shape = (
        jax.ShapeDtypeStruct((bsz, seq, d), F32),
        jax.ShapeDtypeStruct((bsz, RET_HEADS, HEAD_DIM, HEAD_DIM), F32),
        jax.ShapeDtypeStruct((bsz, 1, LRU_WIDTH), F32),
        jax.ShapeDtypeStruct((bsz, CONV_WIDTH - 1, LRU_WIDTH), F32),
        jax.ShapeDtypeStruct((bsz, 1, S5_FLAT), F32),
        jax.ShapeDtypeStruct((bsz, 1, S5_FLAT), F32),
    )
    out_specs = (
        pl.BlockSpec((1, tl, d), lambda b, t: (b, t, 0)),
        pl.BlockSpec((1, RET_HEADS, HEAD_DIM, HEAD_DIM), lambda b, t: (b, 0, 0, 0)),
        pl.BlockSpec((1, 1, LRU_WIDTH), lambda b, t: (b, 0, 0)),
        pl.BlockSpec((1, CONV_WIDTH - 1, LRU_WIDTH), lambda b, t: (b, 0, 0)),
        pl.BlockSpec((1, 1, S5_FLAT), lambda b, t: (b, 0, 0)),
        pl.BlockSpec((1, 1, S5_FLAT), lambda b, t: (b, 0, 0)),
    )
    scratch = [
        pltpu.VMEM((tl, OFF_XL), F32),
        pltpu.VMEM((tl, d), BF16),
        pltpu.VMEM(((D_IN - OFF_XL) // LANES, tl, LANES), F32),
        pltpu.VMEM((tl + (CONV_WIDTH - 1) * SUBLANES, LRU_WIDTH), F32),
        pltpu.VMEM((tl, 2 * LRU_WIDTH), F32),
        pltpu.VMEM((tl, 2 * S5_FLAT), F32),
        pltpu.VMEM((tl, 2 * S5_FLAT), BF16),
        pltpu.VMEM(((LRU_WIDTH + S5_WIDTH) // LANES, tl, LANES), F32),
        pltpu.VMEM((tl, d), BF16),
        pltpu.VMEM((tl, 2 * LRU_WIDTH), F32),
        pltpu.VMEM((tl, 2 * S5_FLAT), F32),
    ]
    return pl.pallas_call(
        functools.partial(_prompt_kernel, final=final),
        grid=(bsz, nt),
        in_specs=in_specs,
        out_specs=out_specs,
        out_shape=out_shape,
        scratch_shapes=scratch,
        compiler_params=pltpu.CompilerParams(
            dimension_semantics=("arbitrary", "arbitrary"), vmem_limit_bytes=VMEM_LIMIT),
        name="prompt_layer",
    )(x, mod, w["norm_g"], w["w_in"], w["w_out"],
      tabs["cos_q"], tabs["sin_q"], tabs["cos_k"], tabs["sin_k"],
      tabs["intra"], tabs["q_dec"], tabs["k_dec"], tabs["c_dec"],
      w["gn_g"], w["conv_w"], w["conv_b"], w["w_gate"], w["b_gate"], w["lam"],
      w["bd"], w["cd"], w["apr"], w["api"], w["s5_d"], w["w_glu"], w["final_g"])


def _decode_kernel(x_ref, mod_ref, ng_ref, win_ref, wout_ref,
                   cq_ref, sq_ref, ck_ref, sk_ref, gam_ref,
                   gng_ref, cw_ref, cb_ref, wg_ref, bg_ref, lam_ref,
                   bd_ref, cd_ref, apr_ref, api_ref, s5d_ref, wglu_ref, fg_ref,
                   sret_in, hl_in, cv_in, sre_in, sim_in, sel_ref,
                   y_ref, sret_ref, hl_ref, cv_ref, sre_ref, sim_ref,
                   qkt_scr, v_scr, o_scr, gr_scr, ycat_scr, x_scr):
    layer = pl.program_id(0)
    i = pl.program_id(1)
    nb = pl.num_programs(1)
    rows_n, d = x_ref.shape
    (mod_ref, ng_ref, win_ref, wout_ref, gng_ref, cw_ref, cb_ref, wg_ref, bg_ref, lam_ref,
     bd_ref, cd_ref, apr_ref, api_ref, s5d_ref, wglu_ref,
     sret_in, hl_in, cv_in, sre_in, sim_in, sret_ref, hl_ref, cv_ref, sre_ref, sim_ref) = (
        r.at[0] for r in (mod_ref, ng_ref, win_ref, wout_ref, gng_ref, cw_ref, cb_ref, wg_ref,
                          bg_ref, lam_ref, bd_ref, cd_ref, apr_ref, api_ref, s5d_ref, wglu_ref,
                          sret_in, hl_in, cv_in, sre_in, sim_in,
                          sret_ref, hl_ref, cv_ref, sre_ref, sim_ref))

    @pl.when((i == 0) & (layer == 0))
    def _():
        x_scr[...] = x_ref[...]

    @pl.when(i == 0)
    def _():
        x = x_scr[...]
        shift, scale = mod_ref[:, pl.ds(0, d)], mod_ref[:, pl.ds(d, d)]
        h = _rmsnorm(x, ng_ref[...]) * (1.0 + scale) + shift
        proj = _dot(h, win_ref[...])
        cq, sq, ck, sk = cq_ref[...], sq_ref[...], ck_ref[...], sk_ref[...]
        for hd in range(RET_HEADS):
            lo = hd * HEAD_DIM
            for part, src in enumerate((_rotary(proj[:, OFF_Q + lo:OFF_Q + lo + HEAD_DIM], cq, sq),
                                        _rotary(proj[:, OFF_K + lo:OFF_K + lo + HEAD_DIM], ck, sk))):
                xt = src.T
                hi = xt.astype(BF16)
                trows = pl.ds((2 * hd + part) * HEAD_DIM, HEAD_DIM)
                qkt_scr[trows, pl.ds(0, rows_n)] = hi
                qkt_scr[trows, pl.ds(rows_n, rows_n)] = (xt - hi.astype(F32)).astype(BF16)
        v_scr[...] = proj[:, OFF_V:OFF_V + RET_WIDTH]
        gr_scr[...] = _silu(proj[:, OFF_GR:OFF_GR + RET_WIDTH])
        xl = proj[:, OFF_XL:OFF_XL + LRU_WIDTH]
        cs = cv_in[...]
        xc = cb_ref[...] + xl * cw_ref[pl.ds(CONV_WIDTH - 1, 1), :]
        for j in range(CONV_WIDTH - 1):
            xc = xc + cs[:, j * LRU_WIDTH:(j + 1) * LRU_WIDTH] * cw_ref[pl.ds(j, 1), :]
        cv_ref[:, pl.ds(0, 2 * LRU_WIDTH)] = cs[:, LRU_WIDTH:]
        cv_ref[:, pl.ds(2 * LRU_WIDTH, LRU_WIDTH)] = xl
        gates = _dot(xc, wg_ref[...]) + bg_ref[...]
        a, b = _lru_coeffs(xc, gates, _softplus(-lam_ref[...]))
        hh = b + a * hl_in[...]
        hl_ref[...] = hh
        ycat_scr[:, pl.ds(RET_WIDTH, LRU_WIDTH)] = (
            hh * _silu(proj[:, OFF_GL:OFF_GL + LRU_WIDTH])).astype(BF16)
        u = proj[:, OFF_U:OFF_U + S5_WIDTH]
        bu = _dot(u, bd_ref[...])
        ar, ai = apr_ref[pl.ds(0, 1), :], api_ref[pl.ds(0, 1), :]
        s0r, s0i = sre_in[...], sim_in[...]
        sr = bu[:, :S5_FLAT] + (ar * s0r - ai * s0i)
        si = bu[:, S5_FLAT:] + (ar * s0i + ai * s0r)
        sre_ref[...] = sr
        sim_ref[...] = si
        y_s5 = _s5_tail(jnp.concatenate([sr, si], axis=-1), u, proj[:, OFF_GS:OFF_GS + S5_WIDTH],
                        cd_ref[...], s5d_ref[...], wglu_ref[...])
        ycat_scr[:, pl.ds(RET_WIDTH + LRU_WIDTH, S5_WIDTH)] = y_s5.astype(BF16)

    blk_rows = pl.ds(pl.multiple_of(i * DEC_BLOCK, DEC_BLOCK), DEC_BLOCK)
    o_rows = [[] for _ in range(RET_HEADS)]
    vblks = [v_scr[blk_rows, pl.ds(hd * HEAD_DIM, HEAD_DIM)] for hd in range(RET_HEADS)]
    for jp in range(DEC_BLOCK // 2):
        sel = sel_ref[0, :, pl.ds(jp * 2 * HEAD_DIM, 2 * HEAD_DIM)]
        tiles = jnp.dot(qkt_scr[...], jnp.concatenate([sel, sel], axis=0),
                        preferred_element_type=F32)
        for hd in range(RET_HEADS):
            qb = tiles[2 * hd * HEAD_DIM:(2 * hd + 1) * HEAD_DIM, :]
            kb = tiles[(2 * hd + 1) * HEAD_DIM:(2 * hd + 2) * HEAD_DIM, :]
            for j in (2 * jp, 2 * jp + 1):
                cols = slice((j % 2) * HEAD_DIM, (j % 2 + 1) * HEAD_DIM)
                s_new = gam_ref[hd] * sret_in[j, hd] + kb[:, cols] * vblks[hd][j:j + 1, :]
                sret_ref[j, hd] = s_new
                o_rows[hd].append(jnp.sum(qb[:, cols] * s_new, axis=0, keepdims=True))
    for hd in range(RET_HEADS):
        o_scr[blk_rows, pl.ds(hd * HEAD_DIM, HEAD_DIM)] = jnp.concatenate(o_rows[hd], axis=0)

    @pl.when(i == nb - 1)
    def _():
        for hd in range(RET_HEADS):
            lanes = pl.ds(hd * HEAD_DIM, HEAD_DIM)
            on = _groupnorm(o_scr[:, lanes], gng_ref[:, lanes])
            ycat_scr[:, lanes] = (on * gr_scr[:, lanes]).astype(BF16)
        out = jnp.dot(ycat_scr[...], wout_ref[...], preferred_element_type=F32)
        x_new = x_scr[...] + mod_ref[:, pl.ds(2 * d, d)] * out
        x_scr[...] = x_new
        y_ref[...] = _rmsnorm(x_new, fg_ref[...])


def _decode_layers(x, mod, states, w, tabs):
    rows, d = x.shape
    depth = mod.shape[0]
    s_ret, s_h, s_conv, s_re, s_im = states
    nb = rows // DEC_BLOCK

    def full(a):
        return pl.BlockSpec(a.shape, lambda l, i: (0,) * a.ndim)

    def lay(a):
        return pl.BlockSpec((1,) + a.shape[1:], lambda l, i: (l,) + (0,) * (a.ndim - 1))

    smem = pl.BlockSpec(memory_space=pltpu.SMEM)
    sblk = pl.BlockSpec((1, DEC_BLOCK, RET_HEADS, HEAD_DIM, HEAD_DIM), lambda l, i: (l, i, 0, 0, 0))
    in_specs = [
        full(x), lay(mod), lay(w["norm_g"]), lay(w["w_in"]), lay(w["w_out"]),
        full(tabs["dcos_q"]), full(tabs["dsin_q"]), full(tabs["dcos_k"]), full(tabs["dsin_k"]), smem,
        lay(w["gn_g"]), lay(w["conv_w"]), lay(w["conv_b"]), lay(w["w_gate"]), lay(w["b_gate"]),
        lay(w["lam"]), lay(w["bd"]), lay(w["cd"]), lay(w["apr"]), lay(w["api"]),
        lay(w["s5_d"]), lay(w["w_glu"]), full(w["final_g"]),
        sblk, lay(s_h), lay(s_conv), lay(s_re), lay(s_im),
        pl.BlockSpec((1,) + tabs["row_select"].shape[1:], lambda l, i: (i, 0, 0)),
    ]
    out_shape = (
        jax.ShapeDtypeStruct((rows, d), F32),
        jax.ShapeDtypeStruct(s_ret.shape, F32),
        jax.ShapeDtypeStruct(s_h.shape, F32),
        jax.ShapeDtypeStruct(s_conv.shape, F32),
        jax.ShapeDtypeStruct(s_re.shape, F32),
        jax.ShapeDtypeStruct(s_im.shape, F32),
    )
    out_specs = (full(x), sblk, lay(s_h), lay(s_conv), lay(s_re), lay(s_im))
    scratch = [
        pltpu.VMEM((RET_HEADS * 2 * HEAD_DIM, 2 * rows), BF16),
        pltpu.VMEM((rows, RET_WIDTH), F32),
        pltpu.VMEM((rows, RET_WIDTH), F32),
        pltpu.VMEM((rows, RET_WIDTH), F32),
        pltpu.VMEM((rows, d), BF16),
        pltpu.VMEM((rows, d), F32),
    ]
    return pl.pallas_call(
        _decode_kernel,
        grid=(depth, nb),
        in_specs=in_specs,
        out_specs=out_specs,
        out_shape=out_shape,
        scratch_shapes=scratch,
        compiler_params=pltpu.CompilerParams(
            dimension_semantics=("arbitrary", "arbitrary"), vmem_limit_bytes=VMEM_LIMIT),
        name="decode_layers",
    )(x, mod, w["norm_g"], w["w_in"], w["w_out"],
      tabs["dcos_q"], tabs["dsin_q"], tabs["dcos_k"], tabs["dsin_k"], tabs["gamma"],
      w["gn_g"], w["conv_w"], w["conv_b"], w["w_gate"], w["b_gate"], w["lam"],
      w["bd"], w["cd"], w["apr"], w["api"], w["s5_d"], w["w_glu"], w["final_g"],
      s_ret, s_h, s_conv, s_re, s_im, tabs["row_select"])


def _rope_tables(pos, scale):
    half = HEAD_DIM // 2
    inv = ROPE_BASE ** (-np.arange(half, dtype=np.float64) / half)
    ang = pos[:, None] * inv[None, :]
    cos, sin = np.cos(ang), np.sin(ang)
    cosf = np.concatenate([cos, cos], axis=-1) * scale
    sinf = np.concatenate([-sin, sin], axis=-1) * scale
    return cosf, sinf


def _tables(seq, dec):
    c = RET_CHUNK
    log_g = np.log1p(-np.exp2(-5.0 - np.arange(RET_HEADS, dtype=np.float64)))
    idx = np.arange(c, dtype=np.float64)
    rel = idx[:, None] - idx[None, :]
    intra = np.where(rel[None] >= 0, np.exp(np.maximum(rel, 0.0)[None] * log_g[:, None, None]), 0.0)
    q_dec = np.exp((idx + 1.0)[:, None] * log_g[None, :])
    k_dec = np.exp((c - 1.0 - idx)[:, None] * log_g[None, :])
    q_dec = np.broadcast_to(q_dec.T[:, :, None], (RET_HEADS, c, HEAD_DIM))
    k_dec = np.broadcast_to(k_dec.T[:, :, None], (RET_HEADS, c, HEAD_DIM))
    c_dec = np.exp(c * log_g)
    gamma = np.exp(log_g)
    k_scale = HEAD_DIM ** -0.5
    cos_q, sin_q = _rope_tables(np.arange(seq, dtype=np.float64), 1.0)
    cos_k, sin_k = _rope_tables(np.arange(seq, dtype=np.float64), k_scale)
    dpos = PAST_LEN + np.arange(1, dtype=np.float64)
    dcos_q, dsin_q = _rope_tables(dpos, 1.0)
    dcos_k, dsin_k = _rope_tables(dpos, k_scale)
    blk = np.arange(dec // DEC_BLOCK)[:, None, None]
    row = np.arange(dec)[None, :, None]
    col = np.arange(DEC_BLOCK * HEAD_DIM)[None, None, :]
    row_select = row == blk * DEC_BLOCK + col // HEAD_DIM
    tabs = dict(intra=intra, q_dec=q_dec, k_dec=k_dec, c_dec=c_dec, gamma=gamma,
                cos_q=cos_q, sin_q=sin_q, cos_k=cos_k, sin_k=sin_k,
                dcos_q=dcos_q, dsin_q=dsin_q, dcos_k=dcos_k, dsin_k=dsin_k)
    tabs = {k: jnp.asarray(np.ascontiguousarray(v), dtype=F32) for k, v in tabs.items()}
    tabs["row_select"] = jnp.asarray(row_select, dtype=BF16)
    return tabs


def _block_diag(blocks):
    n, r, c = blocks.shape
    eye = jnp.eye(n, dtype=blocks.dtype)
    return jnp.einsum("nrc,nm->nrmc", blocks, eye).reshape(n * r, n * c)


def kernel(x_prompt, x_sample, state_ret, state_lru_h, state_lru_conv, state_s5_re, state_s5_im,
           c_prompt, c_sample, norm_g, w_ada, b_ada, w_in, ret_gn_g, conv_w, conv_b, w_rg, b_rg,
           w_ig, b_ig, lru_lambda, s5_a_re, s5_a_im, s5_b_re, s5_b_im, s5_c_re, s5_c_im, s5_d,
           s5_log_dt, s5_w_glu, w_out, final_g):
    depth = w_in.shape[0]
    bsz, seq, d = x_prompt.shape
    dec = x_sample.shape[0]
    assert x_sample.shape[1] == 1 and seq % RET_CHUNK == 0 and dec % DEC_BLOCK == 0

    tabs = _tables(seq, dec)
    mod_all = _ada(jnp.concatenate([c_prompt, c_sample], axis=0), w_ada, b_ada)
    n_pow = min(SEQ_TILE, seq) // SUBLANES
    apr, api, bbr, bbi = _s5_prep(s5_a_re, s5_a_im, s5_log_dt, s5_b_re, s5_b_im, n_pow)

    bdiag = jax.vmap(_block_diag)
    bd = jnp.concatenate([bdiag(bbr), bdiag(bbi)], axis=2)
    cd = jnp.concatenate([bdiag(jnp.swapaxes(s5_c_re, 2, 3)),
                          bdiag(jnp.swapaxes(-s5_c_im, 2, 3))], axis=1)
    w = dict(
        norm_g=norm_g.reshape(depth, 1, d),
        w_in=w_in.astype(BF16),
        w_out=w_out.astype(BF16),
        gn_g=ret_gn_g.reshape(depth, 1, RET_WIDTH),
        conv_w=conv_w,
        conv_b=conv_b.reshape(depth, 1, LRU_WIDTH),
        w_gate=jnp.concatenate([bdiag(w_rg), bdiag(w_ig)], axis=2).astype(BF16),
        b_gate=jnp.concatenate([b_rg, b_ig], axis=1).reshape(depth, 1, 2 * LRU_WIDTH),
        lam=lru_lambda.reshape(depth, 1, LRU_WIDTH),
        bd=bd.astype(BF16),
        cd=cd.astype(BF16),
        apr=apr.reshape(depth, n_pow, S5_FLAT),
        api=api.reshape(depth, n_pow, S5_FLAT),
        s5_d=s5_d.reshape(depth, 1, S5_WIDTH),
        w_glu=s5_w_glu.astype(BF16),
        final_g=final_g.reshape(1, d),
    )

    xp = x_prompt
    outs_p = []
    for l in range(depth):
        mod_p = mod_all[l, :bsz].reshape(bsz, 3, d)
        xp, sret, hl, cv, sre, sim = _prompt_layer(xp, mod_p, w, tabs, l, final=l == depth - 1)
        outs_p.append((sret, hl.reshape(bsz, LRU_WIDTH), cv,
                       sre.reshape(bsz, S5_GROUPS, S5_STATE), sim.reshape(bsz, S5_GROUPS, S5_STATE)))

    states = (state_ret, state_lru_h,
              state_lru_conv.reshape(depth, dec, (CONV_WIDTH - 1) * LRU_WIDTH),
              state_s5_re.reshape(depth, dec, S5_FLAT), state_s5_im.reshape(depth, dec, S5_FLAT))
    xs, sret_s, hl_s, cv_s, sre_s, sim_s = _decode_layers(
        x_sample.reshape(dec, d), mod_all[:, bsz:], states, w, tabs)

    def stk(k):
        return jnp.stack([o[k] for o in outs_p])

    return (xp, xs.reshape(dec, 1, d),
            stk(0), sret_s, stk(1), hl_s,
            stk(2), cv_s.reshape(depth, dec, CONV_WIDTH - 1, LRU_WIDTH),
            stk(3), sre_s.reshape(depth, dec, S5_GROUPS, S5_STATE),
            stk(4), sim_s.reshape(depth, dec, S5_GROUPS, S5_STATE))
```

```python
import functools
import math

import jax
import jax.numpy as jnp
import numpy as np
from jax import lax
from jax.experimental import pallas as pl
from jax.experimental.pallas import tpu as pltpu

RET_HEADS = 4
HEAD_DIM = 128
RET_WIDTH = RET_HEADS * HEAD_DIM
RET_CHUNK = 128
LRU_WIDTH = 256
LRU_BLOCKS = 4
LRU_C = 8.0
CONV_WIDTH = 4
S5_WIDTH = 256
S5_GROUP = 16
S5_GROUPS = 16
S5_STATE = 64
S5_FLAT = S5_GROUPS * S5_STATE
ROPE_BASE = 10000.0
EPS = 1e-6
PAST_LEN = 16384

OFF_Q = 0
OFF_K = OFF_Q + RET_WIDTH
OFF_V = OFF_K + RET_WIDTH
OFF_GR = OFF_V + RET_WIDTH
OFF_XL = OFF_GR + RET_WIDTH
OFF_GL = OFF_XL + LRU_WIDTH
OFF_U = OFF_GL + LRU_WIDTH
OFF_GS = OFF_U + S5_WIDTH
D_IN = OFF_GS + S5_WIDTH

SUBLANES = 8
LANES = 128
SEQ_TILE = 512
PROJ_CHUNK = 256
DEC_BLOCK = 16
VMEM_LIMIT = 56 * 1024 * 1024

F32 = jnp.float32
BF16 = jnp.bfloat16


def _sigmoid(x):
    return 1.0 / (1.0 + jnp.exp(-x))


def _silu(x):
    return x * _sigmoid(x)


def _gelu_tanh(x):
    c = math.sqrt(2.0 / math.pi)
    return x * (0.5 * (1.0 + jnp.tanh(c * (x + 0.044715 * (x * x * x)))))


def _softplus(x):
    return jnp.maximum(x, 0.0) + jnp.log1p(jnp.exp(-jnp.abs(x)))


def _rmsnorm(x, g):
    ms = jnp.mean(x * x, axis=-1, keepdims=True)
    return x * lax.rsqrt(ms + EPS) * g


def _dot(a, b):
    return jnp.dot(a.astype(BF16), b.astype(BF16), preferred_element_type=F32)


def _dot_nt(a, b):
    return lax.dot_general(a.astype(BF16), b.astype(BF16), (((1,), (1,)), ((), ())),
                           preferred_element_type=F32)


def _rotary(x, cosf, sinf):
    return x * cosf + pltpu.roll(x, HEAD_DIM // 2, axis=1) * sinf


def _groupnorm(o, g):
    mu = jnp.mean(o, axis=-1, keepdims=True)
    d = o - mu
    var = jnp.mean(d * d, axis=-1, keepdims=True)
    return d * lax.rsqrt(var + EPS) * g


def _lru_coeffs(xc, gates, sp):
    r = _sigmoid(gates[:, :LRU_WIDTH])
    ig = _sigmoid(gates[:, LRU_WIDTH:])
    log_a = (-LRU_C) * r * sp
    a = jnp.exp(log_a)
    th = jnp.tanh(log_a)
    mult = jnp.sqrt(-2.0 * th / (1.0 - th))
    return a, mult * ig * xc


def _s5_tail(sr_si, u, gs, cd, s5d, wglu):
    ys = _dot(sr_si, cd) + s5d * u
    ys = _gelu_tanh(ys)
    ys = ys * _sigmoid(_dot(ys, wglu))
    return ys * _silu(gs)


def _s5_prep_kernel(are_ref, aim_ref, ldt_ref, bre_ref, bim_ref,
                    apr_ref, api_ref, bbr_ref, bbi_ref):
    depth = are_ref.shape[0]
    for l in range(depth):
        a_re = are_ref[l]
        a_im = aim_ref[l]
        step = jnp.exp(ldt_ref[l])
        mag = jnp.exp(step * a_re)
        abar_r = mag * jnp.cos(step * a_im)
        abar_i = mag * jnp.sin(step * a_im)
        nr, ni = abar_r - 1.0, abar_i
        den = a_re * a_re + a_im * a_im
        fr = (nr * a_re + ni * a_im) / den
        fi = (ni * a_re - nr * a_im) / den
        for g in range(S5_GROUPS):
            frg, fig = fr[g:g + 1, :], fi[g:g + 1, :]
            b_re, b_im = bre_ref[l, g], bim_ref[l, g]
            bbr_ref[l, g] = frg * b_re - fig * b_im
            bbi_ref[l, g] = frg * b_im + fig * b_re
        pr, pi = abar_r, abar_i
        apr_ref[l, 0] = pr
        api_ref[l, 0] = pi
        for m in range(1, apr_ref.shape[1]):
            pr, pi = pr * abar_r - pi * abar_i, pr * abar_i + pi * abar_r
            apr_ref[l, m] = pr
            api_ref[l, m] = pi


def _s5_prep(s5_a_re, s5_a_im, s5_log_dt, s5_b_re, s5_b_im, n_pow):
    depth = s5_a_re.shape[0]
    b_re_t = jnp.swapaxes(s5_b_re, 2, 3)
    b_im_t = jnp.swapaxes(s5_b_im, 2, 3)
    ldt = s5_log_dt.reshape(depth, S5_GROUPS, 1)
    out_shape = (
        jax.ShapeDtypeStruct((depth, n_pow, S5_GROUPS, S5_STATE), F32),
        jax.ShapeDtypeStruct((depth, n_pow, S5_GROUPS, S5_STATE), F32),
        jax.ShapeDtypeStruct((depth, S5_GROUPS, S5_GROUP, S5_STATE), F32),
        jax.ShapeDtypeStruct((depth, S5_GROUPS, S5_GROUP, S5_STATE), F32),
    )
    return pl.pallas_call(_s5_prep_kernel, out_shape=out_shape, name="s5_prep")(
        s5_a_re, s5_a_im, ldt, b_re_t, b_im_t)


def _ada_kernel(c_ref, w_ref, b_ref, o_ref):
    s = _silu(c_ref[...])
    o_ref[0] = _dot(s, w_ref[0]) + b_ref[0]


def _ada(c_all, w_ada, b_ada):
    depth, d, n3 = w_ada.shape
    rows = c_all.shape[0]
    tn = d
    return pl.pallas_call(
        _ada_kernel,
        grid=(depth, n3 // tn),
        in_specs=[
            pl.BlockSpec((rows, d), lambda l, j: (0, 0)),
            pl.BlockSpec((1, d, tn), lambda l, j: (l, 0, j)),
            pl.BlockSpec((1, 1, tn), lambda l, j: (l, 0, j)),
        ],
        out_specs=pl.BlockSpec((1, rows, tn), lambda l, j: (l, 0, j)),
        out_shape=jax.ShapeDtypeStruct((depth, rows, n3), F32),
        compiler_params=pltpu.CompilerParams(
            dimension_semantics=("arbitrary", "arbitrary"), vmem_limit_bytes=VMEM_LIMIT),
        name="adaln",
    )(c_all, w_ada, b_ada.reshape(depth, 1, n3))


def _prompt_kernel(x_ref, mod_ref, ng_ref, win_ref, wout_ref,
                   cq_ref, sq_ref, ck_ref, sk_ref, intra_ref, qdec_ref, kdec_ref, cdec_ref,
                   gng_ref, cw_ref, cb_ref, wg_ref, bg_ref, lam_ref,
                   bd_ref, cd_ref, apr_ref, api_ref, s5d_ref, wglu_ref, fg_ref,
                   y_ref, sret_ref, hl_ref, cv_ref, sre_ref, sim_ref,
                   proj_scr, ycat_scr, perm_scr, xpad_scr, ab_scr, st_scr, st16_scr, yp_scr, h_scr,
                   ph_scr, hloc_scr, *, final):
    tl = x_ref.shape[1]
    seg = tl // SUBLANES
    t = pl.program_id(1)
    (ng_ref, win_ref, wout_ref, gng_ref, cw_ref, cb_ref, wg_ref, bg_ref, lam_ref,
     bd_ref, cd_ref, apr_ref, api_ref, s5d_ref, wglu_ref) = (
        r.at[0] for r in (ng_ref, win_ref, wout_ref, gng_ref, cw_ref, cb_ref, wg_ref, bg_ref,
                          lam_ref, bd_ref, cd_ref, apr_ref, api_ref, s5d_ref, wglu_ref))

    @pl.when(t == 0)
    def _():
        sret_ref[...] = jnp.zeros_like(sret_ref)
        hl_ref[...] = jnp.zeros_like(hl_ref)
        sre_ref[...] = jnp.zeros_like(sre_ref)
        sim_ref[...] = jnp.zeros_like(sim_ref)
        cv_ref[...] = jnp.zeros_like(cv_ref)

    x = x_ref[0]
    shift, scale = mod_ref[0, 0:1, :], mod_ref[0, 1:2, :]
    h = _rmsnorm(x, ng_ref[...] * (1.0 + scale)) + shift
    h_scr[...] = h.astype(BF16)
    proj_b = jnp.dot(h_scr[...], win_ref[:, pl.ds(OFF_XL, D_IN - OFF_XL)],
                     preferred_element_type=F32)

    def proj_a_chunk(ci):
        cols = pl.ds(ci * PROJ_CHUNK, PROJ_CHUNK)
        res = jnp.dot(h_scr[...], win_ref[:, cols], preferred_element_type=F32)
        for kl in range(PROJ_CHUNK // LANES):
            proj_scr[ci * (PROJ_CHUNK // LANES) + kl] = res[:, kl * LANES:(kl + 1) * LANES]

    for m in range(tl // SUBLANES):
        s, r0 = divmod(m * SUBLANES, seg)
        for kb in range((D_IN - OFF_XL) // LANES):
            perm_scr[kb, pl.ds(r0 * SUBLANES + s, SUBLANES, stride=SUBLANES), :] = (
                proj_b[m * SUBLANES:(m + 1) * SUBLANES, kb * LANES:(kb + 1) * LANES])

    def ret_piece(c, hd):
        rows = pl.ds(c * RET_CHUNK, RET_CHUNK)
        lanes = pl.ds(hd * HEAD_DIM, HEAD_DIM)
        q = _rotary(proj_scr[OFF_Q // LANES + hd, rows, :], cq_ref[rows, :], sq_ref[rows, :])
        k = _rotary(proj_scr[OFF_K // LANES + hd, rows, :], ck_ref[rows, :], sk_ref[rows, :])
        v = proj_scr[OFF_V // LANES + hd, rows, :].astype(BF16)
        s_old = sret_ref[0, hd]
        sc = _dot_nt(q, k) * intra_ref[hd]
        o = _dot(sc, v) + _dot(q * qdec_ref[hd], s_old)
        kd_t = (k * kdec_ref[hd]).T
        sret_ref[0, hd] = cdec_ref[hd] * s_old + _dot(kd_t, v)
        on = _groupnorm(o, gng_ref[:, lanes])
        g_ret = proj_scr[OFF_GR // LANES + hd, rows, :]
        ycat_scr[hd, rows, :] = (on * _silu(g_ret)).astype(BF16)

    sub = lax.broadcasted_iota(jnp.int32, (SUBLANES, 1), 0)
    rg = lambda r: pl.ds(r * SUBLANES, SUBLANES)

    def pcols(lo, width):
        return jnp.concatenate([perm_scr[kb] for kb in range(lo // LANES, (lo + width) // LANES)],
                               axis=1)

    n_chunks = OFF_XL // PROJ_CHUNK
    s5_parts = n_chunks - 3

    proj_a_chunk(0)
    xl = pcols(OFF_XL - OFF_XL, LRU_WIDTH)
    prev = cv_ref[0]
    npre = CONV_WIDTH - 1
    for m in range(1, CONV_WIDTH):
        grp = xl[(seg - m) * SUBLANES:(seg - m + 1) * SUBLANES, :]
        e_m = jnp.where(sub == 0, prev[npre - m:npre - m + 1, :], pltpu.roll(grp, 1, axis=0))
        xpad_scr[rg(npre - m), :] = e_m
    xpad_scr[pl.ds(npre * SUBLANES, tl), :] = xl
    cv_ref[0] = jnp.concatenate(
        [xl[(seg - m) * SUBLANES + SUBLANES - 1:(seg - m + 1) * SUBLANES, :]
         for m in range(npre, 0, -1)], axis=0)
    xc = cb_ref[...]
    for j in range(CONV_WIDTH):
        xc = xc + xpad_scr[pl.ds(j * SUBLANES, tl), :] * cw_ref[pl.ds(j, 1), :]
    gates = _dot(xc, wg_ref[...]) + bg_ref[...]
    a, b = _lru_coeffs(xc, gates, _softplus(-lam_ref[...]))
    ab_scr[0] = a
    ab_scr[1] = b
    proj_a_chunk(1)
    hloc = jnp.zeros((SUBLANES, LRU_WIDTH), F32)
    prod = jnp.ones((SUBLANES, LRU_WIDTH), F32)
    for r in range(seg):
        ar = ab_scr[0, rg(r), :]
        hloc = ar * hloc + ab_scr[1, rg(r), :]
        prod = prod * ar
        ph_scr[0, rg(r), :] = prod
        ph_scr[1, rg(r), :] = hloc
    c = hl_ref[0]
    cs = [c]
    for s in range(SUBLANES - 1):
        c = hloc[s:s + 1, :] + prod[s:s + 1, :] * c
        cs.append(c)
    hl_ref[0] = hloc[SUBLANES - 1:, :] + prod[SUBLANES - 1:, :] * c
    c_all = jnp.concatenate(cs, axis=0)
    sgl = _silu(pcols(OFF_GL - OFF_XL, LRU_WIDTH))
    for r in range(seg):
        y = (ph_scr[1, rg(r), :] + ph_scr[0, rg(r), :] * c_all) * sgl[r * SUBLANES:(r + 1) * SUBLANES, :]
        for kb in range(LRU_WIDTH // LANES):
            yp_scr[kb, rg(r), :] = y[:, kb * LANES:(kb + 1) * LANES]

    proj_a_chunk(2)
    u = pcols(OFF_U - OFF_XL, S5_WIDTH)
    bu = _dot(u, bd_ref[...])
    st_scr[0] = bu[:, :S5_FLAT]
    st_scr[1] = bu[:, S5_FLAT:]
    bshape = (SUBLANES, S5_FLAT)
    a1r = jnp.broadcast_to(apr_ref[pl.ds(0, 1), :], bshape)
    a1i = jnp.broadcast_to(api_ref[pl.ds(0, 1), :], bshape)
    hr = jnp.zeros((SUBLANES, S5_FLAT), F32)
    hi = jnp.zeros((SUBLANES, S5_FLAT), F32)
    for r in range(seg):
        if r % (-(-seg // s5_parts)) == 0:
            proj_a_chunk(3 + r // (-(-seg // s5_parts)))
        hr, hi = (a1r * hr - a1i * hi + st_scr[0, rg(r), :],
                  a1r * hi + a1i * hr + st_scr[1, rg(r), :])
        hloc_scr[0, rg(r), :] = hr
        hloc_scr[1, rg(r), :] = hi
    asr, asi = apr_ref[pl.ds(seg - 1, 1), :], api_ref[pl.ds(seg - 1, 1), :]
    cr, ci = sre_ref[0], sim_ref[0]
    crs, cis = [cr], [ci]
    for s in range(SUBLANES - 1):
        cr, ci = (hr[s:s + 1, :] + (asr * cr - asi * ci), hi[s:s + 1, :] + (asr * ci + asi * cr))
        crs.append(cr)
        cis.append(ci)
    sre_ref[0] = hr[SUBLANES - 1:, :] + (asr * cr - asi * ci)
    sim_ref[0] = hi[SUBLANES - 1:, :] + (asr * ci + asi * cr)
    cr_all, ci_all = jnp.concatenate(crs, axis=0), jnp.concatenate(cis, axis=0)
    n_pieces = (tl // RET_CHUNK) * RET_HEADS
    per_piece = (seg // 2) // n_pieces
    for r2 in range(seg // 2):
        if r2 % per_piece == 0:
            ret_piece(*divmod(r2 // per_piece, RET_HEADS))
        parts_r, parts_i = [], []
        for r in (2 * r2, 2 * r2 + 1):
            qr = jnp.broadcast_to(apr_ref[pl.ds(r, 1), :], bshape)
            qi = jnp.broadcast_to(api_ref[pl.ds(r, 1), :], bshape)
            parts_r.append(hloc_scr[0, rg(r), :] + (qr * cr_all - qi * ci_all))
            parts_i.append(hloc_scr[1, rg(r), :] + (qr * ci_all + qi * cr_all))
        rows2 = pl.ds(r2 * 2 * SUBLANES, 2 * SUBLANES)
        st16_scr[0, rows2, :] = jnp.concatenate(parts_r, axis=0).astype(BF16)
        st16_scr[1, rows2, :] = jnp.concatenate(parts_i, axis=0).astype(BF16)
    gs = pcols(OFF_GS - OFF_XL, S5_WIDTH)
    y_s5 = _s5_tail(jnp.concatenate([st16_scr[0], st16_scr[1]], axis=1), u, gs,
                    cd_ref[...], s5d_ref[...], wglu_ref[...])
    for kb in range(S5_WIDTH // LANES):
        yp_scr[LRU_WIDTH // LANES + kb] = y_s5[:, kb * LANES:(kb + 1) * LANES]

    for m2 in range(tl // (2 * SUBLANES)):
        for kb in range((LRU_WIDTH + S5_WIDTH) // LANES):
            halves = []
            for m in (2 * m2, 2 * m2 + 1):
                s, r0 = divmod(m * SUBLANES, seg)
                halves.append(yp_scr[kb, pl.ds(r0 * SUBLANES + s, SUBLANES, stride=SUBLANES), :])
            ycat_scr[RET_WIDTH // LANES + kb, pl.ds(m2 * 2 * SUBLANES, 2 * SUBLANES), :] = (
                jnp.concatenate(halves, axis=0).astype(BF16))

    ycat = jnp.concatenate([ycat_scr[kb] for kb in range(ycat_scr.shape[0])], axis=1)
    out = jnp.dot(ycat, wout_ref[...], preferred_element_type=F32)
    x_new = x_ref[0] + mod_ref[0, 2:3, :] * out
    if final:
        x_new = _rmsnorm(x_new, fg_ref[...])
    y_ref[0] = x_new


def _prompt_layer(x, mod, w, tabs, layer, *, final):
    bsz, seq, d = x.shape
    tl = min(SEQ_TILE, seq)
    nt = seq // tl

    def full(a):
        return pl.BlockSpec(a.shape, lambda b, t: (0,) * a.ndim)

    def lay(a):
        return pl.BlockSpec((1,) + a.shape[1:], lambda b, t: (layer,) + (0,) * (a.ndim - 1))

    row_tab = pl.BlockSpec((tl, HEAD_DIM), lambda b, t: (t, 0))
    in_specs = [
        pl.BlockSpec((1, tl, d), lambda b, t: (b, t, 0)),
        pl.BlockSpec((1, 3, d), lambda b, t: (b, 0, 0)),
        lay(w["norm_g"]), lay(w["w_in"]), lay(w["w_out"]),
        row_tab, row_tab, row_tab, row_tab,
        full(tabs["intra"]), full(tabs["q_dec"]), full(tabs["k_dec"]),
        pl.BlockSpec(memory_space=pltpu.SMEM),
        lay(w["gn_g"]), lay(w["conv_w"]), lay(w["conv_b"]), lay(w["w_gate"]), lay(w["b_gate"]),
        lay(w["lam"]), lay(w["bd"]), lay(w["cd"]), lay(w["apr"]), lay(w["api"]),
        lay(w["s5_d"]), lay(w["w_glu"]), full(w["final_g"]),
    ]
    out_shape = (
        jax.ShapeDtypeStruct((bsz, seq, d), F32),
        jax.ShapeDtypeStruct((bsz, RET_HEADS, HEAD_DIM, HEAD_DIM), F32),
        jax.ShapeDtypeStruct((bsz, 1, LRU_WIDTH), F32),
        jax.ShapeDtypeStruct((bsz, CONV_WIDTH - 1, LRU_WIDTH), F32),
        jax.ShapeDtypeStruct((bsz, 1, S5_FLAT), F32),
        jax.ShapeDtypeStruct((bsz, 1, S5_FLAT), F32),
    )
    out_specs = (
        pl.BlockSpec((1, tl, d), lambda b, t: (b, t, 0)),
        pl.BlockSpec((1, RET_HEADS, HEAD_DIM, HEAD_DIM), lambda b, t: (b, 0, 0, 0)),
        pl.BlockSpec((1, 1, LRU_WIDTH), lambda b, t: (b, 0, 0)),
        pl.BlockSpec((1, CONV_WIDTH - 1, LRU_WIDTH), lambda b, t: (b, 0, 0)),
        pl.BlockSpec((1, 1, S5_FLAT), lambda b, t: (b, 0, 0)),
        pl.BlockSpec((1, 1, S5_FLAT), lambda b, t: (b, 0, 0)),
    )
    scratch = [
        pltpu.VMEM((OFF_XL // LANES, tl, LANES), F32),
        pltpu.VMEM((d // LANES, tl, LANES), BF16),
        pltpu.VMEM(((D_IN - OFF_XL) // LANES, tl, LANES), F32),
        pltpu.VMEM((tl + (CONV_WIDTH - 1) * SUBLANES, LRU_WIDTH), F32),
        pltpu.VMEM((2, tl, LRU_WIDTH), F32),
        pltpu.VMEM((2, tl, S5_FLAT), F32),
        pltpu.VMEM((2, tl, S5_FLAT), BF16),
        pltpu.VMEM(((LRU_WIDTH + S5_WIDTH) // LANES, tl, LANES), F32),
        pltpu.VMEM((tl, d), BF16),
        pltpu.VMEM((2, tl, LRU_WIDTH), F32),
        pltpu.VMEM((2, tl, S5_FLAT), F32),
    ]
    return pl.pallas_call(
        functools.partial(_prompt_kernel, final=final),
        grid=(bsz, nt),
        in_specs=in_specs,
        out_specs=out_specs,
        out_shape=out_shape,
        scratch_shapes=scratch,
        compiler_params=pltpu.CompilerParams(
            dimension_semantics=("arbitrary", "arbitrary"), vmem_limit_bytes=VMEM_LIMIT),
        name="prompt_layer",
    )(x, mod, w["norm_g"], w["w_in"], w["w_out"],
      tabs["cos_q"], tabs["sin_q"], tabs["cos_k"], tabs["sin_k"],
      tabs["intra"], tabs["q_dec"], tabs["k_dec"], tabs["c_dec"],
      w["gn_g"], w["conv_w"], w["conv_b"], w["w_gate"], w["b_gate"], w["lam"],
      w["bd"], w["cd"], w["apr"], w["api"], w["s5_d"], w["w_glu"], w["final_g"])


def _decode_kernel(x_ref, mod_ref, ng_ref, win_ref, wout_ref,
                   cq_ref, sq_ref, ck_ref, sk_ref, gam_ref,
                   gng_ref, cw_ref, cb_ref, wg_ref, bg_ref, lam_ref,
                   bd_ref, cd_ref, apr_ref, api_ref, s5d_ref, wglu_ref, fg_ref,
                   sret_in, hl_in, cv_in, sre_in, sim_in, sel_ref,
                   y_ref, sret_ref, hl_ref, cv_ref, sre_ref, sim_ref,
                   qkt_scr, v_scr, o_scr, gr_scr, ycat_scr, x_scr):
    layer = pl.program_id(0)
    i = pl.program_id(1)
    nb = pl.num_programs(1)
    rows_n, d = x_ref.shape
    (mod_ref, ng_ref, win_ref, wout_ref, gng_ref, cw_ref, cb_ref, wg_ref, bg_ref, lam_ref,
     bd_ref, cd_ref, apr_ref, api_ref, s5d_ref, wglu_ref,
     sret_in, hl_in, cv_in, sre_in, sim_in, sret_ref, hl_ref, cv_ref, sre_ref, sim_ref) = (
        r.at[0] for r in (mod_ref, ng_ref, win_ref, wout_ref, gng_ref, cw_ref, cb_ref, wg_ref,
                          bg_ref, lam_ref, bd_ref, cd_ref, apr_ref, api_ref, s5d_ref, wglu_ref,
                          sret_in, hl_in, cv_in, sre_in, sim_in,
                          sret_ref, hl_ref, cv_ref, sre_ref, sim_ref))

    @pl.when((i == 0) & (layer == 0))
    def _():
        x_scr[...] = x_ref[...]

    @pl.when(i == 0)
    def _():
        x = x_scr[...]
        shift, scale = mod_ref[:, pl.ds(0, d)], mod_ref[:, pl.ds(d, d)]
        h = _rmsnorm(x, ng_ref[...]) * (1.0 + scale) + shift
        proj = _dot(h, win_ref[...])
        cq, sq, ck, sk = cq_ref[...], sq_ref[...], ck_ref[...], sk_ref[...]
        for hd in range(RET_HEADS):
            lo = hd * HEAD_DIM
            for part, src in enumerate((_rotary(proj[:, OFF_Q + lo:OFF_Q + lo + HEAD_DIM], cq, sq),
                                        _rotary(proj[:, OFF_K + lo:OFF_K + lo + HEAD_DIM], ck, sk))):
                xt = src.T
                hi = xt.astype(BF16)
                trows = pl.ds((2 * hd + part) * HEAD_DIM, HEAD_DIM)
                qkt_scr[trows, pl.ds(0, rows_n)] = hi
                qkt_scr[trows, pl.ds(rows_n, rows_n)] = (xt - hi.astype(F32)).astype(BF16)
        v_scr[...] = proj[:, OFF_V:OFF_V + RET_WIDTH]
        gr_scr[...] = _silu(proj[:, OFF_GR:OFF_GR + RET_WIDTH])
        xl = proj[:, OFF_XL:OFF_XL + LRU_WIDTH]
        cs = cv_in[...]
        xc = cb_ref[...] + xl * cw_ref[pl.ds(CONV_WIDTH - 1, 1), :]
        for j in range(CONV_WIDTH - 1):
            xc = xc + cs[:, j * LRU_WIDTH:(j + 1) * LRU_WIDTH] * cw_ref[pl.ds(j, 1), :]
        cv_ref[:, pl.ds(0, 2 * LRU_WIDTH)] = cs[:, LRU_WIDTH:]
        cv_ref[:, pl.ds(2 * LRU_WIDTH, LRU_WIDTH)] = xl
        gates = _dot(xc, wg_ref[...]) + bg_ref[...]
        a, b = _lru_coeffs(xc, gates, _softplus(-lam_ref[...]))
        hh = b + a * hl_in[...]
        hl_ref[...] = hh
        ycat_scr[:, pl.ds(RET_WIDTH, LRU_WIDTH)] = (
            hh * _silu(proj[:, OFF_GL:OFF_GL + LRU_WIDTH])).astype(BF16)
        u = proj[:, OFF_U:OFF_U + S5_WIDTH]
        bu = _dot(u, bd_ref[...])
        ar, ai = apr_ref[pl.ds(0, 1), :], api_ref[pl.ds(0, 1), :]
        s0r, s0i = sre_in[...], sim_in[...]
        sr = bu[:, :S5_FLAT] + (ar * s0r - ai * s0i)
        si = bu[:, S5_FLAT:] + (ar * s0i + ai * s0r)
        sre_ref[...] = sr
        sim_ref[...] = si
        y_s5 = _s5_tail(jnp.concatenate([sr, si], axis=-1), u, proj[:, OFF_GS:OFF_GS + S5_WIDTH],
                        cd_ref[...], s5d_ref[...], wglu_ref[...])
        ycat_scr[:, pl.ds(RET_WIDTH + LRU_WIDTH, S5_WIDTH)] = y_s5.astype(BF16)

    blk_rows = pl.ds(pl.multiple_of(i * DEC_BLOCK, DEC_BLOCK), DEC_BLOCK)
    o_rows = [[] for _ in range(RET_HEADS)]
    vblks = [v_scr[blk_rows, pl.ds(hd * HEAD_DIM, HEAD_DIM)] for hd in range(RET_HEADS)]
    for jp in range(DEC_BLOCK // 2):
        sel = sel_ref[0, :, pl.ds(jp * 2 * HEAD_DIM, 2 * HEAD_DIM)]
        tiles = jnp.dot(qkt_scr[...], jnp.concatenate([sel, sel], axis=0),
                        preferred_element_type=F32)
        for hd in range(RET_HEADS):
            qb = tiles[2 * hd * HEAD_DIM:(2 * hd + 1) * HEAD_DIM, :]
            kb = tiles[(2 * hd + 1) * HEAD_DIM:(2 * hd + 2) * HEAD_DIM, :]
            for j in (2 * jp, 2 * jp + 1):
                cols = slice((j % 2) * HEAD_DIM, (j % 2 + 1) * HEAD_DIM)
                s_new = gam_ref[hd] * sret_in[j, hd] + kb[:, cols] * vblks[hd][j:j + 1, :]
                sret_ref[j, hd] = s_new
                o_rows[hd].append(jnp.sum(qb[:, cols] * s_new, axis=0, keepdims=True))
    for hd in range(RET_HEADS):
        o_scr[blk_rows, pl.ds(hd * HEAD_DIM, HEAD_DIM)] = jnp.concatenate(o_rows[hd], axis=0)

    @pl.when(i == nb - 1)
    def _():
        for hd in range(RET_HEADS):
            lanes = pl.ds(hd * HEAD_DIM, HEAD_DIM)
            on = _groupnorm(o_scr[:, lanes], gng_ref[:, lanes])
            ycat_scr[:, lanes] = (on * gr_scr[:, lanes]).astype(BF16)
        out = jnp.dot(ycat_scr[...], wout_ref[...], preferred_element_type=F32)
        x_new = x_scr[...] + mod_ref[:, pl.ds(2 * d, d)] * out
        x_scr[...] = x_new
        y_ref[...] = _rmsnorm(x_new, fg_ref[...])


def _decode_layers(x, mod, states, w, tabs):
    rows, d = x.shape
    depth = mod.shape[0]
    s_ret, s_h, s_conv, s_re, s_im = states
    nb = rows // DEC_BLOCK

    def full(a):
        return pl.BlockSpec(a.shape, lambda l, i: (0,) * a.ndim)

    def lay(a):
        return pl.BlockSpec((1,) + a.shape[1:], lambda l, i: (l,) + (0,) * (a.ndim - 1))

    smem = pl.BlockSpec(memory_space=pltpu.SMEM)
    sblk = pl.BlockSpec((1, DEC_BLOCK, RET_HEADS, HEAD_DIM, HEAD_DIM), lambda l, i: (l, i, 0, 0, 0))
    in_specs = [
        full(x), lay(mod), lay(w["norm_g"]), lay(w["w_in"]), lay(w["w_out"]),
        full(tabs["dcos_q"]), full(tabs["dsin_q"]), full(tabs["dcos_k"]), full(tabs["dsin_k"]), smem,
        lay(w["gn_g"]), lay(w["conv_w"]), lay(w["conv_b"]), lay(w["w_gate"]), lay(w["b_gate"]),
        lay(w["lam"]), lay(w["bd"]), lay(w["cd"]), lay(w["apr"]), lay(w["api"]),
        lay(w["s5_d"]), lay(w["w_glu"]), full(w["final_g"]),
        sblk, lay(s_h), lay(s_conv), lay(s_re), lay(s_im),
        pl.BlockSpec((1,) + tabs["row_select"].shape[1:], lambda l, i: (i, 0, 0)),
    ]
    out_shape = (
        jax.ShapeDtypeStruct((rows, d), F32),
        jax.ShapeDtypeStruct(s_ret.shape, F32),
        jax.ShapeDtypeStruct(s_h.shape, F32),
        jax.ShapeDtypeStruct(s_conv.shape, F32),
        jax.ShapeDtypeStruct(s_re.shape, F32),
        jax.ShapeDtypeStruct(s_im.shape, F32),
    )
    out_specs = (full(x), sblk, lay(s_h), lay(s_conv), lay(s_re), lay(s_im))
    scratch = [
        pltpu.VMEM((RET_HEADS * 2 * HEAD_DIM, 2 * rows), BF16),
        pltpu.VMEM((rows, RET_WIDTH), F32),
        pltpu.VMEM((rows, RET_WIDTH), F32),
        pltpu.VMEM((rows, RET_WIDTH), F32),
        pltpu.VMEM((rows, d), BF16),
        pltpu.VMEM((rows, d), F32),
    ]
    return pl.pallas_call(
        _decode_kernel,
        grid=(depth, nb),
        in_specs=in_specs,
        out_specs=out_specs,
        out_shape=out_shape,
        scratch_shapes=scratch,
        compiler_params=pltpu.CompilerParams(
            dimension_semantics=("arbitrary", "arbitrary"), vmem_limit_bytes=VMEM_LIMIT),
        name="decode_layers",
    )(x, mod, w["norm_g"], w["w_in"], w["w_out"],
      tabs["dcos_q"], tabs["dsin_q"], tabs["dcos_k"], tabs["dsin_k"], tabs["gamma"],
      w["gn_g"], w["conv_w"], w["conv_b"], w["w_gate"], w["b_gate"], w["lam"],
      w["bd"], w["cd"], w["apr"], w["api"], w["s5_d"], w["w_glu"], w["final_g"],
      s_ret, s_h, s_conv, s_re, s_im, tabs["row_select"])


def _rope_tables(pos, scale):
    half = HEAD_DIM // 2
    inv = ROPE_BASE ** (-np.arange(half, dtype=np.float64) / half)
    ang = pos[:, None] * inv[None, :]
    cos, sin = np.cos(ang), np.sin(ang)
    cosf = np.concatenate([cos, cos], axis=-1) * scale
    sinf = np.concatenate([-sin, sin], axis=-1) * scale
    return cosf, sinf


def _tables(seq, dec):
    c = RET_CHUNK
    log_g = np.log1p(-np.exp2(-5.0 - np.arange(RET_HEADS, dtype=np.float64)))
    idx = np.arange(c, dtype=np.float64)
    rel = idx[:, None] - idx[None, :]
    intra = np.where(rel[None] >= 0, np.exp(np.maximum(rel, 0.0)[None] * log_g[:, None, None]), 0.0)
    q_dec = np.exp((idx + 1.0)[:, None] * log_g[None, :])
    k_dec = np.exp((c - 1.0 - idx)[:, None] * log_g[None, :])
    q_dec = np.broadcast_to(q_dec.T[:, :, None], (RET_HEADS, c, HEAD_DIM))
    k_dec = np.broadcast_to(k_dec.T[:, :, None], (RET_HEADS, c, HEAD_DIM))
    c_dec = np.exp(c * log_g)
    gamma = np.exp(log_g)
    k_scale = HEAD_DIM ** -0.5
    cos_q, sin_q = _rope_tables(np.arange(seq, dtype=np.float64), 1.0)
    cos_k, sin_k = _rope_tables(np.arange(seq, dtype=np.float64), k_scale)
    dpos = PAST_LEN + np.arange(1, dtype=np.float64)
    dcos_q, dsin_q = _rope_tables(dpos, 1.0)
    dcos_k, dsin_k = _rope_tables(dpos, k_scale)
    blk = np.arange(dec // DEC_BLOCK)[:, None, None]
    row = np.arange(dec)[None, :, None]
    col = np.arange(DEC_BLOCK * HEAD_DIM)[None, None, :]
    row_select = row == blk * DEC_BLOCK + col // HEAD_DIM
    tabs = dict(intra=intra, q_dec=q_dec, k_dec=k_dec, c_dec=c_dec, gamma=gamma,
                cos_q=cos_q, sin_q=sin_q, cos_k=cos_k, sin_k=sin_k,
                dcos_q=dcos_q, dsin_q=dsin_q, dcos_k=dcos_k, dsin_k=dsin_k)
    tabs = {k: jnp.asarray(np.ascontiguousarray(v), dtype=F32) for k, v in tabs.items()}
    tabs["row_select"] = jnp.asarray(row_select, dtype=BF16)
    return tabs


def _block_diag(blocks):
    n, r, c = blocks.shape
    eye = jnp.eye(n, dtype=blocks.dtype)
    return jnp.einsum("nrc,nm->nrmc", blocks, eye).reshape(n * r, n * c)


def kernel(x_prompt, x_sample, state_ret, state_lru_h, state_lru_conv, state_s5_re, state_s5_im,
           c_prompt, c_sample, norm_g, w_ada, b_ada, w_in, ret_gn_g, conv_w, conv_b, w_rg, b_rg,
           w_ig, b_ig, lru_lambda, s5_a_re, s5_a_im, s5_b_re, s5_b_im, s5_c_re, s5_c_im, s5_d,
           s5_log_dt, s5_w_glu, w_out, final_g):
    depth = w_in.shape[0]
    bsz, seq, d = x_prompt.shape
    dec = x_sample.shape[0]
    assert x_sample.shape[1] == 1 and seq % RET_CHUNK == 0 and dec % DEC_BLOCK == 0

    tabs = _tables(seq, dec)
    mod_all = _ada(jnp.concatenate([c_prompt, c_sample], axis=0), w_ada, b_ada)
    n_pow = min(SEQ_TILE, seq) // SUBLANES
    apr, api, bbr, bbi = _s5_prep(s5_a_re, s5_a_im, s5_log_dt, s5_b_re, s5_b_im, n_pow)

    bdiag = jax.vmap(_block_diag)
    bd = jnp.concatenate([bdiag(bbr), bdiag(bbi)], axis=2)
    cd = jnp.concatenate([bdiag(jnp.swapaxes(s5_c_re, 2, 3)),
                          bdiag(jnp.swapaxes(-s5_c_im, 2, 3))], axis=1)
    w = dict(
        norm_g=norm_g.reshape(depth, 1, d),
        w_in=w_in.astype(BF16),
        w_out=w_out.astype(BF16),
        gn_g=ret_gn_g.reshape(depth, 1, RET_WIDTH),
        conv_w=conv_w,
        conv_b=conv_b.reshape(depth, 1, LRU_WIDTH),
        w_gate=jnp.concatenate([bdiag(w_rg), bdiag(w_ig)], axis=2).astype(BF16),
        b_gate=jnp.concatenate([b_rg, b_ig], axis=1).reshape(depth, 1, 2 * LRU_WIDTH),
        lam=lru_lambda.reshape(depth, 1, LRU_WIDTH),
        bd=bd.astype(BF16),
        cd=cd.astype(BF16),
        apr=apr.reshape(depth, n_pow, S5_FLAT),
        api=api.reshape(depth, n_pow, S5_FLAT),
        s5_d=s5_d.reshape(depth, 1, S5_WIDTH),
        w_glu=s5_w_glu.astype(BF16),
        final_g=final_g.reshape(1, d),
    )

    xp = x_prompt
    outs_p = []
    for l in range(depth):
        mod_p = mod_all[l, :bsz].reshape(bsz, 3, d)
        xp, sret, hl, cv, sre, sim = _prompt_layer(xp, mod_p, w, tabs, l, final=l == depth - 1)
        outs_p.append((sret, hl.reshape(bsz, LRU_WIDTH), cv,
                       sre.reshape(bsz, S5_GROUPS, S5_STATE), sim.reshape(bsz, S5_GROUPS, S5_STATE)))

    states = (state_ret, state_lru_h,
              state_lru_conv.reshape(depth, dec, (CONV_WIDTH - 1) * LRU_WIDTH),
              state_s5_re.reshape(depth, dec, S5_FLAT), state_s5_im.reshape(depth, dec, S5_FLAT))
    xs, sret_s, hl_s, cv_s, sre_s, sim_s = _decode_layers(
        x_sample.reshape(dec, d), mod_all[:, bsz:], states, w, tabs)

    def stk(k):
        return jnp.stack([o[k] for o in outs_p])

    return (xp, xs.reshape(dec, 1, d),
            stk(0), sret_s, stk(1), hl_s,
            stk(2), cv_s.reshape(depth, dec, CONV_WIDTH - 1, LRU_WIDTH),
            stk(3), sre_s.reshape(depth, dec, S5_GROUPS, S5_STATE),
            stk(4), sim_s.reshape(depth, dec, S5_GROUPS, S5_STATE))
```

```python
import functools
import math

import jax
import jax.numpy as jnp
import numpy as np
from jax import lax
from jax.experimental import pallas as pl
from jax.experimental.pallas import tpu as pltpu

RET_HEADS = 4
HEAD_DIM = 128
RET_WIDTH = RET_HEADS * HEAD_DIM
RET_CHUNK = 128
LRU_WIDTH = 256
LRU_C = 8.0
CONV_WIDTH = 4
S5_WIDTH = 256
S5_GROUP = 16
S5_GROUPS = 16
S5_STATE = 64
S5_FLAT = S5_GROUPS * S5_STATE
ROPE_BASE = 10000.0
EPS = 1e-6
PAST_LEN = 16384

OFF_Q = 0
OFF_K = OFF_Q + RET_WIDTH
OFF_V = OFF_K + RET_WIDTH
OFF_GR = OFF_V + RET_WIDTH
OFF_XL = OFF_GR + RET_WIDTH
OFF_GL = OFF_XL + LRU_WIDTH
OFF_U = OFF_GL + LRU_WIDTH
OFF_GS = OFF_U + S5_WIDTH
D_IN = OFF_GS + S5_WIDTH

SUBLANES = 8
LANES = 128
SEQ_TILE = 512
PROJ_CHUNK = 256
DEC_BLOCK = 16
VMEM_LIMIT = 56 * 1024 * 1024

F32 = jnp.float32
BF16 = jnp.bfloat16


def _sigmoid(x):
    return 1.0 / (1.0 + jnp.exp(-x))


def _silu(x):
    return x * _sigmoid(x)


def _gelu_tanh(x):
    c = math.sqrt(2.0 / math.pi)
    return x * (0.5 * (1.0 + jnp.tanh(c * (x + 0.044715 * (x * x * x)))))


def _softplus(x):
    return jnp.maximum(x, 0.0) + jnp.log1p(jnp.exp(-jnp.abs(x)))


def _rmsnorm(x, g):
    ms = jnp.mean(x * x, axis=-1, keepdims=True)
    return x * lax.rsqrt(ms + EPS) * g


def _dot(a, b):
    return jnp.dot(a.astype(BF16), b.astype(BF16), preferred_element_type=F32)


def _dot_nt(a, b):
    return lax.dot_general(a.astype(BF16), b.astype(BF16), (((1,), (1,)), ((), ())),
                           preferred_element_type=F32)


def _rotary(x, cosf, sinf):
    return x * cosf + pltpu.roll(x, HEAD_DIM // 2, axis=1) * sinf


def _groupnorm(o, g):
    mu = jnp.mean(o, axis=-1, keepdims=True)
    d = o - mu
    var = jnp.mean(d * d, axis=-1, keepdims=True)
    return d * lax.rsqrt(var + EPS) * g


def _lru_coeffs(xc, gates, sp):
    r = _sigmoid(gates[:, :LRU_WIDTH])
    ig = _sigmoid(gates[:, LRU_WIDTH:])
    log_a = (-LRU_C) * r * sp
    a = jnp.exp(log_a)
    th = jnp.tanh(log_a)
    mult = jnp.sqrt(-2.0 * th / (1.0 - th))
    return a, mult * ig * xc


def _s5_tail(sr_si, u, gs, cd, s5d, wglu):
    ys = _dot(sr_si, cd) + s5d * u
    ys = _gelu_tanh(ys)
    ys = ys * _sigmoid(_dot(ys, wglu))
    return ys * _silu(gs)


def _s5_prep_kernel(are_ref, aim_ref, ldt_ref, bre_ref, bim_ref,
                    apr_ref, api_ref, bbr_ref, bbi_ref):
    depth = are_ref.shape[0]
    for l in range(depth):
        a_re = are_ref[l]
        a_im = aim_ref[l]
        step = jnp.exp(ldt_ref[l])
        mag = jnp.exp(step * a_re)
        abar_r = mag * jnp.cos(step * a_im)
        abar_i = mag * jnp.sin(step * a_im)
        nr, ni = abar_r - 1.0, abar_i
        den = a_re * a_re + a_im * a_im
        fr = (nr * a_re + ni * a_im) / den
        fi = (ni * a_re - nr * a_im) / den
        for g in range(S5_GROUPS):
            frg, fig = fr[g:g + 1, :], fi[g:g + 1, :]
            b_re, b_im = bre_ref[l, g], bim_ref[l, g]
            bbr_ref[l, g] = frg * b_re - fig * b_im
            bbi_ref[l, g] = frg * b_im + fig * b_re
        pr, pi = abar_r, abar_i
        apr_ref[l, 0] = pr
        api_ref[l, 0] = pi
        for m in range(1, apr_ref.shape[1]):
            pr, pi = pr * abar_r - pi * abar_i, pr * abar_i + pi * abar_r
            apr_ref[l, m] = pr
            api_ref[l, m] = pi


def _s5_prep(s5_a_re, s5_a_im, s5_log_dt, s5_b_re, s5_b_im, n_pow):
    depth = s5_a_re.shape[0]
    b_re_t = jnp.swapaxes(s5_b_re, 2, 3)
    b_im_t = jnp.swapaxes(s5_b_im, 2, 3)
    ldt = s5_log_dt.reshape(depth, S5_GROUPS, 1)
    out_shape = (
        jax.ShapeDtypeStruct((depth, n_pow, S5_GROUPS, S5_STATE), F32),
        jax.ShapeDtypeStruct((depth, n_pow, S5_GROUPS, S5_STATE), F32),
        jax.ShapeDtypeStruct((depth, S5_GROUPS, S5_GROUP, S5_STATE), F32),
        jax.ShapeDtypeStruct((depth, S5_GROUPS, S5_GROUP, S5_STATE), F32),
    )
    return pl.pallas_call(_s5_prep_kernel, out_shape=out_shape, name="s5_prep")(
        s5_a_re, s5_a_im, ldt, b_re_t, b_im_t)


def _ada_kernel(c_ref, w_ref, b_ref, o_ref):
    s = _silu(c_ref[...])
    o_ref[0] = _dot(s, w_ref[0]) + b_ref[0]


def _ada(c_all, w_ada, b_ada):
    depth, d, n3 = w_ada.shape
    rows = c_all.shape[0]
    tn = d
    return pl.pallas_call(
        _ada_kernel,
        grid=(depth, n3 // tn),
        in_specs=[
            pl.BlockSpec((rows, d), lambda l, j: (0, 0)),
            pl.BlockSpec((1, d, tn), lambda l, j: (l, 0, j)),
            pl.BlockSpec((1, 1, tn), lambda l, j: (l, 0, j)),
        ],
        out_specs=pl.BlockSpec((1, rows, tn), lambda l, j: (l, 0, j)),
        out_shape=jax.ShapeDtypeStruct((depth, rows, n3), F32),
        compiler_params=pltpu.CompilerParams(
            dimension_semantics=("arbitrary", "arbitrary"), vmem_limit_bytes=VMEM_LIMIT),
        name="adaln",
    )(c_all, w_ada, b_ada.reshape(depth, 1, n3))


def _prompt_kernel(x_ref, mod_ref, ng_ref, win_ref, wout_ref,
                   cq_ref, sq_ref, ck_ref, sk_ref, intra_ref, qdec_ref, kdec_ref, cdec_ref,
                   gng_ref, cw_ref, cb_ref, wg_ref, bg_ref, lam_ref,
                   bd_ref, cd_ref, apr_ref, api_ref, s5d_ref, wglu_ref, fg_ref,
                   y_ref, sret_ref, hl_ref, cv_ref, sre_ref, sim_ref,
                   proj_scr, ycat_scr, perm_scr, xpad_scr, ab_scr, st_scr, st16_scr, yp_scr, h_scr,
                   ph_scr, hloc_scr, *, final):
    tl = x_ref.shape[1]
    seg = tl // SUBLANES
    t = pl.program_id(1)
    (ng_ref, win_ref, wout_ref, gng_ref, cw_ref, cb_ref, wg_ref, bg_ref, lam_ref,
     bd_ref, cd_ref, apr_ref, api_ref, s5d_ref, wglu_ref) = (
        r.at[0] for r in (ng_ref, win_ref, wout_ref, gng_ref, cw_ref, cb_ref, wg_ref, bg_ref,
                          lam_ref, bd_ref, cd_ref, apr_ref, api_ref, s5d_ref, wglu_ref))

    @pl.when(t == 0)
    def _():
        sret_ref[...] = jnp.zeros_like(sret_ref)
        hl_ref[...] = jnp.zeros_like(hl_ref)
        sre_ref[...] = jnp.zeros_like(sre_ref)
        sim_ref[...] = jnp.zeros_like(sim_ref)
        cv_ref[...] = jnp.zeros_like(cv_ref)

    x = x_ref[0]
    shift, scale = mod_ref[0, 0:1, :], mod_ref[0, 1:2, :]
    h = _rmsnorm(x, ng_ref[...] * (1.0 + scale)) + shift
    h_scr[...] = h.astype(BF16)
    proj_b = jnp.dot(h_scr[...], win_ref[:, pl.ds(OFF_XL, D_IN - OFF_XL)],
                     preferred_element_type=F32)

    def proj_a_chunk(ci):
        cols = pl.ds(ci * PROJ_CHUNK, PROJ_CHUNK)
        res = jnp.dot(h_scr[...], win_ref[:, cols], preferred_element_type=F32)
        for kl in range(PROJ_CHUNK // LANES):
            proj_scr[ci * (PROJ_CHUNK // LANES) + kl] = res[:, kl * LANES:(kl + 1) * LANES]

    for m in range(tl // SUBLANES):
        s, r0 = divmod(m * SUBLANES, seg)
        for kb in range((D_IN - OFF_XL) // LANES):
            perm_scr[kb, pl.ds(r0 * SUBLANES + s, SUBLANES, stride=SUBLANES), :] = (
                proj_b[m * SUBLANES:(m + 1) * SUBLANES, kb * LANES:(kb + 1) * LANES])

    def ret_piece(c, hd):
        rows = pl.ds(c * RET_CHUNK, RET_CHUNK)
        lanes = pl.ds(hd * HEAD_DIM, HEAD_DIM)
        q = _rotary(proj_scr[OFF_Q // LANES + hd, rows, :], cq_ref[rows, :], sq_ref[rows, :])
        k = _rotary(proj_scr[OFF_K // LANES + hd, rows, :], ck_ref[rows, :], sk_ref[rows, :])
        v = proj_scr[OFF_V // LANES + hd, rows, :].astype(BF16)
        s_old = sret_ref[0, hd]
        sc = _dot_nt(q, k) * intra_ref[hd]
        o = _dot(sc, v) + _dot(q * qdec_ref[hd], s_old)
        kd_t = (k * kdec_ref[hd]).T
        sret_ref[0, hd] = cdec_ref[hd] * s_old + _dot(kd_t, v)
        on = _groupnorm(o, gng_ref[:, lanes])
        g_ret = proj_scr[OFF_GR // LANES + hd, rows, :]
        ycat_scr[hd, rows, :] = (on * _silu(g_ret)).astype(BF16)

    sub = lax.broadcasted_iota(jnp.int32, (SUBLANES, 1), 0)
    rg = lambda r: pl.ds(r * SUBLANES, SUBLANES)

    def pcols(lo, width):
        return jnp.concatenate([perm_scr[kb] for kb in range(lo // LANES, (lo + width) // LANES)],
                               axis=1)

    n_chunks = OFF_XL // PROJ_CHUNK
    s5_parts = n_chunks - 3

    proj_a_chunk(0)
    xl = pcols(OFF_XL - OFF_XL, LRU_WIDTH)
    prev = cv_ref[0]
    npre = CONV_WIDTH - 1
    for m in range(1, CONV_WIDTH):
        grp = xl[(seg - m) * SUBLANES:(seg - m + 1) * SUBLANES, :]
        e_m = jnp.where(sub == 0, prev[npre - m:npre - m + 1, :], pltpu.roll(grp, 1, axis=0))
        xpad_scr[rg(npre - m), :] = e_m
    xpad_scr[pl.ds(npre * SUBLANES, tl), :] = xl
    cv_ref[0] = jnp.concatenate(
        [xl[(seg - m) * SUBLANES + SUBLANES - 1:(seg - m + 1) * SUBLANES, :]
         for m in range(npre, 0, -1)], axis=0)
    xc = cb_ref[...]
    for j in range(CONV_WIDTH):
        xc = xc + xpad_scr[pl.ds(j * SUBLANES, tl), :] * cw_ref[pl.ds(j, 1), :]
    gates = _dot(xc, wg_ref[...]) + bg_ref[...]
    a, b = _lru_coeffs(xc, gates, _softplus(-lam_ref[...]))
    ab_scr[0] = a
    ab_scr[1] = b
    proj_a_chunk(1)
    hloc = jnp.zeros((SUBLANES, LRU_WIDTH), F32)
    prod = jnp.ones((SUBLANES, LRU_WIDTH), F32)
    for r in range(seg):
        ar = ab_scr[0, rg(r), :]
        hloc = ar * hloc + ab_scr[1, rg(r), :]
        prod = prod * ar
        ph_scr[0, rg(r), :] = prod
        ph_scr[1, rg(r), :] = hloc
    c = hl_ref[0]
    cs = [c]
    for s in range(SUBLANES - 1):
        c = hloc[s:s + 1, :] + prod[s:s + 1, :] * c
        cs.append(c)
    hl_ref[0] = hloc[SUBLANES - 1:, :] + prod[SUBLANES - 1:, :] * c
    c_all = jnp.concatenate(cs, axis=0)
    sgl = _silu(pcols(OFF_GL - OFF_XL, LRU_WIDTH))
    for r in range(seg):
        y = (ph_scr[1, rg(r), :] + ph_scr[0, rg(r), :] * c_all) * sgl[r * SUBLANES:(r + 1) * SUBLANES, :]
        for kb in range(LRU_WIDTH // LANES):
            yp_scr[kb, rg(r), :] = y[:, kb * LANES:(kb + 1) * LANES]

    proj_a_chunk(2)
    u = pcols(OFF_U - OFF_XL, S5_WIDTH)
    bu = _dot(u, bd_ref[...])
    st_scr[0] = bu[:, :S5_FLAT]
    st_scr[1] = bu[:, S5_FLAT:]
    bshape = (SUBLANES, S5_FLAT)
    a1r = jnp.broadcast_to(apr_ref[pl.ds(0, 1), :], bshape)
    a1i = jnp.broadcast_to(api_ref[pl.ds(0, 1), :], bshape)
    hr = jnp.zeros((SUBLANES, S5_FLAT), F32)
    hi = jnp.zeros((SUBLANES, S5_FLAT), F32)
    for r in range(seg):
        if r % (-(-seg // s5_parts)) == 0:
            proj_a_chunk(3 + r // (-(-seg // s5_parts)))
        hr, hi = (a1r * hr - a1i * hi + st_scr[0, rg(r), :],
                  a1r * hi + a1i * hr + st_scr[1, rg(r), :])
        hloc_scr[0, rg(r), :] = hr
        hloc_scr[1, rg(r), :] = hi
    asr, asi = apr_ref[pl.ds(seg - 1, 1), :], api_ref[pl.ds(seg - 1, 1), :]
    cr, ci = sre_ref[0], sim_ref[0]
    crs, cis = [cr], [ci]
    for s in range(SUBLANES - 1):
        cr, ci = (hr[s:s + 1, :] + (asr * cr - asi * ci), hi[s:s + 1, :] + (asr * ci + asi * cr))
        crs.append(cr)
        cis.append(ci)
    sre_ref[0] = hr[SUBLANES - 1:, :] + (asr * cr - asi * ci)
    sim_ref[0] = hi[SUBLANES - 1:, :] + (asr * ci + asi * cr)
    cr_all, ci_all = jnp.concatenate(crs, axis=0), jnp.concatenate(cis, axis=0)
    n_pieces = (tl // RET_CHUNK) * RET_HEADS
    per_piece = (seg // 2) // n_pieces
    for r2 in range(seg // 2):
        if r2 % per_piece == 0:
            ret_piece(*divmod(r2 // per_piece, RET_HEADS))
        parts_r, parts_i = [], []
        for r in (2 * r2, 2 * r2 + 1):
            qr = jnp.broadcast_to(apr_ref[pl.ds(r, 1), :], bshape)
            qi = jnp.broadcast_to(api_ref[pl.ds(r, 1), :], bshape)
            parts_r.append(hloc_scr[0, rg(r), :] + (qr * cr_all - qi * ci_all))
            parts_i.append(hloc_scr[1, rg(r), :] + (qr * ci_all + qi * cr_all))
        rows2 = pl.ds(r2 * 2 * SUBLANES, 2 * SUBLANES)
        st16_scr[0, rows2, :] = jnp.concatenate(parts_r, axis=0).astype(BF16)
        st16_scr[1, rows2, :] = jnp.concatenate(parts_i, axis=0).astype(BF16)
    gs = pcols(OFF_GS - OFF_XL, S5_WIDTH)
    y_s5 = _s5_tail(jnp.concatenate([st16_scr[0], st16_scr[1]], axis=1), u, gs,
                    cd_ref[...], s5d_ref[...], wglu_ref[...])
    for kb in range(S5_WIDTH // LANES):
        yp_scr[LRU_WIDTH // LANES + kb] = y_s5[:, kb * LANES:(kb + 1) * LANES]

    for m2 in range(tl // (2 * SUBLANES)):
        for kb in range((LRU_WIDTH + S5_WIDTH) // LANES):
            halves = []
            for m in (2 * m2, 2 * m2 + 1):
                s, r0 = divmod(m * SUBLANES, seg)
                halves.append(yp_scr[kb, pl.ds(r0 * SUBLANES + s, SUBLANES, stride=SUBLANES), :])
            ycat_scr[RET_WIDTH // LANES + kb, pl.ds(m2 * 2 * SUBLANES, 2 * SUBLANES), :] = (
                jnp.concatenate(halves, axis=0).astype(BF16))

    ycat = jnp.concatenate([ycat_scr[kb] for kb in range(ycat_scr.shape[0])], axis=1)
    out = jnp.dot(ycat, wout_ref[...], preferred_element_type=F32)
    x_new = x_ref[0] + mod_ref[0, 2:3, :] * out
    if final:
        x_new = _rmsnorm(x_new, fg_ref[...])
    y_ref[0] = x_new


def _prompt_layer(x, mod, w, tabs, layer, *, final):
    bsz, seq, d = x.shape
    tl = min(SEQ_TILE, seq)
    nt = seq // tl

    def full(a):
        return pl.BlockSpec(a.shape, lambda b, t: (0,) * a.ndim)

    def lay(a):
        return pl.BlockSpec((1,) + a.shape[1:], lambda b, t: (layer,) + (0,) * (a.ndim - 1))

    row_tab = pl.BlockSpec((tl, HEAD_DIM), lambda b, t: (t, 0))
    in_specs = [
        pl.BlockSpec((1, tl, d), lambda b, t: (b, t, 0)),
        pl.BlockSpec((1, 3, d), lambda b, t: (b, 0, 0)),
        lay(w["norm_g"]), lay(w["w_in"]), lay(w["w_out"]),
        row_tab, row_tab, row_tab, row_tab,
        full(tabs["intra"]), full(tabs["q_dec"]), full(tabs["k_dec"]),
        pl.BlockSpec(memory_space=pltpu.SMEM),
        lay(w["gn_g"]), lay(w["conv_w"]), lay(w["conv_b"]), lay(w["w_gate"]), lay(w["b_gate"]),
        lay(w["lam"]), lay(w["bd"]), lay(w["cd"]), lay(w["apr"]), lay(w["api"]),
        lay(w["s5_d"]), lay(w["w_glu"]), full(w["final_g"]),
    ]
    out_shape = (
        jax.ShapeDtypeStruct((bsz, seq, d), F32),
        jax.ShapeDtypeStruct((bsz, RET_HEADS, HEAD_DIM, HEAD_DIM), F32),
        jax.ShapeDtypeStruct((bsz, 1, LRU_WIDTH), F32),
        jax.ShapeDtypeStruct((bsz, CONV_WIDTH - 1, LRU_WIDTH), F32),
        jax.ShapeDtypeStruct((bsz, 1, S5_FLAT), F32),
        jax.ShapeDtypeStruct((bsz, 1, S5_FLAT), F32),
    )
    out_specs = (
        pl.BlockSpec((1, tl, d), lambda b, t: (b, t, 0)),
        pl.BlockSpec((1, RET_HEADS, HEAD_DIM, HEAD_DIM), lambda b, t: (b, 0, 0, 0)),
        pl.BlockSpec((1, 1, LRU_WIDTH), lambda b, t: (b, 0, 0)),
        pl.BlockSpec((1, CONV_WIDTH - 1, LRU_WIDTH), lambda b, t: (b, 0, 0)),
        pl.BlockSpec((1, 1, S5_FLAT), lambda b, t: (b, 0, 0)),
        pl.BlockSpec((1, 1, S5_FLAT), lambda b, t: (b, 0, 0)),
    )
    scratch = [
        pltpu.VMEM((OFF_XL // LANES, tl, LANES), F32),
        pltpu.VMEM((d // LANES, tl, LANES), BF16),
        pltpu.VMEM(((D_IN - OFF_XL) // LANES, tl, LANES), F32),
        pltpu.VMEM((tl + (CONV_WIDTH - 1) * SUBLANES, LRU_WIDTH), F32),
        pltpu.VMEM((2, tl, LRU_WIDTH), F32),
        pltpu.VMEM((2, tl, S5_FLAT), F32),
        pltpu.VMEM((2, tl, S5_FLAT), BF16),
        pltpu.VMEM(((LRU_WIDTH + S5_WIDTH) // LANES, tl, LANES), F32),
        pltpu.VMEM((tl, d), BF16),
        pltpu.VMEM((2, tl, LRU_WIDTH), F32),
        pltpu.VMEM((2, tl, S5_FLAT), F32),
    ]
    return pl.pallas_call(
        functools.partial(_prompt_kernel, final=final),
        grid=(bsz, nt),
        in_specs=in_specs,
        out_specs=out_specs,
        out_shape=out_shape,
        scratch_shapes=scratch,
        compiler_params=pltpu.CompilerParams(
            dimension_semantics=("arbitrary", "arbitrary"), vmem_limit_bytes=VMEM_LIMIT),
        name="prompt_layer",
    )(x, mod, w["norm_g"], w["w_in"], w["w_out"],
      tabs["cos_q"], tabs["sin_q"], tabs["cos_k"], tabs["sin_k"],
      tabs["intra"], tabs["q_dec"], tabs["k_dec"], tabs["c_dec"],
      w["gn_g"], w["conv_w"], w["conv_b"], w["w_gate"], w["b_gate"], w["lam"],
      w["bd"], w["cd"], w["apr"], w["api"], w["s5_d"], w["w_glu"], w["final_g"])


def _decode_kernel(x_ref, mod_ref, ng_ref, win_ref, wout_ref,
                   cq_ref, sq_ref, ck_ref, sk_ref, gam_ref,
                   gng_ref, cw_ref, cb_ref, wg_ref, bg_ref, lam_ref,
                   bd_ref, cd_ref, apr_ref, api_ref, s5d_ref, wglu_ref, fg_ref,
                   sret_in, hl_in, cv_in, sre_in, sim_in, sel_ref,
                   y_ref, sret_ref, hl_ref, cv_ref, sre_ref, sim_ref,
                   qkt_scr, v_scr, o_scr, gr_scr, ycat_scr, x_scr):
    layer = pl.program_id(0)
    i = pl.program_id(1)
    nb = pl.num_programs(1)
    rows_n, d = x_ref.shape
    (mod_ref, ng_ref, win_ref, wout_ref, gng_ref, cw_ref, cb_ref, wg_ref, bg_ref, lam_ref,
     bd_ref, cd_ref, apr_ref, api_ref, s5d_ref, wglu_ref,
     sret_in, hl_in, cv_in, sre_in, sim_in, sret_ref, hl_ref, cv_ref, sre_ref, sim_ref) = (
        r.at[0] for r in (mod_ref, ng_ref, win_ref, wout_ref, gng_ref, cw_ref, cb_ref, wg_ref,
                          bg_ref, lam_ref, bd_ref, cd_ref, apr_ref, api_ref, s5d_ref, wglu_ref,
                          sret_in, hl_in, cv_in, sre_in, sim_in,
                          sret_ref, hl_ref, cv_ref, sre_ref, sim_ref))

    @pl.when((i == 0) & (layer == 0))
    def _():
        x_scr[...] = x_ref[...]

    @pl.when(i == 0)
    def _():
        x = x_scr[...]
        shift, scale = mod_ref[:, pl.ds(0, d)], mod_ref[:, pl.ds(d, d)]
        h = _rmsnorm(x, ng_ref[...]) * (1.0 + scale) + shift
        proj = _dot(h, win_ref[...])
        cq, sq, ck, sk = cq_ref[...], sq_ref[...], ck_ref[...], sk_ref[...]
        for hd in range(RET_HEADS):
            lo = hd * HEAD_DIM
            for part, src in enumerate((_rotary(proj[:, OFF_Q + lo:OFF_Q + lo + HEAD_DIM], cq, sq),
                                        _rotary(proj[:, OFF_K + lo:OFF_K + lo + HEAD_DIM], ck, sk))):
                xt = src.T
                hi = xt.astype(BF16)
                trows = pl.ds((2 * hd + part) * HEAD_DIM, HEAD_DIM)
                qkt_scr[trows, pl.ds(0, rows_n)] = hi
                qkt_scr[trows, pl.ds(rows_n, rows_n)] = (xt - hi.astype(F32)).astype(BF16)
        v_scr[...] = proj[:, OFF_V:OFF_V + RET_WIDTH]
        gr_scr[...] = _silu(proj[:, OFF_GR:OFF_GR + RET_WIDTH])
        xl = proj[:, OFF_XL:OFF_XL + LRU_WIDTH]
        cs = cv_in[...]
        xc = cb_ref[...] + xl * cw_ref[pl.ds(CONV_WIDTH - 1, 1), :]
        for j in range(CONV_WIDTH - 1):
            xc = xc + cs[:, j * LRU_WIDTH:(j + 1) * LRU_WIDTH] * cw_ref[pl.ds(j, 1), :]
        cv_ref[:, pl.ds(0, 2 * LRU_WIDTH)] = cs[:, LRU_WIDTH:]
        cv_ref[:, pl.ds(2 * LRU_WIDTH, LRU_WIDTH)] = xl
        gates = _dot(xc, wg_ref[...]) + bg_ref[...]
        a, b = _lru_coeffs(xc, gates, _softplus(-lam_ref[...]))
        hh = b + a * hl_in[...]
        hl_ref[...] = hh
        ycat_scr[:, pl.ds(RET_WIDTH, LRU_WIDTH)] = (
            hh * _silu(proj[:, OFF_GL:OFF_GL + LRU_WIDTH])).astype(BF16)
        u = proj[:, OFF_U:OFF_U + S5_WIDTH]
        bu = _dot(u, bd_ref[...])
        ar, ai = apr_ref[pl.ds(0, 1), :], api_ref[pl.ds(0, 1), :]
        s0r, s0i = sre_in[...], sim_in[...]
        sr = bu[:, :S5_FLAT] + (ar * s0r - ai * s0i)
        si = bu[:, S5_FLAT:] + (ar * s0i + ai * s0r)
        sre_ref[...] = sr
        sim_ref[...] = si
        y_s5 = _s5_tail(jnp.concatenate([sr, si], axis=-1), u, proj[:, OFF_GS:OFF_GS + S5_WIDTH],
                        cd_ref[...], s5d_ref[...], wglu_ref[...])
        ycat_scr[:, pl.ds(RET_WIDTH + LRU_WIDTH, S5_WIDTH)] = y_s5.astype(BF16)

    blk_rows = pl.ds(pl.multiple_of(i * DEC_BLOCK, DEC_BLOCK), DEC_BLOCK)
    o_rows = [[] for _ in range(RET_HEADS)]
    vblks = [v_scr[blk_rows, pl.ds(hd * HEAD_DIM, HEAD_DIM)] for hd in range(RET_HEADS)]
    for jp in range(DEC_BLOCK // 2):
        sel = sel_ref[0, :, pl.ds(jp * 2 * HEAD_DIM, 2 * HEAD_DIM)]
        tiles = jnp.dot(qkt_scr[...], jnp.concatenate([sel, sel], axis=0),
                        preferred_element_type=F32)
        for hd in range(RET_HEADS):
            qb = tiles[2 * hd * HEAD_DIM:(2 * hd + 1) * HEAD_DIM, :]
            kb = tiles[(2 * hd + 1) * HEAD_DIM:(2 * hd + 2) * HEAD_DIM, :]
            for j in (2 * jp, 2 * jp + 1):
                cols = slice((j % 2) * HEAD_DIM, (j % 2 + 1) * HEAD_DIM)
                s_new = gam_ref[hd] * sret_in[j, hd] + kb[:, cols] * vblks[hd][j:j + 1, :]
                sret_ref[j, hd] = s_new
                o_rows[hd].append(jnp.sum(qb[:, cols] * s_new, axis=0, keepdims=True))
    for hd in range(RET_HEADS):
        o_scr[blk_rows, pl.ds(hd * HEAD_DIM, HEAD_DIM)] = jnp.concatenate(o_rows[hd], axis=0)

    @pl.when(i == nb - 1)
    def _():
        for hd in range(RET_HEADS):
            lanes = pl.ds(hd * HEAD_DIM, HEAD_DIM)
            on = _groupnorm(o_scr[:, lanes], gng_ref[:, lanes])
            ycat_scr[:, lanes] = (on * gr_scr[:, lanes]).astype(BF16)
        out = jnp.dot(ycat_scr[...], wout_ref[...], preferred_element_type=F32)
        x_new = x_scr[...] + mod_ref[:, pl.ds(2 * d, d)] * out
        x_scr[...] = x_new
        y_ref[...] = _rmsnorm(x_new, fg_ref[...])


def _decode_layers(x, mod, states, w, tabs):
    rows, d = x.shape
    depth = mod.shape[0]
    s_ret, s_h, s_conv, s_re, s_im = states
    nb = rows // DEC_BLOCK

    def full(a):
        return pl.BlockSpec(a.shape, lambda l, i: (0,) * a.ndim)

    def lay(a):
        return pl.BlockSpec((1,) + a.shape[1:], lambda l, i: (l,) + (0,) * (a.ndim - 1))

    smem = pl.BlockSpec(memory_space=pltpu.SMEM)
    sblk = pl.BlockSpec((1, DEC_BLOCK, RET_HEADS, HEAD_DIM, HEAD_DIM), lambda l, i: (l, i, 0, 0, 0))
    in_specs = [
        full(x), lay(mod), lay(w["norm_g"]), lay(w["w_in"]), lay(w["w_out"]),
        full(tabs["dcos_q"]), full(tabs["dsin_q"]), full(tabs["dcos_k"]), full(tabs["dsin_k"]), smem,
        lay(w["gn_g"]), lay(w["conv_w"]), lay(w["conv_b"]), lay(w["w_gate"]), lay(w["b_gate"]),
        lay(w["lam"]), lay(w["bd"]), lay(w["cd"]), lay(w["apr"]), lay(w["api"]),
        lay(w["s5_d"]), lay(w["w_glu"]), full(w["final_g"]),
        sblk, lay(s_h), lay(s_conv), lay(s_re), lay(s_im),
        pl.BlockSpec((1,) + tabs["row_select"].shape[1:], lambda l, i: (i, 0, 0)),
    ]
    out_shape = (
        jax.ShapeDtypeStruct((rows, d), F32),
        jax.ShapeDtypeStruct(s_ret.shape, F32),
        jax.ShapeDtypeStruct(s_h.shape, F32),
        jax.ShapeDtypeStruct(s_conv.shape, F32),
        jax.ShapeDtypeStruct(s_re.shape, F32),
        jax.ShapeDtypeStruct(s_im.shape, F32),
    )
    out_specs = (full(x), sblk, lay(s_h), lay(s_conv), lay(s_re), lay(s_im))
    scratch = [
        pltpu.VMEM((RET_HEADS * 2 * HEAD_DIM, 2 * rows), BF16),
        pltpu.VMEM((rows, RET_WIDTH), F32),
        pltpu.VMEM((rows, RET_WIDTH), F32),
        pltpu.VMEM((rows, RET_WIDTH), F32),
        pltpu.VMEM((rows, d), BF16),
        pltpu.VMEM((rows, d), F32),
    ]
    return pl.pallas_call(
        _decode_kernel,
        grid=(depth, nb),
        in_specs=in_specs,
        out_specs=out_specs,
        out_shape=out_shape,
        scratch_shapes=scratch,
        compiler_params=pltpu.CompilerParams(
            dimension_semantics=("arbitrary", "arbitrary"), vmem_limit_bytes=VMEM_LIMIT),
        name="decode_layers",
    )(x, mod, w["norm_g"], w["w_in"], w["w_out"],
      tabs["dcos_q"], tabs["dsin_q"], tabs["dcos_k"], tabs["dsin_k"], tabs["gamma"],
      w["gn_g"], w["conv_w"], w["conv_b"], w["w_gate"], w["b_gate"], w["lam"],
      w["bd"], w["cd"], w["apr"], w["api"], w["s5_d"], w["w_glu"], w["final_g"],
      s_ret, s_h, s_conv, s_re, s_im, tabs["row_select"])


def _rope_tables(pos, scale):
    half = HEAD_DIM // 2
    inv = ROPE_BASE ** (-np.arange(half, dtype=np.float64) / half)
    ang = pos[:, None] * inv[None, :]
    cos, sin = np.cos(ang), np.sin(ang)
    cosf = np.concatenate([cos, cos], axis=-1) * scale
    sinf = np.concatenate([-sin, sin], axis=-1) * scale
    return cosf, sinf


def _tables(seq, dec):
    c = RET_CHUNK
    log_g = np.log1p(-np.exp2(-5.0 - np.arange(RET_HEADS, dtype=np.float64)))
    idx = np.arange(c, dtype=np.float64)
    rel = idx[:, None] - idx[None, :]
    intra = np.where(rel[None] >= 0, np.exp(np.maximum(rel, 0.0)[None] * log_g[:, None, None]), 0.0)
    q_dec = np.exp((idx + 1.0)[:, None] * log_g[None, :])
    k_dec = np.exp((c - 1.0 - idx)[:, None] * log_g[None, :])
    q_dec = np.broadcast_to(q_dec.T[:, :, None], (RET_HEADS, c, HEAD_DIM))
    k_dec = np.broadcast_to(k_dec.T[:, :, None], (RET_HEADS, c, HEAD_DIM))
    c_dec = np.exp(c * log_g)
    gamma = np.exp(log_g)
    k_scale = HEAD_DIM ** -0.5
    cos_q, sin_q = _rope_tables(np.arange(seq, dtype=np.float64), 1.0)
    cos_k, sin_k = _rope_tables(np.arange(seq, dtype=np.float64), k_scale)
    dpos = PAST_LEN + np.arange(1, dtype=np.float64)
    dcos_q, dsin_q = _rope_tables(dpos, 1.0)
    dcos_k, dsin_k = _rope_tables(dpos, k_scale)
    blk = np.arange(dec // DEC_BLOCK)[:, None, None]
    row = np.arange(dec)[None, :, None]
    col = np.arange(DEC_BLOCK * HEAD_DIM)[None, None, :]
    row_select = row == blk * DEC_BLOCK + col // HEAD_DIM
    tabs = dict(intra=intra, q_dec=q_dec, k_dec=k_dec, c_dec=c_dec, gamma=gamma,
                cos_q=cos_q, sin_q=sin_q, cos_k=cos_k, sin_k=sin_k,
                dcos_q=dcos_q, dsin_q=dsin_q, dcos_k=dcos_k, dsin_k=dsin_k)
    tabs = {k: jnp.asarray(np.ascontiguousarray(v), dtype=F32) for k, v in tabs.items()}
    tabs["row_select"] = jnp.asarray(row_select, dtype=BF16)
    return tabs


def _block_diag(blocks):
    n, r, c = blocks.shape
    eye = jnp.eye(n, dtype=blocks.dtype)
    return jnp.einsum("nrc,nm->nrmc", blocks, eye).reshape(n * r, n * c)


def kernel(x_prompt, x_sample, state_ret, state_lru_h, state_lru_conv, state_s5_re, state_s5_im,
           c_prompt, c_sample, norm_g, w_ada, b_ada, w_in, ret_gn_g, conv_w, conv_b, w_rg, b_rg,
           w_ig, b_ig, lru_lambda, s5_a_re, s5_a_im, s5_b_re, s5_b_im, s5_c_re, s5_c_im, s5_d,
           s5_log_dt, s5_w_glu, w_out, final_g):
    depth = w_in.shape[0]
    bsz, seq, d = x_prompt.shape
    dec = x_sample.shape[0]
    assert x_sample.shape[1] == 1 and seq % RET_CHUNK == 0 and dec % DEC_BLOCK == 0

    tabs = _tables(seq, dec)
    mod_all = _ada(jnp.concatenate([c_prompt, c_sample], axis=0), w_ada, b_ada)
    n_pow = min(SEQ_TILE, seq) // SUBLANES
    apr, api, bbr, bbi = _s5_prep(s5_a_re, s5_a_im, s5_log_dt, s5_b_re, s5_b_im, n_pow)

    bdiag = jax.vmap(_block_diag)
    bd = jnp.concatenate([bdiag(bbr), bdiag(bbi)], axis=2)
    cd = jnp.concatenate([bdiag(jnp.swapaxes(s5_c_re, 2, 3)),
                          bdiag(jnp.swapaxes(-s5_c_im, 2, 3))], axis=1)
    w = dict(
        norm_g=norm_g.reshape(depth, 1, d),
        w_in=w_in.astype(BF16),
        w_out=w_out.astype(BF16),
        gn_g=ret_gn_g.reshape(depth, 1, RET_WIDTH),
        conv_w=conv_w,
        conv_b=conv_b.reshape(depth, 1, LRU_WIDTH),
        w_gate=jnp.concatenate([bdiag(w_rg), bdiag(w_ig)], axis=2).astype(BF16),
        b_gate=jnp.concatenate([b_rg, b_ig], axis=1).reshape(depth, 1, 2 * LRU_WIDTH),
        lam=lru_lambda.reshape(depth, 1, LRU_WIDTH),
        bd=bd.astype(BF16),
        cd=cd.astype(BF16),
        apr=apr.reshape(depth, n_pow, S5_FLAT),
        api=api.reshape(depth, n_pow, S5_FLAT),
        s5_d=s5_d.reshape(depth, 1, S5_WIDTH),
        w_glu=s5_w_glu.astype(BF16),
        final_g=final_g.reshape(1, d),
    )

    xp = x_prompt
    outs_p = []
    for l in range(depth):
        mod_p = mod_all[l, :bsz].reshape(bsz, 3, d)
        xp, sret, hl, cv, sre, sim = _prompt_layer(xp, mod_p, w, tabs, l, final=l == depth - 1)
        outs_p.append((sret, hl.reshape(bsz, LRU_WIDTH), cv,
                       sre.reshape(bsz, S5_GROUPS, S5_STATE), sim.reshape(bsz, S5_GROUPS, S5_STATE)))

    states = (state_ret, state_lru_h,
              state_lru_conv.reshape(depth, dec, (CONV_WIDTH - 1) * LRU_WIDTH),
              state_s5_re.reshape(depth, dec, S5_FLAT), state_s5_im.reshape(depth, dec, S5_FLAT))
    xs, sret_s, hl_s, cv_s, sre_s, sim_s = _decode_layers(
        x_sample.reshape(dec, d), mod_all[:, bsz:], states, w, tabs)

    def stk(k):
        return jnp.stack([o[k] for o in outs_p])

    return (xp, xs.reshape(dec, 1, d),
            stk(0), sret_s, stk(1), hl_s,
            stk(2), cv_s.reshape(depth, dec, CONV_WIDTH - 1, LRU_WIDTH),
            stk(3), sre_s.reshape(depth, dec, S5_GROUPS, S5_STATE),
            stk(4), sim_s.reshape(depth, dec, S5_GROUPS, S5_STATE))
```

```python
import functools
import math

import jax
import jax.numpy as jnp
import numpy as np
from jax import lax
from jax.experimental import pallas as pl
from jax.experimental.pallas import tpu as pltpu

RET_HEADS = 4
HEAD_DIM = 128
RET_WIDTH = RET_HEADS * HEAD_DIM
RET_CHUNK = 128
LRU_WIDTH = 256
LRU_C = 8.0
CONV_WIDTH = 4
S5_WIDTH = 256
S5_GROUP = 16
S5_GROUPS = 16
S5_STATE = 64
S5_FLAT = S5_GROUPS * S5_STATE
ROPE_BASE = 10000.0
EPS = 1e-6
PAST_LEN = 16384

OFF_Q = 0
OFF_K = OFF_Q + RET_WIDTH
OFF_V = OFF_K + RET_WIDTH
OFF_GR = OFF_V + RET_WIDTH
OFF_XL = OFF_GR + RET_WIDTH
OFF_GL = OFF_XL + LRU_WIDTH
OFF_U = OFF_GL + LRU_WIDTH
OFF_GS = OFF_U + S5_WIDTH
D_IN = OFF_GS + S5_WIDTH

SUBLANES = 8
LANES = 128
SEQ_TILE = 512
PROJ_CHUNK = 256
DEC_BLOCK = 16
VMEM_LIMIT = 56 * 1024 * 1024

F32 = jnp.float32
BF16 = jnp.bfloat16


def _sigmoid(x):
    return 1.0 / (1.0 + jnp.exp(-x))


def _silu(x):
    return x * _sigmoid(x)


def _gelu_tanh(x):
    c = math.sqrt(2.0 / math.pi)
    return x * (0.5 * (1.0 + jnp.tanh(c * (x + 0.044715 * (x * x * x)))))


def _softplus(x):
    return jnp.maximum(x, 0.0) + jnp.log1p(jnp.exp(-jnp.abs(x)))


def _rmsnorm(x, g):
    ms = jnp.mean(x * x, axis=-1, keepdims=True)
    return x * lax.rsqrt(ms + EPS) * g


def _dot(a, b):
    return jnp.dot(a.astype(BF16), b.astype(BF16), preferred_element_type=F32)


def _dot_nt(a, b):
    return lax.dot_general(a.astype(BF16), b.astype(BF16), (((1,), (1,)), ((), ())),
                           preferred_element_type=F32)


def _rotary(x, cosf, sinf):
    return x * cosf + pltpu.roll(x, HEAD_DIM // 2, axis=1) * sinf


def _groupnorm(o, g):
    mu = jnp.mean(o, axis=-1, keepdims=True)
    d = o - mu
    var = jnp.mean(d * d, axis=-1, keepdims=True)
    return d * lax.rsqrt(var + EPS) * g


def _lru_coeffs(xc, gates, sp):
    r = _sigmoid(gates[:, :LRU_WIDTH])
    ig = _sigmoid(gates[:, LRU_WIDTH:])
    log_a = (-LRU_C) * r * sp
    a = jnp.exp(log_a)
    th = jnp.tanh(log_a)
    mult = jnp.sqrt(-2.0 * th / (1.0 - th))
    return a, mult * ig * xc


def _s5_tail(sr_si, u, gs, cd, s5d, wglu, row_parts=1):
    rows = sr_si.shape[0] // row_parts
    ys = jnp.concatenate([_dot(sr_si[p * rows:(p + 1) * rows], cd) for p in range(row_parts)],
                         axis=0) + s5d * u
    ys = _gelu_tanh(ys)
    ys = ys * _sigmoid(_dot(ys, wglu))
    return ys * _silu(gs)


def _s5_prep_kernel(are_ref, aim_ref, ldt_ref, bre_ref, bim_ref,
                    apr_ref, api_ref, bbr_ref, bbi_ref):
    depth = are_ref.shape[0]
    for l in range(depth):
        a_re = are_ref[l]
        a_im = aim_ref[l]
        step = jnp.exp(ldt_ref[l])
        mag = jnp.exp(step * a_re)
        abar_r = mag * jnp.cos(step * a_im)
        abar_i = mag * jnp.sin(step * a_im)
        nr, ni = abar_r - 1.0, abar_i
        den = a_re * a_re + a_im * a_im
        fr = (nr * a_re + ni * a_im) / den
        fi = (ni * a_re - nr * a_im) / den
        for g in range(S5_GROUPS):
            frg, fig = fr[g:g + 1, :], fi[g:g + 1, :]
            b_re, b_im = bre_ref[l, g], bim_ref[l, g]
            bbr_ref[l, g] = frg * b_re - fig * b_im
            bbi_ref[l, g] = frg * b_im + fig * b_re
        pr, pi = abar_r, abar_i
        apr_ref[l, 0] = pr
        api_ref[l, 0] = pi
        for m in range(1, apr_ref.shape[1]):
            pr, pi = pr * abar_r - pi * abar_i, pr * abar_i + pi * abar_r
            apr_ref[l, m] = pr
            api_ref[l, m] = pi


def _s5_prep(s5_a_re, s5_a_im, s5_log_dt, s5_b_re, s5_b_im, n_pow):
    depth = s5_a_re.shape[0]
    b_re_t = jnp.swapaxes(s5_b_re, 2, 3)
    b_im_t = jnp.swapaxes(s5_b_im, 2, 3)
    ldt = s5_log_dt.reshape(depth, S5_GROUPS, 1)
    out_shape = (
        jax.ShapeDtypeStruct((depth, n_pow, S5_GROUPS, S5_STATE), F32),
        jax.ShapeDtypeStruct((depth, n_pow, S5_GROUPS, S5_STATE), F32),
        jax.ShapeDtypeStruct((depth, S5_GROUPS, S5_GROUP, S5_STATE), F32),
        jax.ShapeDtypeStruct((depth, S5_GROUPS, S5_GROUP, S5_STATE), F32),
    )
    return pl.pallas_call(_s5_prep_kernel, out_shape=out_shape, name="s5_prep")(
        s5_a_re, s5_a_im, ldt, b_re_t, b_im_t)


def _ada_kernel(c_ref, w_ref, b_ref, o_ref):
    s = _silu(c_ref[...])
    o_ref[0] = _dot(s, w_ref[0]) + b_ref[0]


def _ada(c_all, w_ada, b_ada):
    depth, d, n3 = w_ada.shape
    rows = c_all.shape[0]
    tn = d
    return pl.pallas_call(
        _ada_kernel,
        grid=(depth, n3 // tn),
        in_specs=[
            pl.BlockSpec((rows, d), lambda l, j: (0, 0)),
            pl.BlockSpec((1, d, tn), lambda l, j: (l, 0, j)),
            pl.BlockSpec((1, 1, tn), lambda l, j: (l, 0, j)),
        ],
        out_specs=pl.BlockSpec((1, rows, tn), lambda l, j: (l, 0, j)),
        out_shape=jax.ShapeDtypeStruct((depth, rows, n3), F32),
        compiler_params=pltpu.CompilerParams(
            dimension_semantics=("arbitrary", "arbitrary"), vmem_limit_bytes=VMEM_LIMIT),
        name="adaln",
    )(c_all, w_ada, b_ada.reshape(depth, 1, n3))


def _prompt_kernel(x_ref, mod_ref, ng_ref, win_ref, wout_ref,
                   cq_ref, sq_ref, ck_ref, sk_ref, intra_ref, qdec_ref, kdec_ref, cdec_ref,
                   gng_ref, cw_ref, cb_ref, wg_ref, bg_ref, lam_ref,
                   bd_ref, cd_ref, apr_ref, api_ref, s5d_ref, wglu_ref, fg_ref,
                   y_ref, sret_ref, hl_ref, cv_ref, sre_ref, sim_ref,
                   proj_scr, ycat_scr, perm_scr, xpad_scr, ab_scr, st_scr, st16_scr, yp_scr, h_scr,
                   ph_scr, hloc_scr, *, final):
    tl = x_ref.shape[1]
    seg = tl // SUBLANES
    t = pl.program_id(1)
    (ng_ref, win_ref, wout_ref, gng_ref, cw_ref, cb_ref, wg_ref, bg_ref, lam_ref,
     bd_ref, cd_ref, apr_ref, api_ref, s5d_ref, wglu_ref) = (
        r.at[0] for r in (ng_ref, win_ref, wout_ref, gng_ref, cw_ref, cb_ref, wg_ref, bg_ref,
                          lam_ref, bd_ref, cd_ref, apr_ref, api_ref, s5d_ref, wglu_ref))

    @pl.when(t == 0)
    def _():
        sret_ref[...] = jnp.zeros_like(sret_ref)
        hl_ref[...] = jnp.zeros_like(hl_ref)
        sre_ref[...] = jnp.zeros_like(sre_ref)
        sim_ref[...] = jnp.zeros_like(sim_ref)
        cv_ref[...] = jnp.zeros_like(cv_ref)

    x = x_ref[0]
    shift, scale = mod_ref[0, 0:1, :], mod_ref[0, 1:2, :]
    h = _rmsnorm(x, ng_ref[...] * (1.0 + scale)) + shift
    h_scr[...] = h.astype(BF16)
    proj_b = jnp.dot(h_scr[...], win_ref[:, pl.ds(OFF_XL, D_IN - OFF_XL)],
                     preferred_element_type=F32)

    def proj_a_chunk(ci):
        cols = pl.ds(ci * PROJ_CHUNK, PROJ_CHUNK)
        res = jnp.dot(h_scr[...], win_ref[:, cols], preferred_element_type=F32)
        for kl in range(PROJ_CHUNK // LANES):
            proj_scr[ci * (PROJ_CHUNK // LANES) + kl] = res[:, kl * LANES:(kl + 1) * LANES]

    for m in range(tl // SUBLANES):
        s, r0 = divmod(m * SUBLANES, seg)
        for kb in range((D_IN - OFF_XL) // LANES):
            perm_scr[kb, pl.ds(r0 * SUBLANES + s, SUBLANES, stride=SUBLANES), :] = (
                proj_b[m * SUBLANES:(m + 1) * SUBLANES, kb * LANES:(kb + 1) * LANES])

    def ret_piece(c, hd):
        rows = pl.ds(c * RET_CHUNK, RET_CHUNK)
        lanes = pl.ds(hd * HEAD_DIM, HEAD_DIM)
        q = _rotary(proj_scr[OFF_Q // LANES + hd, rows, :], cq_ref[rows, :], sq_ref[rows, :])
        k = _rotary(proj_scr[OFF_K // LANES + hd, rows, :], ck_ref[rows, :], sk_ref[rows, :])
        v = proj_scr[OFF_V // LANES + hd, rows, :].astype(BF16)
        s_old = sret_ref[0, hd]
        sc = _dot_nt(q, k) * intra_ref[hd]
        o = _dot(sc, v) + _dot(q * qdec_ref[hd], s_old)
        kd_t = (k * kdec_ref[hd]).T
        sret_ref[0, hd] = cdec_ref[hd] * s_old + _dot(kd_t, v)
        on = _groupnorm(o, gng_ref[:, lanes])
        g_ret = proj_scr[OFF_GR // LANES + hd, rows, :]
        ycat_scr[hd, rows, :] = (on * _silu(g_ret)).astype(BF16)

    sub = lax.broadcasted_iota(jnp.int32, (SUBLANES, 1), 0)
    rg = lambda r: pl.ds(r * SUBLANES, SUBLANES)

    def pcols(lo, width):
        return jnp.concatenate([perm_scr[kb] for kb in range(lo // LANES, (lo + width) // LANES)],
                               axis=1)

    n_chunks = OFF_XL // PROJ_CHUNK
    s5_parts = n_chunks - 3

    proj_a_chunk(0)
    xl = pcols(OFF_XL - OFF_XL, LRU_WIDTH)
    prev = cv_ref[0]
    npre = CONV_WIDTH - 1
    for m in range(1, CONV_WIDTH):
        grp = xl[(seg - m) * SUBLANES:(seg - m + 1) * SUBLANES, :]
        e_m = jnp.where(sub == 0, prev[npre - m:npre - m + 1, :], pltpu.roll(grp, 1, axis=0))
        xpad_scr[rg(npre - m), :] = e_m
    xpad_scr[pl.ds(npre * SUBLANES, tl), :] = xl
    cv_ref[0] = jnp.concatenate(
        [xl[(seg - m) * SUBLANES + SUBLANES - 1:(seg - m + 1) * SUBLANES, :]
         for m in range(npre, 0, -1)], axis=0)
    xc = cb_ref[...]
    for j in range(CONV_WIDTH):
        xc = xc + xpad_scr[pl.ds(j * SUBLANES, tl), :] * cw_ref[pl.ds(j, 1), :]
    gates = _dot(xc, wg_ref[...]) + bg_ref[...]
    a, b = _lru_coeffs(xc, gates, _softplus(-lam_ref[...]))
    ab_scr[0] = a
    ab_scr[1] = b
    proj_a_chunk(1)
    hloc = jnp.zeros((SUBLANES, LRU_WIDTH), F32)
    prod = jnp.ones((SUBLANES, LRU_WIDTH), F32)
    for r in range(seg):
        ar = ab_scr[0, rg(r), :]
        hloc = ar * hloc + ab_scr[1, rg(r), :]
        prod = prod * ar
        ph_scr[0, rg(r), :] = prod
        ph_scr[1, rg(r), :] = hloc
    c = hl_ref[0]
    cs = [c]
    for s in range(SUBLANES - 1):
        c = hloc[s:s + 1, :] + prod[s:s + 1, :] * c
        cs.append(c)
    hl_ref[0] = hloc[SUBLANES - 1:, :] + prod[SUBLANES - 1:, :] * c
    c_all = jnp.concatenate(cs, axis=0)
    sgl = _silu(pcols(OFF_GL - OFF_XL, LRU_WIDTH))
    for r in range(seg):
        y = (ph_scr[1, rg(r), :] + ph_scr[0, rg(r), :] * c_all) * sgl[r * SUBLANES:(r + 1) * SUBLANES, :]
        for kb in range(LRU_WIDTH // LANES):
            yp_scr[kb, rg(r), :] = y[:, kb * LANES:(kb + 1) * LANES]

    proj_a_chunk(2)
    u = pcols(OFF_U - OFF_XL, S5_WIDTH)
    bu = _dot(u, bd_ref[...])
    st_scr[0] = bu[:, :S5_FLAT]
    st_scr[1] = bu[:, S5_FLAT:]
    bshape = (SUBLANES, S5_FLAT)
    a1r = jnp.broadcast_to(apr_ref[pl.ds(0, 1), :], bshape)
    a1i = jnp.broadcast_to(api_ref[pl.ds(0, 1), :], bshape)
    hr = jnp.zeros((SUBLANES, S5_FLAT), F32)
    hi = jnp.zeros((SUBLANES, S5_FLAT), F32)
    for r in range(seg):
        if r % (-(-seg // s5_parts)) == 0:
            proj_a_chunk(3 + r // (-(-seg // s5_parts)))
        hr, hi = (a1r * hr - a1i * hi + st_scr[0, rg(r), :],
                  a1r * hi + a1i * hr + st_scr[1, rg(r), :])
        hloc_scr[0, rg(r), :] = hr
        hloc_scr[1, rg(r), :] = hi
    asr, asi = apr_ref[pl.ds(seg - 1, 1), :], api_ref[pl.ds(seg - 1, 1), :]
    cr, ci = sre_ref[0], sim_ref[0]
    crs, cis = [cr], [ci]
    for s in range(SUBLANES - 1):
        cr, ci = (hr[s:s + 1, :] + (asr * cr - asi * ci), hi[s:s + 1, :] + (asr * ci + asi * cr))
        crs.append(cr)
        cis.append(ci)
    sre_ref[0] = hr[SUBLANES - 1:, :] + (asr * cr - asi * ci)
    sim_ref[0] = hi[SUBLANES - 1:, :] + (asr * ci + asi * cr)
    cr_all, ci_all = jnp.concatenate(crs, axis=0), jnp.concatenate(cis, axis=0)
    n_pieces = (tl // RET_CHUNK) * RET_HEADS
    per_piece = (seg // 2) // n_pieces
    for r2 in range(seg // 2):
        if r2 % per_piece == 0:
            ret_piece(*divmod(r2 // per_piece, RET_HEADS))
        parts_r, parts_i = [], []
        for r in (2 * r2, 2 * r2 + 1):
            qr = jnp.broadcast_to(apr_ref[pl.ds(r, 1), :], bshape)
            qi = jnp.broadcast_to(api_ref[pl.ds(r, 1), :], bshape)
            parts_r.append(hloc_scr[0, rg(r), :] + (qr * cr_all - qi * ci_all))
            parts_i.append(hloc_scr[1, rg(r), :] + (qr * ci_all + qi * cr_all))
        rows2 = pl.ds(r2 * 2 * SUBLANES, 2 * SUBLANES)
        st16_scr[0, rows2, :] = jnp.concatenate(parts_r, axis=0).astype(BF16)
        st16_scr[1, rows2, :] = jnp.concatenate(parts_i, axis=0).astype(BF16)
    gs = pcols(OFF_GS - OFF_XL, S5_WIDTH)
    y_s5 = _s5_tail(jnp.concatenate([st16_scr[0], st16_scr[1]], axis=1), u, gs,
                    cd_ref[...], s5d_ref[...], wglu_ref[...], row_parts=2)
    for kb in range(S5_WIDTH // LANES):
        yp_scr[LRU_WIDTH // LANES + kb] = y_s5[:, kb * LANES:(kb + 1) * LANES]

    for m2 in range(tl // (2 * SUBLANES)):
        for kb in range((LRU_WIDTH + S5_WIDTH) // LANES):
            halves = []
            for m in (2 * m2, 2 * m2 + 1):
                s, r0 = divmod(m * SUBLANES, seg)
                halves.append(yp_scr[kb, pl.ds(r0 * SUBLANES + s, SUBLANES, stride=SUBLANES), :])
            ycat_scr[RET_WIDTH // LANES + kb, pl.ds(m2 * 2 * SUBLANES, 2 * SUBLANES), :] = (
                jnp.concatenate(halves, axis=0).astype(BF16))

    ycat = jnp.concatenate([ycat_scr[kb] for kb in range(ycat_scr.shape[0])], axis=1)
    out = jnp.dot(ycat, wout_ref[...], preferred_element_type=F32)
    x_new = x_ref[0] + mod_ref[0, 2:3, :] * out
    if final:
        x_new = _rmsnorm(x_new, fg_ref[...])
    y_ref[0] = x_new


def _prompt_layer(x, mod, w, tabs, layer, *, final):
    bsz, seq, d = x.shape
    tl = min(SEQ_TILE, seq)
    nt = seq // tl

    def full(a):
        return pl.BlockSpec(a.shape, lambda b, t: (0,) * a.ndim)

    def lay(a):
        return pl.BlockSpec((1,) + a.shape[1:], lambda b, t: (layer,) + (0,) * (a.ndim - 1))

    row_tab = pl.BlockSpec((tl, HEAD_DIM), lambda b, t: (t, 0))
    in_specs = [
        pl.BlockSpec((1, tl, d), lambda b, t: (b, t, 0)),
        pl.BlockSpec((1, 3, d), lambda b, t: (b, 0, 0)),
        lay(w["norm_g"]), lay(w["w_in"]), lay(w["w_out"]),
        row_tab, row_tab, row_tab, row_tab,
        full(tabs["intra"]), full(tabs["q_dec"]), full(tabs["k_dec"]),
        pl.BlockSpec(memory_space=pltpu.SMEM),
        lay(w["gn_g"]), lay(w["conv_w"]), lay(w["conv_b"]), lay(w["w_gate"]), lay(w["b_gate"]),
        lay(w["lam"]), lay(w["bd"]), lay(w["cd"]), lay(w["apr"]), lay(w["api"]),
        lay(w["s5_d"]), lay(w["w_glu"]), full(w["final_g"]),
    ]
    out_shape = (
        jax.ShapeDtypeStruct((bsz, seq, d), F32),
        jax.ShapeDtypeStruct((bsz, RET_HEADS, HEAD_DIM, HEAD_DIM), F32),
        jax.ShapeDtypeStruct((bsz, 1, LRU_WIDTH), F32),
        jax.ShapeDtypeStruct((bsz, CONV_WIDTH - 1, LRU_WIDTH), F32),
        jax.ShapeDtypeStruct((bsz, 1, S5_FLAT), F32),
        jax.ShapeDtypeStruct((bsz, 1, S5_FLAT), F32),
    )
    out_specs = (
        pl.BlockSpec((1, tl, d), lambda b, t: (b, t, 0)),
        pl.BlockSpec((1, RET_HEADS, HEAD_DIM, HEAD_DIM), lambda b, t: (b, 0, 0, 0)),
        pl.BlockSpec((1, 1, LRU_WIDTH), lambda b, t: (b, 0, 0)),
        pl.BlockSpec((1, CONV_WIDTH - 1, LRU_WIDTH), lambda b, t: (b, 0, 0)),
        pl.BlockSpec((1, 1, S5_FLAT), lambda b, t: (b, 0, 0)),
        pl.BlockSpec((1, 1, S5_FLAT), lambda b, t: (b, 0, 0)),
    )
    scratch = [
        pltpu.VMEM((OFF_XL // LANES, tl, LANES), F32),
        pltpu.VMEM((d // LANES, tl, LANES), BF16),
        pltpu.VMEM(((D_IN - OFF_XL) // LANES, tl, LANES), F32),
        pltpu.VMEM((tl + (CONV_WIDTH - 1) * SUBLANES, LRU_WIDTH), F32),
        pltpu.VMEM((2, tl, LRU_WIDTH), F32),
        pltpu.VMEM((2, tl, S5_FLAT), F32),
        pltpu.VMEM((2, tl, S5_FLAT), BF16),
        pltpu.VMEM(((LRU_WIDTH + S5_WIDTH) // LANES, tl, LANES), F32),
        pltpu.VMEM((tl, d), BF16),
        pltpu.VMEM((2, tl, LRU_WIDTH), F32),
        pltpu.VMEM((2, tl, S5_FLAT), F32),
    ]
    return pl.pallas_call(
        functools.partial(_prompt_kernel, final=final),
        grid=(bsz, nt),
        in_specs=in_specs,
        out_specs=out_specs,
        out_shape=out_shape,
        scratch_shapes=scratch,
        compiler_params=pltpu.CompilerParams(
            dimension_semantics=("arbitrary", "arbitrary"), vmem_limit_bytes=VMEM_LIMIT),
        name="prompt_layer",
    )(x, mod, w["norm_g"], w["w_in"], w["w_out"],
      tabs["cos_q"], tabs["sin_q"], tabs["cos_k"], tabs["sin_k"],
      tabs["intra"], tabs["q_dec"], tabs["k_dec"], tabs["c_dec"],
      w["gn_g"], w["conv_w"], w["conv_b"], w["w_gate"], w["b_gate"], w["lam"],
      w["bd"], w["cd"], w["apr"], w["api"], w["s5_d"], w["w_glu"], w["final_g"])


def _decode_kernel(x_ref, mod_ref, ng_ref, win_ref, wout_ref,
                   cq_ref, sq_ref, ck_ref, sk_ref, gam_ref,
                   gng_ref, cw_ref, cb_ref, wg_ref, bg_ref, lam_ref,
                   bd_ref, cd_ref, apr_ref, api_ref, s5d_ref, wglu_ref, fg_ref,
                   sret_in, hl_in, cv_in, sre_in, sim_in, sel_ref,
                   y_ref, sret_ref, hl_ref, cv_ref, sre_ref, sim_ref,
                   qkt_scr, v_scr, o_scr, gr_scr, ycat_scr, x_scr):
    layer = pl.program_id(0)
    i = pl.program_id(1)
    nb = pl.num_programs(1)
    rows_n, d = x_ref.shape
    (mod_ref, ng_ref, win_ref, wout_ref, gng_ref, cw_ref, cb_ref, wg_ref, bg_ref, lam_ref,
     bd_ref, cd_ref, apr_ref, api_ref, s5d_ref, wglu_ref,
     sret_in, hl_in, cv_in, sre_in, sim_in, sret_ref, hl_ref, cv_ref, sre_ref, sim_ref) = (
        r.at[0] for r in (mod_ref, ng_ref, win_ref, wout_ref, gng_ref, cw_ref, cb_ref, wg_ref,
                          bg_ref, lam_ref, bd_ref, cd_ref, apr_ref, api_ref, s5d_ref, wglu_ref,
                          sret_in, hl_in, cv_in, sre_in, sim_in,
                          sret_ref, hl_ref, cv_ref, sre_ref, sim_ref))

    @pl.when((i == 0) & (layer == 0))
    def _():
        x_scr[...] = x_ref[...]

    @pl.when(i == 0)
    def _():
        x = x_scr[...]
        shift, scale = mod_ref[:, pl.ds(0, d)], mod_ref[:, pl.ds(d, d)]
        h = _rmsnorm(x, ng_ref[...]) * (1.0 + scale) + shift
        proj = _dot(h, win_ref[...])
        cq, sq, ck, sk = cq_ref[...], sq_ref[...], ck_ref[...], sk_ref[...]
        for hd in range(RET_HEADS):
            lo = hd * HEAD_DIM
            for part, src in enumerate((_rotary(proj[:, OFF_Q + lo:OFF_Q + lo + HEAD_DIM], cq, sq),
                                        _rotary(proj[:, OFF_K + lo:OFF_K + lo + HEAD_DIM], ck, sk))):
                xt = src.T
                hi = xt.astype(BF16)
                trows = pl.ds((2 * hd + part) * HEAD_DIM, HEAD_DIM)
                qkt_scr[trows, pl.ds(0, rows_n)] = hi
                qkt_scr[trows, pl.ds(rows_n, rows_n)] = (xt - hi.astype(F32)).astype(BF16)
        v_scr[...] = proj[:, OFF_V:OFF_V + RET_WIDTH]
        gr_scr[...] = _silu(proj[:, OFF_GR:OFF_GR + RET_WIDTH])
        xl = proj[:, OFF_XL:OFF_XL + LRU_WIDTH]
        cs = cv_in[...]
        xc = cb_ref[...] + xl * cw_ref[pl.ds(CONV_WIDTH - 1, 1), :]
        for j in range(CONV_WIDTH - 1):
            xc = xc + cs[:, j * LRU_WIDTH:(j + 1) * LRU_WIDTH] * cw_ref[pl.ds(j, 1), :]
        cv_ref[:, pl.ds(0, 2 * LRU_WIDTH)] = cs[:, LRU_WIDTH:]
        cv_ref[:, pl.ds(2 * LRU_WIDTH, LRU_WIDTH)] = xl
        gates = _dot(xc, wg_ref[...]) + bg_ref[...]
        a, b = _lru_coeffs(xc, gates, _softplus(-lam_ref[...]))
        hh = b + a * hl_in[...]
        hl_ref[...] = hh
        ycat_scr[:, pl.ds(RET_WIDTH, LRU_WIDTH)] = (
            hh * _silu(proj[:, OFF_GL:OFF_GL + LRU_WIDTH])).astype(BF16)
        u = proj[:, OFF_U:OFF_U + S5_WIDTH]
        bu = _dot(u, bd_ref[...])
        ar, ai = apr_ref[pl.ds(0, 1), :], api_ref[pl.ds(0, 1), :]
        s0r, s0i = sre_in[...], sim_in[...]
        sr = bu[:, :S5_FLAT] + (ar * s0r - ai * s0i)
        si = bu[:, S5_FLAT:] + (ar * s0i + ai * s0r)
        sre_ref[...] = sr
        sim_ref[...] = si
        y_s5 = _s5_tail(jnp.concatenate([sr, si], axis=-1), u, proj[:, OFF_GS:OFF_GS + S5_WIDTH],
                        cd_ref[...], s5d_ref[...], wglu_ref[...])
        ycat_scr[:, pl.ds(RET_WIDTH + LRU_WIDTH, S5_WIDTH)] = y_s5.astype(BF16)

    blk_rows = pl.ds(pl.multiple_of(i * DEC_BLOCK, DEC_BLOCK), DEC_BLOCK)
    o_rows = [[] for _ in range(RET_HEADS)]
    vblks = [v_scr[blk_rows, pl.ds(hd * HEAD_DIM, HEAD_DIM)] for hd in range(RET_HEADS)]
    for jp in range(DEC_BLOCK // 2):
        sel = sel_ref[0, :, pl.ds(jp * 2 * HEAD_DIM, 2 * HEAD_DIM)]
        tiles = jnp.dot(qkt_scr[...], jnp.concatenate([sel, sel], axis=0),
                        preferred_element_type=F32)
        for hd in range(RET_HEADS):
            qb = tiles[2 * hd * HEAD_DIM:(2 * hd + 1) * HEAD_DIM, :]
            kb = tiles[(2 * hd + 1) * HEAD_DIM:(2 * hd + 2) * HEAD_DIM, :]
            for j in (2 * jp, 2 * jp + 1):
                cols = slice((j % 2) * HEAD_DIM, (j % 2 + 1) * HEAD_DIM)
                s_new = gam_ref[hd] * sret_in[j, hd] + kb[:, cols] * vblks[hd][j:j + 1, :]
                sret_ref[j, hd] = s_new
                o_rows[hd].append(jnp.sum(qb[:, cols] * s_new, axis=0, keepdims=True))
    for hd in range(RET_HEADS):
        o_scr[blk_rows, pl.ds(hd * HEAD_DIM, HEAD_DIM)] = jnp.concatenate(o_rows[hd], axis=0)

    @pl.when(i == nb - 1)
    def _():
        for hd in range(RET_HEADS):
            lanes = pl.ds(hd * HEAD_DIM, HEAD_DIM)
            on = _groupnorm(o_scr[:, lanes], gng_ref[:, lanes])
            ycat_scr[:, lanes] = (on * gr_scr[:, lanes]).astype(BF16)
        out = jnp.dot(ycat_scr[...], wout_ref[...], preferred_element_type=F32)
        x_new = x_scr[...] + mod_ref[:, pl.ds(2 * d, d)] * out
        x_scr[...] = x_new
        y_ref[...] = _rmsnorm(x_new, fg_ref[...])


def _decode_layers(x, mod, states, w, tabs):
    rows, d = x.shape
    depth = mod.shape[0]
    s_ret, s_h, s_conv, s_re, s_im = states
    nb = rows // DEC_BLOCK

    def full(a):
        return pl.BlockSpec(a.shape, lambda l, i: (0,) * a.ndim)

    def lay(a):
        return pl.BlockSpec((1,) + a.shape[1:], lambda l, i: (l,) + (0,) * (a.ndim - 1))

    smem = pl.BlockSpec(memory_space=pltpu.SMEM)
    sblk = pl.BlockSpec((1, DEC_BLOCK, RET_HEADS, HEAD_DIM, HEAD_DIM), lambda l, i: (l, i, 0, 0, 0))
    in_specs = [
        full(x), lay(mod), lay(w["norm_g"]), lay(w["w_in"]), lay(w["w_out"]),
        full(tabs["dcos_q"]), full(tabs["dsin_q"]), full(tabs["dcos_k"]), full(tabs["dsin_k"]), smem,
        lay(w["gn_g"]), lay(w["conv_w"]), lay(w["conv_b"]), lay(w["w_gate"]), lay(w["b_gate"]),
        lay(w["lam"]), lay(w["bd"]), lay(w["cd"]), lay(w["apr"]), lay(w["api"]),
        lay(w["s5_d"]), lay(w["w_glu"]), full(w["final_g"]),
        sblk, lay(s_h), lay(s_conv), lay(s_re), lay(s_im),
        pl.BlockSpec((1,) + tabs["row_select"].shape[1:], lambda l, i: (i, 0, 0)),
    ]
    out_shape = (
        jax.ShapeDtypeStruct((rows, d), F32),
        jax.ShapeDtypeStruct(s_ret.shape, F32),
        jax.ShapeDtypeStruct(s_h.shape, F32),
        jax.ShapeDtypeStruct(s_conv.shape, F32),
        jax.ShapeDtypeStruct(s_re.shape, F32),
        jax.ShapeDtypeStruct(s_im.shape, F32),
    )
    out_specs = (full(x), sblk, lay(s_h), lay(s_conv), lay(s_re), lay(s_im))
    scratch = [
        pltpu.VMEM((RET_HEADS * 2 * HEAD_DIM, 2 * rows), BF16),
        pltpu.VMEM((rows, RET_WIDTH), F32),
        pltpu.VMEM((rows, RET_WIDTH), F32),
        pltpu.VMEM((rows, RET_WIDTH), F32),
        pltpu.VMEM((rows, d), BF16),
        pltpu.VMEM((rows, d), F32),
    ]
    return pl.pallas_call(
        _decode_kernel,
        grid=(depth, nb),
        in_specs=in_specs,
        out_specs=out_specs,
        out_shape=out_shape,
        scratch_shapes=scratch,
        compiler_params=pltpu.CompilerParams(
            dimension_semantics=("arbitrary", "arbitrary"), vmem_limit_bytes=VMEM_LIMIT),
        name="decode_layers",
    )(x, mod, w["norm_g"], w["w_in"], w["w_out"],
      tabs["dcos_q"], tabs["dsin_q"], tabs["dcos_k"], tabs["dsin_k"], tabs["gamma"],
      w["gn_g"], w["conv_w"], w["conv_b"], w["w_gate"], w["b_gate"], w["lam"],
      w["bd"], w["cd"], w["apr"], w["api"], w["s5_d"], w["w_glu"], w["final_g"],
      s_ret, s_h, s_conv, s_re, s_im, tabs["row_select"])


def _rope_tables(pos, scale):
    half = HEAD_DIM // 2
    inv = ROPE_BASE ** (-np.arange(half, dtype=np.float64) / half)
    ang = pos[:, None] * inv[None, :]
    cos, sin = np.cos(ang), np.sin(ang)
    cosf = np.concatenate([cos, cos], axis=-1) * scale
    sinf = np.concatenate([-sin, sin], axis=-1) * scale
    return cosf, sinf


def _tables(seq, dec):
    c = RET_CHUNK
    log_g = np.log1p(-np.exp2(-5.0 - np.arange(RET_HEADS, dtype=np.float64)))
    idx = np.arange(c, dtype=np.float64)
    rel = idx[:, None] - idx[None, :]
    intra = np.where(rel[None] >= 0, np.exp(np.maximum(rel, 0.0)[None] * log_g[:, None, None]), 0.0)
    q_dec = np.exp((idx + 1.0)[:, None] * log_g[None, :])
    k_dec = np.exp((c - 1.0 - idx)[:, None] * log_g[None, :])
    q_dec = np.broadcast_to(q_dec.T[:, :, None], (RET_HEADS, c, HEAD_DIM))
    k_dec = np.broadcast_to(k_dec.T[:, :, None], (RET_HEADS, c, HEAD_DIM))
    c_dec = np.exp(c * log_g)
    gamma = np.exp(log_g)
    k_scale = HEAD_DIM ** -0.5
    cos_q, sin_q = _rope_tables(np.arange(seq, dtype=np.float64), 1.0)
    cos_k, sin_k = _rope_tables(np.arange(seq, dtype=np.float64), k_scale)
    dpos = PAST_LEN + np.arange(1, dtype=np.float64)
    dcos_q, dsin_q = _rope_tables(dpos, 1.0)
    dcos_k, dsin_k = _rope_tables(dpos, k_scale)
    blk = np.arange(dec // DEC_BLOCK)[:, None, None]
    row = np.arange(dec)[None, :, None]
    col = np.arange(DEC_BLOCK * HEAD_DIM)[None, None, :]
    row_select = row == blk * DEC_BLOCK + col // HEAD_DIM
    tabs = dict(intra=intra, q_dec=q_dec, k_dec=k_dec, c_dec=c_dec, gamma=gamma,
                cos_q=cos_q, sin_q=sin_q, cos_k=cos_k, sin_k=sin_k,
                dcos_q=dcos_q, dsin_q=dsin_q, dcos_k=dcos_k, dsin_k=dsin_k)
    tabs = {k: jnp.asarray(np.ascontiguousarray(v), dtype=F32) for k, v in tabs.items()}
    tabs["row_select"] = jnp.asarray(row_select, dtype=BF16)
    return tabs


def _block_diag(blocks):
    n, r, c = blocks.shape
    eye = jnp.eye(n, dtype=blocks.dtype)
    return jnp.einsum("nrc,nm->nrmc", blocks, eye).reshape(n * r, n * c)


def kernel(x_prompt, x_sample, state_ret, state_lru_h, state_lru_conv, state_s5_re, state_s5_im,
           c_prompt, c_sample, norm_g, w_ada, b_ada, w_in, ret_gn_g, conv_w, conv_b, w_rg, b_rg,
           w_ig, b_ig, lru_lambda, s5_a_re, s5_a_im, s5_b_re, s5_b_im, s5_c_re, s5_c_im, s5_d,
           s5_log_dt, s5_w_glu, w_out, final_g):
    depth = w_in.shape[0]
    bsz, seq, d = x_prompt.shape
    dec = x_sample.shape[0]
    assert x_sample.shape[1] == 1 and seq % RET_CHUNK == 0 and dec % DEC_BLOCK == 0

    tabs = _tables(seq, dec)
    mod_all = _ada(jnp.concatenate([c_prompt, c_sample], axis=0), w_ada, b_ada)
    n_pow = min(SEQ_TILE, seq) // SUBLANES
    apr, api, bbr, bbi = _s5_prep(s5_a_re, s5_a_im, s5_log_dt, s5_b_re, s5_b_im, n_pow)

    bdiag = jax.vmap(_block_diag)
    bd = jnp.concatenate([bdiag(bbr), bdiag(bbi)], axis=2)
    cd = jnp.concatenate([bdiag(jnp.swapaxes(s5_c_re, 2, 3)),
                          bdiag(jnp.swapaxes(-s5_c_im, 2, 3))], axis=1)
    w = dict(
        norm_g=norm_g.reshape(depth, 1, d),
        w_in=w_in.astype(BF16),
        w_out=w_out.astype(BF16),
        gn_g=ret_gn_g.reshape(depth, 1, RET_WIDTH),
        conv_w=conv_w,
        conv_b=conv_b.reshape(depth, 1, LRU_WIDTH),
        w_gate=jnp.concatenate([bdiag(w_rg), bdiag(w_ig)], axis=2).astype(BF16),
        b_gate=jnp.concatenate([b_rg, b_ig], axis=1).reshape(depth, 1, 2 * LRU_WIDTH),
        lam=lru_lambda.reshape(depth, 1, LRU_WIDTH),
        bd=bd.astype(BF16),
        cd=cd.astype(BF16),
        apr=apr.reshape(depth, n_pow, S5_FLAT),
        api=api.reshape(depth, n_pow, S5_FLAT),
        s5_d=s5_d.reshape(depth, 1, S5_WIDTH),
        w_glu=s5_w_glu.astype(BF16),
        final_g=final_g.reshape(1, d),
    )

    xp = x_prompt
    outs_p = []
    for l in range(depth):
        mod_p = mod_all[l, :bsz].reshape(bsz, 3, d)
        xp, sret, hl, cv, sre, sim = _prompt_layer(xp, mod_p, w, tabs, l, final=l == depth - 1)
        outs_p.append((sret, hl.reshape(bsz, LRU_WIDTH), cv,
                       sre.reshape(bsz, S5_GROUPS, S5_STATE), sim.reshape(bsz, S5_GROUPS, S5_STATE)))

    states = (state_ret, state_lru_h,
              state_lru_conv.reshape(depth, dec, (CONV_WIDTH - 1) * LRU_WIDTH),
              state_s5_re.reshape(depth, dec, S5_FLAT), state_s5_im.reshape(depth, dec, S5_FLAT))
    xs, sret_s, hl_s, cv_s, sre_s, sim_s = _decode_layers(
        x_sample.reshape(dec, d), mod_all[:, bsz:], states, w, tabs)

    def stk(k):
        return jnp.stack([o[k] for o in outs_p])

    return (xp, xs.reshape(dec, 1, d),
            stk(0), sret_s, stk(1), hl_s,
            stk(2), cv_s.reshape(depth, dec, CONV_WIDTH - 1, LRU_WIDTH),
            stk(3), sre_s.reshape(depth, dec, S5_GROUPS, S5_STATE),
            stk(4), sim_s.reshape(depth, dec, S5_GROUPS, S5_STATE))
```

```python
import functools
import math

import jax
import jax.numpy as jnp
import numpy as np
from jax import lax
from jax.experimental import pallas as pl
from jax.experimental.pallas import tpu as pltpu

RET_HEADS = 4
HEAD_DIM = 128
RET_WIDTH = RET_HEADS * HEAD_DIM
RET_CHUNK = 128
LRU_WIDTH = 256
LRU_C = 8.0
CONV_WIDTH = 4
S5_WIDTH = 256
S5_GROUP = 16
S5_GROUPS = 16
S5_STATE = 64
S5_FLAT = S5_GROUPS * S5_STATE
ROPE_BASE = 10000.0
EPS = 1e-6
PAST_LEN = 16384

OFF_Q = 0
OFF_K = OFF_Q + RET_WIDTH
OFF_V = OFF_K + RET_WIDTH
OFF_GR = OFF_V + RET_WIDTH
OFF_XL = OFF_GR + RET_WIDTH
OFF_GL = OFF_XL + LRU_WIDTH
OFF_U = OFF_GL + LRU_WIDTH
OFF_GS = OFF_U + S5_WIDTH
D_IN = OFF_GS + S5_WIDTH

SUBLANES = 8
LANES = 128
SEQ_TILE = 512
PROJ_CHUNK = 256
DEC_BLOCK = 16
VMEM_LIMIT = 56 * 1024 * 1024

F32 = jnp.float32
BF16 = jnp.bfloat16


def _sigmoid(x):
    return 1.0 / (1.0 + jnp.exp(-x))


def _silu(x):
    return x * _sigmoid(x)


def _gelu_tanh(x):
    c = math.sqrt(2.0 / math.pi)
    return x * (0.5 * (1.0 + jnp.tanh(c * (x + 0.044715 * (x * x * x)))))


def _softplus(x):
    return jnp.maximum(x, 0.0) + jnp.log1p(jnp.exp(-jnp.abs(x)))


def _rmsnorm(x, g):
    ms = jnp.mean(x * x, axis=-1, keepdims=True)
    return x * lax.rsqrt(ms + EPS) * g


def _dot(a, b):
    return jnp.dot(a.astype(BF16), b.astype(BF16), preferred_element_type=F32)


def _dot_nt(a, b):
    return lax.dot_general(a.astype(BF16), b.astype(BF16), (((1,), (1,)), ((), ())),
                           preferred_element_type=F32)


def _rotary(x, cosf, sinf):
    return x * cosf + pltpu.roll(x, HEAD_DIM // 2, axis=1) * sinf


def _groupnorm(o, g):
    mu = jnp.mean(o, axis=-1, keepdims=True)
    d = o - mu
    var = jnp.mean(d * d, axis=-1, keepdims=True)
    return d * lax.rsqrt(var + EPS) * g


def _lru_coeffs(xc, gates, sp):
    r = _sigmoid(gates[:, :LRU_WIDTH])
    ig = _sigmoid(gates[:, LRU_WIDTH:])
    log_a = (-LRU_C) * r * sp
    a = jnp.exp(log_a)
    th = jnp.tanh(log_a)
    mult = jnp.sqrt(-2.0 * th / (1.0 - th))
    return a, mult * ig * xc


def _s5_tail(sr_si, u, gs, cd, s5d, wglu, row_parts=1):
    rows = sr_si.shape[0] // row_parts
    ys = jnp.concatenate([_dot(sr_si[p * rows:(p + 1) * rows], cd) for p in range(row_parts)],
                         axis=0) + s5d * u
    ys = _gelu_tanh(ys)
    ys = ys * _sigmoid(_dot(ys, wglu))
    return ys * _silu(gs)


def _s5_prep_kernel(are_ref, aim_ref, ldt_ref, bre_ref, bim_ref,
                    apr_ref, api_ref, bbr_ref, bbi_ref):
    depth = are_ref.shape[0]
    for l in range(depth):
        a_re = are_ref[l]
        a_im = aim_ref[l]
        step = jnp.exp(ldt_ref[l])
        mag = jnp.exp(step * a_re)
        abar_r = mag * jnp.cos(step * a_im)
        abar_i = mag * jnp.sin(step * a_im)
        nr, ni = abar_r - 1.0, abar_i
        den = a_re * a_re + a_im * a_im
        fr = (nr * a_re + ni * a_im) / den
        fi = (ni * a_re - nr * a_im) / den
        for g in range(S5_GROUPS):
            frg, fig = fr[g:g + 1, :], fi[g:g + 1, :]
            b_re, b_im = bre_ref[l, g], bim_ref[l, g]
            bbr_ref[l, g] = frg * b_re - fig * b_im
            bbi_ref[l, g] = frg * b_im + fig * b_re
        pr, pi = abar_r, abar_i
        apr_ref[l, 0] = pr
        api_ref[l, 0] = pi
        for m in range(1, apr_ref.shape[1]):
            pr, pi = pr * abar_r - pi * abar_i, pr * abar_i + pi * abar_r
            apr_ref[l, m] = pr
            api_ref[l, m] = pi


def _s5_prep(s5_a_re, s5_a_im, s5_log_dt, s5_b_re, s5_b_im, n_pow):
    depth = s5_a_re.shape[0]
    b_re_t = jnp.swapaxes(s5_b_re, 2, 3)
    b_im_t = jnp.swapaxes(s5_b_im, 2, 3)
    ldt = s5_log_dt.reshape(depth, S5_GROUPS, 1)
    out_shape = (
        jax.ShapeDtypeStruct((depth, n_pow, S5_GROUPS, S5_STATE), F32),
        jax.ShapeDtypeStruct((depth, n_pow, S5_GROUPS, S5_STATE), F32),
        jax.ShapeDtypeStruct((depth, S5_GROUPS, S5_GROUP, S5_STATE), F32),
        jax.ShapeDtypeStruct((depth, S5_GROUPS, S5_GROUP, S5_STATE), F32),
    )
    return pl.pallas_call(_s5_prep_kernel, out_shape=out_shape, name="s5_prep")(
        s5_a_re, s5_a_im, ldt, b_re_t, b_im_t)


def _ada_kernel(c_ref, w_ref, b_ref, o_ref):
    s = _silu(c_ref[...])
    o_ref[0] = _dot(s, w_ref[0]) + b_ref[0]


def _ada(c_all, w_ada, b_ada):
    depth, d, n3 = w_ada.shape
    rows = c_all.shape[0]
    tn = d
    return pl.pallas_call(
        _ada_kernel,
        grid=(depth, n3 // tn),
        in_specs=[
            pl.BlockSpec((rows, d), lambda l, j: (0, 0)),
            pl.BlockSpec((1, d, tn), lambda l, j: (l, 0, j)),
            pl.BlockSpec((1, 1, tn), lambda l, j: (l, 0, j)),
        ],
        out_specs=pl.BlockSpec((1, rows, tn), lambda l, j: (l, 0, j)),
        out_shape=jax.ShapeDtypeStruct((depth, rows, n3), F32),
        compiler_params=pltpu.CompilerParams(
            dimension_semantics=("arbitrary", "arbitrary"), vmem_limit_bytes=VMEM_LIMIT),
        name="adaln",
    )(c_all, w_ada, b_ada.reshape(depth, 1, n3))


def _prompt_kernel(x_ref, mod_ref, ng_ref, win_ref, wout_ref,
                   cq_ref, sq_ref, ck_ref, sk_ref, intra_ref, qdec_ref, kdec_ref, cdec_ref,
                   gng_ref, cw_ref, cb_ref, wg_ref, bg_ref, lam_ref,
                   bd_ref, cd_ref, apr_ref, api_ref, s5d_ref, wglu_ref, fg_ref,
                   y_ref, sret_ref, hl_ref, cv_ref, sre_ref, sim_ref,
                   proj_scr, ycat_scr, perm_scr, xpad_scr, ab_scr, st_scr, st16_scr, yp_scr, h_scr,
                   ph_scr, hloc_scr, *, final):
    tl = x_ref.shape[1]
    seg = tl // SUBLANES
    t = pl.program_id(1)
    (ng_ref, win_ref, wout_ref, gng_ref, cw_ref, cb_ref, wg_ref, bg_ref, lam_ref,
     bd_ref, cd_ref, apr_ref, api_ref, s5d_ref, wglu_ref) = (
        r.at[0] for r in (ng_ref, win_ref, wout_ref, gng_ref, cw_ref, cb_ref, wg_ref, bg_ref,
                          lam_ref, bd_ref, cd_ref, apr_ref, api_ref, s5d_ref, wglu_ref))

    @pl.when(t == 0)
    def _():
        sret_ref[...] = jnp.zeros_like(sret_ref)
        hl_ref[...] = jnp.zeros_like(hl_ref)
        sre_ref[...] = jnp.zeros_like(sre_ref)
        sim_ref[...] = jnp.zeros_like(sim_ref)
        cv_ref[...] = jnp.zeros_like(cv_ref)

    x = x_ref[0]
    shift, scale = mod_ref[0, 0:1, :], mod_ref[0, 1:2, :]
    h = _rmsnorm(x, ng_ref[...] * (1.0 + scale)) + shift
    h_scr[...] = h.astype(BF16)
    proj_b = jnp.dot(h_scr[...], win_ref[:, pl.ds(OFF_XL, D_IN - OFF_XL)],
                     preferred_element_type=F32)

    def proj_a_chunk(ci):
        cols = pl.ds(ci * PROJ_CHUNK, PROJ_CHUNK)
        res = jnp.dot(h_scr[...], win_ref[:, cols], preferred_element_type=F32)
        for kl in range(PROJ_CHUNK // LANES):
            proj_scr[ci * (PROJ_CHUNK // LANES) + kl] = res[:, kl * LANES:(kl + 1) * LANES]

    for m in range(tl // SUBLANES):
        s, r0 = divmod(m * SUBLANES, seg)
        for kb in range((D_IN - OFF_XL) // LANES):
            perm_scr[kb, pl.ds(r0 * SUBLANES + s, SUBLANES, stride=SUBLANES), :] = (
                proj_b[m * SUBLANES:(m + 1) * SUBLANES, kb * LANES:(kb + 1) * LANES])

    def ret_piece(c, hd):
        rows = pl.ds(c * RET_CHUNK, RET_CHUNK)
        lanes = pl.ds(hd * HEAD_DIM, HEAD_DIM)
        q = _rotary(proj_scr[OFF_Q // LANES + hd, rows, :], cq_ref[rows, :], sq_ref[rows, :])
        k = _rotary(proj_scr[OFF_K // LANES + hd, rows, :], ck_ref[rows, :], sk_ref[rows, :])
        v = proj_scr[OFF_V // LANES + hd, rows, :].astype(BF16)
        s_old = sret_ref[0, hd]
        sc = _dot_nt(q, k) * intra_ref[hd]
        o = _dot(sc, v) + _dot(q * qdec_ref[hd], s_old)
        kd_t = (k * kdec_ref[hd]).T
        sret_ref[0, hd] = cdec_ref[hd] * s_old + _dot(kd_t, v)
        on = _groupnorm(o, gng_ref[:, lanes])
        g_ret = proj_scr[OFF_GR // LANES + hd, rows, :]
        ycat_scr[hd, rows, :] = (on * _silu(g_ret)).astype(BF16)

    sub = lax.broadcasted_iota(jnp.int32, (SUBLANES, 1), 0)
    rg = lambda r: pl.ds(r * SUBLANES, SUBLANES)

    def pcols(lo, width):
        return jnp.concatenate([perm_scr[kb] for kb in range(lo // LANES, (lo + width) // LANES)],
                               axis=1)

    n_chunks = OFF_XL // PROJ_CHUNK
    s5_parts = n_chunks - 3

    proj_a_chunk(0)
    xl = pcols(OFF_XL - OFF_XL, LRU_WIDTH)
    prev = cv_ref[0]
    npre = CONV_WIDTH - 1
    for m in range(1, CONV_WIDTH):
        grp = xl[(seg - m) * SUBLANES:(seg - m + 1) * SUBLANES, :]
        e_m = jnp.where(sub == 0, prev[npre - m:npre - m + 1, :], pltpu.roll(grp, 1, axis=0))
        xpad_scr[rg(npre - m), :] = e_m
    xpad_scr[pl.ds(npre * SUBLANES, tl), :] = xl
    cv_ref[0] = jnp.concatenate(
        [xl[(seg - m) * SUBLANES + SUBLANES - 1:(seg - m + 1) * SUBLANES, :]
         for m in range(npre, 0, -1)], axis=0)
    xc = cb_ref[...]
    for j in range(CONV_WIDTH):
        xc = xc + xpad_scr[pl.ds(j * SUBLANES, tl), :] * cw_ref[pl.ds(j, 1), :]
    gates = _dot(xc, wg_ref[...]) + bg_ref[...]
    a, b = _lru_coeffs(xc, gates, _softplus(-lam_ref[...]))
    ab_scr[0] = a
    ab_scr[1] = b
    proj_a_chunk(1)
    hloc = jnp.zeros((SUBLANES, LRU_WIDTH), F32)
    prod = jnp.ones((SUBLANES, LRU_WIDTH), F32)
    for r in range(seg):
        ar = ab_scr[0, rg(r), :]
        hloc = ar * hloc + ab_scr[1, rg(r), :]
        prod = prod * ar
        ph_scr[0, rg(r), :] = prod
        ph_scr[1, rg(r), :] = hloc
    c = hl_ref[0]
    cs = [c]
    for s in range(SUBLANES - 1):
        c = hloc[s:s + 1, :] + prod[s:s + 1, :] * c
        cs.append(c)
    hl_ref[0] = hloc[SUBLANES - 1:, :] + prod[SUBLANES - 1:, :] * c
    c_all = jnp.concatenate(cs, axis=0)
    sgl = _silu(pcols(OFF_GL - OFF_XL, LRU_WIDTH))
    for r in range(seg):
        y = (ph_scr[1, rg(r), :] + ph_scr[0, rg(r), :] * c_all) * sgl[r * SUBLANES:(r + 1) * SUBLANES, :]
        for kb in range(LRU_WIDTH // LANES):
            yp_scr[kb, rg(r), :] = y[:, kb * LANES:(kb + 1) * LANES]

    proj_a_chunk(2)
    u = pcols(OFF_U - OFF_XL, S5_WIDTH)
    bu = _dot(u, bd_ref[...])
    st_scr[0] = bu[:, :S5_FLAT]
    st_scr[1] = bu[:, S5_FLAT:]
    bshape = (SUBLANES, S5_FLAT)
    a1r = jnp.broadcast_to(apr_ref[pl.ds(0, 1), :], bshape)
    a1i = jnp.broadcast_to(api_ref[pl.ds(0, 1), :], bshape)
    hr = jnp.zeros((SUBLANES, S5_FLAT), F32)
    hi = jnp.zeros((SUBLANES, S5_FLAT), F32)
    for r in range(seg):
        if r % (-(-seg // s5_parts)) == 0:
            proj_a_chunk(3 + r // (-(-seg // s5_parts)))
        hr, hi = (a1r * hr - a1i * hi + st_scr[0, rg(r), :],
                  a1r * hi + a1i * hr + st_scr[1, rg(r), :])
        hloc_scr[0, rg(r), :] = hr
        hloc_scr[1, rg(r), :] = hi
    asr, asi = apr_ref[pl.ds(seg - 1, 1), :], api_ref[pl.ds(seg - 1, 1), :]
    cr, ci = sre_ref[0], sim_ref[0]
    crs, cis = [cr], [ci]
    for s in range(SUBLANES - 1):
        cr, ci = (hr[s:s + 1, :] + (asr * cr - asi * ci), hi[s:s + 1, :] + (asr * ci + asi * cr))
        crs.append(cr)
        cis.append(ci)
    sre_ref[0] = hr[SUBLANES - 1:, :] + (asr * cr - asi * ci)
    sim_ref[0] = hi[SUBLANES - 1:, :] + (asr * ci + asi * cr)
    cr_all, ci_all = jnp.concatenate(crs, axis=0), jnp.concatenate(cis, axis=0)
    n_pieces = (tl // RET_CHUNK) * RET_HEADS
    per_piece = (seg // 2) // n_pieces
    for r2 in range(seg // 2):
        if r2 % per_piece == 0:
            ret_piece(*divmod(r2 // per_piece, RET_HEADS))
        parts_r, parts_i = [], []
        for r in (2 * r2, 2 * r2 + 1):
            qr = jnp.broadcast_to(apr_ref[pl.ds(r, 1), :], bshape)
            qi = jnp.broadcast_to(api_ref[pl.ds(r, 1), :], bshape)
            parts_r.append(hloc_scr[0, rg(r), :] + (qr * cr_all - qi * ci_all))
            parts_i.append(hloc_scr[1, rg(r), :] + (qr * ci_all + qi * cr_all))
        rows2 = pl.ds(r2 * 2 * SUBLANES, 2 * SUBLANES)
        st16_scr[0, rows2, :] = jnp.concatenate(parts_r, axis=0).astype(BF16)
        st16_scr[1, rows2, :] = jnp.concatenate(parts_i, axis=0).astype(BF16)
    gs = pcols(OFF_GS - OFF_XL, S5_WIDTH)
    y_s5 = _s5_tail(jnp.concatenate([st16_scr[0], st16_scr[1]], axis=1), u, gs,
                    cd_ref[...], s5d_ref[...], wglu_ref[...], row_parts=2)
    for kb in range(S5_WIDTH // LANES):
        yp_scr[LRU_WIDTH // LANES + kb] = y_s5[:, kb * LANES:(kb + 1) * LANES]

    for m2 in range(tl // (2 * SUBLANES)):
        for kb in range((LRU_WIDTH + S5_WIDTH) // LANES):
            halves = []
            for m in (2 * m2, 2 * m2 + 1):
                s, r0 = divmod(m * SUBLANES, seg)
                halves.append(yp_scr[kb, pl.ds(r0 * SUBLANES + s, SUBLANES, stride=SUBLANES), :])
            ycat_scr[RET_WIDTH // LANES + kb, pl.ds(m2 * 2 * SUBLANES, 2 * SUBLANES), :] = (
                jnp.concatenate(halves, axis=0).astype(BF16))

    ycat = jnp.concatenate([ycat_scr[kb] for kb in range(ycat_scr.shape[0])], axis=1)
    out = jnp.dot(ycat, wout_ref[...], preferred_element_type=F32)
    x_new = x_ref[0] + mod_ref[0, 2:3, :] * out
    if final:
        x_new = _rmsnorm(x_new, fg_ref[...])
    y_ref[0] = x_new


def _prompt_layer(x, mod, w, tabs, layer, *, final):
    bsz, seq, d = x.shape
    tl = min(SEQ_TILE, seq)
    nt = seq // tl

    def full(a):
        return pl.BlockSpec(a.shape, lambda b, t: (0,) * a.ndim)

    def lay(a):
        return pl.BlockSpec((1,) + a.shape[1:], lambda b, t: (layer,) + (0,) * (a.ndim - 1))

    row_tab = pl.BlockSpec((tl, HEAD_DIM), lambda b, t: (t, 0))
    in_specs = [
        pl.BlockSpec((1, tl, d), lambda b, t: (b, t, 0)),
        pl.BlockSpec((1, 3, d), lambda b, t: (b, 0, 0)),
        lay(w["norm_g"]), lay(w["w_in"]), lay(w["w_out"]),
        row_tab, row_tab, row_tab, row_tab,
        full(tabs["intra"]), full(tabs["q_dec"]), full(tabs["k_dec"]),
        pl.BlockSpec(memory_space=pltpu.SMEM),
        lay(w["gn_g"]), lay(w["conv_w"]), lay(w["conv_b"]), lay(w["w_gate"]), lay(w["b_gate"]),
        lay(w["lam"]), lay(w["bd"]), lay(w["cd"]), lay(w["apr"]), lay(w["api"]),
        lay(w["s5_d"]), lay(w["w_glu"]), full(w["final_g"]),
    ]
    out_shape = (
        jax.ShapeDtypeStruct((bsz, seq, d), F32),
        jax.ShapeDtypeStruct((bsz, RET_HEADS, HEAD_DIM, HEAD_DIM), F32),
        jax.ShapeDtypeStruct((bsz, 1, LRU_WIDTH), F32),
        jax.ShapeDtypeStruct((bsz, CONV_WIDTH - 1, LRU_WIDTH), F32),
        jax.ShapeDtypeStruct((bsz, 1, S5_FLAT), F32),
        jax.ShapeDtypeStruct((bsz, 1, S5_FLAT), F32),
    )
    out_specs = (
        pl.BlockSpec((1, tl, d), lambda b, t: (b, t, 0)),
        pl.BlockSpec((1, RET_HEADS, HEAD_DIM, HEAD_DIM), lambda b, t: (b, 0, 0, 0)),
        pl.BlockSpec((1, 1, LRU_WIDTH), lambda b, t: (b, 0, 0)),
        pl.BlockSpec((1, CONV_WIDTH - 1, LRU_WIDTH), lambda b, t: (b, 0, 0)),
        pl.BlockSpec((1, 1, S5_FLAT), lambda b, t: (b, 0, 0)),
        pl.BlockSpec((1, 1, S5_FLAT), lambda b, t: (b, 0, 0)),
    )
    scratch = [
        pltpu.VMEM((OFF_XL // LANES, tl, LANES), F32),
        pltpu.VMEM((d // LANES, tl, LANES), BF16),
        pltpu.VMEM(((D_IN - OFF_XL) // LANES, tl, LANES), F32),
        pltpu.VMEM((tl + (CONV_WIDTH - 1) * SUBLANES, LRU_WIDTH), F32),
        pltpu.VMEM((2, tl, LRU_WIDTH), F32),
        pltpu.VMEM((2, tl, S5_FLAT), F32),
        pltpu.VMEM((2, tl, S5_FLAT), BF16),
        pltpu.VMEM(((LRU_WIDTH + S5_WIDTH) // LANES, tl, LANES), F32),
        pltpu.VMEM((tl, d), BF16),
        pltpu.VMEM((2, tl, LRU_WIDTH), F32),
        pltpu.VMEM((2, tl, S5_FLAT), F32),
    ]
    return pl.pallas_call(
        functools.partial(_prompt_kernel, final=final),
        grid=(bsz, nt),
        in_specs=in_specs,
        out_specs=out_specs,
        out_shape=out_shape,
        scratch_shapes=scratch,
        compiler_params=pltpu.CompilerParams(
            dimension_semantics=("arbitrary", "arbitrary"), vmem_limit_bytes=VMEM_LIMIT),
        name="prompt_layer",
    )(x, mod, w["norm_g"], w["w_in"], w["w_out"],
      tabs["cos_q"], tabs["sin_q"], tabs["cos_k"], tabs["sin_k"],
      tabs["intra"], tabs["q_dec"], tabs["k_dec"], tabs["c_dec"],
      w["gn_g"], w["conv_w"], w["conv_b"], w["w_gate"], w["b_gate"], w["lam"],
      w["bd"], w["cd"], w["apr"], w["api"], w["s5_d"], w["w_glu"], w["final_g"])


def _decode_kernel(x_ref, mod_ref, ng_ref, win_ref, wout_ref,
                   cq_ref, sq_ref, ck_ref, sk_ref, gam_ref,
                   gng_ref, cw_ref, cb_ref, wg_ref, bg_ref, lam_ref,
                   bd_ref, cd_ref, apr_ref, api_ref, s5d_ref, wglu_ref, fg_ref,
                   sret_in, hl_in, cv_in, sre_in, sim_in, sel_ref,
                   y_ref, sret_ref, hl_ref, cv_ref, sre_ref, sim_ref,
                   qkt_scr, v_scr, o_scr, gr_scr, ycat_scr, x_scr):
    layer = pl.program_id(0)
    i = pl.program_id(1)
    nb = pl.num_programs(1)
    rows_n, d = x_ref.shape
    (mod_ref, ng_ref, win_ref, wout_ref, gng_ref, cw_ref, cb_ref, wg_ref, bg_ref, lam_ref,
     bd_ref, cd_ref, apr_ref, api_ref, s5d_ref, wglu_ref,
     sret_in, hl_in, cv_in, sre_in, sim_in, sret_ref, hl_ref, cv_ref, sre_ref, sim_ref) = (
        r.at[0] for r in (mod_ref, ng_ref, win_ref, wout_ref, gng_ref, cw_ref, cb_ref, wg_ref,
                          bg_ref, lam_ref, bd_ref, cd_ref, apr_ref, api_ref, s5d_ref, wglu_ref,
                          sret_in, hl_in, cv_in, sre_in, sim_in,
                          sret_ref, hl_ref, cv_ref, sre_ref, sim_ref))

    @pl.when((i == 0) & (layer == 0))
    def _():
        x_scr[...] = x_ref[...]

    @pl.when(i == 0)
    def _():
        x = x_scr[...]
        shift, scale = mod_ref[:, pl.ds(0, d)], mod_ref[:, pl.ds(d, d)]
        h = _rmsnorm(x, ng_ref[...]) * (1.0 + scale) + shift
        proj = _dot(h, win_ref[...])
        cq, sq, ck, sk = cq_ref[...], sq_ref[...], ck_ref[...], sk_ref[...]
        for hd in range(RET_HEADS):
            lo = hd * HEAD_DIM
            for part, src in enumerate((_rotary(proj[:, OFF_Q + lo:OFF_Q + lo + HEAD_DIM], cq, sq),
                                        _rotary(proj[:, OFF_K + lo:OFF_K + lo + HEAD_DIM], ck, sk))):
                xt = src.T
                hi = xt.astype(BF16)
                trows = pl.ds((2 * hd + part) * HEAD_DIM, HEAD_DIM)
                qkt_scr[trows, pl.ds(0, rows_n)] = hi
                qkt_scr[trows, pl.ds(rows_n, rows_n)] = (xt - hi.astype(F32)).astype(BF16)
        v_scr[...] = proj[:, OFF_V:OFF_V + RET_WIDTH]
        gr_scr[...] = _silu(proj[:, OFF_GR:OFF_GR + RET_WIDTH])
        xl = proj[:, OFF_XL:OFF_XL + LRU_WIDTH]
        cs = cv_in[...]
        xc = cb_ref[...] + xl * cw_ref[pl.ds(CONV_WIDTH - 1, 1), :]
        for j in range(CONV_WIDTH - 1):
            xc = xc + cs[:, j * LRU_WIDTH:(j + 1) * LRU_WIDTH] * cw_ref[pl.ds(j, 1), :]
        cv_ref[:, pl.ds(0, 2 * LRU_WIDTH)] = cs[:, LRU_WIDTH:]
        cv_ref[:, pl.ds(2 * LRU_WIDTH, LRU_WIDTH)] = xl
        gates = _dot(xc, wg_ref[...]) + bg_ref[...]
        a, b = _lru_coeffs(xc, gates, _softplus(-lam_ref[...]))
        hh = b + a * hl_in[...]
        hl_ref[...] = hh
        ycat_scr[:, pl.ds(RET_WIDTH, LRU_WIDTH)] = (
            hh * _silu(proj[:, OFF_GL:OFF_GL + LRU_WIDTH])).astype(BF16)
        u = proj[:, OFF_U:OFF_U + S5_WIDTH]
        bu = _dot(u, bd_ref[...])
        ar, ai = apr_ref[pl.ds(0, 1), :], api_ref[pl.ds(0, 1), :]
        s0r, s0i = sre_in[...], sim_in[...]
        sr = bu[:, :S5_FLAT] + (ar * s0r - ai * s0i)
        si = bu[:, S5_FLAT:] + (ar * s0i + ai * s0r)
        sre_ref[...] = sr
        sim_ref[...] = si
        y_s5 = _s5_tail(jnp.concatenate([sr, si], axis=-1), u, proj[:, OFF_GS:OFF_GS + S5_WIDTH],
                        cd_ref[...], s5d_ref[...], wglu_ref[...])
        ycat_scr[:, pl.ds(RET_WIDTH + LRU_WIDTH, S5_WIDTH)] = y_s5.astype(BF16)

    blk_rows = pl.ds(pl.multiple_of(i * DEC_BLOCK, DEC_BLOCK), DEC_BLOCK)
    o_rows = [[] for _ in range(RET_HEADS)]
    vblks = [v_scr[blk_rows, pl.ds(hd * HEAD_DIM, HEAD_DIM)] for hd in range(RET_HEADS)]
    for jp in range(DEC_BLOCK // 2):
        sel = sel_ref[0, :, pl.ds(jp * 2 * HEAD_DIM, 2 * HEAD_DIM)]
        sel2 = jnp.concatenate([sel, sel], axis=0)
        half = qkt_scr.shape[0] // 2
        tiles = jnp.concatenate(
            [jnp.dot(qkt_scr[pl.ds(p * half, half), :], sel2, preferred_element_type=F32)
             for p in range(2)], axis=0)
        for hd in range(RET_HEADS):
            qb = tiles[2 * hd * HEAD_DIM:(2 * hd + 1) * HEAD_DIM, :]
            kb = tiles[(2 * hd + 1) * HEAD_DIM:(2 * hd + 2) * HEAD_DIM, :]
            for j in (2 * jp, 2 * jp + 1):
                cols = slice((j % 2) * HEAD_DIM, (j % 2 + 1) * HEAD_DIM)
                s_new = gam_ref[hd] * sret_in[j, hd] + kb[:, cols] * vblks[hd][j:j + 1, :]
                sret_ref[j, hd] = s_new
                o_rows[hd].append(jnp.sum(qb[:, cols] * s_new, axis=0, keepdims=True))
    for hd in range(RET_HEADS):
        o_scr[blk_rows, pl.ds(hd * HEAD_DIM, HEAD_DIM)] = jnp.concatenate(o_rows[hd], axis=0)

    @pl.when(i == nb - 1)
    def _():
        for hd in range(RET_HEADS):
            lanes = pl.ds(hd * HEAD_DIM, HEAD_DIM)
            on = _groupnorm(o_scr[:, lanes], gng_ref[:, lanes])
            ycat_scr[:, lanes] = (on * gr_scr[:, lanes]).astype(BF16)
        out = jnp.dot(ycat_scr[...], wout_ref[...], preferred_element_type=F32)
        x_new = x_scr[...] + mod_ref[:, pl.ds(2 * d, d)] * out
        x_scr[...] = x_new
        y_ref[...] = _rmsnorm(x_new, fg_ref[...])


def _decode_layers(x, mod, states, w, tabs):
    rows, d = x.shape
    depth = mod.shape[0]
    s_ret, s_h, s_conv, s_re, s_im = states
    nb = rows // DEC_BLOCK

    def full(a):
        return pl.BlockSpec(a.shape, lambda l, i: (0,) * a.ndim)

    def lay(a):
        return pl.BlockSpec((1,) + a.shape[1:], lambda l, i: (l,) + (0,) * (a.ndim - 1))

    smem = pl.BlockSpec(memory_space=pltpu.SMEM)
    sblk = pl.BlockSpec((1, DEC_BLOCK, RET_HEADS, HEAD_DIM, HEAD_DIM), lambda l, i: (l, i, 0, 0, 0))
    in_specs = [
        full(x), lay(mod), lay(w["norm_g"]), lay(w["w_in"]), lay(w["w_out"]),
        full(tabs["dcos_q"]), full(tabs["dsin_q"]), full(tabs["dcos_k"]), full(tabs["dsin_k"]), smem,
        lay(w["gn_g"]), lay(w["conv_w"]), lay(w["conv_b"]), lay(w["w_gate"]), lay(w["b_gate"]),
        lay(w["lam"]), lay(w["bd"]), lay(w["cd"]), lay(w["apr"]), lay(w["api"]),
        lay(w["s5_d"]), lay(w["w_glu"]), full(w["final_g"]),
        sblk, lay(s_h), lay(s_conv), lay(s_re), lay(s_im),
        pl.BlockSpec((1,) + tabs["row_select"].shape[1:], lambda l, i: (i, 0, 0)),
    ]
    out_shape = (
        jax.ShapeDtypeStruct((rows, d), F32),
        jax.ShapeDtypeStruct(s_ret.shape, F32),
        jax.ShapeDtypeStruct(s_h.shape, F32),
        jax.ShapeDtypeStruct(s_conv.shape, F32),
        jax.ShapeDtypeStruct(s_re.shape, F32),
        jax.ShapeDtypeStruct(s_im.shape, F32),
    )
    out_specs = (full(x), sblk, lay(s_h), lay(s_conv), lay(s_re), lay(s_im))
    scratch = [
        pltpu.VMEM((RET_HEADS * 2 * HEAD_DIM, 2 * rows), BF16),
        pltpu.VMEM((rows, RET_WIDTH), F32),
        pltpu.VMEM((rows, RET_WIDTH), F32),
        pltpu.VMEM((rows, RET_WIDTH), F32),
        pltpu.VMEM((rows, d), BF16),
        pltpu.VMEM((rows, d), F32),
    ]
    return pl.pallas_call(
        _decode_kernel,
        grid=(depth, nb),
        in_specs=in_specs,
        out_specs=out_specs,
        out_shape=out_shape,
        scratch_shapes=scratch,
        compiler_params=pltpu.CompilerParams(
            dimension_semantics=("arbitrary", "arbitrary"), vmem_limit_bytes=VMEM_LIMIT),
        name="decode_layers",
    )(x, mod, w["norm_g"], w["w_in"], w["w_out"],
      tabs["dcos_q"], tabs["dsin_q"], tabs["dcos_k"], tabs["dsin_k"], tabs["gamma"],
      w["gn_g"], w["conv_w"], w["conv_b"], w["w_gate"], w["b_gate"], w["lam"],
      w["bd"], w["cd"], w["apr"], w["api"], w["s5_d"], w["w_glu"], w["final_g"],
      s_ret, s_h, s_conv, s_re, s_im, tabs["row_select"])


def _rope_tables(pos, scale):
    half = HEAD_DIM // 2
    inv = ROPE_BASE ** (-np.arange(half, dtype=np.float64) / half)
    ang = pos[:, None] * inv[None, :]
    cos, sin = np.cos(ang), np.sin(ang)
    cosf = np.concatenate([cos, cos], axis=-1) * scale
    sinf = np.concatenate([-sin, sin], axis=-1) * scale
    return cosf, sinf


def _tables(seq, dec):
    c = RET_CHUNK
    log_g = np.log1p(-np.exp2(-5.0 - np.arange(RET_HEADS, dtype=np.float64)))
    idx = np.arange(c, dtype=np.float64)
    rel = idx[:, None] - idx[None, :]
    intra = np.where(rel[None] >= 0, np.exp(np.maximum(rel, 0.0)[None] * log_g[:, None, None]), 0.0)
    q_dec = np.exp((idx + 1.0)[:, None] * log_g[None, :])
    k_dec = np.exp((c - 1.0 - idx)[:, None] * log_g[None, :])
    q_dec = np.broadcast_to(q_dec.T[:, :, None], (RET_HEADS, c, HEAD_DIM))
    k_dec = np.broadcast_to(k_dec.T[:, :, None], (RET_HEADS, c, HEAD_DIM))
    c_dec = np.exp(c * log_g)
    gamma = np.exp(log_g)
    k_scale = HEAD_DIM ** -0.5
    cos_q, sin_q = _rope_tables(np.arange(seq, dtype=np.float64), 1.0)
    cos_k, sin_k = _rope_tables(np.arange(seq, dtype=np.float64), k_scale)
    dpos = PAST_LEN + np.arange(1, dtype=np.float64)
    dcos_q, dsin_q = _rope_tables(dpos, 1.0)
    dcos_k, dsin_k = _rope_tables(dpos, k_scale)
    blk = np.arange(dec // DEC_BLOCK)[:, None, None]
    row = np.arange(dec)[None, :, None]
    col = np.arange(DEC_BLOCK * HEAD_DIM)[None, None, :]
    row_select = row == blk * DEC_BLOCK + col // HEAD_DIM
    tabs = dict(intra=intra, q_dec=q_dec, k_dec=k_dec, c_dec=c_dec, gamma=gamma,
                cos_q=cos_q, sin_q=sin_q, cos_k=cos_k, sin_k=sin_k,
                dcos_q=dcos_q, dsin_q=dsin_q, dcos_k=dcos_k, dsin_k=dsin_k)
    tabs = {k: jnp.asarray(np.ascontiguousarray(v), dtype=F32) for k, v in tabs.items()}
    tabs["row_select"] = jnp.asarray(row_select, dtype=BF16)
    return tabs


def _block_diag(blocks):
    n, r, c = blocks.shape
    eye = jnp.eye(n, dtype=blocks.dtype)
    return jnp.einsum("nrc,nm->nrmc", blocks, eye).reshape(n * r, n * c)


def kernel(x_prompt, x_sample, state_ret, state_lru_h, state_lru_conv, state_s5_re, state_s5_im,
           c_prompt, c_sample, norm_g, w_ada, b_ada, w_in, ret_gn_g, conv_w, conv_b, w_rg, b_rg,
           w_ig, b_ig, lru_lambda, s5_a_re, s5_a_im, s5_b_re, s5_b_im, s5_c_re, s5_c_im, s5_d,
           s5_log_dt, s5_w_glu, w_out, final_g):
    depth = w_in.shape[0]
    bsz, seq, d = x_prompt.shape
    dec = x_sample.shape[0]
    assert x_sample.shape[1] == 1 and seq % RET_CHUNK == 0 and dec % DEC_BLOCK == 0

    tabs = _tables(seq, dec)
    mod_all = _ada(jnp.concatenate([c_prompt, c_sample], axis=0), w_ada, b_ada)
    n_pow = min(SEQ_TILE, seq) // SUBLANES
    apr, api, bbr, bbi = _s5_prep(s5_a_re, s5_a_im, s5_log_dt, s5_b_re, s5_b_im, n_pow)

    bdiag = jax.vmap(_block_diag)
    bd = jnp.concatenate([bdiag(bbr), bdiag(bbi)], axis=2)
    cd = jnp.concatenate([bdiag(jnp.swapaxes(s5_c_re, 2, 3)),
                          bdiag(jnp.swapaxes(-s5_c_im, 2, 3))], axis=1)
    w = dict(
        norm_g=norm_g.reshape(depth, 1, d),
        w_in=w_in.astype(BF16),
        w_out=w_out.astype(BF16),
        gn_g=ret_gn_g.reshape(depth, 1, RET_WIDTH),
        conv_w=conv_w,
        conv_b=conv_b.reshape(depth, 1, LRU_WIDTH),
        w_gate=jnp.concatenate([bdiag(w_rg), bdiag(w_ig)], axis=2).astype(BF16),
        b_gate=jnp.concatenate([b_rg, b_ig], axis=1).reshape(depth, 1, 2 * LRU_WIDTH),
        lam=lru_lambda.reshape(depth, 1, LRU_WIDTH),
        bd=bd.astype(BF16),
        cd=cd.astype(BF16),
        apr=apr.reshape(depth, n_pow, S5_FLAT),
        api=api.reshape(depth, n_pow, S5_FLAT),
        s5_d=s5_d.reshape(depth, 1, S5_WIDTH),
        w_glu=s5_w_glu.astype(BF16),
        final_g=final_g.reshape(1, d),
    )

    xp = x_prompt
    outs_p = []
    for l in range(depth):
        mod_p = mod_all[l, :bsz].reshape(bsz, 3, d)
        xp, sret, hl, cv, sre, sim = _prompt_layer(xp, mod_p, w, tabs, l, final=l == depth - 1)
        outs_p.append((sret, hl.reshape(bsz, LRU_WIDTH), cv,
                       sre.reshape(bsz, S5_GROUPS, S5_STATE), sim.reshape(bsz, S5_GROUPS, S5_STATE)))

    states = (state_ret, state_lru_h,
              state_lru_conv.reshape(depth, dec, (CONV_WIDTH - 1) * LRU_WIDTH),
              state_s5_re.reshape(depth, dec, S5_FLAT), state_s5_im.reshape(depth, dec, S5_FLAT))
    xs, sret_s, hl_s, cv_s, sre_s, sim_s = _decode_layers(
        x_sample.reshape(dec, d), mod_all[:, bsz:], states, w, tabs)

    def stk(k):
        return jnp.stack([o[k] for o in outs_p])

    return (xp, xs.reshape(dec, 1, d),
            stk(0), sret_s, stk(1), hl_s,
            stk(2), cv_s.reshape(depth, dec, CONV_WIDTH - 1, LRU_WIDTH),
            stk(3), sre_s.reshape(depth, dec, S5_GROUPS, S5_STATE),
            stk(4), sim_s.reshape(depth, dec, S5_GROUPS, S5_STATE))
```

```python
import functools
import math

import jax
import jax.numpy as jnp
import numpy as np
from jax import lax
from jax.experimental import pallas as pl
from jax.experimental.pallas import tpu as pltpu

RET_HEADS = 4
HEAD_DIM = 128
RET_WIDTH = RET_HEADS * HEAD_DIM
RET_CHUNK = 128
LRU_WIDTH = 256
LRU_C = 8.0
CONV_WIDTH = 4
S5_WIDTH = 256
S5_GROUP = 16
S5_GROUPS = 16
S5_STATE = 64
S5_FLAT = S5_GROUPS * S5_STATE
ROPE_BASE = 10000.0
EPS = 1e-6
PAST_LEN = 16384

OFF_Q = 0
OFF_K = OFF_Q + RET_WIDTH
OFF_V = OFF_K + RET_WIDTH
OFF_GR = OFF_V + RET_WIDTH
OFF_XL = OFF_GR + RET_WIDTH
OFF_GL = OFF_XL + LRU_WIDTH
OFF_U = OFF_GL + LRU_WIDTH
OFF_GS = OFF_U + S5_WIDTH
D_IN = OFF_GS + S5_WIDTH

SUBLANES = 8
LANES = 128
SEQ_TILE = 512
PROJ_CHUNK = 256
DEC_BLOCK = 16
VMEM_LIMIT = 56 * 1024 * 1024

F32 = jnp.float32
BF16 = jnp.bfloat16


def _sigmoid(x):
    return 0.5 * jnp.tanh(0.5 * x) + 0.5


def _silu(x):
    return x * _sigmoid(x)


def _gelu_tanh(x):
    c = math.sqrt(2.0 / math.pi)
    return x * (0.5 * (1.0 + jnp.tanh(c * (x + 0.044715 * (x * x * x)))))


def _softplus(x):
    return jnp.maximum(x, 0.0) + jnp.log1p(jnp.exp(-jnp.abs(x)))


def _rmsnorm(x, g):
    ms = jnp.mean(x * x, axis=-1, keepdims=True)
    return x * lax.rsqrt(ms + EPS) * g


def _dot(a, b):
    return jnp.dot(a.astype(BF16), b.astype(BF16), preferred_element_type=F32)


def _dot_nt(a, b):
    return lax.dot_general(a.astype(BF16), b.astype(BF16), (((1,), (1,)), ((), ())),
                           preferred_element_type=F32)


def _rotary(x, cosf, sinf):
    return x * cosf + pltpu.roll(x, HEAD_DIM // 2, axis=1) * sinf


def _groupnorm(o, g):
    mu = jnp.mean(o, axis=-1, keepdims=True)
    d = o - mu
    var = jnp.mean(d * d, axis=-1, keepdims=True)
    return d * lax.rsqrt(var + EPS) * g


def _lru_coeffs(xc, gates, sp):
    r = _sigmoid(gates[:, :LRU_WIDTH])
    ig = _sigmoid(gates[:, LRU_WIDTH:])
    log_a = (-LRU_C) * r * sp
    a = jnp.exp(log_a)
    th = jnp.tanh(log_a)
    mult = jnp.sqrt(-2.0 * th / (1.0 - th))
    return a, mult * ig * xc


def _s5_tail(sr_si, u, gs, cd, s5d, wglu, row_parts=1):
    rows = sr_si.shape[0] // row_parts
    ys = jnp.concatenate([_dot(sr_si[p * rows:(p + 1) * rows], cd) for p in range(row_parts)],
                         axis=0) + s5d * u
    ys = _gelu_tanh(ys)
    ys = ys * _sigmoid(_dot(ys, wglu))
    return ys * _silu(gs)


def _s5_prep_kernel(are_ref, aim_ref, ldt_ref, bre_ref, bim_ref,
                    apr_ref, api_ref, bbr_ref, bbi_ref):
    depth = are_ref.shape[0]
    for l in range(depth):
        a_re = are_ref[l]
        a_im = aim_ref[l]
        step = jnp.exp(ldt_ref[l])
        mag = jnp.exp(step * a_re)
        abar_r = mag * jnp.cos(step * a_im)
        abar_i = mag * jnp.sin(step * a_im)
        nr, ni = abar_r - 1.0, abar_i
        den = a_re * a_re + a_im * a_im
        fr = (nr * a_re + ni * a_im) / den
        fi = (ni * a_re - nr * a_im) / den
        for g in range(S5_GROUPS):
            frg, fig = fr[g:g + 1, :], fi[g:g + 1, :]
            b_re, b_im = bre_ref[l, g], bim_ref[l, g]
            bbr_ref[l, g] = frg * b_re - fig * b_im
            bbi_ref[l, g] = frg * b_im + fig * b_re
        pr, pi = abar_r, abar_i
        apr_ref[l, 0] = pr
        api_ref[l, 0] = pi
        for m in range(1, apr_ref.shape[1]):
            pr, pi = pr * abar_r - pi * abar_i, pr * abar_i + pi * abar_r
            apr_ref[l, m] = pr
            api_ref[l, m] = pi


def _s5_prep(s5_a_re, s5_a_im, s5_log_dt, s5_b_re, s5_b_im, n_pow):
    depth = s5_a_re.shape[0]
    b_re_t = jnp.swapaxes(s5_b_re, 2, 3)
    b_im_t = jnp.swapaxes(s5_b_im, 2, 3)
    ldt = s5_log_dt.reshape(depth, S5_GROUPS, 1)
    out_shape = (
        jax.ShapeDtypeStruct((depth, n_pow, S5_GROUPS, S5_STATE), F32),
        jax.ShapeDtypeStruct((depth, n_pow, S5_GROUPS, S5_STATE), F32),
        jax.ShapeDtypeStruct((depth, S5_GROUPS, S5_GROUP, S5_STATE), F32),
        jax.ShapeDtypeStruct((depth, S5_GROUPS, S5_GROUP, S5_STATE), F32),
    )
    return pl.pallas_call(_s5_prep_kernel, out_shape=out_shape, name="s5_prep")(
        s5_a_re, s5_a_im, ldt, b_re_t, b_im_t)


def _ada_kernel(c_ref, w_ref, b_ref, o_ref):
    s = _silu(c_ref[...])
    o_ref[0] = _dot(s, w_ref[0]) + b_ref[0]


def _ada(c_all, w_ada, b_ada):
    depth, d, n3 = w_ada.shape
    rows = c_all.shape[0]
    tn = d
    return pl.pallas_call(
        _ada_kernel,
        grid=(depth, n3 // tn),
        in_specs=[
            pl.BlockSpec((rows, d), lambda l, j: (0, 0)),
            pl.BlockSpec((1, d, tn), lambda l, j: (l, 0, j)),
            pl.BlockSpec((1, 1, tn), lambda l, j: (l, 0, j)),
        ],
        out_specs=pl.BlockSpec((1, rows, tn), lambda l, j: (l, 0, j)),
        out_shape=jax.ShapeDtypeStruct((depth, rows, n3), F32),
        compiler_params=pltpu.CompilerParams(
            dimension_semantics=("arbitrary", "arbitrary"), vmem_limit_bytes=VMEM_LIMIT),
        name="adaln",
    )(c_all, w_ada, b_ada.reshape(depth, 1, n3))


def _prompt_kernel(x_ref, mod_ref, ng_ref, win_ref, wout_ref,
                   cq_ref, sq_ref, ck_ref, sk_ref, intra_ref, qdec_ref, kdec_ref, cdec_ref,
                   gng_ref, cw_ref, cb_ref, wg_ref, bg_ref, lam_ref,
                   bd_ref, cd_ref, apr_ref, api_ref, s5d_ref, wglu_ref, fg_ref,
                   y_ref, sret_ref, hl_ref, cv_ref, sre_ref, sim_ref,
                   proj_scr, ycat_scr, perm_scr, xpad_scr, ab_scr, st_scr, st16_scr, yp_scr, h_scr,
                   ph_scr, hloc_scr, *, final):
    tl = x_ref.shape[1]
    seg = tl // SUBLANES
    t = pl.program_id(1)
    (ng_ref, win_ref, wout_ref, gng_ref, cw_ref, cb_ref, wg_ref, bg_ref, lam_ref,
     bd_ref, cd_ref, apr_ref, api_ref, s5d_ref, wglu_ref) = (
        r.at[0] for r in (ng_ref, win_ref, wout_ref, gng_ref, cw_ref, cb_ref, wg_ref, bg_ref,
                          lam_ref, bd_ref, cd_ref, apr_ref, api_ref, s5d_ref, wglu_ref))

    @pl.when(t == 0)
    def _():
        sret_ref[...] = jnp.zeros_like(sret_ref)
        hl_ref[...] = jnp.zeros_like(hl_ref)
        sre_ref[...] = jnp.zeros_like(sre_ref)
        sim_ref[...] = jnp.zeros_like(sim_ref)
        cv_ref[...] = jnp.zeros_like(cv_ref)

    x = x_ref[0]
    shift, scale = mod_ref[0, 0:1, :], mod_ref[0, 1:2, :]
    h = _rmsnorm(x, ng_ref[...] * (1.0 + scale)) + shift
    h_scr[...] = h.astype(BF16)
    proj_b = jnp.dot(h_scr[...], win_ref[:, pl.ds(OFF_XL, D_IN - OFF_XL)],
                     preferred_element_type=F32)

    def proj_a_chunk(ci):
        cols = pl.ds(ci * PROJ_CHUNK, PROJ_CHUNK)
        res = jnp.dot(h_scr[...], win_ref[:, cols], preferred_element_type=F32)
        for kl in range(PROJ_CHUNK // LANES):
            proj_scr[ci * (PROJ_CHUNK // LANES) + kl] = res[:, kl * LANES:(kl + 1) * LANES]

    for m in range(tl // SUBLANES):
        s, r0 = divmod(m * SUBLANES, seg)
        for kb in range((D_IN - OFF_XL) // LANES):
            perm_scr[kb, pl.ds(r0 * SUBLANES + s, SUBLANES, stride=SUBLANES), :] = (
                proj_b[m * SUBLANES:(m + 1) * SUBLANES, kb * LANES:(kb + 1) * LANES])

    def ret_piece(c, hd):
        rows = pl.ds(c * RET_CHUNK, RET_CHUNK)
        lanes = pl.ds(hd * HEAD_DIM, HEAD_DIM)
        q = _rotary(proj_scr[OFF_Q // LANES + hd, rows, :], cq_ref[rows, :], sq_ref[rows, :])
        k = _rotary(proj_scr[OFF_K // LANES + hd, rows, :], ck_ref[rows, :], sk_ref[rows, :])
        v = proj_scr[OFF_V // LANES + hd, rows, :].astype(BF16)
        s_old = sret_ref[0, hd]
        sc = _dot_nt(q, k) * intra_ref[hd]
        o = _dot(sc, v) + _dot(q * qdec_ref[hd], s_old)
        kd_t = (k * kdec_ref[hd]).T
        sret_ref[0, hd] = cdec_ref[hd] * s_old + _dot(kd_t, v)
        on = _groupnorm(o, gng_ref[:, lanes])
        g_ret = proj_scr[OFF_GR // LANES + hd, rows, :]
        ycat_scr[hd, rows, :] = (on * _silu(g_ret)).astype(BF16)

    sub = lax.broadcasted_iota(jnp.int32, (SUBLANES, 1), 0)
    rg = lambda r: pl.ds(r * SUBLANES, SUBLANES)

    def pcols(lo, width):
        return jnp.concatenate([perm_scr[kb] for kb in range(lo // LANES, (lo + width) // LANES)],
                               axis=1)

    n_chunks = OFF_XL // PROJ_CHUNK
    s5_parts = n_chunks - 3

    proj_a_chunk(0)
    xl = pcols(OFF_XL - OFF_XL, LRU_WIDTH)
    prev = cv_ref[0]
    npre = CONV_WIDTH - 1
    for m in range(1, CONV_WIDTH):
        grp = xl[(seg - m) * SUBLANES:(seg - m + 1) * SUBLANES, :]
        e_m = jnp.where(sub == 0, prev[npre - m:npre - m + 1, :], pltpu.roll(grp, 1, axis=0))
        xpad_scr[rg(npre - m), :] = e_m
    xpad_scr[pl.ds(npre * SUBLANES, tl), :] = xl
    cv_ref[0] = jnp.concatenate(
        [xl[(seg - m) * SUBLANES + SUBLANES - 1:(seg - m + 1) * SUBLANES, :]
         for m in range(npre, 0, -1)], axis=0)
    xc = cb_ref[...]
    for j in range(CONV_WIDTH):
        xc = xc + xpad_scr[pl.ds(j * SUBLANES, tl), :] * cw_ref[pl.ds(j, 1), :]
    gates = _dot(xc, wg_ref[...]) + bg_ref[...]
    a, b = _lru_coeffs(xc, gates, _softplus(-lam_ref[...]))
    ab_scr[0] = a
    ab_scr[1] = b
    proj_a_chunk(1)
    hloc = jnp.zeros((SUBLANES, LRU_WIDTH), F32)
    prod = jnp.ones((SUBLANES, LRU_WIDTH), F32)
    for r in range(seg):
        ar = ab_scr[0, rg(r), :]
        hloc = ar * hloc + ab_scr[1, rg(r), :]
        prod = prod * ar
        ph_scr[0, rg(r), :] = prod
        ph_scr[1, rg(r), :] = hloc
    c = hl_ref[0]
    cs = [c]
    for s in range(SUBLANES - 1):
        c = hloc[s:s + 1, :] + prod[s:s + 1, :] * c
        cs.append(c)
    hl_ref[0] = hloc[SUBLANES - 1:, :] + prod[SUBLANES - 1:, :] * c
    c_all = jnp.concatenate(cs, axis=0)
    sgl = _silu(pcols(OFF_GL - OFF_XL, LRU_WIDTH))
    for r in range(seg):
        y = (ph_scr[1, rg(r), :] + ph_scr[0, rg(r), :] * c_all) * sgl[r * SUBLANES:(r + 1) * SUBLANES, :]
        for kb in range(LRU_WIDTH // LANES):
            yp_scr[kb, rg(r), :] = y[:, kb * LANES:(kb + 1) * LANES]

    proj_a_chunk(2)
    u = pcols(OFF_U - OFF_XL, S5_WIDTH)
    bu = _dot(u, bd_ref[...])
    st_scr[0] = bu[:, :S5_FLAT]
    st_scr[1] = bu[:, S5_FLAT:]
    bshape = (SUBLANES, S5_FLAT)
    a1r = jnp.broadcast_to(apr_ref[pl.ds(0, 1), :], bshape)
    a1i = jnp.broadcast_to(api_ref[pl.ds(0, 1), :], bshape)
    hr = jnp.zeros((SUBLANES, S5_FLAT), F32)
    hi = jnp.zeros((SUBLANES, S5_FLAT), F32)
    for r in range(seg):
        if r % (-(-seg // s5_parts)) == 0:
            proj_a_chunk(3 + r // (-(-seg // s5_parts)))
        hr, hi = (a1r * hr - a1i * hi + st_scr[0, rg(r), :],
                  a1r * hi + a1i * hr + st_scr[1, rg(r), :])
        hloc_scr[0, rg(r), :] = hr
        hloc_scr[1, rg(r), :] = hi
    asr, asi = apr_ref[pl.ds(seg - 1, 1), :], api_ref[pl.ds(seg - 1, 1), :]
    cr, ci = sre_ref[0], sim_ref[0]
    crs, cis = [cr], [ci]
    for s in range(SUBLANES - 1):
        cr, ci = (hr[s:s + 1, :] + (asr * cr - asi * ci), hi[s:s + 1, :] + (asr * ci + asi * cr))
        crs.append(cr)
        cis.append(ci)
    sre_ref[0] = hr[SUBLANES - 1:, :] + (asr * cr - asi * ci)
    sim_ref[0] = hi[SUBLANES - 1:, :] + (asr * ci + asi * cr)
    cr_all, ci_all = jnp.concatenate(crs, axis=0), jnp.concatenate(cis, axis=0)
    n_pieces = (tl // RET_CHUNK) * RET_HEADS
    per_piece = (seg // 2) // n_pieces
    for r2 in range(seg // 2):
        if r2 % per_piece == 0:
            ret_piece(*divmod(r2 // per_piece, RET_HEADS))
        parts_r, parts_i = [], []
        for r in (2 * r2, 2 * r2 + 1):
            qr = jnp.broadcast_to(apr_ref[pl.ds(r, 1), :], bshape)
            qi = jnp.broadcast_to(api_ref[pl.ds(r, 1), :], bshape)
            parts_r.append(hloc_scr[0, rg(r), :] + (qr * cr_all - qi * ci_all))
            parts_i.append(hloc_scr[1, rg(r), :] + (qr * ci_all + qi * cr_all))
        rows2 = pl.ds(r2 * 2 * SUBLANES, 2 * SUBLANES)
        st16_scr[0, rows2, :] = jnp.concatenate(parts_r, axis=0).astype(BF16)
        st16_scr[1, rows2, :] = jnp.concatenate(parts_i, axis=0).astype(BF16)
    gs = pcols(OFF_GS - OFF_XL, S5_WIDTH)
    y_s5 = _s5_tail(jnp.concatenate([st16_scr[0], st16_scr[1]], axis=1), u, gs,
                    cd_ref[...], s5d_ref[...], wglu_ref[...], row_parts=2)
    for kb in range(S5_WIDTH // LANES):
        yp_scr[LRU_WIDTH // LANES + kb] = y_s5[:, kb * LANES:(kb + 1) * LANES]

    for m2 in range(tl // (2 * SUBLANES)):
        for kb in range((LRU_WIDTH + S5_WIDTH) // LANES):
            halves = []
            for m in (2 * m2, 2 * m2 + 1):
                s, r0 = divmod(m * SUBLANES, seg)
                halves.append(yp_scr[kb, pl.ds(r0 * SUBLANES + s, SUBLANES, stride=SUBLANES), :])
            ycat_scr[RET_WIDTH // LANES + kb, pl.ds(m2 * 2 * SUBLANES, 2 * SUBLANES), :] = (
                jnp.concatenate(halves, axis=0).astype(BF16))

    ycat = jnp.concatenate([ycat_scr[kb] for kb in range(ycat_scr.shape[0])], axis=1)
    out = jnp.dot(ycat, wout_ref[...], preferred_element_type=F32)
    x_new = x_ref[0] + mod_ref[0, 2:3, :] * out
    if final:
        x_new = _rmsnorm(x_new, fg_ref[...])
    y_ref[0] = x_new


def _prompt_layer(x, mod, w, tabs, layer, *, final):
    bsz, seq, d = x.shape
    tl = min(SEQ_TILE, seq)
    nt = seq // tl

    def full(a):
        return pl.BlockSpec(a.shape, lambda b, t: (0,) * a.ndim)

    def lay(a):
        return pl.BlockSpec((1,) + a.shape[1:], lambda b, t: (layer,) + (0,) * (a.ndim - 1))

    row_tab = pl.BlockSpec((tl, HEAD_DIM), lambda b, t: (t, 0))
    in_specs = [
        pl.BlockSpec((1, tl, d), lambda b, t: (b, t, 0)),
        pl.BlockSpec((1, 3, d), lambda b, t: (b, 0, 0)),
        lay(w["norm_g"]), lay(w["w_in"]), lay(w["w_out"]),
        row_tab, row_tab, row_tab, row_tab,
        full(tabs["intra"]), full(tabs["q_dec"]), full(tabs["k_dec"]),
        pl.BlockSpec(memory_space=pltpu.SMEM),
        lay(w["gn_g"]), lay(w["conv_w"]), lay(w["conv_b"]), lay(w["w_gate"]), lay(w["b_gate"]),
        lay(w["lam"]), lay(w["bd"]), lay(w["cd"]), lay(w["apr"]), lay(w["api"]),
        lay(w["s5_d"]), lay(w["w_glu"]), full(w["final_g"]),
    ]
    out_shape = (
        jax.ShapeDtypeStruct((bsz, seq, d), F32),
        jax.ShapeDtypeStruct((bsz, RET_HEADS, HEAD_DIM, HEAD_DIM), F32),
        jax.ShapeDtypeStruct((bsz, 1, LRU_WIDTH), F32),
        jax.ShapeDtypeStruct((bsz, CONV_WIDTH - 1, LRU_WIDTH), F32),
        jax.ShapeDtypeStruct((bsz, 1, S5_FLAT), F32),
        jax.ShapeDtypeStruct((bsz, 1, S5_FLAT), F32),
    )
    out_specs = (
        pl.BlockSpec((1, tl, d), lambda b, t: (b, t, 0)),
        pl.BlockSpec((1, RET_HEADS, HEAD_DIM, HEAD_DIM), lambda b, t: (b, 0, 0, 0)),
        pl.BlockSpec((1, 1, LRU_WIDTH), lambda b, t: (b, 0, 0)),
        pl.BlockSpec((1, CONV_WIDTH - 1, LRU_WIDTH), lambda b, t: (b, 0, 0)),
        pl.BlockSpec((1, 1, S5_FLAT), lambda b, t: (b, 0, 0)),
        pl.BlockSpec((1, 1, S5_FLAT), lambda b, t: (b, 0, 0)),
    )
    scratch = [
        pltpu.VMEM((OFF_XL // LANES, tl, LANES), F32),
        pltpu.VMEM((d // LANES, tl, LANES), BF16),
        pltpu.VMEM(((D_IN - OFF_XL) // LANES, tl, LANES), F32),
        pltpu.VMEM((tl + (CONV_WIDTH - 1) * SUBLANES, LRU_WIDTH), F32),
        pltpu.VMEM((2, tl, LRU_WIDTH), F32),
        pltpu.VMEM((2, tl, S5_FLAT), F32),
        pltpu.VMEM((2, tl, S5_FLAT), BF16),
        pltpu.VMEM(((LRU_WIDTH + S5_WIDTH) // LANES, tl, LANES), F32),
        pltpu.VMEM((tl, d), BF16),
        pltpu.VMEM((2, tl, LRU_WIDTH), F32),
        pltpu.VMEM((2, tl, S5_FLAT), F32),
    ]
    return pl.pallas_call(
        functools.partial(_prompt_kernel, final=final),
        grid=(bsz, nt),
        in_specs=in_specs,
        out_specs=out_specs,
        out_shape=out_shape,
        scratch_shapes=scratch,
        compiler_params=pltpu.CompilerParams(
            dimension_semantics=("arbitrary", "arbitrary"), vmem_limit_bytes=VMEM_LIMIT),
        name="prompt_layer",
    )(x, mod, w["norm_g"], w["w_in"], w["w_out"],
      tabs["cos_q"], tabs["sin_q"], tabs["cos_k"], tabs["sin_k"],
      tabs["intra"], tabs["q_dec"], tabs["k_dec"], tabs["c_dec"],
      w["gn_g"], w["conv_w"], w["conv_b"], w["w_gate"], w["b_gate"], w["lam"],
      w["bd"], w["cd"], w["apr"], w["api"], w["s5_d"], w["w_glu"], w["final_g"])


def _decode_kernel(x_ref, mod_ref, ng_ref, win_ref, wout_ref,
                   cq_ref, sq_ref, ck_ref, sk_ref, gam_ref,
                   gng_ref, cw_ref, cb_ref, wg_ref, bg_ref, lam_ref,
                   bd_ref, cd_ref, apr_ref, api_ref, s5d_ref, wglu_ref, fg_ref,
                   sret_in, hl_in, cv_in, sre_in, sim_in, sel_ref,
                   y_ref, sret_ref, hl_ref, cv_ref, sre_ref, sim_ref,
                   qkt_scr, v_scr, o_scr, gr_scr, ycat_scr, x_scr):
    layer = pl.program_id(0)
    i = pl.program_id(1)
    nb = pl.num_programs(1)
    rows_n, d = x_ref.shape
    (mod_ref, ng_ref, win_ref, wout_ref, gng_ref, cw_ref, cb_ref, wg_ref, bg_ref, lam_ref,
     bd_ref, cd_ref, apr_ref, api_ref, s5d_ref, wglu_ref,
     sret_in, hl_in, cv_in, sre_in, sim_in, sret_ref, hl_ref, cv_ref, sre_ref, sim_ref) = (
        r.at[0] for r in (mod_ref, ng_ref, win_ref, wout_ref, gng_ref, cw_ref, cb_ref, wg_ref,
                          bg_ref, lam_ref, bd_ref, cd_ref, apr_ref, api_ref, s5d_ref, wglu_ref,
                          sret_in, hl_in, cv_in, sre_in, sim_in,
                          sret_ref, hl_ref, cv_ref, sre_ref, sim_ref))

    @pl.when((i == 0) & (layer == 0))
    def _():
        x_scr[...] = x_ref[...]

    @pl.when(i == 0)
    def _():
        x = x_scr[...]
        shift, scale = mod_ref[:, pl.ds(0, d)], mod_ref[:, pl.ds(d, d)]
        h = _rmsnorm(x, ng_ref[...]) * (1.0 + scale) + shift
        proj = _dot(h, win_ref[...])
        cq, sq, ck, sk = cq_ref[...], sq_ref[...], ck_ref[...], sk_ref[...]
        for hd in range(RET_HEADS):
            lo = hd * HEAD_DIM
            for part, src in enumerate((_rotary(proj[:, OFF_Q + lo:OFF_Q + lo + HEAD_DIM], cq, sq),
                                        _rotary(proj[:, OFF_K + lo:OFF_K + lo + HEAD_DIM], ck, sk))):
                xt = src.T
                hi = xt.astype(BF16)
                trows = pl.ds((2 * hd + part) * HEAD_DIM, HEAD_DIM)
                qkt_scr[trows, pl.ds(0, rows_n)] = hi
                qkt_scr[trows, pl.ds(rows_n, rows_n)] = (xt - hi.astype(F32)).astype(BF16)
        v_scr[...] = proj[:, OFF_V:OFF_V + RET_WIDTH]
        gr_scr[...] = _silu(proj[:, OFF_GR:OFF_GR + RET_WIDTH])
        xl = proj[:, OFF_XL:OFF_XL + LRU_WIDTH]
        cs = cv_in[...]
        xc = cb_ref[...] + xl * cw_ref[pl.ds(CONV_WIDTH - 1, 1), :]
        for j in range(CONV_WIDTH - 1):
            xc = xc + cs[:, j * LRU_WIDTH:(j + 1) * LRU_WIDTH] * cw_ref[pl.ds(j, 1), :]
        cv_ref[:, pl.ds(0, 2 * LRU_WIDTH)] = cs[:, LRU_WIDTH:]
        cv_ref[:, pl.ds(2 * LRU_WIDTH, LRU_WIDTH)] = xl
        gates = _dot(xc, wg_ref[...]) + bg_ref[...]
        a, b = _lru_coeffs(xc, gates, _softplus(-lam_ref[...]))
        hh = b + a * hl_in[...]
        hl_ref[...] = hh
        ycat_scr[:, pl.ds(RET_WIDTH, LRU_WIDTH)] = (
            hh * _silu(proj[:, OFF_GL:OFF_GL + LRU_WIDTH])).astype(BF16)
        u = proj[:, OFF_U:OFF_U + S5_WIDTH]
        bu = _dot(u, bd_ref[...])
        ar, ai = apr_ref[pl.ds(0, 1), :], api_ref[pl.ds(0, 1), :]
        s0r, s0i = sre_in[...], sim_in[...]
        sr = bu[:, :S5_FLAT] + (ar * s0r - ai * s0i)
        si = bu[:, S5_FLAT:] + (ar * s0i + ai * s0r)
        sre_ref[...] = sr
        sim_ref[...] = si
        y_s5 = _s5_tail(jnp.concatenate([sr, si], axis=-1), u, proj[:, OFF_GS:OFF_GS + S5_WIDTH],
                        cd_ref[...], s5d_ref[...], wglu_ref[...])
        ycat_scr[:, pl.ds(RET_WIDTH + LRU_WIDTH, S5_WIDTH)] = y_s5.astype(BF16)

    blk_rows = pl.ds(pl.multiple_of(i * DEC_BLOCK, DEC_BLOCK), DEC_BLOCK)
    o_rows = [[] for _ in range(RET_HEADS)]
    vblks = [v_scr[blk_rows, pl.ds(hd * HEAD_DIM, HEAD_DIM)] for hd in range(RET_HEADS)]
    for jp in range(DEC_BLOCK // 2):
        sel = sel_ref[0, :, pl.ds(jp * 2 * HEAD_DIM, 2 * HEAD_DIM)]
        tiles = jnp.dot(qkt_scr[...], jnp.concatenate([sel, sel], axis=0),
                        preferred_element_type=F32)
        for hd in range(RET_HEADS):
            qb = tiles[2 * hd * HEAD_DIM:(2 * hd + 1) * HEAD_DIM, :]
            kb = tiles[(2 * hd + 1) * HEAD_DIM:(2 * hd + 2) * HEAD_DIM, :]
            for j in (2 * jp, 2 * jp + 1):
                cols = slice((j % 2) * HEAD_DIM, (j % 2 + 1) * HEAD_DIM)
                s_new = gam_ref[hd] * sret_in[j, hd] + kb[:, cols] * vblks[hd][j:j + 1, :]
                sret_ref[j, hd] = s_new
                o_rows[hd].append(jnp.sum(qb[:, cols] * s_new, axis=0, keepdims=True))
    for hd in range(RET_HEADS):
        o_scr[blk_rows, pl.ds(hd * HEAD_DIM, HEAD_DIM)] = jnp.concatenate(o_rows[hd], axis=0)

    @pl.when(i == nb - 1)
    def _():
        for hd in range(RET_HEADS):
            lanes = pl.ds(hd * HEAD_DIM, HEAD_DIM)
            on = _groupnorm(o_scr[:, lanes], gng_ref[:, lanes])
            ycat_scr[:, lanes] = (on * gr_scr[:, lanes]).astype(BF16)
        out = jnp.dot(ycat_scr[...], wout_ref[...], preferred_element_type=F32)
        x_new = x_scr[...] + mod_ref[:, pl.ds(2 * d, d)] * out
        x_scr[...] = x_new
        y_ref[...] = _rmsnorm(x_new, fg_ref[...])


def _decode_layers(x, mod, states, w, tabs):
    rows, d = x.shape
    depth = mod.shape[0]
    s_ret, s_h, s_conv, s_re, s_im = states
    nb = rows // DEC_BLOCK

    def full(a):
        return pl.BlockSpec(a.shape, lambda l, i: (0,) * a.ndim)

    def lay(a):
        return pl.BlockSpec((1,) + a.shape[1:], lambda l, i: (l,) + (0,) * (a.ndim - 1))

    smem = pl.BlockSpec(memory_space=pltpu.SMEM)
    sblk = pl.BlockSpec((1, DEC_BLOCK, RET_HEADS, HEAD_DIM, HEAD_DIM), lambda l, i: (l, i, 0, 0, 0))
    in_specs = [
        full(x), lay(mod), lay(w["norm_g"]), lay(w["w_in"]), lay(w["w_out"]),
        full(tabs["dcos_q"]), full(tabs["dsin_q"]), full(tabs["dcos_k"]), full(tabs["dsin_k"]), smem,
        lay(w["gn_g"]), lay(w["conv_w"]), lay(w["conv_b"]), lay(w["w_gate"]), lay(w["b_gate"]),
        lay(w["lam"]), lay(w["bd"]), lay(w["cd"]), lay(w["apr"]), lay(w["api"]),
        lay(w["s5_d"]), lay(w["w_glu"]), full(w["final_g"]),
        sblk, lay(s_h), lay(s_conv), lay(s_re), lay(s_im),
        pl.BlockSpec((1,) + tabs["row_select"].shape[1:], lambda l, i: (i, 0, 0)),
    ]
    out_shape = (
        jax.ShapeDtypeStruct((rows, d), F32),
        jax.ShapeDtypeStruct(s_ret.shape, F32),
        jax.ShapeDtypeStruct(s_h.shape, F32),
        jax.ShapeDtypeStruct(s_conv.shape, F32),
        jax.ShapeDtypeStruct(s_re.shape, F32),
        jax.ShapeDtypeStruct(s_im.shape, F32),
    )
    out_specs = (full(x), sblk, lay(s_h), lay(s_conv), lay(s_re), lay(s_im))
    scratch = [
        pltpu.VMEM((RET_HEADS * 2 * HEAD_DIM, 2 * rows), BF16),
        pltpu.VMEM((rows, RET_WIDTH), F32),
        pltpu.VMEM((rows, RET_WIDTH), F32),
        pltpu.VMEM((rows, RET_WIDTH), F32),
        pltpu.VMEM((rows, d), BF16),
        pltpu.VMEM((rows, d), F32),
    ]
    return pl.pallas_call(
        _decode_kernel,
        grid=(depth, nb),
        in_specs=in_specs,
        out_specs=out_specs,
        out_shape=out_shape,
        scratch_shapes=scratch,
        compiler_params=pltpu.CompilerParams(
            dimension_semantics=("arbitrary", "arbitrary"), vmem_limit_bytes=VMEM_LIMIT),
        name="decode_layers",
    )(x, mod, w["norm_g"], w["w_in"], w["w_out"],
      tabs["dcos_q"], tabs["dsin_q"], tabs["dcos_k"], tabs["dsin_k"], tabs["gamma"],
      w["gn_g"], w["conv_w"], w["conv_b"], w["w_gate"], w["b_gate"], w["lam"],
      w["bd"], w["cd"], w["apr"], w["api"], w["s5_d"], w["w_glu"], w["final_g"],
      s_ret, s_h, s_conv, s_re, s_im, tabs["row_select"])


def _rope_tables(pos, scale):
    half = HEAD_DIM // 2
    inv = ROPE_BASE ** (-np.arange(half, dtype=np.float64) / half)
    ang = pos[:, None] * inv[None, :]
    cos, sin = np.cos(ang), np.sin(ang)
    cosf = np.concatenate([cos, cos], axis=-1) * scale
    sinf = np.concatenate([-sin, sin], axis=-1) * scale
    return cosf, sinf


def _tables(seq, dec):
    c = RET_CHUNK
    log_g = np.log1p(-np.exp2(-5.0 - np.arange(RET_HEADS, dtype=np.float64)))
    idx = np.arange(c, dtype=np.float64)
    rel = idx[:, None] - idx[None, :]
    intra = np.where(rel[None] >= 0, np.exp(np.maximum(rel, 0.0)[None] * log_g[:, None, None]), 0.0)
    q_dec = np.exp((idx + 1.0)[:, None] * log_g[None, :])
    k_dec = np.exp((c - 1.0 - idx)[:, None] * log_g[None, :])
    q_dec = np.broadcast_to(q_dec.T[:, :, None], (RET_HEADS, c, HEAD_DIM))
    k_dec = np.broadcast_to(k_dec.T[:, :, None], (RET_HEADS, c, HEAD_DIM))
    c_dec = np.exp(c * log_g)
    gamma = np.exp(log_g)
    k_scale = HEAD_DIM ** -0.5
    cos_q, sin_q = _rope_tables(np.arange(seq, dtype=np.float64), 1.0)
    cos_k, sin_k = _rope_tables(np.arange(seq, dtype=np.float64), k_scale)
    dpos = PAST_LEN + np.arange(1, dtype=np.float64)
    dcos_q, dsin_q = _rope_tables(dpos, 1.0)
    dcos_k, dsin_k = _rope_tables(dpos, k_scale)
    blk = np.arange(dec // DEC_BLOCK)[:, None, None]
    row = np.arange(dec)[None, :, None]
    col = np.arange(DEC_BLOCK * HEAD_DIM)[None, None, :]
    row_select = row == blk * DEC_BLOCK + col // HEAD_DIM
    tabs = dict(intra=intra, q_dec=q_dec, k_dec=k_dec, c_dec=c_dec, gamma=gamma,
                cos_q=cos_q, sin_q=sin_q, cos_k=cos_k, sin_k=sin_k,
                dcos_q=dcos_q, dsin_q=dsin_q, dcos_k=dcos_k, dsin_k=dsin_k)
    tabs = {k: jnp.asarray(np.ascontiguousarray(v), dtype=F32) for k, v in tabs.items()}
    tabs["row_select"] = jnp.asarray(row_select, dtype=BF16)
    return tabs


def _block_diag(blocks):
    n, r, c = blocks.shape
    eye = jnp.eye(n, dtype=blocks.dtype)
    return jnp.einsum("nrc,nm->nrmc", blocks, eye).reshape(n * r, n * c)


def kernel(x_prompt, x_sample, state_ret, state_lru_h, state_lru_conv, state_s5_re, state_s5_im,
           c_prompt, c_sample, norm_g, w_ada, b_ada, w_in, ret_gn_g, conv_w, conv_b, w_rg, b_rg,
           w_ig, b_ig, lru_lambda, s5_a_re, s5_a_im, s5_b_re, s5_b_im, s5_c_re, s5_c_im, s5_d,
           s5_log_dt, s5_w_glu, w_out, final_g):
    depth = w_in.shape[0]
    bsz, seq, d = x_prompt.shape
    dec = x_sample.shape[0]
    assert x_sample.shape[1] == 1 and seq % RET_CHUNK == 0 and dec % DEC_BLOCK == 0

    tabs = _tables(seq, dec)
    mod_all = _ada(jnp.concatenate([c_prompt, c_sample], axis=0), w_ada, b_ada)
    n_pow = min(SEQ_TILE, seq) // SUBLANES
    apr, api, bbr, bbi = _s5_prep(s5_a_re, s5_a_im, s5_log_dt, s5_b_re, s5_b_im, n_pow)

    bdiag = jax.vmap(_block_diag)
    bd = jnp.concatenate([bdiag(bbr), bdiag(bbi)], axis=2)
    cd = jnp.concatenate([bdiag(jnp.swapaxes(s5_c_re, 2, 3)),
                          bdiag(jnp.swapaxes(-s5_c_im, 2, 3))], axis=1)
    w = dict(
        norm_g=norm_g.reshape(depth, 1, d),
        w_in=w_in.astype(BF16),
        w_out=w_out.astype(BF16),
        gn_g=ret_gn_g.reshape(depth, 1, RET_WIDTH),
        conv_w=conv_w,
        conv_b=conv_b.reshape(depth, 1, LRU_WIDTH),
        w_gate=jnp.concatenate([bdiag(w_rg), bdiag(w_ig)], axis=2).astype(BF16),
        b_gate=jnp.concatenate([b_rg, b_ig], axis=1).reshape(depth, 1, 2 * LRU_WIDTH),
        lam=lru_lambda.reshape(depth, 1, LRU_WIDTH),
        bd=bd.astype(BF16),
        cd=cd.astype(BF16),
        apr=apr.reshape(depth, n_pow, S5_FLAT),
        api=api.reshape(depth, n_pow, S5_FLAT),
        s5_d=s5_d.reshape(depth, 1, S5_WIDTH),
        w_glu=s5_w_glu.astype(BF16),
        final_g=final_g.reshape(1, d),
    )

    xp = x_prompt
    outs_p = []
    for l in range(depth):
        mod_p = mod_all[l, :bsz].reshape(bsz, 3, d)
        xp, sret, hl, cv, sre, sim = _prompt_layer(xp, mod_p, w, tabs, l, final=l == depth - 1)
        outs_p.append((sret, hl.reshape(bsz, LRU_WIDTH), cv,
                       sre.reshape(bsz, S5_GROUPS, S5_STATE), sim.reshape(bsz, S5_GROUPS, S5_STATE)))

    states = (state_ret, state_lru_h,
              state_lru_conv.reshape(depth, dec, (CONV_WIDTH - 1) * LRU_WIDTH),
              state_s5_re.reshape(depth, dec, S5_FLAT), state_s5_im.reshape(depth, dec, S5_FLAT))
    xs, sret_s, hl_s, cv_s, sre_s, sim_s = _decode_layers(
        x_sample.reshape(dec, d), mod_all[:, bsz:], states, w, tabs)

    def stk(k):
        return jnp.stack([o[k] for o in outs_p])

    return (xp, xs.reshape(dec, 1, d),
            stk(0), sret_s, stk(1), hl_s,
            stk(2), cv_s.reshape(depth, dec, CONV_WIDTH - 1, LRU_WIDTH),
            stk(3), sre_s.reshape(depth, dec, S5_GROUPS, S5_STATE),
            stk(4), sim_s.reshape(depth, dec, S5_GROUPS, S5_STATE))
```

```python
import functools
import math

import jax
import jax.numpy as jnp
import numpy as np
from jax import lax
from jax.experimental import pallas as pl
from jax.experimental.pallas import tpu as pltpu

RET_HEADS = 4
HEAD_DIM = 128
RET_WIDTH = RET_HEADS * HEAD_DIM
LRU_WIDTH = 256
LRU_C = 8.0
CONV_WIDTH = 4
S5_WIDTH = 256
S5_GROUP = 16
S5_GROUPS = 16
S5_STATE = 64
S5_FLAT = S5_GROUPS * S5_STATE
ROPE_BASE = 10000.0
EPS = 1e-6
PAST_LEN = 16384

OFF_Q = 0
OFF_K = OFF_Q + RET_WIDTH
OFF_V = OFF_K + RET_WIDTH
OFF_GR = OFF_V + RET_WIDTH
OFF_XL = OFF_GR + RET_WIDTH
OFF_GL = OFF_XL + LRU_WIDTH
OFF_U = OFF_GL + LRU_WIDTH
OFF_GS = OFF_U + S5_WIDTH
D_IN = OFF_GS + S5_WIDTH

SUBLANES = 8
LANES = 128
PROMPT_T = 64
PROJ_CHUNK = 256
DEC_BLOCK = 16
VMEM_LIMIT = 56 * 1024 * 1024

F32 = jnp.float32
BF16 = jnp.bfloat16


def _sigmoid(x):
    return 1.0 / (1.0 + jnp.exp(-x))


def _silu(x):
    return x * _sigmoid(x)


def _gelu_tanh(x):
    c = math.sqrt(2.0 / math.pi)
    return x * (0.5 * (1.0 + jnp.tanh(c * (x + 0.044715 * (x * x * x)))))


def _softplus(x):
    return jnp.maximum(x, 0.0) + jnp.log1p(jnp.exp(-jnp.abs(x)))


def _rmsnorm(x, g):
    ms = jnp.mean(x * x, axis=-1, keepdims=True)
    return x * lax.rsqrt(ms + EPS) * g


def _dot(a, b):
    return jnp.dot(a.astype(BF16), b.astype(BF16), preferred_element_type=F32)


def _dot_nt(a, b):
    return lax.dot_general(a.astype(BF16), b.astype(BF16), (((1,), (1,)), ((), ())),
                           preferred_element_type=F32)


def _rotary(x, cosf, sinf):
    return x * cosf + pltpu.roll(x, HEAD_DIM // 2, axis=1) * sinf


def _groupnorm(o, g):
    mu = jnp.mean(o, axis=-1, keepdims=True)
    d = o - mu
    var = jnp.mean(d * d, axis=-1, keepdims=True)
    return d * lax.rsqrt(var + EPS) * g


def _lru_coeffs(xc, gates, sp):
    r = _sigmoid(gates[:, :LRU_WIDTH])
    ig = _sigmoid(gates[:, LRU_WIDTH:])
    log_a = (-LRU_C) * r * sp
    a = jnp.exp(log_a)
    th = jnp.tanh(log_a)
    mult = jnp.sqrt(-2.0 * th / (1.0 - th))
    return a, mult * ig * xc


def _s5_tail(sr_si, u, gs, cd, s5d, wglu, row_parts=1):
    rows = sr_si.shape[0] // row_parts
    ys = jnp.concatenate([_dot(sr_si[p * rows:(p + 1) * rows], cd) for p in range(row_parts)],
                         axis=0) + s5d * u
    ys = _gelu_tanh(ys)
    ys = ys * _sigmoid(_dot(ys, wglu))
    return ys * _silu(gs)


def _s5_prep_kernel(are_ref, aim_ref, ldt_ref, bre_ref, bim_ref,
                    apr_ref, api_ref, bbr_ref, bbi_ref):
    depth = are_ref.shape[0]
    for l in range(depth):
        a_re = are_ref[l]
        a_im = aim_ref[l]
        step = jnp.exp(ldt_ref[l])
        mag = jnp.exp(step * a_re)
        abar_r = mag * jnp.cos(step * a_im)
        abar_i = mag * jnp.sin(step * a_im)
        nr, ni = abar_r - 1.0, abar_i
        den = a_re * a_re + a_im * a_im
        fr = (nr * a_re + ni * a_im) / den
        fi = (ni * a_re - nr * a_im) / den
        for g in range(S5_GROUPS):
            frg, fig = fr[g:g + 1, :], fi[g:g + 1, :]
            b_re, b_im = bre_ref[l, g], bim_ref[l, g]
            bbr_ref[l, g] = frg * b_re - fig * b_im
            bbi_ref[l, g] = frg * b_im + fig * b_re
        pr, pi = abar_r, abar_i
        apr_ref[l, 0] = pr
        api_ref[l, 0] = pi
        for m in range(1, apr_ref.shape[1]):
            pr, pi = pr * abar_r - pi * abar_i, pr * abar_i + pi * abar_r
            apr_ref[l, m] = pr
            api_ref[l, m] = pi


def _s5_prep(s5_a_re, s5_a_im, s5_log_dt, s5_b_re, s5_b_im, n_pow):
    depth = s5_a_re.shape[0]
    b_re_t = jnp.swapaxes(s5_b_re, 2, 3)
    b_im_t = jnp.swapaxes(s5_b_im, 2, 3)
    ldt = s5_log_dt.reshape(depth, S5_GROUPS, 1)
    out_shape = (
        jax.ShapeDtypeStruct((depth, n_pow, S5_GROUPS, S5_STATE), F32),
        jax.ShapeDtypeStruct((depth, n_pow, S5_GROUPS, S5_STATE), F32),
        jax.ShapeDtypeStruct((depth, S5_GROUPS, S5_GROUP, S5_STATE), F32),
        jax.ShapeDtypeStruct((depth, S5_GROUPS, S5_GROUP, S5_STATE), F32),
    )
    return pl.pallas_call(_s5_prep_kernel, out_shape=out_shape, name="s5_prep")(
        s5_a_re, s5_a_im, ldt, b_re_t, b_im_t)


def _ada_kernel(c_ref, w_ref, b_ref, o_ref):
    s = _silu(c_ref[...])
    o_ref[0] = _dot(s, w_ref[0]) + b_ref[0]


def _ada(c_all, w_ada, b_ada):
    depth, d, n3 = w_ada.shape
    rows = c_all.shape[0]
    tn = d
    return pl.pallas_call(
        _ada_kernel,
        grid=(depth, n3 // tn),
        in_specs=[
            pl.BlockSpec((rows, d), lambda l, j: (0, 0)),
            pl.BlockSpec((1, d, tn), lambda l, j: (l, 0, j)),
            pl.BlockSpec((1, 1, tn), lambda l, j: (l, 0, j)),
        ],
        out_specs=pl.BlockSpec((1, rows, tn), lambda l, j: (l, 0, j)),
        out_shape=jax.ShapeDtypeStruct((depth, rows, n3), F32),
        compiler_params=pltpu.CompilerParams(
            dimension_semantics=("arbitrary", "arbitrary"), vmem_limit_bytes=VMEM_LIMIT),
        name="adaln",
    )(c_all, w_ada, b_ada.reshape(depth, 1, n3))


def _prompt_kernel_bm(x_ref, mod_ref, ng_ref, win_ref, wout_ref,
                      cq_ref, sq_ref, ck_ref, sk_ref, intra_ref, qdec_ref, kdec_ref, cdec_ref,
                      gng_ref, cw_ref, cb_ref, wg_ref, bg_ref, lam_ref,
                      bd_ref, cd_ref, apr_ref, api_ref, s5d_ref, wglu_ref, fg_ref,
                      y_ref, sret_ref, hl_ref, cv_ref, sre_ref, sim_ref,
                      proj_scr, ycat_scr, perm_scr, xpad_scr, st_scr, st16_scr, yp_scr, h_scr,
                      *, final):
    nb, tt, _ = x_ref.shape
    tl = nb * tt
    t = pl.program_id(0)
    npre = CONV_WIDTH - 1
    (ng_ref, win_ref, wout_ref, gng_ref, cw_ref, cb_ref, wg_ref, bg_ref, lam_ref,
     bd_ref, cd_ref, apr_ref, api_ref, s5d_ref, wglu_ref) = (
        r.at[0] for r in (ng_ref, win_ref, wout_ref, gng_ref, cw_ref, cb_ref, wg_ref, bg_ref,
                          lam_ref, bd_ref, cd_ref, apr_ref, api_ref, s5d_ref, wglu_ref))

    @pl.when(t == 0)
    def _():
        sret_ref[...] = jnp.zeros_like(sret_ref)
        hl_ref[...] = jnp.zeros_like(hl_ref)
        sre_ref[...] = jnp.zeros_like(sre_ref)
        sim_ref[...] = jnp.zeros_like(sim_ref)
        xpad_scr[pl.ds(0, npre * SUBLANES), :] = jnp.zeros((npre * SUBLANES, LRU_WIDTH), F32)

    ng = ng_ref[...]
    for b in range(nb):
        hb = _rmsnorm(x_ref[b], ng * (1.0 + mod_ref[b, 1:2, :])) + mod_ref[b, 0:1, :]
        h_scr[pl.ds(b * tt, tt), :] = hb.astype(BF16)
    proj_b = jnp.concatenate(
        [jnp.dot(h_scr[pl.ds(p * (tl // 2), tl // 2), :], win_ref[:, pl.ds(OFF_XL, D_IN - OFF_XL)],
                 preferred_element_type=F32) for p in range(2)], axis=0)

    def proj_a_chunk(ci):
        cols = pl.ds(ci * PROJ_CHUNK, PROJ_CHUNK)
        res = jnp.dot(h_scr[...], win_ref[:, cols], preferred_element_type=F32)
        for kl in range(PROJ_CHUNK // LANES):
            proj_scr[ci * (PROJ_CHUNK // LANES) + kl] = res[:, kl * LANES:(kl + 1) * LANES]

    for m in range(tl // SUBLANES):
        b, r0 = divmod(m * SUBLANES, tt)
        for kb in range((D_IN - OFF_XL) // LANES):
            perm_scr[kb, pl.ds(r0 * SUBLANES + b, SUBLANES, stride=SUBLANES), :] = (
                proj_b[m * SUBLANES:(m + 1) * SUBLANES, kb * LANES:(kb + 1) * LANES])

    def ret_piece(b, hd):
        rows = pl.ds(b * tt, tt)
        lanes = pl.ds(hd * HEAD_DIM, HEAD_DIM)
        q = _rotary(proj_scr[OFF_Q // LANES + hd, rows, :], cq_ref[...], sq_ref[...])
        k = _rotary(proj_scr[OFF_K // LANES + hd, rows, :], ck_ref[...], sk_ref[...])
        v = proj_scr[OFF_V // LANES + hd, rows, :].astype(BF16)
        s_old = sret_ref[b, hd]
        sc = _dot_nt(q, k) * intra_ref[hd]
        o = _dot(sc, v) + _dot(q * qdec_ref[hd], s_old)
        kd_t = (k * kdec_ref[hd]).T
        sret_ref[b, hd] = cdec_ref[hd] * s_old + _dot(kd_t, v)
        on = _groupnorm(o, gng_ref[:, lanes])
        g_ret = proj_scr[OFF_GR // LANES + hd, rows, :]
        ycat_scr[hd, rows, :] = (on * _silu(g_ret)).astype(BF16)

    rg = lambda r: pl.ds(r * SUBLANES, SUBLANES)

    def pcols(lo, width):
        return jnp.concatenate([perm_scr[kb] for kb in range(lo // LANES, (lo + width) // LANES)],
                               axis=1)

    proj_a_chunk(0)
    proj_a_chunk(1)
    xl = pcols(OFF_XL - OFF_XL, LRU_WIDTH)
    xpad_scr[pl.ds(npre * SUBLANES, tl), :] = xl
    xc = cb_ref[...]
    for j in range(CONV_WIDTH):
        xc = xc + xpad_scr[pl.ds(j * SUBLANES, tl), :] * cw_ref[pl.ds(j, 1), :]
    for j in range(npre):
        tail = xl[(tt - npre + j) * SUBLANES:(tt - npre + j + 1) * SUBLANES, :]
        xpad_scr[rg(j), :] = tail
        cv_ref[j] = tail
    gates = _dot(xc, wg_ref[...]) + bg_ref[...]
    a, bb = _lru_coeffs(xc, gates, _softplus(-lam_ref[...]))
    proj_a_chunk(2)
    proj_a_chunk(3)
    sgl = _silu(pcols(OFF_GL - OFF_XL, LRU_WIDTH))
    hl = hl_ref[...]
    for r in range(tt):
        hl = a[r * SUBLANES:(r + 1) * SUBLANES, :] * hl + bb[r * SUBLANES:(r + 1) * SUBLANES, :]
        y = hl * sgl[r * SUBLANES:(r + 1) * SUBLANES, :]
        for kb in range(LRU_WIDTH // LANES):
            yp_scr[kb, rg(r), :] = y[:, kb * LANES:(kb + 1) * LANES]
    hl_ref[...] = hl

    proj_a_chunk(4)
    proj_a_chunk(5)
    u = pcols(OFF_U - OFF_XL, S5_WIDTH)
    bu = _dot(u, bd_ref[...])
    st_scr[0] = bu[:, :S5_FLAT]
    st_scr[1] = bu[:, S5_FLAT:]
    proj_a_chunk(6)
    proj_a_chunk(7)
    bshape = (SUBLANES, S5_FLAT)
    a1r = jnp.broadcast_to(apr_ref[pl.ds(0, 1), :], bshape)
    a1i = jnp.broadcast_to(api_ref[pl.ds(0, 1), :], bshape)
    hr, hi = sre_ref[...], sim_ref[...]
    n_pieces = nb * RET_HEADS
    per_piece = (tt // 2) // n_pieces
    for r2 in range(tt // 2):
        if r2 % per_piece == 0:
            ret_piece(*divmod(r2 // per_piece, RET_HEADS))
        parts_r, parts_i = [], []
        for r in (2 * r2, 2 * r2 + 1):
            hr, hi = (a1r * hr - a1i * hi + st_scr[0, rg(r), :],
                      a1r * hi + a1i * hr + st_scr[1, rg(r), :])
            parts_r.append(hr)
            parts_i.append(hi)
        rows2 = pl.ds(r2 * 2 * SUBLANES, 2 * SUBLANES)
        st16_scr[0, rows2, :] = jnp.concatenate(parts_r, axis=0).astype(BF16)
        st16_scr[1, rows2, :] = jnp.concatenate(parts_i, axis=0).astype(BF16)
    sre_ref[...] = hr
    sim_ref[...] = hi
    gs = pcols(OFF_GS - OFF_XL, S5_WIDTH)
    y_s5 = _s5_tail(jnp.concatenate([st16_scr[0], st16_scr[1]], axis=1), u, gs,
                    cd_ref[...], s5d_ref[...], wglu_ref[...], row_parts=2)
    for kb in range(S5_WIDTH // LANES):
        yp_scr[LRU_WIDTH // LANES + kb] = y_s5[:, kb * LANES:(kb + 1) * LANES]

    for m2 in range(tl // (2 * SUBLANES)):
        for kb in range((LRU_WIDTH + S5_WIDTH) // LANES):
            halves = []
            for m in (2 * m2, 2 * m2 + 1):
                b, r0 = divmod(m * SUBLANES, tt)
                halves.append(yp_scr[kb, pl.ds(r0 * SUBLANES + b, SUBLANES, stride=SUBLANES), :])
            ycat_scr[RET_WIDTH // LANES + kb, pl.ds(m2 * 2 * SUBLANES, 2 * SUBLANES), :] = (
                jnp.concatenate(halves, axis=0).astype(BF16))

    ycat = jnp.concatenate([ycat_scr[kb] for kb in range(ycat_scr.shape[0])], axis=1)
    half = tl // 2
    out = jnp.concatenate(
        [jnp.dot(ycat[p * half:(p + 1) * half, :], wout_ref[...], preferred_element_type=F32)
         for p in range(2)], axis=0)
    for b in range(nb):
        x_new = x_ref[b] + mod_ref[b, 2:3, :] * out[b * tt:(b + 1) * tt, :]
        if final:
            x_new = _rmsnorm(x_new, fg_ref[...])
        y_ref[b] = x_new


def _prompt_layer_bm(x, mod, w, tabs, layer, *, final):
    bsz, seq, d = x.shape
    tt = PROMPT_T
    tl = bsz * tt
    nt = seq // tt

    def full(a):
        return pl.BlockSpec(a.shape, lambda t: (0,) * a.ndim)

    def lay(a):
        return pl.BlockSpec((1,) + a.shape[1:], lambda t: (layer,) + (0,) * (a.ndim - 1))

    row_tab = pl.BlockSpec((tt, HEAD_DIM), lambda t: (t, 0))
    in_specs = [
        pl.BlockSpec((bsz, tt, d), lambda t: (0, t, 0)),
        full(mod),
        lay(w["norm_g"]), lay(w["w_in"]), lay(w["w_out"]),
        row_tab, row_tab, row_tab, row_tab,
        full(tabs["intra"]), full(tabs["q_dec"]), full(tabs["k_dec"]),
        pl.BlockSpec(memory_space=pltpu.SMEM),
        lay(w["gn_g"]), lay(w["conv_w"]), lay(w["conv_b"]), lay(w["w_gate"]), lay(w["b_gate"]),
        lay(w["lam"]), lay(w["bd"]), lay(w["cd"]), lay(w["apr"]), lay(w["api"]),
        lay(w["s5_d"]), lay(w["w_glu"]), full(w["final_g"]),
    ]
    out_shape = (
        jax.ShapeDtypeStruct((bsz, seq, d), F32),
        jax.ShapeDtypeStruct((bsz, RET_HEADS, HEAD_DIM, HEAD_DIM), F32),
        jax.ShapeDtypeStruct((bsz, LRU_WIDTH), F32),
        jax.ShapeDtypeStruct((CONV_WIDTH - 1, bsz, LRU_WIDTH), F32),
        jax.ShapeDtypeStruct((bsz, S5_FLAT), F32),
        jax.ShapeDtypeStruct((bsz, S5_FLAT), F32),
    )
    out_specs = (
        pl.BlockSpec((bsz, tt, d), lambda t: (0, t, 0)),
        pl.BlockSpec((bsz, RET_HEADS, HEAD_DIM, HEAD_DIM), lambda t: (0, 0, 0, 0)),
        pl.BlockSpec((bsz, LRU_WIDTH), lambda t: (0, 0)),
        pl.BlockSpec((CONV_WIDTH - 1, bsz, LRU_WIDTH), lambda t: (0, 0, 0)),
        pl.BlockSpec((bsz, S5_FLAT), lambda t: (0, 0)),
        pl.BlockSpec((bsz, S5_FLAT), lambda t: (0, 0)),
    )
    scratch = [
        pltpu.VMEM((OFF_XL // LANES, tl, LANES), F32),
        pltpu.VMEM((d // LANES, tl, LANES), BF16),
        pltpu.VMEM(((D_IN - OFF_XL) // LANES, tl, LANES), F32),
        pltpu.VMEM((tl + (CONV_WIDTH - 1) * SUBLANES, LRU_WIDTH), F32),
        pltpu.VMEM((2, tl, S5_FLAT), F32),
        pltpu.VMEM((2, tl, S5_FLAT), BF16),
        pltpu.VMEM(((LRU_WIDTH + S5_WIDTH) // LANES, tl, LANES), F32),
        pltpu.VMEM((tl, d), BF16),
    ]
    return pl.pallas_call(
        functools.partial(_prompt_kernel_bm, final=final),
        grid=(nt,),
        in_specs=in_specs,
        out_specs=out_specs,
        out_shape=out_shape,
        scratch_shapes=scratch,
        compiler_params=pltpu.CompilerParams(
            dimension_semantics=("arbitrary",), vmem_limit_bytes=VMEM_LIMIT),
        name="prompt_layer",
    )(x, mod, w["norm_g"], w["w_in"], w["w_out"],
      tabs["cos_q"], tabs["sin_q"], tabs["cos_k"], tabs["sin_k"],
      tabs["intra"], tabs["q_dec"], tabs["k_dec"], tabs["c_dec"],
      w["gn_g"], w["conv_w"], w["conv_b"], w["w_gate"], w["b_gate"], w["lam"],
      w["bd"], w["cd"], w["apr"], w["api"], w["s5_d"], w["w_glu"], w["final_g"])


def _decode_kernel(x_ref, mod_ref, ng_ref, win_ref, wout_ref,
                   cq_ref, sq_ref, ck_ref, sk_ref, gam_ref,
                   gng_ref, cw_ref, cb_ref, wg_ref, bg_ref, lam_ref,
                   bd_ref, cd_ref, apr_ref, api_ref, s5d_ref, wglu_ref, fg_ref,
                   sret_in, hl_in, cv_in, sre_in, sim_in, sel_ref,
                   y_ref, sret_ref, hl_ref, cv_ref, sre_ref, sim_ref,
                   qkt_scr, v_scr, o_scr, gr_scr, ycat_scr, x_scr):
    layer = pl.program_id(0)
    i = pl.program_id(1)
    nb = pl.num_programs(1)
    rows_n, d = x_ref.shape
    (mod_ref, ng_ref, win_ref, wout_ref, gng_ref, cw_ref, cb_ref, wg_ref, bg_ref, lam_ref,
     bd_ref, cd_ref, apr_ref, api_ref, s5d_ref, wglu_ref,
     sret_in, hl_in, cv_in, sre_in, sim_in, sret_ref, hl_ref, cv_ref, sre_ref, sim_ref) = (
        r.at[0] for r in (mod_ref, ng_ref, win_ref, wout_ref, gng_ref, cw_ref, cb_ref, wg_ref,
                          bg_ref, lam_ref, bd_ref, cd_ref, apr_ref, api_ref, s5d_ref, wglu_ref,
                          sret_in, hl_in, cv_in, sre_in, sim_in,
                          sret_ref, hl_ref, cv_ref, sre_ref, sim_ref))

    @pl.when((i == 0) & (layer == 0))
    def _():
        x_scr[...] = x_ref[...]

    @pl.when(i == 0)
    def _():
        x = x_scr[...]
        shift, scale = mod_ref[:, pl.ds(0, d)], mod_ref[:, pl.ds(d, d)]
        h = _rmsnorm(x, ng_ref[...]) * (1.0 + scale) + shift
        proj = _dot(h, win_ref[...])
        cq, sq, ck, sk = cq_ref[...], sq_ref[...], ck_ref[...], sk_ref[...]
        for hd in range(RET_HEADS):
            lo = hd * HEAD_DIM
            for part, src in enumerate((_rotary(proj[:, OFF_Q + lo:OFF_Q + lo + HEAD_DIM], cq, sq),
                                        _rotary(proj[:, OFF_K + lo:OFF_K + lo + HEAD_DIM], ck, sk))):
                xt = src.T
                hi = xt.astype(BF16)
                trows = pl.ds((2 * hd + part) * HEAD_DIM, HEAD_DIM)
                qkt_scr[trows, pl.ds(0, rows_n)] = hi
                qkt_scr[trows, pl.ds(rows_n, rows_n)] = (xt - hi.astype(F32)).astype(BF16)
        v_scr[...] = proj[:, OFF_V:OFF_V + RET_WIDTH]
        gr_scr[...] = _silu(proj[:, OFF_GR:OFF_GR + RET_WIDTH])
        xl = proj[:, OFF_XL:OFF_XL + LRU_WIDTH]
        cs = cv_in[...]
        xc = cb_ref[...] + xl * cw_ref[pl.ds(CONV_WIDTH - 1, 1), :]
        for j in range(CONV_WIDTH - 1):
            xc = xc + cs[:, j * LRU_WIDTH:(j + 1) * LRU_WIDTH] * cw_ref[pl.ds(j, 1), :]
        cv_ref[:, pl.ds(0, 2 * LRU_WIDTH)] = cs[:, LRU_WIDTH:]
        cv_ref[:, pl.ds(2 * LRU_WIDTH, LRU_WIDTH)] = xl
        gates = _dot(xc, wg_ref[...]) + bg_ref[...]
        a, b = _lru_coeffs(xc, gates, _softplus(-lam_ref[...]))
        hh = b + a * hl_in[...]
        hl_ref[...] = hh
        ycat_scr[:, pl.ds(RET_WIDTH, LRU_WIDTH)] = (
            hh * _silu(proj[:, OFF_GL:OFF_GL + LRU_WIDTH])).astype(BF16)
        u = proj[:, OFF_U:OFF_U + S5_WIDTH]
        bu = _dot(u, bd_ref[...])
        ar, ai = apr_ref[pl.ds(0, 1), :], api_ref[pl.ds(0, 1), :]
        s0r, s0i = sre_in[...], sim_in[...]
        sr = bu[:, :S5_FLAT] + (ar * s0r - ai * s0i)
        si = bu[:, S5_FLAT:] + (ar * s0i + ai * s0r)
        sre_ref[...] = sr
        sim_ref[...] = si
        y_s5 = _s5_tail(jnp.concatenate([sr, si], axis=-1), u, proj[:, OFF_GS:OFF_GS + S5_WIDTH],
                        cd_ref[...], s5d_ref[...], wglu_ref[...])
        ycat_scr[:, pl.ds(RET_WIDTH + LRU_WIDTH, S5_WIDTH)] = y_s5.astype(BF16)

    blk_rows = pl.ds(pl.multiple_of(i * DEC_BLOCK, DEC_BLOCK), DEC_BLOCK)
    o_rows = [[] for _ in range(RET_HEADS)]
    vblks = [v_scr[blk_rows, pl.ds(hd * HEAD_DIM, HEAD_DIM)] for hd in range(RET_HEADS)]
    for jp in range(DEC_BLOCK // 2):
        sel = sel_ref[0, :, pl.ds(jp * 2 * HEAD_DIM, 2 * HEAD_DIM)]
        tiles = jnp.dot(qkt_scr[...], jnp.concatenate([sel, sel], axis=0),
                        preferred_element_type=F32)
        for hd in range(RET_HEADS):
            qb = tiles[2 * hd * HEAD_DIM:(2 * hd + 1) * HEAD_DIM, :]
            kb = tiles[(2 * hd + 1) * HEAD_DIM:(2 * hd + 2) * HEAD_DIM, :]
            for j in (2 * jp, 2 * jp + 1):
                cols = slice((j % 2) * HEAD_DIM, (j % 2 + 1) * HEAD_DIM)
                s_new = gam_ref[hd] * sret_in[j, hd] + kb[:, cols] * vblks[hd][j:j + 1, :]
                sret_ref[j, hd] = s_new
                o_rows[hd].append(jnp.sum(qb[:, cols] * s_new, axis=0, keepdims=True))
    for hd in range(RET_HEADS):
        o_scr[blk_rows, pl.ds(hd * HEAD_DIM, HEAD_DIM)] = jnp.concatenate(o_rows[hd], axis=0)

    @pl.when(i == nb - 1)
    def _():
        for hd in range(RET_HEADS):
            lanes = pl.ds(hd * HEAD_DIM, HEAD_DIM)
            on = _groupnorm(o_scr[:, lanes], gng_ref[:, lanes])
            ycat_scr[:, lanes] = (on * gr_scr[:, lanes]).astype(BF16)
        out = jnp.dot(ycat_scr[...], wout_ref[...], preferred_element_type=F32)
        x_new = x_scr[...] + mod_ref[:, pl.ds(2 * d, d)] * out
        x_scr[...] = x_new
        y_ref[...] = _rmsnorm(x_new, fg_ref[...])


def _decode_layers(x, mod, states, w, tabs):
    rows, d = x.shape
    depth = mod.shape[0]
    s_ret, s_h, s_conv, s_re, s_im = states
    nb = rows // DEC_BLOCK

    def full(a):
        return pl.BlockSpec(a.shape, lambda l, i: (0,) * a.ndim)

    def lay(a):
        return pl.BlockSpec((1,) + a.shape[1:], lambda l, i: (l,) + (0,) * (a.ndim - 1))

    smem = pl.BlockSpec(memory_space=pltpu.SMEM)
    sblk = pl.BlockSpec((1, DEC_BLOCK, RET_HEADS, HEAD_DIM, HEAD_DIM), lambda l, i: (l, i, 0, 0, 0))
    in_specs = [
        full(x), lay(mod), lay(w["norm_g"]), lay(w["w_in"]), lay(w["w_out"]),
        full(tabs["dcos_q"]), full(tabs["dsin_q"]), full(tabs["dcos_k"]), full(tabs["dsin_k"]), smem,
        lay(w["gn_g"]), lay(w["conv_w"]), lay(w["conv_b"]), lay(w["w_gate"]), lay(w["b_gate"]),
        lay(w["lam"]), lay(w["bd"]), lay(w["cd"]), lay(w["apr"]), lay(w["api"]),
        lay(w["s5_d"]), lay(w["w_glu"]), full(w["final_g"]),
        sblk, lay(s_h), lay(s_conv), lay(s_re), lay(s_im),
        pl.BlockSpec((1,) + tabs["row_select"].shape[1:], lambda l, i: (i, 0, 0)),
    ]
    out_shape = (
        jax.ShapeDtypeStruct((rows, d), F32),
        jax.ShapeDtypeStruct(s_ret.shape, F32),
        jax.ShapeDtypeStruct(s_h.shape, F32),
        jax.ShapeDtypeStruct(s_conv.shape, F32),
        jax.ShapeDtypeStruct(s_re.shape, F32),
        jax.ShapeDtypeStruct(s_im.shape, F32),
    )
    out_specs = (full(x), sblk, lay(s_h), lay(s_conv), lay(s_re), lay(s_im))
    scratch = [
        pltpu.VMEM((RET_HEADS * 2 * HEAD_DIM, 2 * rows), BF16),
        pltpu.VMEM((rows, RET_WIDTH), F32),
        pltpu.VMEM((rows, RET_WIDTH), F32),
        pltpu.VMEM((rows, RET_WIDTH), F32),
        pltpu.VMEM((rows, d), BF16),
        pltpu.VMEM((rows, d), F32),
    ]
    return pl.pallas_call(
        _decode_kernel,
        grid=(depth, nb),
        in_specs=in_specs,
        out_specs=out_specs,
        out_shape=out_shape,
        scratch_shapes=scratch,
        compiler_params=pltpu.CompilerParams(
            dimension_semantics=("arbitrary", "arbitrary"), vmem_limit_bytes=VMEM_LIMIT),
        name="decode_layers",
    )(x, mod, w["norm_g"], w["w_in"], w["w_out"],
      tabs["dcos_q"], tabs["dsin_q"], tabs["dcos_k"], tabs["dsin_k"], tabs["gamma"],
      w["gn_g"], w["conv_w"], w["conv_b"], w["w_gate"], w["b_gate"], w["lam"],
      w["bd"], w["cd"], w["apr"], w["api"], w["s5_d"], w["w_glu"], w["final_g"],
      s_ret, s_h, s_conv, s_re, s_im, tabs["row_select"])


def _rope_tables(pos, scale):
    half = HEAD_DIM // 2
    inv = ROPE_BASE ** (-np.arange(half, dtype=np.float64) / half)
    ang = pos[:, None] * inv[None, :]
    cos, sin = np.cos(ang), np.sin(ang)
    cosf = np.concatenate([cos, cos], axis=-1) * scale
    sinf = np.concatenate([-sin, sin], axis=-1) * scale
    return cosf, sinf


def _tables(seq, dec, c):
    log_g = np.log1p(-np.exp2(-5.0 - np.arange(RET_HEADS, dtype=np.float64)))
    idx = np.arange(c, dtype=np.float64)
    rel = idx[:, None] - idx[None, :]
    intra = np.where(rel[None] >= 0, np.exp(np.maximum(rel, 0.0)[None] * log_g[:, None, None]), 0.0)
    q_dec = np.exp((idx + 1.0)[:, None] * log_g[None, :])
    k_dec = np.exp((c - 1.0 - idx)[:, None] * log_g[None, :])
    q_dec = np.broadcast_to(q_dec.T[:, :, None], (RET_HEADS, c, HEAD_DIM))
    k_dec = np.broadcast_to(k_dec.T[:, :, None], (RET_HEADS, c, HEAD_DIM))
    c_dec = np.exp(c * log_g)
    gamma = np.exp(log_g)
    k_scale = HEAD_DIM ** -0.5
    cos_q, sin_q = _rope_tables(np.arange(seq, dtype=np.float64), 1.0)
    cos_k, sin_k = _rope_tables(np.arange(seq, dtype=np.float64), k_scale)
    dpos = PAST_LEN + np.arange(1, dtype=np.float64)
    dcos_q, dsin_q = _rope_tables(dpos, 1.0)
    dcos_k, dsin_k = _rope_tables(dpos, k_scale)
    blk = np.arange(dec // DEC_BLOCK)[:, None, None]
    row = np.arange(dec)[None, :, None]
    col = np.arange(DEC_BLOCK * HEAD_DIM)[None, None, :]
    row_select = row == blk * DEC_BLOCK + col // HEAD_DIM
    tabs = dict(intra=intra, q_dec=q_dec, k_dec=k_dec, c_dec=c_dec, gamma=gamma,
                cos_q=cos_q, sin_q=sin_q, cos_k=cos_k, sin_k=sin_k,
                dcos_q=dcos_q, dsin_q=dsin_q, dcos_k=dcos_k, dsin_k=dsin_k)
    tabs = {k: jnp.asarray(np.ascontiguousarray(v), dtype=F32) for k, v in tabs.items()}
    tabs["row_select"] = jnp.asarray(row_select, dtype=BF16)
    return tabs


def _block_diag(blocks):
    n, r, c = blocks.shape
    eye = jnp.eye(n, dtype=blocks.dtype)
    return jnp.einsum("nrc,nm->nrmc", blocks, eye).reshape(n * r, n * c)


def kernel(x_prompt, x_sample, state_ret, state_lru_h, state_lru_conv, state_s5_re, state_s5_im,
           c_prompt, c_sample, norm_g, w_ada, b_ada, w_in, ret_gn_g, conv_w, conv_b, w_rg, b_rg,
           w_ig, b_ig, lru_lambda, s5_a_re, s5_a_im, s5_b_re, s5_b_im, s5_c_re, s5_c_im, s5_d,
           s5_log_dt, s5_w_glu, w_out, final_g):
    depth = w_in.shape[0]
    bsz, seq, d = x_prompt.shape
    dec = x_sample.shape[0]
    assert x_sample.shape[1] == 1 and dec % DEC_BLOCK == 0

    assert bsz == SUBLANES and seq % PROMPT_T == 0
    tabs = _tables(seq, dec, PROMPT_T)
    mod_all = _ada(jnp.concatenate([c_prompt, c_sample], axis=0), w_ada, b_ada)
    n_pow = SUBLANES
    apr, api, bbr, bbi = _s5_prep(s5_a_re, s5_a_im, s5_log_dt, s5_b_re, s5_b_im, n_pow)

    bdiag = jax.vmap(_block_diag)
    bd = jnp.concatenate([bdiag(bbr), bdiag(bbi)], axis=2)
    cd = jnp.concatenate([bdiag(jnp.swapaxes(s5_c_re, 2, 3)),
                          bdiag(jnp.swapaxes(-s5_c_im, 2, 3))], axis=1)
    w = dict(
        norm_g=norm_g.reshape(depth, 1, d),
        w_in=w_in.astype(BF16),
        w_out=w_out.astype(BF16),
        gn_g=ret_gn_g.reshape(depth, 1, RET_WIDTH),
        conv_w=conv_w,
        conv_b=conv_b.reshape(depth, 1, LRU_WIDTH),
        w_gate=jnp.concatenate([bdiag(w_rg), bdiag(w_ig)], axis=2).astype(BF16),
        b_gate=jnp.concatenate([b_rg, b_ig], axis=1).reshape(depth, 1, 2 * LRU_WIDTH),
        lam=lru_lambda.reshape(depth, 1, LRU_WIDTH),
        bd=bd.astype(BF16),
        cd=cd.astype(BF16),
        apr=apr.reshape(depth, n_pow, S5_FLAT),
        api=api.reshape(depth, n_pow, S5_FLAT),
        s5_d=s5_d.reshape(depth, 1, S5_WIDTH),
        w_glu=s5_w_glu.astype(BF16),
        final_g=final_g.reshape(1, d),
    )

    xp = x_prompt
    outs_p = []
    for l in range(depth):
        mod_p = mod_all[l, :bsz].reshape(bsz, 3, d)
        xp, sret, hl, cv, sre, sim = _prompt_layer_bm(xp, mod_p, w, tabs, l, final=l == depth - 1)
        outs_p.append((sret, hl, jnp.swapaxes(cv, 0, 1),
                       sre.reshape(bsz, S5_GROUPS, S5_STATE), sim.reshape(bsz, S5_GROUPS, S5_STATE)))

    states = (state_ret, state_lru_h,
              state_lru_conv.reshape(depth, dec, (CONV_WIDTH - 1) * LRU_WIDTH),
              state_s5_re.reshape(depth, dec, S5_FLAT), state_s5_im.reshape(depth, dec, S5_FLAT))
    xs, sret_s, hl_s, cv_s, sre_s, sim_s = _decode_layers(
        x_sample.reshape(dec, d), mod_all[:, bsz:], states, w, tabs)

    def stk(k):
        return jnp.stack([o[k] for o in outs_p])

    return (xp, xs.reshape(dec, 1, d),
            stk(0), sret_s, stk(1), hl_s,
            stk(2), cv_s.reshape(depth, dec, CONV_WIDTH - 1, LRU_WIDTH),
            stk(3), sre_s.reshape(depth, dec, S5_GROUPS, S5_STATE),
            stk(4), sim_s.reshape(depth, dec, S5_GROUPS, S5_STATE))
```

```python
import functools
import math

import jax
import jax.numpy as jnp
import numpy as np
from jax import lax
from jax.experimental import pallas as pl
from jax.experimental.pallas import tpu as pltpu

RET_HEADS = 4
HEAD_DIM = 128
RET_WIDTH = RET_HEADS * HEAD_DIM
LRU_WIDTH = 256
LRU_C = 8.0
CONV_WIDTH = 4
S5_WIDTH = 256
S5_GROUP = 16
S5_GROUPS = 16
S5_STATE = 64
S5_FLAT = S5_GROUPS * S5_STATE
ROPE_BASE = 10000.0
EPS = 1e-6
PAST_LEN = 16384

OFF_Q = 0
OFF_K = OFF_Q + RET_WIDTH
OFF_V = OFF_K + RET_WIDTH
OFF_GR = OFF_V + RET_WIDTH
OFF_XL = OFF_GR + RET_WIDTH
OFF_GL = OFF_XL + LRU_WIDTH
OFF_U = OFF_GL + LRU_WIDTH
OFF_GS = OFF_U + S5_WIDTH
D_IN = OFF_GS + S5_WIDTH

SUBLANES = 8
LANES = 128
PROMPT_T = 64
PROJ_CHUNK = 256
DEC_BLOCK = 16
VMEM_LIMIT = 56 * 1024 * 1024

F32 = jnp.float32
BF16 = jnp.bfloat16


def _sigmoid(x):
    return 1.0 / (1.0 + jnp.exp(-x))


def _silu(x):
    return x * _sigmoid(x)


def _gelu_tanh(x):
    c = math.sqrt(2.0 / math.pi)
    return x * (0.5 * (1.0 + jnp.tanh(c * (x + 0.044715 * (x * x * x)))))


def _softplus(x):
    return jnp.maximum(x, 0.0) + jnp.log1p(jnp.exp(-jnp.abs(x)))


def _rmsnorm(x, g):
    ms = jnp.mean(x * x, axis=-1, keepdims=True)
    return x * lax.rsqrt(ms + EPS) * g


def _dot(a, b):
    return jnp.dot(a.astype(BF16), b.astype(BF16), preferred_element_type=F32)


def _dot_nt(a, b):
    return lax.dot_general(a.astype(BF16), b.astype(BF16), (((1,), (1,)), ((), ())),
                           preferred_element_type=F32)


def _rotary(x, cosf, sinf):
    return x * cosf + pltpu.roll(x, HEAD_DIM // 2, axis=1) * sinf


def _groupnorm(o, g):
    mu = jnp.mean(o, axis=-1, keepdims=True)
    d = o - mu
    var = jnp.mean(d * d, axis=-1, keepdims=True)
    return d * lax.rsqrt(var + EPS) * g


def _lru_coeffs(xc, gates, sp):
    r = _sigmoid(gates[:, :LRU_WIDTH])
    ig = _sigmoid(gates[:, LRU_WIDTH:])
    log_a = (-LRU_C) * r * sp
    a = jnp.exp(log_a)
    th = jnp.tanh(log_a)
    mult = jnp.sqrt(-2.0 * th / (1.0 - th))
    return a, mult * ig * xc


def _s5_tail(sr_si, u, gs, cd, s5d, wglu, row_parts=1):
    rows = sr_si.shape[0] // row_parts
    ys = jnp.concatenate([_dot(sr_si[p * rows:(p + 1) * rows], cd) for p in range(row_parts)],
                         axis=0) + s5d * u
    ys = _gelu_tanh(ys)
    ys = ys * _sigmoid(_dot(ys, wglu))
    return ys * _silu(gs)


def _s5_prep_kernel(are_ref, aim_ref, ldt_ref, bre_ref, bim_ref,
                    apr_ref, api_ref, bbr_ref, bbi_ref):
    depth = are_ref.shape[0]
    for l in range(depth):
        a_re = are_ref[l]
        a_im = aim_ref[l]
        step = jnp.exp(ldt_ref[l])
        mag = jnp.exp(step * a_re)
        abar_r = mag * jnp.cos(step * a_im)
        abar_i = mag * jnp.sin(step * a_im)
        nr, ni = abar_r - 1.0, abar_i
        den = a_re * a_re + a_im * a_im
        fr = (nr * a_re + ni * a_im) / den
        fi = (ni * a_re - nr * a_im) / den
        for g in range(S5_GROUPS):
            frg, fig = fr[g:g + 1, :], fi[g:g + 1, :]
            b_re, b_im = bre_ref[l, g], bim_ref[l, g]
            bbr_ref[l, g] = frg * b_re - fig * b_im
            bbi_ref[l, g] = frg * b_im + fig * b_re
        pr, pi = abar_r, abar_i
        apr_ref[l, 0] = pr
        api_ref[l, 0] = pi
        for m in range(1, apr_ref.shape[1]):
            pr, pi = pr * abar_r - pi * abar_i, pr * abar_i + pi * abar_r
            apr_ref[l, m] = pr
            api_ref[l, m] = pi


def _s5_prep(s5_a_re, s5_a_im, s5_log_dt, s5_b_re, s5_b_im, n_pow):
    depth = s5_a_re.shape[0]
    b_re_t = jnp.swapaxes(s5_b_re, 2, 3)
    b_im_t = jnp.swapaxes(s5_b_im, 2, 3)
    ldt = s5_log_dt.reshape(depth, S5_GROUPS, 1)
    out_shape = (
        jax.ShapeDtypeStruct((depth, n_pow, S5_GROUPS, S5_STATE), F32),
        jax.ShapeDtypeStruct((depth, n_pow, S5_GROUPS, S5_STATE), F32),
        jax.ShapeDtypeStruct((depth, S5_GROUPS, S5_GROUP, S5_STATE), F32),
        jax.ShapeDtypeStruct((depth, S5_GROUPS, S5_GROUP, S5_STATE), F32),
    )
    return pl.pallas_call(_s5_prep_kernel, out_shape=out_shape, name="s5_prep")(
        s5_a_re, s5_a_im, ldt, b_re_t, b_im_t)


def _ada_kernel(c_ref, w_ref, b_ref, o_ref):
    s = _silu(c_ref[...])
    o_ref[0] = _dot(s, w_ref[0]) + b_ref[0]


def _ada(c_all, w_ada, b_ada):
    depth, d, n3 = w_ada.shape
    rows = c_all.shape[0]
    tn = d
    return pl.pallas_call(
        _ada_kernel,
        grid=(depth, n3 // tn),
        in_specs=[
            pl.BlockSpec((rows, d), lambda l, j: (0, 0)),
            pl.BlockSpec((1, d, tn), lambda l, j: (l, 0, j)),
            pl.BlockSpec((1, 1, tn), lambda l, j: (l, 0, j)),
        ],
        out_specs=pl.BlockSpec((1, rows, tn), lambda l, j: (l, 0, j)),
        out_shape=jax.ShapeDtypeStruct((depth, rows, n3), F32),
        compiler_params=pltpu.CompilerParams(
            dimension_semantics=("arbitrary", "arbitrary"), vmem_limit_bytes=VMEM_LIMIT),
        name="adaln",
    )(c_all, w_ada, b_ada.reshape(depth, 1, n3))


def _prompt_kernel_bm(x_ref, mod_ref, ng_ref, win_ref, wout_ref,
                      cq_ref, sq_ref, ck_ref, sk_ref, intra_ref, qdec_ref, kdec_ref, cdec_ref,
                      gng_ref, cw_ref, cb_ref, wg_ref, bg_ref, lam_ref,
                      bd_ref, cd_ref, apr_ref, api_ref, s5d_ref, wglu_ref, fg_ref,
                      y_ref, sret_ref, hl_ref, cv_ref, sre_ref, sim_ref,
                      proj_scr, ycat_scr, perm_scr, xpad_scr, st_scr, st16_scr, yp_scr, h_scr,
                      *, final):
    nb, tt, _ = x_ref.shape
    tl = nb * tt
    t = pl.program_id(0)
    npre = CONV_WIDTH - 1
    (ng_ref, win_ref, wout_ref, gng_ref, cw_ref, cb_ref, wg_ref, bg_ref, lam_ref,
     bd_ref, cd_ref, apr_ref, api_ref, s5d_ref, wglu_ref) = (
        r.at[0] for r in (ng_ref, win_ref, wout_ref, gng_ref, cw_ref, cb_ref, wg_ref, bg_ref,
                          lam_ref, bd_ref, cd_ref, apr_ref, api_ref, s5d_ref, wglu_ref))

    @pl.when(t == 0)
    def _():
        sret_ref[...] = jnp.zeros_like(sret_ref)
        hl_ref[...] = jnp.zeros_like(hl_ref)
        sre_ref[...] = jnp.zeros_like(sre_ref)
        sim_ref[...] = jnp.zeros_like(sim_ref)
        xpad_scr[pl.ds(0, npre * SUBLANES), :] = jnp.zeros((npre * SUBLANES, LRU_WIDTH), F32)

    ng = ng_ref[...]
    for b in range(nb):
        hb = _rmsnorm(x_ref[b], ng * (1.0 + mod_ref[b, 1:2, :])) + mod_ref[b, 0:1, :]
        h_scr[pl.ds(b * tt, tt), :] = hb.astype(BF16)
    proj_b = jnp.dot(h_scr[...], win_ref[:, pl.ds(OFF_XL, D_IN - OFF_XL)],
                     preferred_element_type=F32)

    def proj_a_chunk(ci):
        cols = pl.ds(ci * PROJ_CHUNK, PROJ_CHUNK)
        res = jnp.dot(h_scr[...], win_ref[:, cols], preferred_element_type=F32)
        for kl in range(PROJ_CHUNK // LANES):
            proj_scr[ci * (PROJ_CHUNK // LANES) + kl] = res[:, kl * LANES:(kl + 1) * LANES]

    for m in range(tl // SUBLANES):
        b, r0 = divmod(m * SUBLANES, tt)
        for kb in range((D_IN - OFF_XL) // LANES):
            perm_scr[kb, pl.ds(r0 * SUBLANES + b, SUBLANES, stride=SUBLANES), :] = (
                proj_b[m * SUBLANES:(m + 1) * SUBLANES, kb * LANES:(kb + 1) * LANES])

    def ret_piece(b, hd):
        rows = pl.ds(b * tt, tt)
        lanes = pl.ds(hd * HEAD_DIM, HEAD_DIM)
        q = _rotary(proj_scr[OFF_Q // LANES + hd, rows, :], cq_ref[...], sq_ref[...])
        k = _rotary(proj_scr[OFF_K // LANES + hd, rows, :], ck_ref[...], sk_ref[...])
        v = proj_scr[OFF_V // LANES + hd, rows, :].astype(BF16)
        s_old = sret_ref[b, hd]
        sc = _dot_nt(q, k) * intra_ref[hd]
        o = _dot(sc, v) + _dot(q * qdec_ref[hd], s_old)
        kd_t = (k * kdec_ref[hd]).T
        sret_ref[b, hd] = cdec_ref[hd] * s_old + _dot(kd_t, v)
        on = _groupnorm(o, gng_ref[:, lanes])
        g_ret = proj_scr[OFF_GR // LANES + hd, rows, :]
        ycat_scr[hd, rows, :] = (on * _silu(g_ret)).astype(BF16)

    rg = lambda r: pl.ds(r * SUBLANES, SUBLANES)

    def pcols(lo, width):
        return jnp.concatenate([perm_scr[kb] for kb in range(lo // LANES, (lo + width) // LANES)],
                               axis=1)

    proj_a_chunk(0)
    proj_a_chunk(1)
    xl = pcols(OFF_XL - OFF_XL, LRU_WIDTH)
    xpad_scr[pl.ds(npre * SUBLANES, tl), :] = xl
    xc = cb_ref[...]
    for j in range(CONV_WIDTH):
        xc = xc + xpad_scr[pl.ds(j * SUBLANES, tl), :] * cw_ref[pl.ds(j, 1), :]
    for j in range(npre):
        tail = xl[(tt - npre + j) * SUBLANES:(tt - npre + j + 1) * SUBLANES, :]
        xpad_scr[rg(j), :] = tail
        cv_ref[j] = tail
    gates = _dot(xc, wg_ref[...]) + bg_ref[...]
    a, bb = _lru_coeffs(xc, gates, _softplus(-lam_ref[...]))
    proj_a_chunk(2)
    proj_a_chunk(3)
    sgl = _silu(pcols(OFF_GL - OFF_XL, LRU_WIDTH))
    hl = hl_ref[...]
    for r in range(tt):
        hl = a[r * SUBLANES:(r + 1) * SUBLANES, :] * hl + bb[r * SUBLANES:(r + 1) * SUBLANES, :]
        y = hl * sgl[r * SUBLANES:(r + 1) * SUBLANES, :]
        for kb in range(LRU_WIDTH // LANES):
            yp_scr[kb, rg(r), :] = y[:, kb * LANES:(kb + 1) * LANES]
    hl_ref[...] = hl

    proj_a_chunk(4)
    proj_a_chunk(5)
    u = pcols(OFF_U - OFF_XL, S5_WIDTH)
    bu = _dot(u, bd_ref[...])
    st_scr[0] = bu[:, :S5_FLAT]
    st_scr[1] = bu[:, S5_FLAT:]
    proj_a_chunk(6)
    proj_a_chunk(7)
    bshape = (SUBLANES, S5_FLAT)
    a1r = jnp.broadcast_to(apr_ref[pl.ds(0, 1), :], bshape)
    a1i = jnp.broadcast_to(api_ref[pl.ds(0, 1), :], bshape)
    hr, hi = sre_ref[...], sim_ref[...]
    n_pieces = nb * RET_HEADS
    per_piece = (tt // 2) // n_pieces
    for r2 in range(tt // 2):
        if r2 % per_piece == 0:
            ret_piece(*divmod(r2 // per_piece, RET_HEADS))
        parts_r, parts_i = [], []
        for r in (2 * r2, 2 * r2 + 1):
            hr, hi = (a1r * hr - a1i * hi + st_scr[0, rg(r), :],
                      a1r * hi + a1i * hr + st_scr[1, rg(r), :])
            parts_r.append(hr)
            parts_i.append(hi)
        rows2 = pl.ds(r2 * 2 * SUBLANES, 2 * SUBLANES)
        st16_scr[0, rows2, :] = jnp.concatenate(parts_r, axis=0).astype(BF16)
        st16_scr[1, rows2, :] = jnp.concatenate(parts_i, axis=0).astype(BF16)
    sre_ref[...] = hr
    sim_ref[...] = hi
    gs = pcols(OFF_GS - OFF_XL, S5_WIDTH)
    y_s5 = _s5_tail(jnp.concatenate([st16_scr[0], st16_scr[1]], axis=1), u, gs,
                    cd_ref[...], s5d_ref[...], wglu_ref[...], row_parts=2)
    for kb in range(S5_WIDTH // LANES):
        yp_scr[LRU_WIDTH // LANES + kb] = y_s5[:, kb * LANES:(kb + 1) * LANES]

    for m2 in range(tl // (2 * SUBLANES)):
        for kb in range((LRU_WIDTH + S5_WIDTH) // LANES):
            halves = []
            for m in (2 * m2, 2 * m2 + 1):
                b, r0 = divmod(m * SUBLANES, tt)
                halves.append(yp_scr[kb, pl.ds(r0 * SUBLANES + b, SUBLANES, stride=SUBLANES), :])
            ycat_scr[RET_WIDTH // LANES + kb, pl.ds(m2 * 2 * SUBLANES, 2 * SUBLANES), :] = (
                jnp.concatenate(halves, axis=0).astype(BF16))

    ycat = jnp.concatenate([ycat_scr[kb] for kb in range(ycat_scr.shape[0])], axis=1)
    out = jnp.dot(ycat, wout_ref[...], preferred_element_type=F32)
    for b in range(nb):
        x_new = x_ref[b] + mod_ref[b, 2:3, :] * out[b * tt:(b + 1) * tt, :]
        if final:
            x_new = _rmsnorm(x_new, fg_ref[...])
        y_ref[b] = x_new


def _prompt_layer_bm(x, mod, w, tabs, layer, *, final):
    bsz, seq, d = x.shape
    tt = PROMPT_T
    tl = bsz * tt
    nt = seq // tt

    def full(a):
        return pl.BlockSpec(a.shape, lambda t: (0,) * a.ndim)

    def lay(a):
        return pl.BlockSpec((1,) + a.shape[1:], lambda t: (layer,) + (0,) * (a.ndim - 1))

    row_tab = pl.BlockSpec((tt, HEAD_DIM), lambda t: (t, 0))
    in_specs = [
        pl.BlockSpec((bsz, tt, d), lambda t: (0, t, 0)),
        full(mod),
        lay(w["norm_g"]), lay(w["w_in"]), lay(w["w_out"]),
        row_tab, row_tab, row_tab, row_tab,
        full(tabs["intra"]), full(tabs["q_dec"]), full(tabs["k_dec"]),
        pl.BlockSpec(memory_space=pltpu.SMEM),
        lay(w["gn_g"]), lay(w["conv_w"]), lay(w["conv_b"]), lay(w["w_gate"]), lay(w["b_gate"]),
        lay(w["lam"]), lay(w["bd"]), lay(w["cd"]), lay(w["apr"]), lay(w["api"]),
        lay(w["s5_d"]), lay(w["w_glu"]), full(w["final_g"]),
    ]
    out_shape = (
        jax.ShapeDtypeStruct((bsz, seq, d), F32),
        jax.ShapeDtypeStruct((bsz, RET_HEADS, HEAD_DIM, HEAD_DIM), F32),
        jax.ShapeDtypeStruct((bsz, LRU_WIDTH), F32),
        jax.ShapeDtypeStruct((CONV_WIDTH - 1, bsz, LRU_WIDTH), F32),
        jax.ShapeDtypeStruct((bsz, S5_FLAT), F32),
        jax.ShapeDtypeStruct((bsz, S5_FLAT), F32),
    )
    out_specs = (
        pl.BlockSpec((bsz, tt, d), lambda t: (0, t, 0)),
        pl.BlockSpec((bsz, RET_HEADS, HEAD_DIM, HEAD_DIM), lambda t: (0, 0, 0, 0)),
        pl.BlockSpec((bsz, LRU_WIDTH), lambda t: (0, 0)),
        pl.BlockSpec((CONV_WIDTH - 1, bsz, LRU_WIDTH), lambda t: (0, 0, 0)),
        pl.BlockSpec((bsz, S5_FLAT), lambda t: (0, 0)),
        pl.BlockSpec((bsz, S5_FLAT), lambda t: (0, 0)),
    )
    scratch = [
        pltpu.VMEM((OFF_XL // LANES, tl, LANES), F32),
        pltpu.VMEM((d // LANES, tl, LANES), BF16),
        pltpu.VMEM(((D_IN - OFF_XL) // LANES, tl, LANES), F32),
        pltpu.VMEM((tl + (CONV_WIDTH - 1) * SUBLANES, LRU_WIDTH), F32),
        pltpu.VMEM((2, tl, S5_FLAT), F32),
        pltpu.VMEM((2, tl, S5_FLAT), BF16),
        pltpu.VMEM(((LRU_WIDTH + S5_WIDTH) // LANES, tl, LANES), F32),
        pltpu.VMEM((tl, d), BF16),
    ]
    return pl.pallas_call(
        functools.partial(_prompt_kernel_bm, final=final),
        grid=(nt,),
        in_specs=in_specs,
        out_specs=out_specs,
        out_shape=out_shape,
        scratch_shapes=scratch,
        compiler_params=pltpu.CompilerParams(
            dimension_semantics=("arbitrary",), vmem_limit_bytes=VMEM_LIMIT),
        name="prompt_layer",
    )(x, mod, w["norm_g"], w["w_in"], w["w_out"],
      tabs["cos_q"], tabs["sin_q"], tabs["cos_k"], tabs["sin_k"],
      tabs["intra"], tabs["q_dec"], tabs["k_dec"], tabs["c_dec"],
      w["gn_g"], w["conv_w"], w["conv_b"], w["w_gate"], w["b_gate"], w["lam"],
      w["bd"], w["cd"], w["apr"], w["api"], w["s5_d"], w["w_glu"], w["final_g"])


def _decode_kernel(x_ref, mod_ref, ng_ref, win_ref, wout_ref,
                   cq_ref, sq_ref, ck_ref, sk_ref, gam_ref,
                   gng_ref, cw_ref, cb_ref, wg_ref, bg_ref, lam_ref,
                   bd_ref, cd_ref, apr_ref, api_ref, s5d_ref, wglu_ref, fg_ref,
                   sret_in, hl_in, cv_in, sre_in, sim_in, sel_ref,
                   y_ref, sret_ref, hl_ref, cv_ref, sre_ref, sim_ref,
                   qkt_scr, v_scr, o_scr, gr_scr, ycat_scr, x_scr):
    layer = pl.program_id(0)
    i = pl.program_id(1)
    nb = pl.num_programs(1)
    rows_n, d = x_ref.shape
    (mod_ref, ng_ref, win_ref, wout_ref, gng_ref, cw_ref, cb_ref, wg_ref, bg_ref, lam_ref,
     bd_ref, cd_ref, apr_ref, api_ref, s5d_ref, wglu_ref,
     sret_in, hl_in, cv_in, sre_in, sim_in, sret_ref, hl_ref, cv_ref, sre_ref, sim_ref) = (
        r.at[0] for r in (mod_ref, ng_ref, win_ref, wout_ref, gng_ref, cw_ref, cb_ref, wg_ref,
                          bg_ref, lam_ref, bd_ref, cd_ref, apr_ref, api_ref, s5d_ref, wglu_ref,
                          sret_in, hl_in, cv_in, sre_in, sim_in,
                          sret_ref, hl_ref, cv_ref, sre_ref, sim_ref))

    @pl.when((i == 0) & (layer == 0))
    def _():
        x_scr[...] = x_ref[...]

    @pl.when(i == 0)
    def _():
        x = x_scr[...]
        shift, scale = mod_ref[:, pl.ds(0, d)], mod_ref[:, pl.ds(d, d)]
        h = _rmsnorm(x, ng_ref[...]) * (1.0 + scale) + shift
        proj = _dot(h, win_ref[...])
        cq, sq, ck, sk = cq_ref[...], sq_ref[...], ck_ref[...], sk_ref[...]
        for hd in range(RET_HEADS):
            lo = hd * HEAD_DIM
            for part, src in enumerate((_rotary(proj[:, OFF_Q + lo:OFF_Q + lo + HEAD_DIM], cq, sq),
                                        _rotary(proj[:, OFF_K + lo:OFF_K + lo + HEAD_DIM], ck, sk))):
                xt = src.T
                hi = xt.astype(BF16)
                trows = pl.ds((2 * hd + part) * HEAD_DIM, HEAD_DIM)
                qkt_scr[trows, pl.ds(0, rows_n)] = hi
                qkt_scr[trows, pl.ds(rows_n, rows_n)] = (xt - hi.astype(F32)).astype(BF16)
        v_scr[...] = proj[:, OFF_V:OFF_V + RET_WIDTH]
        gr_scr[...] = _silu(proj[:, OFF_GR:OFF_GR + RET_WIDTH])
        xl = proj[:, OFF_XL:OFF_XL + LRU_WIDTH]
        cs = cv_in[...]
        xc = cb_ref[...] + xl * cw_ref[pl.ds(CONV_WIDTH - 1, 1), :]
        for j in range(CONV_WIDTH - 1):
            xc = xc + cs[:, j * LRU_WIDTH:(j + 1) * LRU_WIDTH] * cw_ref[pl.ds(j, 1), :]
        cv_ref[:, pl.ds(0, 2 * LRU_WIDTH)] = cs[:, LRU_WIDTH:]
        cv_ref[:, pl.ds(2 * LRU_WIDTH, LRU_WIDTH)] = xl
        gates = _dot(xc, wg_ref[...]) + bg_ref[...]
        a, b = _lru_coeffs(xc, gates, _softplus(-lam_ref[...]))
        hh = b + a * hl_in[...]
        hl_ref[...] = hh
        ycat_scr[:, pl.ds(RET_WIDTH, LRU_WIDTH)] = (
            hh * _silu(proj[:, OFF_GL:OFF_GL + LRU_WIDTH])).astype(BF16)
        u = proj[:, OFF_U:OFF_U + S5_WIDTH]
        bu = _dot(u, bd_ref[...])
        ar, ai = apr_ref[pl.ds(0, 1), :], api_ref[pl.ds(0, 1), :]
        s0r, s0i = sre_in[...], sim_in[...]
        sr = bu[:, :S5_FLAT] + (ar * s0r - ai * s0i)
        si = bu[:, S5_FLAT:] + (ar * s0i + ai * s0r)
        sre_ref[...] = sr
        sim_ref[...] = si
        y_s5 = _s5_tail(jnp.concatenate([sr, si], axis=-1), u, proj[:, OFF_GS:OFF_GS + S5_WIDTH],
                        cd_ref[...], s5d_ref[...], wglu_ref[...])
        ycat_scr[:, pl.ds(RET_WIDTH + LRU_WIDTH, S5_WIDTH)] = y_s5.astype(BF16)

    blk_rows = pl.ds(pl.multiple_of(i * DEC_BLOCK, DEC_BLOCK), DEC_BLOCK)
    o_rows = [[] for _ in range(RET_HEADS)]
    vblks = [v_scr[blk_rows, pl.ds(hd * HEAD_DIM, HEAD_DIM)] for hd in range(RET_HEADS)]
    for jp in range(DEC_BLOCK // 2):
        sel = sel_ref[0, :, pl.ds(jp * 2 * HEAD_DIM, 2 * HEAD_DIM)]
        tiles = jnp.dot(qkt_scr[...], jnp.concatenate([sel, sel], axis=0),
                        preferred_element_type=F32)
        for hd in range(RET_HEADS):
            qb = tiles[2 * hd * HEAD_DIM:(2 * hd + 1) * HEAD_DIM, :]
            kb = tiles[(2 * hd + 1) * HEAD_DIM:(2 * hd + 2) * HEAD_DIM, :]
            for j in (2 * jp, 2 * jp + 1):
                cols = slice((j % 2) * HEAD_DIM, (j % 2 + 1) * HEAD_DIM)
                s_new = gam_ref[hd] * sret_in[j, hd] + kb[:, cols] * vblks[hd][j:j + 1, :]
                sret_ref[j, hd] = s_new
                o_rows[hd].append(jnp.sum(qb[:, cols] * s_new, axis=0, keepdims=True))
    for hd in range(RET_HEADS):
        o_scr[blk_rows, pl.ds(hd * HEAD_DIM, HEAD_DIM)] = jnp.concatenate(o_rows[hd], axis=0)

    @pl.when(i == nb - 1)
    def _():
        for hd in range(RET_HEADS):
            lanes = pl.ds(hd * HEAD_DIM, HEAD_DIM)
            on = _groupnorm(o_scr[:, lanes], gng_ref[:, lanes])
            ycat_scr[:, lanes] = (on * gr_scr[:, lanes]).astype(BF16)
        out = jnp.dot(ycat_scr[...], wout_ref[...], preferred_element_type=F32)
        x_new = x_scr[...] + mod_ref[:, pl.ds(2 * d, d)] * out
        x_scr[...] = x_new
        y_ref[...] = _rmsnorm(x_new, fg_ref[...])


def _decode_layers(x, mod, states, w, tabs):
    rows, d = x.shape
    depth = mod.shape[0]
    s_ret, s_h, s_conv, s_re, s_im = states
    nb = rows // DEC_BLOCK

    def full(a):
        return pl.BlockSpec(a.shape, lambda l, i: (0,) * a.ndim)

    def lay(a):
        return pl.BlockSpec((1,) + a.shape[1:], lambda l, i: (l,) + (0,) * (a.ndim - 1))

    smem = pl.BlockSpec(memory_space=pltpu.SMEM)
    sblk = pl.BlockSpec((1, DEC_BLOCK, RET_HEADS, HEAD_DIM, HEAD_DIM), lambda l, i: (l, i, 0, 0, 0))
    in_specs = [
        full(x), lay(mod), lay(w["norm_g"]), lay(w["w_in"]), lay(w["w_out"]),
        full(tabs["dcos_q"]), full(tabs["dsin_q"]), full(tabs["dcos_k"]), full(tabs["dsin_k"]), smem,
        lay(w["gn_g"]), lay(w["conv_w"]), lay(w["conv_b"]), lay(w["w_gate"]), lay(w["b_gate"]),
        lay(w["lam"]), lay(w["bd"]), lay(w["cd"]), lay(w["apr"]), lay(w["api"]),
        lay(w["s5_d"]), lay(w["w_glu"]), full(w["final_g"]),
        sblk, lay(s_h), lay(s_conv), lay(s_re), lay(s_im),
        pl.BlockSpec((1,) + tabs["row_select"].shape[1:], lambda l, i: (i, 0, 0)),
    ]
    out_shape = (
        jax.ShapeDtypeStruct((rows, d), F32),
        jax.ShapeDtypeStruct(s_ret.shape, F32),
        jax.ShapeDtypeStruct(s_h.shape, F32),
        jax.ShapeDtypeStruct(s_conv.shape, F32),
        jax.ShapeDtypeStruct(s_re.shape, F32),
        jax.ShapeDtypeStruct(s_im.shape, F32),
    )
    out_specs = (full(x), sblk, lay(s_h), lay(s_conv), lay(s_re), lay(s_im))
    scratch = [
        pltpu.VMEM((RET_HEADS * 2 * HEAD_DIM, 2 * rows), BF16),
        pltpu.VMEM((rows, RET_WIDTH), F32),
        pltpu.VMEM((rows, RET_WIDTH), F32),
        pltpu.VMEM((rows, RET_WIDTH), F32),
        pltpu.VMEM((rows, d), BF16),
        pltpu.VMEM((rows, d), F32),
    ]
    return pl.pallas_call(
        _decode_kernel,
        grid=(depth, nb),
        in_specs=in_specs,
        out_specs=out_specs,
        out_shape=out_shape,
        scratch_shapes=scratch,
        compiler_params=pltpu.CompilerParams(
            dimension_semantics=("arbitrary", "arbitrary"), vmem_limit_bytes=VMEM_LIMIT),
        name="decode_layers",
    )(x, mod, w["norm_g"], w["w_in"], w["w_out"],
      tabs["dcos_q"], tabs["dsin_q"], tabs["dcos_k"], tabs["dsin_k"], tabs["gamma"],
      w["gn_g"], w["conv_w"], w["conv_b"], w["w_gate"], w["b_gate"], w["lam"],
      w["bd"], w["cd"], w["apr"], w["api"], w["s5_d"], w["w_glu"], w["final_g"],
      s_ret, s_h, s_conv, s_re, s_im, tabs["row_select"])


def _rope_tables(pos, scale):
    half = HEAD_DIM // 2
    inv = ROPE_BASE ** (-np.arange(half, dtype=np.float64) / half)
    ang = pos[:, None] * inv[None, :]
    cos, sin = np.cos(ang), np.sin(ang)
    cosf = np.concatenate([cos, cos], axis=-1) * scale
    sinf = np.concatenate([-sin, sin], axis=-1) * scale
    return cosf, sinf


def _tables(seq, dec, c):
    log_g = np.log1p(-np.exp2(-5.0 - np.arange(RET_HEADS, dtype=np.float64)))
    idx = np.arange(c, dtype=np.float64)
    rel = idx[:, None] - idx[None, :]
    intra = np.where(rel[None] >= 0, np.exp(np.maximum(rel, 0.0)[None] * log_g[:, None, None]), 0.0)
    q_dec = np.exp((idx + 1.0)[:, None] * log_g[None, :])
    k_dec = np.exp((c - 1.0 - idx)[:, None] * log_g[None, :])
    q_dec = np.broadcast_to(q_dec.T[:, :, None], (RET_HEADS, c, HEAD_DIM))
    k_dec = np.broadcast_to(k_dec.T[:, :, None], (RET_HEADS, c, HEAD_DIM))
    c_dec = np.exp(c * log_g)
    gamma = np.exp(log_g)
    k_scale = HEAD_DIM ** -0.5
    cos_q, sin_q = _rope_tables(np.arange(seq, dtype=np.float64), 1.0)
    cos_k, sin_k = _rope_tables(np.arange(seq, dtype=np.float64), k_scale)
    dpos = PAST_LEN + np.arange(1, dtype=np.float64)
    dcos_q, dsin_q = _rope_tables(dpos, 1.0)
    dcos_k, dsin_k = _rope_tables(dpos, k_scale)
    blk = np.arange(dec // DEC_BLOCK)[:, None, None]
    row = np.arange(dec)[None, :, None]
    col = np.arange(DEC_BLOCK * HEAD_DIM)[None, None, :]
    row_select = row == blk * DEC_BLOCK + col // HEAD_DIM
    tabs = dict(intra=intra, q_dec=q_dec, k_dec=k_dec, c_dec=c_dec, gamma=gamma,
                cos_q=cos_q, sin_q=sin_q, cos_k=cos_k, sin_k=sin_k,
                dcos_q=dcos_q, dsin_q=dsin_q, dcos_k=dcos_k, dsin_k=dsin_k)
    tabs = {k: jnp.asarray(np.ascontiguousarray(v), dtype=F32) for k, v in tabs.items()}
    tabs["row_select"] = jnp.asarray(row_select, dtype=BF16)
    return tabs


def _block_diag(blocks):
    n, r, c = blocks.shape
    eye = jnp.eye(n, dtype=blocks.dtype)
    return jnp.einsum("nrc,nm->nrmc", blocks, eye).reshape(n * r, n * c)


def kernel(x_prompt, x_sample, state_ret, state_lru_h, state_lru_conv, state_s5_re, state_s5_im,
           c_prompt, c_sample, norm_g, w_ada, b_ada, w_in, ret_gn_g, conv_w, conv_b, w_rg, b_rg,
           w_ig, b_ig, lru_lambda, s5_a_re, s5_a_im, s5_b_re, s5_b_im, s5_c_re, s5_c_im, s5_d,
           s5_log_dt, s5_w_glu, w_out, final_g):
    depth = w_in.shape[0]
    bsz, seq, d = x_prompt.shape
    dec = x_sample.shape[0]
    assert x_sample.shape[1] == 1 and dec % DEC_BLOCK == 0

    assert bsz == SUBLANES and seq % PROMPT_T == 0
    tabs = _tables(seq, dec, PROMPT_T)
    mod_all = _ada(jnp.concatenate([c_prompt, c_sample], axis=0), w_ada, b_ada)
    n_pow = SUBLANES
    apr, api, bbr, bbi = _s5_prep(s5_a_re, s5_a_im, s5_log_dt, s5_b_re, s5_b_im, n_pow)

    bdiag = jax.vmap(_block_diag)
    bd = jnp.concatenate([bdiag(bbr), bdiag(bbi)], axis=2)
    cd = jnp.concatenate([bdiag(jnp.swapaxes(s5_c_re, 2, 3)),
                          bdiag(jnp.swapaxes(-s5_c_im, 2, 3))], axis=1)
    w = dict(
        norm_g=norm_g.reshape(depth, 1, d),
        w_in=w_in.astype(BF16),
        w_out=w_out.astype(BF16),
        gn_g=ret_gn_g.reshape(depth, 1, RET_WIDTH),
        conv_w=conv_w,
        conv_b=conv_b.reshape(depth, 1, LRU_WIDTH),
        w_gate=jnp.concatenate([bdiag(w_rg), bdiag(w_ig)], axis=2).astype(BF16),
        b_gate=jnp.concatenate([b_rg, b_ig], axis=1).reshape(depth, 1, 2 * LRU_WIDTH),
        lam=lru_lambda.reshape(depth, 1, LRU_WIDTH),
        bd=bd.astype(BF16),
        cd=cd.astype(BF16),
        apr=apr.reshape(depth, n_pow, S5_FLAT),
        api=api.reshape(depth, n_pow, S5_FLAT),
        s5_d=s5_d.reshape(depth, 1, S5_WIDTH),
        w_glu=s5_w_glu.astype(BF16),
        final_g=final_g.reshape(1, d),
    )

    xp = x_prompt
    outs_p = []
    for l in range(depth):
        mod_p = mod_all[l, :bsz].reshape(bsz, 3, d)
        xp, sret, hl, cv, sre, sim = _prompt_layer_bm(xp, mod_p, w, tabs, l, final=l == depth - 1)
        outs_p.append((sret, hl, jnp.swapaxes(cv, 0, 1),
                       sre.reshape(bsz, S5_GROUPS, S5_STATE), sim.reshape(bsz, S5_GROUPS, S5_STATE)))

    states = (state_ret, state_lru_h,
              state_lru_conv.reshape(depth, dec, (CONV_WIDTH - 1) * LRU_WIDTH),
              state_s5_re.reshape(depth, dec, S5_FLAT), state_s5_im.reshape(depth, dec, S5_FLAT))
    xs, sret_s, hl_s, cv_s, sre_s, sim_s = _decode_layers(
        x_sample.reshape(dec, d), mod_all[:, bsz:], states, w, tabs)

    def stk(k):
        return jnp.stack([o[k] for o in outs_p])

    return (xp, xs.reshape(dec, 1, d),
            stk(0), sret_s, stk(1), hl_s,
            stk(2), cv_s.reshape(depth, dec, CONV_WIDTH - 1, LRU_WIDTH),
            stk(3), sre_s.reshape(depth, dec, S5_GROUPS, S5_STATE),
            stk(4), sim_s.reshape(depth, dec, S5_GROUPS, S5_STATE))
```

```python
import functools
import math

import jax
import jax.numpy as jnp
import numpy as np
from jax import lax
from jax.experimental import pallas as pl
from jax.experimental.pallas import tpu as pltpu

RET_HEADS = 4
HEAD_DIM = 128
RET_WIDTH = RET_HEADS * HEAD_DIM
LRU_WIDTH = 256
LRU_C = 8.0
CONV_WIDTH = 4
S5_WIDTH = 256
S5_GROUP = 16
S5_GROUPS = 16
S5_STATE = 64
S5_FLAT = S5_GROUPS * S5_STATE
ROPE_BASE = 10000.0
EPS = 1e-6
PAST_LEN = 16384

OFF_Q = 0
OFF_K = OFF_Q + RET_WIDTH
OFF_V = OFF_K + RET_WIDTH
OFF_GR = OFF_V + RET_WIDTH
OFF_XL = OFF_GR + RET_WIDTH
OFF_GL = OFF_XL + LRU_WIDTH
OFF_U = OFF_GL + LRU_WIDTH
OFF_GS = OFF_U + S5_WIDTH
D_IN = OFF_GS + S5_WIDTH

SUBLANES = 8
LANES = 128
PROMPT_T = 64
PROJ_CHUNK = 256
DEC_BLOCK = 16
VMEM_LIMIT = 56 * 1024 * 1024

F32 = jnp.float32
BF16 = jnp.bfloat16


def _sigmoid(x):
    return 1.0 / (1.0 + jnp.exp(-x))


def _silu(x):
    return x * _sigmoid(x)


def _gelu_tanh(x):
    c = math.sqrt(2.0 / math.pi)
    return x * (0.5 * (1.0 + jnp.tanh(c * (x + 0.044715 * (x * x * x)))))


def _softplus(x):
    return jnp.maximum(x, 0.0) + jnp.log1p(jnp.exp(-jnp.abs(x)))


def _rmsnorm(x, g):
    ms = jnp.mean(x * x, axis=-1, keepdims=True)
    return x * lax.rsqrt(ms + EPS) * g


def _dot(a, b):
    return jnp.dot(a.astype(BF16), b.astype(BF16), preferred_element_type=F32)


def _dot_nt(a, b):
    return lax.dot_general(a.astype(BF16), b.astype(BF16), (((1,), (1,)), ((), ())),
                           preferred_element_type=F32)


def _rotary(x, cosf, sinf):
    return x * cosf + pltpu.roll(x, HEAD_DIM // 2, axis=1) * sinf


def _groupnorm(o, g):
    mu = jnp.mean(o, axis=-1, keepdims=True)
    d = o - mu
    var = jnp.mean(d * d, axis=-1, keepdims=True)
    return d * lax.rsqrt(var + EPS) * g


def _lru_coeffs(xc, gates, sp):
    r = _sigmoid(gates[:, :LRU_WIDTH])
    ig = _sigmoid(gates[:, LRU_WIDTH:])
    log_a = (-LRU_C) * r * sp
    a = jnp.exp(log_a)
    th = jnp.tanh(log_a)
    mult = jnp.sqrt(-2.0 * th / (1.0 - th))
    return a, mult * ig * xc


def _s5_tail(sr_si, u, gs, cd, s5d, wglu, row_parts=1):
    rows = sr_si.shape[0] // row_parts
    ys = jnp.concatenate([_dot(sr_si[p * rows:(p + 1) * rows], cd) for p in range(row_parts)],
                         axis=0) + s5d * u
    ys = _gelu_tanh(ys)
    ys = ys * _sigmoid(_dot(ys, wglu))
    return ys * _silu(gs)


def _s5_prep_kernel(are_ref, aim_ref, ldt_ref, bre_ref, bim_ref,
                    apr_ref, api_ref, bbr_ref, bbi_ref):
    depth = are_ref.shape[0]
    for l in range(depth):
        a_re = are_ref[l]
        a_im = aim_ref[l]
        step = jnp.exp(ldt_ref[l])
        mag = jnp.exp(step * a_re)
        abar_r = mag * jnp.cos(step * a_im)
        abar_i = mag * jnp.sin(step * a_im)
        nr, ni = abar_r - 1.0, abar_i
        den = a_re * a_re + a_im * a_im
        fr = (nr * a_re + ni * a_im) / den
        fi = (ni * a_re - nr * a_im) / den
        for g in range(S5_GROUPS):
            frg, fig = fr[g:g + 1, :], fi[g:g + 1, :]
            b_re, b_im = bre_ref[l, g], bim_ref[l, g]
            bbr_ref[l, g] = frg * b_re - fig * b_im
            bbi_ref[l, g] = frg * b_im + fig * b_re
        pr, pi = abar_r, abar_i
        apr_ref[l, 0] = pr
        api_ref[l, 0] = pi
        for m in range(1, apr_ref.shape[1]):
            pr, pi = pr * abar_r - pi * abar_i, pr * abar_i + pi * abar_r
            apr_ref[l, m] = pr
            api_ref[l, m] = pi


def _s5_prep(s5_a_re, s5_a_im, s5_log_dt, s5_b_re, s5_b_im, n_pow):
    depth = s5_a_re.shape[0]
    b_re_t = jnp.swapaxes(s5_b_re, 2, 3)
    b_im_t = jnp.swapaxes(s5_b_im, 2, 3)
    ldt = s5_log_dt.reshape(depth, S5_GROUPS, 1)
    out_shape = (
        jax.ShapeDtypeStruct((depth, n_pow, S5_GROUPS, S5_STATE), F32),
        jax.ShapeDtypeStruct((depth, n_pow, S5_GROUPS, S5_STATE), F32),
        jax.ShapeDtypeStruct((depth, S5_GROUPS, S5_GROUP, S5_STATE), F32),
        jax.ShapeDtypeStruct((depth, S5_GROUPS, S5_GROUP, S5_STATE), F32),
    )
    return pl.pallas_call(_s5_prep_kernel, out_shape=out_shape, name="s5_prep")(
        s5_a_re, s5_a_im, ldt, b_re_t, b_im_t)


def _ada_kernel(c_ref, w_ref, b_ref, o_ref):
    s = _silu(c_ref[...])
    o_ref[0] = _dot(s, w_ref[0]) + b_ref[0]


def _ada(c_all, w_ada, b_ada):
    depth, d, n3 = w_ada.shape
    rows = c_all.shape[0]
    tn = d
    return pl.pallas_call(
        _ada_kernel,
        grid=(depth, n3 // tn),
        in_specs=[
            pl.BlockSpec((rows, d), lambda l, j: (0, 0)),
            pl.BlockSpec((1, d, tn), lambda l, j: (l, 0, j)),
            pl.BlockSpec((1, 1, tn), lambda l, j: (l, 0, j)),
        ],
        out_specs=pl.BlockSpec((1, rows, tn), lambda l, j: (l, 0, j)),
        out_shape=jax.ShapeDtypeStruct((depth, rows, n3), F32),
        compiler_params=pltpu.CompilerParams(
            dimension_semantics=("arbitrary", "arbitrary"), vmem_limit_bytes=VMEM_LIMIT),
        name="adaln",
    )(c_all, w_ada, b_ada.reshape(depth, 1, n3))


def _prompt_kernel_bm(x_ref, mod_ref, ng_ref, win_ref, wout_ref,
                      cq_ref, sq_ref, ck_ref, sk_ref, intra_ref, qdec_ref, kdec_ref, cdec_ref,
                      gng_ref, cw_ref, cb_ref, wg_ref, bg_ref, lam_ref,
                      bd_ref, cd_ref, apr_ref, api_ref, s5d_ref, wglu_ref, fg_ref,
                      y_ref, sret_ref, hl_ref, cv_ref, sre_ref, sim_ref,
                      proj_scr, ycat_scr, perm_scr, xpad_scr, st_scr, st16_scr, yp_scr, h_scr,
                      *, final):
    nb, tt, dm = x_ref.shape
    tl = nb * tt
    t = pl.program_id(0)
    npre = CONV_WIDTH - 1
    (ng_ref, win_ref, wout_ref, gng_ref, cw_ref, cb_ref, wg_ref, bg_ref, lam_ref,
     bd_ref, cd_ref, apr_ref, api_ref, s5d_ref, wglu_ref) = (
        r.at[0] for r in (ng_ref, win_ref, wout_ref, gng_ref, cw_ref, cb_ref, wg_ref, bg_ref,
                          lam_ref, bd_ref, cd_ref, apr_ref, api_ref, s5d_ref, wglu_ref))

    @pl.when(t == 0)
    def _():
        sret_ref[...] = jnp.zeros_like(sret_ref)
        hl_ref[...] = jnp.zeros_like(hl_ref)
        sre_ref[...] = jnp.zeros_like(sre_ref)
        sim_ref[...] = jnp.zeros_like(sim_ref)
        xpad_scr[pl.ds(0, npre * SUBLANES), :] = jnp.zeros((npre * SUBLANES, LRU_WIDTH), F32)

    ng = ng_ref[...]
    for b in range(nb):
        hb = (_rmsnorm(x_ref[b], ng * (1.0 + mod_ref[0, b:b + 1, pl.ds(dm, dm)]))
              + mod_ref[0, b:b + 1, pl.ds(0, dm)])
        h_scr[pl.ds(b * tt, tt), :] = hb.astype(BF16)
    proj_b = jnp.dot(h_scr[...], win_ref[:, pl.ds(OFF_XL, D_IN - OFF_XL)],
                     preferred_element_type=F32)

    def proj_a_chunk(ci):
        cols = pl.ds(ci * PROJ_CHUNK, PROJ_CHUNK)
        res = jnp.dot(h_scr[...], win_ref[:, cols], preferred_element_type=F32)
        for kl in range(PROJ_CHUNK // LANES):
            proj_scr[ci * (PROJ_CHUNK // LANES) + kl] = res[:, kl * LANES:(kl + 1) * LANES]

    for m in range(tl // SUBLANES):
        b, r0 = divmod(m * SUBLANES, tt)
        for kb in range((D_IN - OFF_XL) // LANES):
            perm_scr[kb, pl.ds(r0 * SUBLANES + b, SUBLANES, stride=SUBLANES), :] = (
                proj_b[m * SUBLANES:(m + 1) * SUBLANES, kb * LANES:(kb + 1) * LANES])

    def ret_piece(b, hd):
        rows = pl.ds(b * tt, tt)
        lanes = pl.ds(hd * HEAD_DIM, HEAD_DIM)
        q = _rotary(proj_scr[OFF_Q // LANES + hd, rows, :], cq_ref[...], sq_ref[...])
        k = _rotary(proj_scr[OFF_K // LANES + hd, rows, :], ck_ref[...], sk_ref[...])
        v = proj_scr[OFF_V // LANES + hd, rows, :].astype(BF16)
        s_old = sret_ref[b, hd]
        sc = _dot_nt(q, k) * intra_ref[hd]
        o = _dot(sc, v) + _dot(q * qdec_ref[hd], s_old)
        kd_t = (k * kdec_ref[hd]).T
        sret_ref[b, hd] = cdec_ref[hd] * s_old + _dot(kd_t, v)
        on = _groupnorm(o, gng_ref[:, lanes])
        g_ret = proj_scr[OFF_GR // LANES + hd, rows, :]
        ycat_scr[hd, rows, :] = (on * _silu(g_ret)).astype(BF16)

    rg = lambda r: pl.ds(r * SUBLANES, SUBLANES)

    def pcols(lo, width):
        return jnp.concatenate([perm_scr[kb] for kb in range(lo // LANES, (lo + width) // LANES)],
                               axis=1)

    proj_a_chunk(0)
    proj_a_chunk(1)
    xl = pcols(OFF_XL - OFF_XL, LRU_WIDTH)
    xpad_scr[pl.ds(npre * SUBLANES, tl), :] = xl
    xc = cb_ref[...]
    for j in range(CONV_WIDTH):
        xc = xc + xpad_scr[pl.ds(j * SUBLANES, tl), :] * cw_ref[pl.ds(j, 1), :]
    for j in range(npre):
        tail = xl[(tt - npre + j) * SUBLANES:(tt - npre + j + 1) * SUBLANES, :]
        xpad_scr[rg(j), :] = tail
        cv_ref[j] = tail
    gates = _dot(xc, wg_ref[...]) + bg_ref[...]
    a, bb = _lru_coeffs(xc, gates, _softplus(-lam_ref[...]))
    proj_a_chunk(2)
    proj_a_chunk(3)
    sgl = _silu(pcols(OFF_GL - OFF_XL, LRU_WIDTH))
    hl = hl_ref[...]
    for r in range(tt):
        hl = a[r * SUBLANES:(r + 1) * SUBLANES, :] * hl + bb[r * SUBLANES:(r + 1) * SUBLANES, :]
        y = hl * sgl[r * SUBLANES:(r + 1) * SUBLANES, :]
        for kb in range(LRU_WIDTH // LANES):
            yp_scr[kb, rg(r), :] = y[:, kb * LANES:(kb + 1) * LANES]
    hl_ref[...] = hl

    proj_a_chunk(4)
    proj_a_chunk(5)
    u = pcols(OFF_U - OFF_XL, S5_WIDTH)
    bu = _dot(u, bd_ref[...])
    st_scr[0] = bu[:, :S5_FLAT]
    st_scr[1] = bu[:, S5_FLAT:]
    proj_a_chunk(6)
    proj_a_chunk(7)
    bshape = (SUBLANES, S5_FLAT)
    a1r = jnp.broadcast_to(apr_ref[pl.ds(0, 1), :], bshape)
    a1i = jnp.broadcast_to(api_ref[pl.ds(0, 1), :], bshape)
    hr, hi = sre_ref[...], sim_ref[...]
    n_pieces = nb * RET_HEADS
    per_piece = (tt // 2) // n_pieces
    for r2 in range(tt // 2):
        if r2 % per_piece == 0:
            ret_piece(*divmod(r2 // per_piece, RET_HEADS))
        parts_r, parts_i = [], []
        for r in (2 * r2, 2 * r2 + 1):
            hr, hi = (a1r * hr - a1i * hi + st_scr[0, rg(r), :],
                      a1r * hi + a1i * hr + st_scr[1, rg(r), :])
            parts_r.append(hr)
            parts_i.append(hi)
        rows2 = pl.ds(r2 * 2 * SUBLANES, 2 * SUBLANES)
        st16_scr[0, rows2, :] = jnp.concatenate(parts_r, axis=0).astype(BF16)
        st16_scr[1, rows2, :] = jnp.concatenate(parts_i, axis=0).astype(BF16)
    sre_ref[...] = hr
    sim_ref[...] = hi
    gs = pcols(OFF_GS - OFF_XL, S5_WIDTH)
    y_s5 = _s5_tail(jnp.concatenate([st16_scr[0], st16_scr[1]], axis=1), u, gs,
                    cd_ref[...], s5d_ref[...], wglu_ref[...], row_parts=2)
    for kb in range(S5_WIDTH // LANES):
        yp_scr[LRU_WIDTH // LANES + kb] = y_s5[:, kb * LANES:(kb + 1) * LANES]

    for m2 in range(tl // (2 * SUBLANES)):
        for kb in range((LRU_WIDTH + S5_WIDTH) // LANES):
            halves = []
            for m in (2 * m2, 2 * m2 + 1):
                b, r0 = divmod(m * SUBLANES, tt)
                halves.append(yp_scr[kb, pl.ds(r0 * SUBLANES + b, SUBLANES, stride=SUBLANES), :])
            ycat_scr[RET_WIDTH // LANES + kb, pl.ds(m2 * 2 * SUBLANES, 2 * SUBLANES), :] = (
                jnp.concatenate(halves, axis=0).astype(BF16))

    ycat = jnp.concatenate([ycat_scr[kb] for kb in range(ycat_scr.shape[0])], axis=1)
    out = jnp.dot(ycat, wout_ref[...], preferred_element_type=F32)
    for b in range(nb):
        x_new = x_ref[b] + mod_ref[0, b:b + 1, pl.ds(2 * dm, dm)] * out[b * tt:(b + 1) * tt, :]
        if final:
            x_new = _rmsnorm(x_new, fg_ref[...])
        y_ref[b] = x_new


def _prompt_layer_bm(x, mod, w, tabs, layer, *, final):
    bsz, seq, d = x.shape
    tt = PROMPT_T
    tl = bsz * tt
    nt = seq // tt

    def full(a):
        return pl.BlockSpec(a.shape, lambda t: (0,) * a.ndim)

    def lay(a):
        return pl.BlockSpec((1,) + a.shape[1:], lambda t: (layer,) + (0,) * (a.ndim - 1))

    row_tab = pl.BlockSpec((tt, HEAD_DIM), lambda t: (t, 0))
    in_specs = [
        pl.BlockSpec((bsz, tt, d), lambda t: (0, t, 0)),
        pl.BlockSpec((1, bsz, mod.shape[2]), lambda t: (layer, 0, 0)),
        lay(w["norm_g"]), lay(w["w_in"]), lay(w["w_out"]),
        row_tab, row_tab, row_tab, row_tab,
        full(tabs["intra"]), full(tabs["q_dec"]), full(tabs["k_dec"]),
        pl.BlockSpec(memory_space=pltpu.SMEM),
        lay(w["gn_g"]), lay(w["conv_w"]), lay(w["conv_b"]), lay(w["w_gate"]), lay(w["b_gate"]),
        lay(w["lam"]), lay(w["bd"]), lay(w["cd"]), lay(w["apr"]), lay(w["api"]),
        lay(w["s5_d"]), lay(w["w_glu"]), full(w["final_g"]),
    ]
    out_shape = (
        jax.ShapeDtypeStruct((bsz, seq, d), F32),
        jax.ShapeDtypeStruct((bsz, RET_HEADS, HEAD_DIM, HEAD_DIM), F32),
        jax.ShapeDtypeStruct((bsz, LRU_WIDTH), F32),
        jax.ShapeDtypeStruct((CONV_WIDTH - 1, bsz, LRU_WIDTH), F32),
        jax.ShapeDtypeStruct((bsz, S5_FLAT), F32),
        jax.ShapeDtypeStruct((bsz, S5_FLAT), F32),
    )
    out_specs = (
        pl.BlockSpec((bsz, tt, d), lambda t: (0, t, 0)),
        pl.BlockSpec((bsz, RET_HEADS, HEAD_DIM, HEAD_DIM), lambda t: (0, 0, 0, 0)),
        pl.BlockSpec((bsz, LRU_WIDTH), lambda t: (0, 0)),
        pl.BlockSpec((CONV_WIDTH - 1, bsz, LRU_WIDTH), lambda t: (0, 0, 0)),
        pl.BlockSpec((bsz, S5_FLAT), lambda t: (0, 0)),
        pl.BlockSpec((bsz, S5_FLAT), lambda t: (0, 0)),
    )
    scratch = [
        pltpu.VMEM((OFF_XL // LANES, tl, LANES), F32),
        pltpu.VMEM((d // LANES, tl, LANES), BF16),
        pltpu.VMEM(((D_IN - OFF_XL) // LANES, tl, LANES), F32),
        pltpu.VMEM((tl + (CONV_WIDTH - 1) * SUBLANES, LRU_WIDTH), F32),
        pltpu.VMEM((2, tl, S5_FLAT), F32),
        pltpu.VMEM((2, tl, S5_FLAT), BF16),
        pltpu.VMEM(((LRU_WIDTH + S5_WIDTH) // LANES, tl, LANES), F32),
        pltpu.VMEM((tl, d), BF16),
    ]
    return pl.pallas_call(
        functools.partial(_prompt_kernel_bm, final=final),
        grid=(nt,),
        in_specs=in_specs,
        out_specs=out_specs,
        out_shape=out_shape,
        scratch_shapes=scratch,
        compiler_params=pltpu.CompilerParams(
            dimension_semantics=("arbitrary",), vmem_limit_bytes=VMEM_LIMIT),
        name="prompt_layer",
    )(x, mod, w["norm_g"], w["w_in"], w["w_out"],
      tabs["cos_q"], tabs["sin_q"], tabs["cos_k"], tabs["sin_k"],
      tabs["intra"], tabs["q_dec"], tabs["k_dec"], tabs["c_dec"],
      w["gn_g"], w["conv_w"], w["conv_b"], w["w_gate"], w["b_gate"], w["lam"],
      w["bd"], w["cd"], w["apr"], w["api"], w["s5_d"], w["w_glu"], w["final_g"])


def _decode_kernel(x_ref, mod_ref, ng_ref, win_ref, wout_ref,
                   cq_ref, sq_ref, ck_ref, sk_ref, gam_ref,
                   gng_ref, cw_ref, cb_ref, wg_ref, bg_ref, lam_ref,
                   bd_ref, cd_ref, apr_ref, api_ref, s5d_ref, wglu_ref, fg_ref,
                   sret_in, hl_in, cv_in, sre_in, sim_in, sel_ref,
                   y_ref, sret_ref, hl_ref, cv_ref, sre_ref, sim_ref,
                   qkt_scr, v_scr, o_scr, gr_scr, ycat_scr, x_scr):
    layer = pl.program_id(0)
    i = pl.program_id(1)
    nb = pl.num_programs(1)
    rows_n, d = x_ref.shape
    (mod_ref, ng_ref, win_ref, wout_ref, gng_ref, cw_ref, cb_ref, wg_ref, bg_ref, lam_ref,
     bd_ref, cd_ref, apr_ref, api_ref, s5d_ref, wglu_ref,
     sret_in, hl_in, cv_in, sre_in, sim_in, sret_ref, hl_ref, cv_ref, sre_ref, sim_ref) = (
        r.at[0] for r in (mod_ref, ng_ref, win_ref, wout_ref, gng_ref, cw_ref, cb_ref, wg_ref,
                          bg_ref, lam_ref, bd_ref, cd_ref, apr_ref, api_ref, s5d_ref, wglu_ref,
                          sret_in, hl_in, cv_in, sre_in, sim_in,
                          sret_ref, hl_ref, cv_ref, sre_ref, sim_ref))

    @pl.when((i == 0) & (layer == 0))
    def _():
        x_scr[...] = x_ref[...]

    @pl.when(i == 0)
    def _():
        x = x_scr[...]
        shift, scale = mod_ref[:, pl.ds(0, d)], mod_ref[:, pl.ds(d, d)]
        h = _rmsnorm(x, ng_ref[...]) * (1.0 + scale) + shift
        proj = _dot(h, win_ref[...])
        cq, sq, ck, sk = cq_ref[...], sq_ref[...], ck_ref[...], sk_ref[...]
        for hd in range(RET_HEADS):
            lo = hd * HEAD_DIM
            for part, src in enumerate((_rotary(proj[:, OFF_Q + lo:OFF_Q + lo + HEAD_DIM], cq, sq),
                                        _rotary(proj[:, OFF_K + lo:OFF_K + lo + HEAD_DIM], ck, sk))):
                xt = src.T
                hi = xt.astype(BF16)
                trows = pl.ds((2 * hd + part) * HEAD_DIM, HEAD_DIM)
                qkt_scr[trows, pl.ds(0, rows_n)] = hi
                qkt_scr[trows, pl.ds(rows_n, rows_n)] = (xt - hi.astype(F32)).astype(BF16)
        v_scr[...] = proj[:, OFF_V:OFF_V + RET_WIDTH]
        gr_scr[...] = _silu(proj[:, OFF_GR:OFF_GR + RET_WIDTH])
        xl = proj[:, OFF_XL:OFF_XL + LRU_WIDTH]
        cs = cv_in[...]
        xc = cb_ref[...] + xl * cw_ref[pl.ds(CONV_WIDTH - 1, 1), :]
        for j in range(CONV_WIDTH - 1):
            xc = xc + cs[:, j * LRU_WIDTH:(j + 1) * LRU_WIDTH] * cw_ref[pl.ds(j, 1), :]
        cv_ref[:, pl.ds(0, 2 * LRU_WIDTH)] = cs[:, LRU_WIDTH:]
        cv_ref[:, pl.ds(2 * LRU_WIDTH, LRU_WIDTH)] = xl
        gates = _dot(xc, wg_ref[...]) + bg_ref[...]
        a, b = _lru_coeffs(xc, gates, _softplus(-lam_ref[...]))
        hh = b + a * hl_in[...]
        hl_ref[...] = hh
        ycat_scr[:, pl.ds(RET_WIDTH, LRU_WIDTH)] = (
            hh * _silu(proj[:, OFF_GL:OFF_GL + LRU_WIDTH])).astype(BF16)
        u = proj[:, OFF_U:OFF_U + S5_WIDTH]
        bu = _dot(u, bd_ref[...])
        ar, ai = apr_ref[pl.ds(0, 1), :], api_ref[pl.ds(0, 1), :]
        s0r, s0i = sre_in[...], sim_in[...]
        sr = bu[:, :S5_FLAT] + (ar * s0r - ai * s0i)
        si = bu[:, S5_FLAT:] + (ar * s0i + ai * s0r)
        sre_ref[...] = sr
        sim_ref[...] = si
        y_s5 = _s5_tail(jnp.concatenate([sr, si], axis=-1), u, proj[:, OFF_GS:OFF_GS + S5_WIDTH],
                        cd_ref[...], s5d_ref[...], wglu_ref[...])
        ycat_scr[:, pl.ds(RET_WIDTH + LRU_WIDTH, S5_WIDTH)] = y_s5.astype(BF16)

    blk_rows = pl.ds(pl.multiple_of(i * DEC_BLOCK, DEC_BLOCK), DEC_BLOCK)
    o_rows = [[] for _ in range(RET_HEADS)]
    vblks = [v_scr[blk_rows, pl.ds(hd * HEAD_DIM, HEAD_DIM)] for hd in range(RET_HEADS)]
    for jp in range(DEC_BLOCK // 2):
        sel = sel_ref[0, :, pl.ds(jp * 2 * HEAD_DIM, 2 * HEAD_DIM)]
        tiles = jnp.dot(qkt_scr[...], jnp.concatenate([sel, sel], axis=0),
                        preferred_element_type=F32)
        for hd in range(RET_HEADS):
            qb = tiles[2 * hd * HEAD_DIM:(2 * hd + 1) * HEAD_DIM, :]
            kb = tiles[(2 * hd + 1) * HEAD_DIM:(2 * hd + 2) * HEAD_DIM, :]
            for j in (2 * jp, 2 * jp + 1):
                cols = slice((j % 2) * HEAD_DIM, (j % 2 + 1) * HEAD_DIM)
                s_new = gam_ref[hd] * sret_in[j, hd] + kb[:, cols] * vblks[hd][j:j + 1, :]
                sret_ref[j, hd] = s_new
                o_rows[hd].append(jnp.sum(qb[:, cols] * s_new, axis=0, keepdims=True))
    for hd in range(RET_HEADS):
        o_scr[blk_rows, pl.ds(hd * HEAD_DIM, HEAD_DIM)] = jnp.concatenate(o_rows[hd], axis=0)

    @pl.when(i == nb - 1)
    def _():
        for hd in range(RET_HEADS):
            lanes = pl.ds(hd * HEAD_DIM, HEAD_DIM)
            on = _groupnorm(o_scr[:, lanes], gng_ref[:, lanes])
            ycat_scr[:, lanes] = (on * gr_scr[:, lanes]).astype(BF16)
        out = jnp.dot(ycat_scr[...], wout_ref[...], preferred_element_type=F32)
        x_new = x_scr[...] + mod_ref[:, pl.ds(2 * d, d)] * out
        x_scr[...] = x_new
        y_ref[...] = _rmsnorm(x_new, fg_ref[...])


def _decode_layers(x, mod, states, w, tabs):
    rows, d = x.shape
    depth = mod.shape[0]
    s_ret, s_h, s_conv, s_re, s_im = states
    nb = rows // DEC_BLOCK

    def full(a):
        return pl.BlockSpec(a.shape, lambda l, i: (0,) * a.ndim)

    def lay(a):
        return pl.BlockSpec((1,) + a.shape[1:], lambda l, i: (l,) + (0,) * (a.ndim - 1))

    smem = pl.BlockSpec(memory_space=pltpu.SMEM)
    sblk = pl.BlockSpec((1, DEC_BLOCK, RET_HEADS, HEAD_DIM, HEAD_DIM), lambda l, i: (l, i, 0, 0, 0))
    in_specs = [
        full(x), lay(mod), lay(w["norm_g"]), lay(w["w_in"]), lay(w["w_out"]),
        full(tabs["dcos_q"]), full(tabs["dsin_q"]), full(tabs["dcos_k"]), full(tabs["dsin_k"]), smem,
        lay(w["gn_g"]), lay(w["conv_w"]), lay(w["conv_b"]), lay(w["w_gate"]), lay(w["b_gate"]),
        lay(w["lam"]), lay(w["bd"]), lay(w["cd"]), lay(w["apr"]), lay(w["api"]),
        lay(w["s5_d"]), lay(w["w_glu"]), full(w["final_g"]),
        sblk, lay(s_h), lay(s_conv), lay(s_re), lay(s_im),
        pl.BlockSpec((1,) + tabs["row_select"].shape[1:], lambda l, i: (i, 0, 0)),
    ]
    out_shape = (
        jax.ShapeDtypeStruct((rows, d), F32),
        jax.ShapeDtypeStruct(s_ret.shape, F32),
        jax.ShapeDtypeStruct(s_h.shape, F32),
        jax.ShapeDtypeStruct(s_conv.shape, F32),
        jax.ShapeDtypeStruct(s_re.shape, F32),
        jax.ShapeDtypeStruct(s_im.shape, F32),
    )
    out_specs = (full(x), sblk, lay(s_h), lay(s_conv), lay(s_re), lay(s_im))
    scratch = [
        pltpu.VMEM((RET_HEADS * 2 * HEAD_DIM, 2 * rows), BF16),
        pltpu.VMEM((rows, RET_WIDTH), F32),
        pltpu.VMEM((rows, RET_WIDTH), F32),
        pltpu.VMEM((rows, RET_WIDTH), F32),
        pltpu.VMEM((rows, d), BF16),
        pltpu.VMEM((rows, d), F32),
    ]
    return pl.pallas_call(
        _decode_kernel,
        grid=(depth, nb),
        in_specs=in_specs,
        out_specs=out_specs,
        out_shape=out_shape,
        scratch_shapes=scratch,
        compiler_params=pltpu.CompilerParams(
            dimension_semantics=("arbitrary", "arbitrary"), vmem_limit_bytes=VMEM_LIMIT),
        name="decode_layers",
    )(x, mod, w["norm_g"], w["w_in"], w["w_out"],
      tabs["dcos_q"], tabs["dsin_q"], tabs["dcos_k"], tabs["dsin_k"], tabs["gamma"],
      w["gn_g"], w["conv_w"], w["conv_b"], w["w_gate"], w["b_gate"], w["lam"],
      w["bd"], w["cd"], w["apr"], w["api"], w["s5_d"], w["w_glu"], w["final_g"],
      s_ret, s_h, s_conv, s_re, s_im, tabs["row_select"])


def _rope_tables(pos, scale):
    half = HEAD_DIM // 2
    inv = ROPE_BASE ** (-np.arange(half, dtype=np.float64) / half)
    ang = pos[:, None] * inv[None, :]
    cos, sin = np.cos(ang), np.sin(ang)
    cosf = np.concatenate([cos, cos], axis=-1) * scale
    sinf = np.concatenate([-sin, sin], axis=-1) * scale
    return cosf, sinf


def _tables(seq, dec, c):
    log_g = np.log1p(-np.exp2(-5.0 - np.arange(RET_HEADS, dtype=np.float64)))
    idx = np.arange(c, dtype=np.float64)
    rel = idx[:, None] - idx[None, :]
    intra = np.where(rel[None] >= 0, np.exp(np.maximum(rel, 0.0)[None] * log_g[:, None, None]), 0.0)
    q_dec = np.exp((idx + 1.0)[:, None] * log_g[None, :])
    k_dec = np.exp((c - 1.0 - idx)[:, None] * log_g[None, :])
    q_dec = np.broadcast_to(q_dec.T[:, :, None], (RET_HEADS, c, HEAD_DIM))
    k_dec = np.broadcast_to(k_dec.T[:, :, None], (RET_HEADS, c, HEAD_DIM))
    c_dec = np.exp(c * log_g)
    gamma = np.exp(log_g)
    k_scale = HEAD_DIM ** -0.5
    cos_q, sin_q = _rope_tables(np.arange(seq, dtype=np.float64), 1.0)
    cos_k, sin_k = _rope_tables(np.arange(seq, dtype=np.float64), k_scale)
    dpos = PAST_LEN + np.arange(1, dtype=np.float64)
    dcos_q, dsin_q = _rope_tables(dpos, 1.0)
    dcos_k, dsin_k = _rope_tables(dpos, k_scale)
    blk = np.arange(dec // DEC_BLOCK)[:, None, None]
    row = np.arange(dec)[None, :, None]
    col = np.arange(DEC_BLOCK * HEAD_DIM)[None, None, :]
    row_select = row == blk * DEC_BLOCK + col // HEAD_DIM
    tabs = dict(intra=intra, q_dec=q_dec, k_dec=k_dec, c_dec=c_dec, gamma=gamma,
                cos_q=cos_q, sin_q=sin_q, cos_k=cos_k, sin_k=sin_k,
                dcos_q=dcos_q, dsin_q=dsin_q, dcos_k=dcos_k, dsin_k=dsin_k)
    tabs = {k: jnp.asarray(np.ascontiguousarray(v), dtype=F32) for k, v in tabs.items()}
    tabs["row_select"] = jnp.asarray(row_select, dtype=BF16)
    return tabs


def _block_diag(blocks):
    n, r, c = blocks.shape
    eye = jnp.eye(n, dtype=blocks.dtype)
    return jnp.einsum("nrc,nm->nrmc", blocks, eye).reshape(n * r, n * c)


def kernel(x_prompt, x_sample, state_ret, state_lru_h, state_lru_conv, state_s5_re, state_s5_im,
           c_prompt, c_sample, norm_g, w_ada, b_ada, w_in, ret_gn_g, conv_w, conv_b, w_rg, b_rg,
           w_ig, b_ig, lru_lambda, s5_a_re, s5_a_im, s5_b_re, s5_b_im, s5_c_re, s5_c_im, s5_d,
           s5_log_dt, s5_w_glu, w_out, final_g):
    depth = w_in.shape[0]
    bsz, seq, d = x_prompt.shape
    dec = x_sample.shape[0]
    assert x_sample.shape[1] == 1 and dec % DEC_BLOCK == 0

    assert bsz == SUBLANES and seq % PROMPT_T == 0
    tabs = _tables(seq, dec, PROMPT_T)
    mod_all = _ada(jnp.concatenate([c_prompt, c_sample], axis=0), w_ada, b_ada)
    n_pow = SUBLANES
    apr, api, bbr, bbi = _s5_prep(s5_a_re, s5_a_im, s5_log_dt, s5_b_re, s5_b_im, n_pow)

    bdiag = jax.vmap(_block_diag)
    bd = jnp.concatenate([bdiag(bbr), bdiag(bbi)], axis=2)
    cd = jnp.concatenate([bdiag(jnp.swapaxes(s5_c_re, 2, 3)),
                          bdiag(jnp.swapaxes(-s5_c_im, 2, 3))], axis=1)
    w = dict(
        norm_g=norm_g.reshape(depth, 1, d),
        w_in=w_in.astype(BF16),
        w_out=w_out.astype(BF16),
        gn_g=ret_gn_g.reshape(depth, 1, RET_WIDTH),
        conv_w=conv_w,
        conv_b=conv_b.reshape(depth, 1, LRU_WIDTH),
        w_gate=jnp.concatenate([bdiag(w_rg), bdiag(w_ig)], axis=2).astype(BF16),
        b_gate=jnp.concatenate([b_rg, b_ig], axis=1).reshape(depth, 1, 2 * LRU_WIDTH),
        lam=lru_lambda.reshape(depth, 1, LRU_WIDTH),
        bd=bd.astype(BF16),
        cd=cd.astype(BF16),
        apr=apr.reshape(depth, n_pow, S5_FLAT),
        api=api.reshape(depth, n_pow, S5_FLAT),
        s5_d=s5_d.reshape(depth, 1, S5_WIDTH),
        w_glu=s5_w_glu.astype(BF16),
        final_g=final_g.reshape(1, d),
    )

    xp = x_prompt
    outs_p = []
    for l in range(depth):
        xp, sret, hl, cv, sre, sim = _prompt_layer_bm(xp, mod_all, w, tabs, l, final=l == depth - 1)
        outs_p.append((sret, hl, jnp.swapaxes(cv, 0, 1),
                       sre.reshape(bsz, S5_GROUPS, S5_STATE), sim.reshape(bsz, S5_GROUPS, S5_STATE)))

    states = (state_ret, state_lru_h,
              state_lru_conv.reshape(depth, dec, (CONV_WIDTH - 1) * LRU_WIDTH),
              state_s5_re.reshape(depth, dec, S5_FLAT), state_s5_im.reshape(depth, dec, S5_FLAT))
    xs, sret_s, hl_s, cv_s, sre_s, sim_s = _decode_layers(
        x_sample.reshape(dec, d), mod_all[:, bsz:], states, w, tabs)

    def stk(k):
        return jnp.stack([o[k] for o in outs_p])

    return (xp, xs.reshape(dec, 1, d),
            stk(0), sret_s, stk(1), hl_s,
            stk(2), cv_s.reshape(depth, dec, CONV_WIDTH - 1, LRU_WIDTH),
            stk(3), sre_s.reshape(depth, dec, S5_GROUPS, S5_STATE),
            stk(4), sim_s.reshape(depth, dec, S5_GROUPS, S5_STATE))
```
